```python
import math
import jax
import jax.numpy as jnp
from jax import lax
import numpy as np

D_MODEL = 2048
BATCH = 1
SEQ = 8192
DEPTH = 2
DEC_BATCH = 16
DEC_SEQ = 64
PAST_LEN = 2048

CHUNK = 64
N_EVEN = (DEPTH + 1) // 2
N_ODD = DEPTH // 2
N_HEADS = 16
N_KV_HEADS = 2
HEAD_DIM = 64
Q_PER_KV = N_HEADS // N_KV_HEADS
WINDOW = 128
WINDOW_CHUNKS = WINDOW // CHUNK
BAND = (WINDOW_CHUNKS + 1) * CHUNK
D_ATTN = N_HEADS * HEAD_DIM
D_KV = N_KV_HEADS * HEAD_DIM
GM_CHUNK = 128
D_GM = D_MODEL // 2
N_GM_GROUPS = 16
GM_GROUP_DIM = D_GM // N_GM_GROUPS
D_IN0 = D_ATTN + 2 * D_KV + 2 * D_GM
D_MIX0 = D_ATTN + D_GM
S5_GROUP = 16
N_S5_GROUPS = D_MODEL // S5_GROUP
S5_STATE = 64
DT_MIN = 0.001
DT_MAX = 0.1
D_FF = ((8 * D_MODEL + 3 * 256 - 1) // (3 * 256)) * 256
RMS_EPS = 1e-5
NEG_INF = -1e30

kernel_name = "hybrid_swa_sink_gmlp_s5_stream_step"


def _rmsnorm(x, g):
    xf = x.astype(jnp.float32)
    y = xf * lax.rsqrt(jnp.mean(xf * xf, axis=-1, keepdims=True) + RMS_EPS)
    return (y * g.astype(jnp.float32)).astype(x.dtype)


def _swiglu_block(x, g, w_gate, w_up, w_down):
    h = _rmsnorm(x, g)
    y = (jax.nn.silu(h @ w_gate) * (h @ w_up)) @ w_down
    return x + y.astype(x.dtype)


def _alibi_slopes():
    return jnp.exp2(-8.0 * jnp.arange(1, N_HEADS + 1, dtype=jnp.float32) / N_HEADS)


def _sink_attention(q, k, v, q_pos, k_pos, k_valid, sinks):
    s = jnp.einsum('bnqkrd,bnskd->bnkrqs', q.astype(jnp.float32), k.astype(jnp.float32)) * (HEAD_DIM ** -0.5)
    slopes = _alibi_slopes().reshape(N_KV_HEADS, Q_PER_KV)
    dist = jnp.abs(q_pos[:, :, None] - k_pos[:, None, :]).astype(jnp.float32)
    s = s - slopes[None, None, :, :, None, None] * dist[None, :, None, None, :, :]
    s = jnp.where(k_valid[None, :, None, None, None, :], s, NEG_INF)
    sink = sinks.astype(jnp.float32).reshape(N_KV_HEADS, Q_PER_KV)[None, None, :, :, None, None]
    m = jnp.maximum(jnp.max(s, axis=-1, keepdims=True), sink)
    p = jnp.exp(s - m)
    denom = jnp.sum(p, axis=-1, keepdims=True) + jnp.exp(sink - m)
    return jnp.einsum('bnkrqs,bnskd->bnqkrd', p / denom, v.astype(jnp.float32))


def _split_ab(z):
    q = z[..., :D_ATTN]
    k = z[..., D_ATTN:D_ATTN + D_KV]
    v = z[..., D_ATTN + D_KV:D_ATTN + 2 * D_KV]
    gu = z[..., D_ATTN + 2 * D_KV:D_ATTN + 2 * D_KV + D_GM]
    gv = z[..., D_ATTN + 2 * D_KV + D_GM:]
    return q, k, v, gu, gv


def _gm_prepare(gu, gv, gm_norm):
    u = jax.nn.gelu(gu, approximate=False)
    v = _rmsnorm(jax.nn.gelu(gv, approximate=False), gm_norm)
    return u, v


def _gm_spatial_weights(gm_ws):
    blk = jnp.arange(GM_CHUNK) // CHUNK
    mask = blk[:, None] >= blk[None, :]
    return jnp.where(mask[None], gm_ws, jnp.zeros((), gm_ws.dtype))


def _band(z, n_chunks):
    bsz = z.shape[0]
    zp = jnp.pad(z, ((0, 0), (WINDOW_CHUNKS * CHUNK, 0), (0, 0), (0, 0)))
    zp = zp.reshape(bsz, n_chunks + WINDOW_CHUNKS, CHUNK, N_KV_HEADS, HEAD_DIM)
    return jnp.concatenate([zp[:, j:j + n_chunks] for j in range(WINDOW_CHUNKS + 1)], axis=2)


def _layer_ab_prompt(x, g_mix, w_in, sinks, gm_norm, gm_ws, gm_b, w_out):
    bsz, t_len, _ = x.shape
    n_chunks = t_len // CHUNK
    h = _rmsnorm(x, g_mix)
    q, k, v, gu, gv = _split_ab(h @ w_in)
    q = q.reshape(bsz, n_chunks, CHUNK, N_KV_HEADS, Q_PER_KV, HEAD_DIM)
    k = k.reshape(bsz, t_len, N_KV_HEADS, HEAD_DIM)
    v = v.reshape(bsz, t_len, N_KV_HEADS, HEAD_DIM)
    q_pos = jnp.arange(t_len).reshape(n_chunks, CHUNK)
    k_pos = (jnp.arange(n_chunks)[:, None] - WINDOW_CHUNKS) * CHUNK + jnp.arange(BAND)[None, :]
    o_attn = _sink_attention(q, _band(k, n_chunks), _band(v, n_chunks), q_pos, k_pos, k_pos >= 0, sinks)
    o_attn = o_attn.reshape(bsz, t_len, D_ATTN)
    u, gvn = _gm_prepare(gu, gv, gm_norm)
    gv_c = gvn.reshape(bsz, t_len // GM_CHUNK, GM_CHUNK, N_GM_GROUPS, GM_GROUP_DIM)
    sp = jnp.einsum('gts,bcsgd->bctgd', _gm_spatial_weights(gm_ws), gv_c) + gm_b.T[None, None, :, :, None]
    o_gm = u * sp.reshape(bsz, t_len, D_GM).astype(u.dtype)
    y = jnp.concatenate([o_attn.astype(x.dtype), o_gm.astype(x.dtype)], axis=-1) @ w_out
    keep = min(WINDOW, t_len)
    return x + y.astype(x.dtype), k[:, t_len - keep:], v[:, t_len - keep:]


def _layer_ab_sample(x, cache_k, cache_v, g_mix, w_in, sinks, gm_norm, gm_ws, gm_b, w_out):
    bsz, t_len, _ = x.shape
    n_past = cache_k.shape[1]
    h = _rmsnorm(x, g_mix)
    q, k, v, gu, gv = _split_ab(h @ w_in)
    q = q.reshape(bsz, 1, t_len, N_KV_HEADS, Q_PER_KV, HEAD_DIM)
    k = k.reshape(bsz, t_len, N_KV_HEADS, HEAD_DIM)
    v = v.reshape(bsz, t_len, N_KV_HEADS, HEAD_DIM)
    k_all = jnp.concatenate([cache_k.astype(k.dtype), k], axis=1)[:, None]
    v_all = jnp.concatenate([cache_v.astype(v.dtype), v], axis=1)[:, None]
    q_pos = (PAST_LEN + jnp.arange(t_len))[None]
    k_pos = jnp.concatenate([PAST_LEN - n_past + jnp.arange(n_past), PAST_LEN + jnp.arange(t_len)])[None]
    o_attn = _sink_attention(q, k_all, v_all, q_pos, k_pos, k_pos >= 0, sinks).reshape(bsz, t_len, D_ATTN)
    u, gvn = _gm_prepare(gu, gv, gm_norm)
    w_s = _gm_spatial_weights(gm_ws)[:, :t_len, :t_len]
    sp = jnp.einsum('gts,bsgd->btgd', w_s, gvn.reshape(bsz, t_len, N_GM_GROUPS, GM_GROUP_DIM))
    sp = sp + gm_b[:, :t_len].T[None, :, :, None]
    o_gm = u * sp.reshape(bsz, t_len, D_GM).astype(u.dtype)
    y = jnp.concatenate([o_attn.astype(x.dtype), o_gm.astype(x.dtype)], axis=-1) @ w_out
    return x + y.astype(x.dtype), k, v, gvn


def _complex_affine_combine(e1, e2):
    a1r, a1i, b1r, b1i = e1
    a2r, a2i, b2r, b2i = e2
    return (a2r * a1r - a2i * a1i,
            a2r * a1i + a2i * a1r,
            a2r * b1r - a2i * b1i + b2r,
            a2r * b1i + a2i * b1r + b2i)


def _s5_discretize(lam_re, lam_im, log_dt, b_re, b_im):
    dt = jnp.exp(log_dt.astype(jnp.float32))[:, None]
    lr = lam_re.astype(jnp.float32)
    li = lam_im.astype(jnp.float32)
    mag = jnp.exp(lr * dt)
    a_re = mag * jnp.cos(li * dt)
    a_im = mag * jnp.sin(li * dt)
    den = lr * lr + li * li
    nr = a_re - 1.0
    f_re = (nr * lr + a_im * li) / den
    f_im = (a_im * lr - nr * li) / den
    br = b_re.astype(jnp.float32)
    bi = b_im.astype(jnp.float32)
    bb_re = f_re[..., None] * br - f_im[..., None] * bi
    bb_im = f_re[..., None] * bi + f_im[..., None] * br
    return a_re, a_im, bb_re, bb_im


def _layer_s5(x, s0_re, s0_im, g_mix, lam_re, lam_im, log_dt, b_re, b_im, c_re, c_im, d_skip, w_glu):
    bsz, t_len, _ = x.shape
    h = _rmsnorm(x, g_mix)
    hf = h.astype(jnp.float32)
    u = hf.reshape(bsz, t_len, N_S5_GROUPS, S5_GROUP)
    a_re, a_im, bb_re, bb_im = _s5_discretize(lam_re, lam_im, log_dt, b_re, b_im)
    bu_re = jnp.einsum('gpi,btgi->btgp', bb_re, u)
    bu_im = jnp.einsum('gpi,btgi->btgp', bb_im, u)
    elems = (jnp.broadcast_to(a_re, bu_re.shape), jnp.broadcast_to(a_im, bu_re.shape), bu_re, bu_im)
    ac_re, ac_im, st_re, st_im = lax.associative_scan(_complex_affine_combine, elems, axis=1)
    if s0_re is not None:
        s0r = s0_re.astype(jnp.float32)[:, None]
        s0i = s0_im.astype(jnp.float32)[:, None]
        st_re, st_im = (ac_re * s0r - ac_im * s0i + st_re, ac_re * s0i + ac_im * s0r + st_im)
    y = (jnp.einsum('gip,btgp->btgi', c_re.astype(jnp.float32), st_re)
         - jnp.einsum('gip,btgp->btgi', c_im.astype(jnp.float32), st_im))
    y = y.reshape(bsz, t_len, D_MODEL) + d_skip.astype(jnp.float32) * hf
    zg = jax.nn.gelu(y, approximate=False) @ w_glu.astype(jnp.float32)
    out = zg[..., :D_MODEL] * jax.nn.sigmoid(zg[..., D_MODEL:])
    return x + out.astype(x.dtype), st_re[:, -1], st_im[:, -1]


def setup_inputs(seed: int = 0) -> dict:
    key = jax.random.key(seed)
    ks = jax.random.split(key, 32)

    def nrm(k, shape, scale=1.0):
        return scale * jax.random.normal(k, shape, jnp.float32)

    swa_rows = min(WINDOW, PAST_LEN)
    n_idx = jnp.arange(S5_STATE, dtype=jnp.float32)
    s5_shape = (N_ODD, N_S5_GROUPS, S5_STATE)
    return {
        "x_prompt": nrm(ks[0], (BATCH, SEQ, D_MODEL)),
        "x_sample": nrm(ks[1], (DEC_BATCH, DEC_SEQ, D_MODEL)),
        "cache_swa_k": nrm(ks[2], (N_EVEN, DEC_BATCH, swa_rows, N_KV_HEADS, HEAD_DIM)),
        "cache_swa_v": nrm(ks[3], (N_EVEN, DEC_BATCH, swa_rows, N_KV_HEADS, HEAD_DIM)),
        "state_s5_re": nrm(ks[4], (N_ODD, DEC_BATCH, N_S5_GROUPS, S5_STATE), 0.3),
        "state_s5_im": nrm(ks[5], (N_ODD, DEC_BATCH, N_S5_GROUPS, S5_STATE), 0.3),
        "norm_mix": 1.0 + nrm(ks[6], (DEPTH, D_MODEL), 0.02),
        "norm_ffn": 1.0 + nrm(ks[7], (DEPTH, D_MODEL), 0.02),
        "norm_final": 1.0 + nrm(ks[8], (D_MODEL,), 0.02),
        "w_in0": nrm(ks[9], (N_EVEN, D_MODEL, D_IN0), D_MODEL ** -0.5),
        "attn_sinks": nrm(ks[10], (N_EVEN, N_HEADS)),
        "gm_norm": 1.0 + nrm(ks[11], (N_EVEN, D_GM), 0.02),
        "gm_ws": nrm(ks[12], (N_EVEN, N_GM_GROUPS, GM_CHUNK, GM_CHUNK), GM_CHUNK ** -0.5),
        "gm_b": 1.0 + nrm(ks[13], (N_EVEN, N_GM_GROUPS, GM_CHUNK), 0.02),
        "w_out0": nrm(ks[14], (N_EVEN, D_MIX0, D_MODEL), D_MIX0 ** -0.5),
        "s5_lam_re": -0.5 + nrm(ks[15], s5_shape, 0.01),
        "s5_lam_im": math.pi * n_idx + nrm(ks[16], s5_shape, 0.01),
        "s5_log_dt": jax.random.uniform(ks[17], (N_ODD, N_S5_GROUPS), jnp.float32,
                                        math.log(DT_MIN), math.log(DT_MAX)),
        "s5_b_re": nrm(ks[18], (N_ODD, N_S5_GROUPS, S5_STATE, S5_GROUP), (2 * S5_GROUP) ** -0.5),
        "s5_b_im": nrm(ks[19], (N_ODD, N_S5_GROUPS, S5_STATE, S5_GROUP), (2 * S5_GROUP) ** -0.5),
        "s5_c_re": nrm(ks[20], (N_ODD, N_S5_GROUPS, S5_GROUP, S5_STATE), S5_STATE ** -0.5),
        "s5_c_im": nrm(ks[21], (N_ODD, N_S5_GROUPS, S5_GROUP, S5_STATE), S5_STATE ** -0.5),
        "s5_d": nrm(ks[22], (N_ODD, D_MODEL)),
        "s5_w_glu": nrm(ks[23], (N_ODD, D_MODEL, 2 * D_MODEL), D_MODEL ** -0.5),
        "ffn_w_gate": nrm(ks[24], (DEPTH, D_MODEL, D_FF), D_MODEL ** -0.5),
        "ffn_w_up": nrm(ks[25], (DEPTH, D_MODEL, D_FF), D_MODEL ** -0.5),
        "ffn_w_down": nrm(ks[26], (DEPTH, D_FF, D_MODEL), D_FF ** -0.5),
    }


def reference(x_prompt, x_sample, cache_swa_k, cache_swa_v, state_s5_re, state_s5_im,
              norm_mix, norm_ffn, norm_final, w_in0, attn_sinks, gm_norm, gm_ws, gm_b, w_out0,
              s5_lam_re, s5_lam_im, s5_log_dt, s5_b_re, s5_b_im, s5_c_re, s5_c_im, s5_d, s5_w_glu,
              ffn_w_gate, ffn_w_up, ffn_w_down):
    xp = x_prompt
    xs = x_sample
    kp_list, vp_list, ks_list, vs_list, gv_list = [], [], [], [], []
    spr_list, spi_list, ssr_list, ssi_list = [], [], [], []
    for layer in range(DEPTH):
        if layer % 2 == 0:
            e = layer // 2
            xp, kp, vp = _layer_ab_prompt(xp, norm_mix[layer], w_in0[e], attn_sinks[e], gm_norm[e],
                                          gm_ws[e], gm_b[e], w_out0[e])
            xs, ks_new, vs_new, gv_new = _layer_ab_sample(xs, cache_swa_k[e], cache_swa_v[e], norm_mix[layer],
                                                          w_in0[e], attn_sinks[e], gm_norm[e], gm_ws[e],
                                                          gm_b[e], w_out0[e])
            kp_list.append(kp)
            vp_list.append(vp)
            ks_list.append(ks_new)
            vs_list.append(vs_new)
            gv_list.append(gv_new)
        else:
            o = layer // 2
            xp, spr, spi = _layer_s5(xp, None, None, norm_mix[layer], s5_lam_re[o], s5_lam_im[o], s5_log_dt[o],
                                     s5_b_re[o], s5_b_im[o], s5_c_re[o], s5_c_im[o], s5_d[o], s5_w_glu[o])
            xs, ssr, ssi = _layer_s5(xs, state_s5_re[o], state_s5_im[o], norm_mix[layer], s5_lam_re[o],
                                     s5_lam_im[o], s5_log_dt[o], s5_b_re[o], s5_b_im[o], s5_c_re[o],
                                     s5_c_im[o], s5_d[o], s5_w_glu[o])
            spr_list.append(spr)
            spi_list.append(spi)
            ssr_list.append(ssr)
            ssi_list.append(ssi)
        xp = _swiglu_block(xp, norm_ffn[layer], ffn_w_gate[layer], ffn_w_up[layer], ffn_w_down[layer])
        xs = _swiglu_block(xs, norm_ffn[layer], ffn_w_gate[layer], ffn_w_up[layer], ffn_w_down[layer])
    y_prompt = _rmsnorm(xp, norm_final)
    y_sample = _rmsnorm(xs, norm_final)
    swa_k_prompt = jnp.stack(kp_list)
    swa_v_prompt = jnp.stack(vp_list)
    swa_k_sample = jnp.stack(ks_list)
    swa_v_sample = jnp.stack(vs_list)
    gm_v_sample = jnp.stack(gv_list)
    s5_re_prompt = jnp.stack(spr_list)
    s5_im_prompt = jnp.stack(spi_list)
    s5_re_sample = jnp.stack(ssr_list)
    s5_im_sample = jnp.stack(ssi_list)
    return (y_prompt, y_sample, swa_k_prompt, swa_v_prompt, swa_k_sample, swa_v_sample,
            gm_v_sample, s5_re_prompt, s5_im_prompt, s5_re_sample, s5_im_sample)
```

```python
import functools
import math

import jax
import jax.numpy as jnp
from jax import lax
from jax.experimental import pallas as pl
from jax.experimental.pallas import tpu as pltpu

F32 = jnp.float32
BF16 = jnp.bfloat16

CHUNK = 64
HEAD_DIM = 64
N_HEADS = 16
N_KV_HEADS = 2
Q_PER_KV = N_HEADS // N_KV_HEADS
WINDOW = 128
BAND = WINDOW + CHUNK
D_ATTN = N_HEADS * HEAD_DIM
D_KV = N_KV_HEADS * HEAD_DIM
GM_CHUNK = 128
N_GM_GROUPS = 16
S5_GROUP = 16
S5_STATE = 64
RMS_EPS = 1e-5
NEG_INF = -1e30

LANES = 128
SUBLANES = 8
VMEM_LIMIT_BYTES = 56 * 1024 * 1024

S5_L = SUBLANES
S5_LANE_GROUPS = LANES // S5_GROUP
S5_BLOCK_STATE = S5_LANE_GROUPS * S5_STATE


def _gelu(x):
    return 0.5 * x * (1.0 + lax.erf(x * math.sqrt(0.5)))


def _rms_scale(x):
    return x * lax.rsqrt(jnp.mean(x * x, axis=-1, keepdims=True) + RMS_EPS)


def _const_spec(shape):
    zeros = (0,) * len(shape)
    return pl.BlockSpec(shape, lambda *_: zeros, pipeline_mode=pl.Buffered(1))


def _mix0_kernel(n_prompt_tiles, units,
                 x_ref, ck_ref, cv_ref, g_ref, win_ref, bias_ref, sink_ref, gmn_ref, wsp_ref,
                 gmb_ref, wout_ref,
                 x1_ref, k_ref, v_ref, gvn_ref,
                 z_ref, q_ref, ocat_ref, kprev_ref, vprev_ref):
    i = pl.program_id(0)
    is_sample = i >= n_prompt_tiles
    tm = units * GM_CHUNK
    d_gm = gmn_ref.shape[-1]
    off_k = D_ATTN
    off_v = D_ATTN + D_KV
    off_gu = D_ATTN + 2 * D_KV
    off_gv = off_gu + d_gm

    @pl.when(i == 0)
    def _():
        kprev_ref[...] = jnp.zeros_like(kprev_ref)
        vprev_ref[...] = jnp.zeros_like(vprev_ref)

    x = x_ref[...]
    h = (_rms_scale(x) * g_ref[...]).astype(BF16)
    z_ref[...] = jnp.dot(h, win_ref[...], preferred_element_type=F32)

    q_ref[...] = (z_ref[:, 0:D_ATTN] * (HEAD_DIM ** -0.5)).astype(BF16)
    k = z_ref[:, off_k:off_k + D_KV]
    v = z_ref[:, off_v:off_v + D_KV]
    k_ref[...] = k
    v_ref[...] = v

    def lane_lo(rows):
        return lax.broadcasted_iota(jnp.int32, (rows, LANES), 1) < HEAD_DIM

    def replicate(a):
        r = pltpu.roll(a, HEAD_DIM, axis=1)
        lo = lane_lo(a.shape[0])
        return jnp.where(lo, a, r).astype(BF16), jnp.where(lo, r, a).astype(BF16)

    k_rep = replicate(k)
    v_rep = replicate(v)
    ck_rep = replicate(ck_ref[...])
    cv_rep = replicate(cv_ref[...])

    lo64 = lane_lo(CHUNK)
    col = lax.broadcasted_iota(jnp.int32, (Q_PER_KV * CHUNK, BAND), 1)
    chunks_per_tile = tm // CHUNK

    for u in range(units):
        for c2 in range(GM_CHUNK // CHUNK):
            r0 = u * GM_CHUNK + c2 * CHUNK
            stream = r0 // CHUNK
            chunk_index = i * chunks_per_tile + stream
            valid_from = jnp.where(is_sample, 0, jnp.maximum(WINDOW - CHUNK * chunk_index, 0))
            valid = col >= valid_from
            for kv in range(N_KV_HEADS):
                def band(cur, prev_ref, cached):
                    if u == 0:
                        prev_unit = prev_ref[kv]
                    else:
                        prev_unit = cur[(u - 1) * GM_CHUNK:u * GM_CHUNK]
                    if c2 == 0:
                        prompt_prev = prev_unit
                    else:
                        prompt_prev = jnp.concatenate(
                            [prev_unit[CHUNK:], cur[u * GM_CHUNK:u * GM_CHUNK + CHUNK]], axis=0)
                    sample_prev = cached[stream * WINDOW:(stream + 1) * WINDOW]
                    prev = jnp.where(is_sample, sample_prev, prompt_prev)
                    return jnp.concatenate([prev, cur[r0:r0 + CHUNK]], axis=0)

                kb = band(k_rep[kv], kprev_ref, ck_rep[kv])
                vb = band(v_rep[kv], vprev_ref, cv_rep[kv])

                pieces = []
                for m in range(Q_PER_KV // 2):
                    c0 = kv * Q_PER_KV * HEAD_DIM + m * LANES
                    qp = q_ref[r0:r0 + CHUNK, c0:c0 + LANES]
                    pieces.append(jnp.where(lo64, qp, jnp.zeros_like(qp)))
                    pieces.append(jnp.where(lo64, jnp.zeros_like(qp), qp))
                qs = jnp.concatenate(pieces, axis=0)
                s = lax.dot_general(qs, kb, (((1,), (1,)), ((), ())),
                                    preferred_element_type=F32)
                rows = slice(kv * Q_PER_KV * CHUNK, (kv + 1) * Q_PER_KV * CHUNK)
                s = s - bias_ref[rows, :]
                s = jnp.where(valid, s, NEG_INF)
                sink = sink_ref[rows, :]
                mx = jnp.maximum(jnp.max(s, axis=-1, keepdims=True), sink)
                p = jnp.exp(s - mx)
                denom = jnp.sum(p, axis=-1, keepdims=True) + jnp.exp(sink - mx)
                pn = (p / denom).astype(BF16)
                o = jnp.dot(pn, vb, preferred_element_type=F32)
                for m in range(Q_PER_KV // 2):
                    o_pair = jnp.where(lo64, o[(2 * m) * CHUNK:(2 * m + 1) * CHUNK],
                                       o[(2 * m + 1) * CHUNK:(2 * m + 2) * CHUNK])
                    c0 = kv * Q_PER_KV * HEAD_DIM + m * LANES
                    ocat_ref[r0:r0 + CHUNK, c0:c0 + LANES] = o_pair.astype(BF16)

    lo128 = lane_lo(GM_CHUNK)
    for u in range(units):
        rows = slice(u * GM_CHUNK, (u + 1) * GM_CHUNK)
        ua = _gelu(z_ref[rows, off_gu:off_gu + d_gm])
        gvn = _rms_scale(_gelu(z_ref[rows, off_gv:off_gv + d_gm])) * gmn_ref[...]

        @pl.when(is_sample)
        def _():
            gvn_ref[rows, :] = gvn

        gb = gvn.astype(BF16)
        for m in range(N_GM_GROUPS // 2):
            cols = slice(m * LANES, (m + 1) * LANES)
            rhs = gb[:, cols]
            a0 = jnp.dot(wsp_ref[0, 2 * m], rhs, preferred_element_type=F32)
            a1 = jnp.dot(wsp_ref[0, 2 * m + 1], rhs, preferred_element_type=F32)
            sp = jnp.where(lo128, a0, a1) + gmb_ref[0, :, cols]
            ocat_ref[rows, D_ATTN + m * LANES:D_ATTN + (m + 1) * LANES] = (ua[:, cols] * sp).astype(BF16)

    x1_ref[...] = x_ref[...] + jnp.dot(ocat_ref[...], wout_ref[...], preferred_element_type=F32)

    @pl.when(jnp.logical_not(is_sample))
    def _():
        for kv in range(N_KV_HEADS):
            kprev_ref[kv] = k_rep[kv][tm - GM_CHUNK:]
            vprev_ref[kv] = v_rep[kv][tm - GM_CHUNK:]


def _mix0(x, cache_k, cache_v, g, w_in, bias_tbl, sink_rows, gm_norm, wsp, gmb, w_out, *,
          n_prompt_rows, units):
    m_rows, d = x.shape
    tm = units * GM_CHUNK
    assert m_rows % tm == 0 and n_prompt_rows % tm == 0
    n_tiles = m_rows // tm
    n_prompt_tiles = n_prompt_rows // tm
    n_sample_rows = m_rows - n_prompt_rows
    d_in = w_in.shape[1]
    d_gm = gm_norm.shape[-1]
    cache_rows = (tm // CHUNK) * WINDOW

    def sample_block(i):
        return jnp.maximum(i - n_prompt_tiles, 0)

    def kind(i):
        return jnp.where(i >= n_prompt_tiles, 1, 0)

    in_specs = [
        pl.BlockSpec((tm, d), lambda i: (i, 0)),
        pl.BlockSpec((cache_rows, D_KV), lambda i: (sample_block(i), 0)),
        pl.BlockSpec((cache_rows, D_KV), lambda i: (sample_block(i), 0)),
        _const_spec((1, d)),
        _const_spec((d, d_in)),
        _const_spec(bias_tbl.shape),
        _const_spec(sink_rows.shape),
        _const_spec((1, d_gm)),
        pl.BlockSpec((1, N_GM_GROUPS, GM_CHUNK, GM_CHUNK), lambda i: (kind(i), 0, 0, 0)),
        pl.BlockSpec((1, GM_CHUNK, d_gm), lambda i: (kind(i), 0, 0)),
        _const_spec(w_out.shape),
    ]
    out_specs = [
        pl.BlockSpec((tm, d), lambda i: (i, 0)),
        pl.BlockSpec((tm, D_KV), lambda i: (i, 0)),
        pl.BlockSpec((tm, D_KV), lambda i: (i, 0)),
        pl.BlockSpec((tm, d_gm), lambda i: (sample_block(i), 0)),
    ]
    out_shape = [
        jax.ShapeDtypeStruct((m_rows, d), F32),
        jax.ShapeDtypeStruct((m_rows, D_KV), F32),
        jax.ShapeDtypeStruct((m_rows, D_KV), F32),
        jax.ShapeDtypeStruct((n_sample_rows, d_gm), F32),
    ]
    scratch = [
        pltpu.VMEM((tm, d_in), F32),
        pltpu.VMEM((tm, D_ATTN), BF16),
        pltpu.VMEM((tm, D_ATTN + d_gm), BF16),
        pltpu.VMEM((N_KV_HEADS, GM_CHUNK, LANES), BF16),
        pltpu.VMEM((N_KV_HEADS, GM_CHUNK, LANES), BF16),
    ]
    return pl.pallas_call(
        functools.partial(_mix0_kernel, n_prompt_tiles, units),
        grid=(n_tiles,),
        in_specs=in_specs,
        out_specs=out_specs,
        out_shape=out_shape,
        scratch_shapes=scratch,
        compiler_params=pltpu.CompilerParams(
            dimension_semantics=("arbitrary",), vmem_limit_bytes=VMEM_LIMIT_BYTES),
        name="mix0",
    )(x, cache_k, cache_v, g, w_in, bias_tbl, sink_rows, gm_norm, wsp, gmb, w_out)


def _ffn_kernel(x_ref, g_ref, wg_ref, wu_ref, wd_ref, gnext_ref, out_ref, hnext_ref, h_ref, acc_ref):
    j = pl.program_id(1)

    @pl.when(j == 0)
    def _():
        h_ref[...] = (_rms_scale(x_ref[...]) * g_ref[...]).astype(BF16)
        acc_ref[...] = jnp.zeros_like(acc_ref)

    h = h_ref[...]
    gate = jnp.dot(h, wg_ref[...], preferred_element_type=F32)
    up = jnp.dot(h, wu_ref[...], preferred_element_type=F32)
    act = (gate * jax.nn.sigmoid(gate) * up).astype(BF16)
    acc_ref[...] += jnp.dot(act, wd_ref[...], preferred_element_type=F32)

    @pl.when(j == pl.num_programs(1) - 1)
    def _():
        out = x_ref[...] + acc_ref[...]
        out_ref[...] = out
        hnext_ref[...] = _rms_scale(out) * gnext_ref[...]


def _ffn(x, g, wg, wu, wd, gnext, *, tm, tf):
    m_rows, d = x.shape
    f = wg.shape[1]
    assert m_rows % tm == 0 and f % tf == 0
    return pl.pallas_call(
        _ffn_kernel,
        grid=(m_rows // tm, f // tf),
        in_specs=[
            pl.BlockSpec((tm, d), lambda i, j: (i, 0)),
            pl.BlockSpec((1, d), lambda i, j: (0, 0)),
            pl.BlockSpec((d, tf), lambda i, j: (0, j)),
            pl.BlockSpec((d, tf), lambda i, j: (0, j)),
            pl.BlockSpec((tf, d), lambda i, j: (j, 0)),
            pl.BlockSpec((1, d), lambda i, j: (0, 0)),
        ],
        out_specs=[
            pl.BlockSpec((tm, d), lambda i, j: (i, 0)),
            pl.BlockSpec((tm, d), lambda i, j: (i, 0)),
        ],
        out_shape=[jax.ShapeDtypeStruct((m_rows, d), F32), jax.ShapeDtypeStruct((m_rows, d), F32)],
        scratch_shapes=[pltpu.VMEM((tm, d), BF16), pltpu.VMEM((tm, d), F32)],
        compiler_params=pltpu.CompilerParams(
            dimension_semantics=("arbitrary", "arbitrary"), vmem_limit_bytes=VMEM_LIMIT_BYTES),
        name="ffn",
    )(x, g, wg, wu, wd, gnext)


TAB_ROWS = 10


def _s5_prep_kernel(lre_ref, lim_ref, ldt_ref, bre_ref, bim_ref, cre_ref, cim_ref,
                    wt_ref, we_ref, wst_ref, tab_ref):
    ns = S5_BLOCK_STATE
    lr = lre_ref[0]
    li = lim_ref[0]
    dt = jnp.exp(ldt_ref[0])
    mag = jnp.exp(lr * dt)
    ar = mag * jnp.cos(li * dt)
    ai = mag * jnp.sin(li * dt)
    den = lr * lr + li * li
    nr = ar - 1.0
    fr = (nr * lr + ai * li) / den
    fi = (ai * lr - nr * li) / den
    b_re = bre_ref[0]
    b_im = bim_ref[0]
    bbr = fr * b_re - fi * b_im
    bbi = fr * b_im + fi * b_re
    c_re = cre_ref[0]
    c_im = cim_ref[0]

    def cmul(pr, pi, qr, qi):
        return pr * qr - pi * qi, pr * qi + pi * qr

    powers = [(jnp.ones_like(ar), jnp.zeros_like(ar))]
    for _ in range(S5_L):
        powers.append(cmul(*powers[-1], ar, ai))

    dmats = []
    for l in range(S5_L):
        pr, pi = powers[l]
        bkr, bki = cmul(pr, pi, bbr, bbi)
        dk = (lax.dot_general(bkr, c_re, (((1,), (1,)), ((), ())), precision=lax.Precision.HIGHEST,
                              preferred_element_type=F32)
              - lax.dot_general(bki, c_im, (((1,), (1,)), ((), ())), precision=lax.Precision.HIGHEST,
                                preferred_element_type=F32))
        dmats.append(dk.astype(BF16))
        we_ref[0, (S5_L - 1 - l) * LANES:(S5_L - l) * LANES, 0:ns] = bkr.astype(BF16)
        we_ref[0, (S5_L - 1 - l) * LANES:(S5_L - l) * LANES, ns:2 * ns] = bki.astype(BF16)
        qr, qi = powers[l + 1]
        wst_ref[0, l * LANES:(l + 1) * LANES, 0:ns] = (c_re * qr - c_im * qi).astype(BF16)
        wst_ref[0, l * LANES:(l + 1) * LANES, ns:2 * ns] = (-c_re * qi - c_im * qr).astype(BF16)

    zero_blk = jnp.zeros((LANES, LANES), BF16)
    for l in range(S5_L):
        for l2 in range(S5_L):
            blk = dmats[l2 - l] if l2 >= l else zero_blk
            wt_ref[0, l * LANES:(l + 1) * LANES, l2 * LANES:(l2 + 1) * LANES] = blk

    row = lax.broadcasted_iota(jnp.int32, (SUBLANES, ns), 0)

    def bcast(a):
        return jnp.broadcast_to(a, (SUBLANES, ns))

    a1 = powers[S5_L]
    a2 = cmul(*a1, *a1)
    a4 = cmul(*a2, *a2)
    a8 = cmul(*a4, *a4)
    t = 0
    for shift, (pr, pi) in ((1, a1), (2, a2), (4, a4)):
        tab_ref[0, t] = jnp.where(row >= shift, bcast(pr), 0.0)
        tab_ref[0, t + 1] = jnp.where(row >= shift, bcast(pi), 0.0)
        t += 2
    pr_tab = jnp.zeros((SUBLANES, ns), F32)
    pi_tab = jnp.zeros((SUBLANES, ns), F32)
    cur = a1
    for r in range(SUBLANES):
        pr_tab = jnp.where(row == r, bcast(cur[0]), pr_tab)
        pi_tab = jnp.where(row == r, bcast(cur[1]), pi_tab)
        cur = cmul(*cur, *a1)
    tab_ref[0, 6] = pr_tab
    tab_ref[0, 7] = pi_tab
    tab_ref[0, 8] = bcast(a8[0])
    tab_ref[0, 9] = bcast(a8[1])


def _s5_prep(lam_re, lam_im, log_dt, b_re_emb, b_im_emb, c_re_emb, c_im_emb):
    nb = b_re_emb.shape[0]
    ns = S5_BLOCK_STATE
    k = S5_L * LANES
    vec = pl.BlockSpec((1, 1, ns), lambda j: (j, 0, 0))
    emb = pl.BlockSpec((1, LANES, ns), lambda j: (j, 0, 0))
    wspec = pl.BlockSpec((1, k, 2 * ns), lambda j: (j, 0, 0))
    return pl.pallas_call(
        _s5_prep_kernel,
        grid=(nb,),
        in_specs=[vec, vec, vec, emb, emb, emb, emb],
        out_specs=[
            pl.BlockSpec((1, k, k), lambda j: (j, 0, 0)),
            wspec,
            wspec,
            pl.BlockSpec((1, TAB_ROWS, SUBLANES, ns), lambda j: (j, 0, 0, 0)),
        ],
        out_shape=[
            jax.ShapeDtypeStruct((nb, k, k), BF16),
            jax.ShapeDtypeStruct((nb, k, 2 * ns), BF16),
            jax.ShapeDtypeStruct((nb, k, 2 * ns), BF16),
            jax.ShapeDtypeStruct((nb, TAB_ROWS, SUBLANES, ns), F32),
        ],
        compiler_params=pltpu.CompilerParams(
            dimension_semantics=("arbitrary",), vmem_limit_bytes=VMEM_LIMIT_BYTES),
        name="s5_prep",
    )(lam_re, lam_im, log_dt, b_re_emb, b_im_emb, c_re_emb, c_im_emb)


def _s5_core_kernel(n_prompt_groups, n_streams,
                    h_ref, wt_ref, we_ref, wst_ref, tab_ref, s0r_ref, s0i_ref, d_ref,
                    y_ref, sfr_ref, sfi_ref,
                    u2f_ref, u2_ref, e_ref):
    ns = S5_BLOCK_STATE
    n_rows = h_ref.shape[0] // S5_L

    for l in range(S5_L):
        u2f_ref[:, l * LANES:(l + 1) * LANES] = h_ref[pl.ds(l, n_rows, stride=S5_L), :]
    u2_ref[...] = u2f_ref[...].astype(BF16)
    e_ref[...] = jnp.dot(u2_ref[...], we_ref[0], preferred_element_type=F32)

    m1r, m1i, m2r, m2i, m4r, m4i, pr, pi, a8r, a8i = [tab_ref[0, t] for t in range(TAB_ROWS)]
    row = lax.broadcasted_iota(jnp.int32, (SUBLANES, ns), 0)

    def group_step(g8, cr, ci):
        xr = e_ref[pl.ds(g8, SUBLANES), 0:ns]
        xi = e_ref[pl.ds(g8, SUBLANES), ns:2 * ns]
        for shift, tr, ti in ((1, m1r, m1i), (2, m2r, m2i), (4, m4r, m4i)):
            sr = pltpu.roll(xr, shift, axis=0)
            si = pltpu.roll(xi, shift, axis=0)
            xr, xi = xr + tr * sr - ti * si, xi + tr * si + ti * sr
        last_r = jnp.broadcast_to(xr[SUBLANES - 1:SUBLANES], (SUBLANES, ns))
        last_i = jnp.broadcast_to(xi[SUBLANES - 1:SUBLANES], (SUBLANES, ns))
        st_r = xr + pr * cr - pi * ci
        st_i = xi + pr * ci + pi * cr
        e_ref[pl.ds(g8, SUBLANES), 0:ns] = jnp.where(row == 0, cr, pltpu.roll(st_r, 1, axis=0))
        e_ref[pl.ds(g8, SUBLANES), ns:2 * ns] = jnp.where(row == 0, ci, pltpu.roll(st_i, 1, axis=0))
        return a8r * cr - a8i * ci + last_r, a8r * ci + a8i * cr + last_i

    def prompt_body(g, carry):
        return group_step(pl.multiple_of(g * SUBLANES, SUBLANES), *carry)

    zero = jnp.zeros((SUBLANES, ns), F32)
    fr, fi = lax.fori_loop(0, n_prompt_groups, prompt_body, (zero, zero))
    sfr_ref[...] = jnp.zeros_like(sfr_ref)
    sfi_ref[...] = jnp.zeros_like(sfi_ref)
    sfr_ref[n_streams:n_streams + 1, :] = fr[0:1]
    sfi_ref[n_streams:n_streams + 1, :] = fi[0:1]
    for b in range(n_streams):
        cr = jnp.broadcast_to(s0r_ref[b:b + 1, :], (SUBLANES, ns))
        ci = jnp.broadcast_to(s0i_ref[b:b + 1, :], (SUBLANES, ns))
        fr, fi = group_step((n_prompt_groups + b) * SUBLANES, cr, ci)
        sfr_ref[b:b + 1, :] = fr[0:1]
        sfi_ref[b:b + 1, :] = fi[0:1]

    y2 = jnp.dot(u2_ref[...], wt_ref[0], preferred_element_type=F32)
    y2 = y2 + lax.dot_general(e_ref[...].astype(BF16), wst_ref[0], (((1,), (1,)), ((), ())),
                              preferred_element_type=F32)
    u2f_ref[...] = y2 + jnp.concatenate([d_ref[...]] * S5_L, axis=1) * u2f_ref[...]
    for l in range(S5_L):
        y_ref[pl.ds(l, n_rows, stride=S5_L), :] = u2f_ref[:, l * LANES:(l + 1) * LANES]


def _s5_core(h, wt, we, wst, tab, s0_re, s0_im, d_skip, *, n_prompt_rows):
    m_rows, d = h.shape
    nb = d // LANES
    ns = S5_BLOCK_STATE
    k = S5_L * LANES
    n_rows = m_rows // S5_L
    group_frames = S5_L * SUBLANES
    n_prompt_groups = n_prompt_rows // group_frames
    n_streams = s0_re.shape[0]
    assert (m_rows - n_prompt_rows) == n_streams * group_frames
    sf_rows = n_streams + SUBLANES
    lane_blk = pl.BlockSpec((m_rows, LANES), lambda j: (0, j))
    wspec = pl.BlockSpec((1, k, 2 * ns), lambda j: (j, 0, 0))
    st = pl.BlockSpec((n_streams, ns), lambda j: (0, j))
    sf = pl.BlockSpec((sf_rows, ns), lambda j: (0, j))
    return pl.pallas_call(
        functools.partial(_s5_core_kernel, n_prompt_groups, n_streams),
        grid=(nb,),
        in_specs=[
            lane_blk,
            pl.BlockSpec((1, k, k), lambda j: (j, 0, 0)),
            wspec,
            wspec,
            pl.BlockSpec((1, TAB_ROWS, SUBLANES, ns), lambda j: (j, 0, 0, 0)),
            st, st,
            pl.BlockSpec((1, LANES), lambda j: (0, j)),
        ],
        out_specs=[lane_blk, sf, sf],
        out_shape=[
            jax.ShapeDtypeStruct((m_rows, d), F32),
            jax.ShapeDtypeStruct((sf_rows, nb * ns), F32),
            jax.ShapeDtypeStruct((sf_rows, nb * ns), F32),
        ],
        scratch_shapes=[
            pltpu.VMEM((n_rows, k), F32),
            pltpu.VMEM((n_rows, k), BF16),
            pltpu.VMEM((n_rows, 2 * ns), F32),
        ],
        compiler_params=pltpu.CompilerParams(
            dimension_semantics=("arbitrary",), vmem_limit_bytes=VMEM_LIMIT_BYTES),
        name="s5_core",
    )(h, wt, we, wst, tab, s0_re, s0_im, d_skip)


def _glu_kernel(n_col_blocks, x_ref, y_ref, w_ref, out_ref):
    d = x_ref.shape[1]
    a = _gelu(y_ref[...]).astype(BF16)
    cb = d // n_col_blocks
    for n in range(n_col_blocks):
        cols = slice(n * cb, (n + 1) * cb)
        val = jnp.dot(a, w_ref[:, n * cb:(n + 1) * cb], preferred_element_type=F32)
        gate = jnp.dot(a, w_ref[:, d + n * cb:d + (n + 1) * cb], preferred_element_type=F32)
        out_ref[:, cols] = x_ref[:, cols] + val * jax.nn.sigmoid(gate)


def _glu(x, y, w, *, tm, n_col_blocks=4):
    m_rows, d = x.shape
    assert m_rows % tm == 0 and d % n_col_blocks == 0
    tile = pl.BlockSpec((tm, d), lambda i: (i, 0))
    return pl.pallas_call(
        functools.partial(_glu_kernel, n_col_blocks),
        grid=(m_rows // tm,),
        in_specs=[tile, tile, _const_spec(w.shape)],
        out_specs=tile,
        out_shape=jax.ShapeDtypeStruct((m_rows, d), F32),
        compiler_params=pltpu.CompilerParams(
            dimension_semantics=("arbitrary",), vmem_limit_bytes=VMEM_LIMIT_BYTES),
        name="glu",
    )(x, y, w)


def _attention_tables(sinks):
    slopes = jnp.exp2(-8.0 * jnp.arange(1, N_HEADS + 1, dtype=F32) / N_HEADS)
    frame = jnp.arange(CHUNK, dtype=F32)[:, None]
    band = jnp.arange(BAND, dtype=F32)[None, :]
    dist = jnp.abs(frame - (band - WINDOW))
    bias = (slopes[:, None, None] * dist[None]).reshape(N_HEADS * CHUNK, BAND)
    sink_rows = jnp.repeat(sinks.astype(F32), CHUNK).reshape(N_HEADS * CHUNK, 1)
    return bias, sink_rows


def _gm_tables(gm_ws, gm_b, d_gm):
    blk = jnp.arange(GM_CHUNK) // CHUNK
    w_prompt = jnp.where((blk[:, None] >= blk[None, :])[None], gm_ws, 0.0)
    top = gm_ws[:, :CHUNK, :CHUNK]
    zeros = jnp.zeros_like(top)
    w_sample = jnp.concatenate(
        [jnp.concatenate([top, zeros], axis=2), jnp.concatenate([zeros, top], axis=2)], axis=1)
    wsp = jnp.stack([w_prompt, w_sample]).astype(BF16)
    b_prompt = gm_b.T
    b_sample = jnp.concatenate([gm_b[:, :CHUNK].T, gm_b[:, :CHUNK].T], axis=0)
    gmb = jnp.stack([b_prompt, b_sample]).astype(F32)
    gmb = jnp.repeat(gmb, d_gm // N_GM_GROUPS, axis=2)
    return wsp, gmb


def _s5_embed(b_re, b_im, c_re, c_im):
    n_groups = b_re.shape[0]
    nb = n_groups // S5_LANE_GROUPS
    eye = jnp.eye(S5_LANE_GROUPS, dtype=F32)

    def emb_b(b):
        b = b.astype(F32).reshape(nb, S5_LANE_GROUPS, S5_STATE, S5_GROUP)
        return jnp.einsum('jgpi,gh->jgihp', b, eye).reshape(nb, LANES, S5_BLOCK_STATE)

    def emb_c(c):
        c = c.astype(F32).reshape(nb, S5_LANE_GROUPS, S5_GROUP, S5_STATE)
        return jnp.einsum('jgip,gh->jgihp', c, eye).reshape(nb, LANES, S5_BLOCK_STATE)

    return emb_b(b_re), emb_b(b_im), emb_c(c_re), emb_c(c_im)


def kernel(x_prompt, x_sample, cache_swa_k, cache_swa_v, state_s5_re, state_s5_im, norm_mix, norm_ffn, norm_final, w_in0, attn_sinks, gm_norm, gm_ws, gm_b, w_out0, s5_lam_re, s5_lam_im, s5_log_dt, s5_b_re, s5_b_im, s5_c_re, s5_c_im, s5_d, s5_w_glu, ffn_w_gate, ffn_w_up, ffn_w_down):
    batch, seq, d = x_prompt.shape
    dec_batch, dec_seq, _ = x_sample.shape
    assert batch == 1 and dec_seq == CHUNK and norm_mix.shape[0] == 2
    n_prompt = batch * seq
    n_sample = dec_batch * dec_seq
    d_gm = gm_norm.shape[-1]

    x = jnp.concatenate([x_prompt.reshape(n_prompt, d), x_sample.reshape(n_sample, d)], axis=0)

    bias_tbl, sink_rows = _attention_tables(attn_sinks[0])
    wsp, gmb = _gm_tables(gm_ws[0], gm_b[0], d_gm)
    x1, k_all, v_all, gvn = _mix0(
        x, cache_swa_k[0].reshape(dec_batch * WINDOW, D_KV), cache_swa_v[0].reshape(dec_batch * WINDOW, D_KV),
        norm_mix[0].reshape(1, d), w_in0[0].astype(BF16), bias_tbl, sink_rows,
        gm_norm[0].reshape(1, d_gm), wsp, gmb, w_out0[0].astype(BF16),
        n_prompt_rows=n_prompt, units=2)

    x2, h2 = _ffn(x1, norm_ffn[0].reshape(1, d), ffn_w_gate[0].astype(BF16), ffn_w_up[0].astype(BF16),
                  ffn_w_down[0].astype(BF16), norm_mix[1].reshape(1, d), tm=512, tf=512)

    n_groups = s5_lam_re.shape[1]
    nb = n_groups // S5_LANE_GROUPS
    bre, bim, cre, cim = _s5_embed(s5_b_re[0], s5_b_im[0], s5_c_re[0], s5_c_im[0])
    wt, we, wst, tab = _s5_prep(
        s5_lam_re[0].reshape(nb, 1, S5_BLOCK_STATE), s5_lam_im[0].reshape(nb, 1, S5_BLOCK_STATE),
        jnp.repeat(s5_log_dt[0], S5_STATE).reshape(nb, 1, S5_BLOCK_STATE), bre, bim, cre, cim)
    ys5, sf_re, sf_im = _s5_core(
        h2, wt, we, wst, tab, state_s5_re[0].reshape(dec_batch, n_groups * S5_STATE),
        state_s5_im[0].reshape(dec_batch, n_groups * S5_STATE), s5_d[0].reshape(1, d),
        n_prompt_rows=n_prompt)
    x3 = _glu(x2, ys5, s5_w_glu[0].astype(BF16), tm=256)

    _, y = _ffn(x3, norm_ffn[1].reshape(1, d), ffn_w_gate[1].astype(BF16), ffn_w_up[1].astype(BF16),
                ffn_w_down[1].astype(BF16), norm_final.reshape(1, d), tm=512, tf=512)

    keep = min(WINDOW, seq)
    y_prompt = y[:n_prompt].reshape(batch, seq, d)
    y_sample = y[n_prompt:].reshape(dec_batch, dec_seq, d)
    kv_shape_p = (1, batch, keep, N_KV_HEADS, HEAD_DIM)
    kv_shape_s = (1, dec_batch, dec_seq, N_KV_HEADS, HEAD_DIM)
    st_p = (1, batch, n_groups, S5_STATE)
    st_s = (1, dec_batch, n_groups, S5_STATE)
    return (y_prompt, y_sample,
            k_all[n_prompt - keep:n_prompt].reshape(kv_shape_p),
            v_all[n_prompt - keep:n_prompt].reshape(kv_shape_p),
            k_all[n_prompt:].reshape(kv_shape_s),
            v_all[n_prompt:].reshape(kv_shape_s),
            gvn.reshape(1, dec_batch, dec_seq, d_gm),
            sf_re[dec_batch].reshape(st_p), sf_im[dec_batch].reshape(st_p),
            sf_re[:dec_batch].reshape(st_s), sf_im[:dec_batch].reshape(st_s))
```

```python
import functools
import math

import jax
import jax.numpy as jnp
from jax import lax
from jax.experimental import pallas as pl
from jax.experimental.pallas import tpu as pltpu

F32 = jnp.float32
BF16 = jnp.bfloat16

CHUNK = 64
HEAD_DIM = 64
N_HEADS = 16
N_KV_HEADS = 2
Q_PER_KV = N_HEADS // N_KV_HEADS
WINDOW = 128
BAND = WINDOW + CHUNK
D_ATTN = N_HEADS * HEAD_DIM
D_KV = N_KV_HEADS * HEAD_DIM
GM_CHUNK = 128
N_GM_GROUPS = 16
S5_GROUP = 16
S5_STATE = 64
RMS_EPS = 1e-5
NEG_INF = -1e30

LANES = 128
SUBLANES = 8
VMEM_LIMIT_BYTES = 56 * 1024 * 1024

S5_L = SUBLANES
S5_LANE_GROUPS = LANES // S5_GROUP
S5_BLOCK_STATE = S5_LANE_GROUPS * S5_STATE


def _gelu(x):
    return 0.5 * x * (1.0 + lax.erf(x * math.sqrt(0.5)))


def _rms_scale(x):
    return x * lax.rsqrt(jnp.mean(x * x, axis=-1, keepdims=True) + RMS_EPS)


def _const_spec(shape):
    zeros = (0,) * len(shape)
    return pl.BlockSpec(shape, lambda *_: zeros, pipeline_mode=pl.Buffered(1))


def _mix0_kernel(n_prompt_tiles, units,
                 xp_ref, xs_ref, ck_ref, cv_ref, g_ref, win_ref, bias_ref, sink_ref, gmn_ref,
                 wsp_ref, gmb_ref, wout_ref,
                 x1_ref, k_ref, v_ref, gvn_ref,
                 z_ref, q_ref, ocat_ref, kprev_ref, vprev_ref):
    i = pl.program_id(0)
    is_sample = i >= n_prompt_tiles
    tm = units * GM_CHUNK
    d_gm = gmn_ref.shape[-1]
    off_k = D_ATTN
    off_v = D_ATTN + D_KV
    off_gu = D_ATTN + 2 * D_KV
    off_gv = off_gu + d_gm
    n_q = Q_PER_KV * CHUNK

    @pl.when(i == 0)
    def _():
        kprev_ref[...] = jnp.zeros_like(kprev_ref)
        vprev_ref[...] = jnp.zeros_like(vprev_ref)

    x = jnp.where(is_sample, xs_ref[...], xp_ref[...])
    h = (_rms_scale(x) * g_ref[...]).astype(BF16)
    z_ref[...] = jnp.dot(h, win_ref[...], preferred_element_type=F32)

    q_ref[...] = (z_ref[:, 0:D_ATTN] * (HEAD_DIM ** -0.5)).astype(BF16)
    k = z_ref[:, off_k:off_k + D_KV]
    v = z_ref[:, off_v:off_v + D_KV]
    k_ref[...] = k
    v_ref[...] = v

    def lane_lo(rows):
        return lax.broadcasted_iota(jnp.int32, (rows, LANES), 1) < HEAD_DIM

    def replicate(a):
        r = pltpu.roll(a, HEAD_DIM, axis=1)
        lo = lane_lo(a.shape[0])
        return jnp.where(lo, a, r).astype(BF16), jnp.where(lo, r, a).astype(BF16)

    k_rep = replicate(k)
    v_rep = replicate(v)
    ck_rep = replicate(ck_ref[...])
    cv_rep = replicate(cv_ref[...])

    lo64 = lane_lo(CHUNK)
    band_pos = lax.broadcasted_iota(jnp.int32, (BAND, n_q), 0)
    chunks_per_tile = tm // CHUNK

    for u in range(units):
        for c2 in range(GM_CHUNK // CHUNK):
            r0 = u * GM_CHUNK + c2 * CHUNK
            stream = r0 // CHUNK
            chunk_index = i * chunks_per_tile + stream
            valid_from = jnp.where(is_sample, 0, jnp.maximum(WINDOW - CHUNK * chunk_index, 0))
            valid = band_pos >= valid_from
            for kv in range(N_KV_HEADS):
                def band(cur, prev_ref, cached):
                    if u == 0:
                        prev_unit = prev_ref[kv]
                    else:
                        prev_unit = cur[(u - 1) * GM_CHUNK:u * GM_CHUNK]
                    if c2 == 0:
                        prompt_prev = prev_unit
                    else:
                        prompt_prev = jnp.concatenate(
                            [prev_unit[CHUNK:], cur[u * GM_CHUNK:u * GM_CHUNK + CHUNK]], axis=0)
                    sample_prev = cached[stream * WINDOW:(stream + 1) * WINDOW]
                    prev = jnp.where(is_sample, sample_prev, prompt_prev)
                    return jnp.concatenate([prev, cur[r0:r0 + CHUNK]], axis=0)

                kb = band(k_rep[kv], kprev_ref, ck_rep[kv])
                vb = band(v_rep[kv], vprev_ref, cv_rep[kv])

                pieces = []
                for m in range(Q_PER_KV // 2):
                    c0 = kv * Q_PER_KV * HEAD_DIM + m * LANES
                    qp = q_ref[r0:r0 + CHUNK, c0:c0 + LANES]
                    pieces.append(jnp.where(lo64, qp, jnp.zeros_like(qp)))
                    pieces.append(jnp.where(lo64, jnp.zeros_like(qp), qp))
                qs = jnp.concatenate(pieces, axis=0)
                cols = slice(kv * n_q, (kv + 1) * n_q)
                st = lax.dot_general(kb, qs, (((1,), (1,)), ((), ())),
                                     preferred_element_type=F32)
                st = jnp.where(valid, st - bias_ref[:, cols], NEG_INF)
                sink = sink_ref[:, cols]
                mx = jnp.maximum(jnp.max(st, axis=0, keepdims=True), sink)
                p = jnp.exp(st - mx)
                denom = jnp.sum(p, axis=0, keepdims=True) + jnp.exp(sink - mx)
                pn = (p * (1.0 / denom)).astype(BF16)
                o = lax.dot_general(pn, vb, (((0,), (0,)), ((), ())),
                                    preferred_element_type=F32)
                for m in range(Q_PER_KV // 2):
                    o_pair = jnp.where(lo64, o[(2 * m) * CHUNK:(2 * m + 1) * CHUNK],
                                       o[(2 * m + 1) * CHUNK:(2 * m + 2) * CHUNK])
                    c0 = kv * Q_PER_KV * HEAD_DIM + m * LANES
                    ocat_ref[r0:r0 + CHUNK, c0:c0 + LANES] = o_pair.astype(BF16)

    lo128 = lane_lo(GM_CHUNK)
    for u in range(units):
        rows = slice(u * GM_CHUNK, (u + 1) * GM_CHUNK)
        ua = _gelu(z_ref[rows, off_gu:off_gu + d_gm])
        gvn = _rms_scale(_gelu(z_ref[rows, off_gv:off_gv + d_gm])) * gmn_ref[...]
        gvn_ref[rows, :] = gvn
        gb = gvn.astype(BF16)
        for m in range(N_GM_GROUPS // 2):
            cols = slice(m * LANES, (m + 1) * LANES)
            rhs = gb[:, cols]
            rhs2 = jnp.concatenate([jnp.where(lo128, rhs, jnp.zeros_like(rhs)),
                                    jnp.where(lo128, jnp.zeros_like(rhs), rhs)], axis=0)
            sp = jnp.dot(wsp_ref[0, m], rhs2, preferred_element_type=F32) + gmb_ref[0, :, cols]
            ocat_ref[rows, D_ATTN + m * LANES:D_ATTN + (m + 1) * LANES] = (ua[:, cols] * sp).astype(BF16)

    x1_ref[...] = x + jnp.dot(ocat_ref[...], wout_ref[...], preferred_element_type=F32)

    for kv in range(N_KV_HEADS):
        kprev_ref[kv] = k_rep[kv][tm - GM_CHUNK:]
        vprev_ref[kv] = v_rep[kv][tm - GM_CHUNK:]


def _mix0(xp, xs, cache_k, cache_v, g, w_in, bias_tbl, sink_row, gm_norm, wsp, gmb, w_out, *, units):
    n_prompt_rows, d = xp.shape
    n_sample_rows = xs.shape[0]
    m_rows = n_prompt_rows + n_sample_rows
    tm = units * GM_CHUNK
    assert n_prompt_rows % tm == 0 and n_sample_rows % tm == 0
    n_tiles = m_rows // tm
    n_prompt_tiles = n_prompt_rows // tm
    d_in = w_in.shape[1]
    d_gm = gm_norm.shape[-1]
    cache_rows = (tm // CHUNK) * WINDOW

    def prompt_block(i):
        return jnp.minimum(i, n_prompt_tiles - 1)

    def sample_block(i):
        return jnp.maximum(i - n_prompt_tiles, 0)

    def kind(i):
        return jnp.where(i >= n_prompt_tiles, 1, 0)

    in_specs = [
        pl.BlockSpec((tm, d), lambda i: (prompt_block(i), 0)),
        pl.BlockSpec((tm, d), lambda i: (sample_block(i), 0)),
        pl.BlockSpec((cache_rows, D_KV), lambda i: (sample_block(i), 0)),
        pl.BlockSpec((cache_rows, D_KV), lambda i: (sample_block(i), 0)),
        _const_spec((1, d)),
        _const_spec((d, d_in)),
        _const_spec(bias_tbl.shape),
        _const_spec(sink_row.shape),
        _const_spec((1, d_gm)),
        pl.BlockSpec((1,) + wsp.shape[1:], lambda i: (kind(i), 0, 0, 0)),
        pl.BlockSpec((1, GM_CHUNK, d_gm), lambda i: (kind(i), 0, 0)),
        _const_spec(w_out.shape),
    ]
    out_specs = [
        pl.BlockSpec((tm, d), lambda i: (i, 0)),
        pl.BlockSpec((tm, D_KV), lambda i: (i, 0)),
        pl.BlockSpec((tm, D_KV), lambda i: (i, 0)),
        pl.BlockSpec((tm, d_gm), lambda i: (sample_block(i), 0)),
    ]
    out_shape = [
        jax.ShapeDtypeStruct((m_rows, d), F32),
        jax.ShapeDtypeStruct((m_rows, D_KV), F32),
        jax.ShapeDtypeStruct((m_rows, D_KV), F32),
        jax.ShapeDtypeStruct((n_sample_rows, d_gm), F32),
    ]
    scratch = [
        pltpu.VMEM((tm, d_in), F32),
        pltpu.VMEM((tm, D_ATTN), BF16),
        pltpu.VMEM((tm, D_ATTN + d_gm), BF16),
        pltpu.VMEM((N_KV_HEADS, GM_CHUNK, LANES), BF16),
        pltpu.VMEM((N_KV_HEADS, GM_CHUNK, LANES), BF16),
    ]
    return pl.pallas_call(
        functools.partial(_mix0_kernel, n_prompt_tiles, units),
        grid=(n_tiles,),
        in_specs=in_specs,
        out_specs=out_specs,
        out_shape=out_shape,
        scratch_shapes=scratch,
        compiler_params=pltpu.CompilerParams(
            dimension_semantics=("arbitrary",), vmem_limit_bytes=VMEM_LIMIT_BYTES),
        name="mix0",
    )(xp, xs, cache_k, cache_v, g, w_in, bias_tbl, sink_row, gm_norm, wsp, gmb, w_out)


def _ffn_kernel(n_prompt_tiles, x_ref, g_ref, wg_ref, wu_ref, wd_ref, gnext_ref, *rest):
    out_refs, (h_ref, acc_ref) = rest[:2], rest[2:]
    i = pl.program_id(0)
    j = pl.program_id(1)

    @pl.when(j == 0)
    def _():
        h_ref[...] = (_rms_scale(x_ref[...]) * g_ref[...]).astype(BF16)
        acc_ref[...] = jnp.zeros_like(acc_ref)

    h = h_ref[...]
    gate = jnp.dot(h, wg_ref[...], preferred_element_type=F32)
    up = jnp.dot(h, wu_ref[...], preferred_element_type=F32)
    act = (gate * jax.nn.sigmoid(gate) * up).astype(BF16)
    acc_ref[...] += jnp.dot(act, wd_ref[...], preferred_element_type=F32)

    last = j == pl.num_programs(1) - 1
    if n_prompt_tiles is None:
        @pl.when(last)
        def _():
            out = x_ref[...] + acc_ref[...]
            out_refs[0][...] = out
            out_refs[1][...] = _rms_scale(out) * gnext_ref[...]
    else:
        @pl.when(jnp.logical_and(last, i < n_prompt_tiles))
        def _():
            out_refs[0][...] = _rms_scale(x_ref[...] + acc_ref[...]) * gnext_ref[...]

        @pl.when(jnp.logical_and(last, i >= n_prompt_tiles))
        def _():
            out_refs[1][...] = _rms_scale(x_ref[...] + acc_ref[...]) * gnext_ref[...]


def _ffn(x, g, wg, wu, wd, gnext, *, tm, tf, n_prompt_rows=None):
    m_rows, d = x.shape
    f = wg.shape[1]
    assert m_rows % tm == 0 and f % tf == 0
    if n_prompt_rows is None:
        n_prompt_tiles = None
        out_specs = [pl.BlockSpec((tm, d), lambda i, j: (i, 0))] * 2
        out_shape = [jax.ShapeDtypeStruct((m_rows, d), F32)] * 2
    else:
        assert n_prompt_rows % tm == 0
        n_prompt_tiles = n_prompt_rows // tm
        out_specs = [
            pl.BlockSpec((tm, d), lambda i, j: (jnp.minimum(i, n_prompt_tiles - 1), 0)),
            pl.BlockSpec((tm, d), lambda i, j: (jnp.maximum(i - n_prompt_tiles, 0), 0)),
        ]
        out_shape = [jax.ShapeDtypeStruct((n_prompt_rows, d), F32),
                     jax.ShapeDtypeStruct((m_rows - n_prompt_rows, d), F32)]
    return pl.pallas_call(
        functools.partial(_ffn_kernel, n_prompt_tiles),
        grid=(m_rows // tm, f // tf),
        in_specs=[
            pl.BlockSpec((tm, d), lambda i, j: (i, 0)),
            pl.BlockSpec((1, d), lambda i, j: (0, 0)),
            pl.BlockSpec((d, tf), lambda i, j: (0, j)),
            pl.BlockSpec((d, tf), lambda i, j: (0, j)),
            pl.BlockSpec((tf, d), lambda i, j: (j, 0)),
            pl.BlockSpec((1, d), lambda i, j: (0, 0)),
        ],
        out_specs=out_specs,
        out_shape=out_shape,
        scratch_shapes=[pltpu.VMEM((tm, d), BF16), pltpu.VMEM((tm, d), F32)],
        compiler_params=pltpu.CompilerParams(
            dimension_semantics=("arbitrary", "arbitrary"), vmem_limit_bytes=VMEM_LIMIT_BYTES),
        name="ffn",
    )(x, g, wg, wu, wd, gnext)


TAB_ROWS = 10


def _s5_prep_kernel(lre_ref, lim_ref, ldt_ref, bre_ref, bim_ref, cre_ref, cim_ref,
                    wt_ref, we_ref, wst_ref, tab_ref):
    ns = S5_BLOCK_STATE
    lr = lre_ref[0]
    li = lim_ref[0]
    dt = jnp.exp(ldt_ref[0])
    mag = jnp.exp(lr * dt)
    ar = mag * jnp.cos(li * dt)
    ai = mag * jnp.sin(li * dt)
    den = lr * lr + li * li
    nr = ar - 1.0
    fr = (nr * lr + ai * li) / den
    fi = (ai * lr - nr * li) / den
    b_re = bre_ref[0]
    b_im = bim_ref[0]
    bbr = fr * b_re - fi * b_im
    bbi = fr * b_im + fi * b_re
    c_re = cre_ref[0]
    c_im = cim_ref[0]

    def cmul(pr, pi, qr, qi):
        return pr * qr - pi * qi, pr * qi + pi * qr

    powers = [(jnp.ones_like(ar), jnp.zeros_like(ar))]
    for _ in range(S5_L):
        powers.append(cmul(*powers[-1], ar, ai))

    dmats = []
    for l in range(S5_L):
        pr, pi = powers[l]
        bkr, bki = cmul(pr, pi, bbr, bbi)
        dk = (lax.dot_general(bkr, c_re, (((1,), (1,)), ((), ())), precision=lax.Precision.HIGHEST,
                              preferred_element_type=F32)
              - lax.dot_general(bki, c_im, (((1,), (1,)), ((), ())), precision=lax.Precision.HIGHEST,
                                preferred_element_type=F32))
        dmats.append(dk.astype(BF16))
        we_ref[0, (S5_L - 1 - l) * LANES:(S5_L - l) * LANES, 0:ns] = bkr.astype(BF16)
        we_ref[0, (S5_L - 1 - l) * LANES:(S5_L - l) * LANES, ns:2 * ns] = bki.astype(BF16)
        qr, qi = powers[l + 1]
        wst_ref[0, l * LANES:(l + 1) * LANES, 0:ns] = (c_re * qr - c_im * qi).astype(BF16)
        wst_ref[0, l * LANES:(l + 1) * LANES, ns:2 * ns] = (-c_re * qi - c_im * qr).astype(BF16)

    zero_blk = jnp.zeros((LANES, LANES), BF16)
    for l in range(S5_L):
        for l2 in range(S5_L):
            blk = dmats[l2 - l] if l2 >= l else zero_blk
            wt_ref[0, l * LANES:(l + 1) * LANES, l2 * LANES:(l2 + 1) * LANES] = blk

    row = lax.broadcasted_iota(jnp.int32, (SUBLANES, ns), 0)

    def bcast(a):
        return jnp.broadcast_to(a, (SUBLANES, ns))

    a1 = powers[S5_L]
    a2 = cmul(*a1, *a1)
    a4 = cmul(*a2, *a2)
    a8 = cmul(*a4, *a4)
    t = 0
    for shift, (pr, pi) in ((1, a1), (2, a2), (4, a4)):
        tab_ref[0, t] = jnp.where(row >= shift, bcast(pr), 0.0)
        tab_ref[0, t + 1] = jnp.where(row >= shift, bcast(pi), 0.0)
        t += 2
    pr_tab = jnp.zeros((SUBLANES, ns), F32)
    pi_tab = jnp.zeros((SUBLANES, ns), F32)
    cur = a1
    for r in range(SUBLANES):
        pr_tab = jnp.where(row == r, bcast(cur[0]), pr_tab)
        pi_tab = jnp.where(row == r, bcast(cur[1]), pi_tab)
        cur = cmul(*cur, *a1)
    tab_ref[0, 6] = pr_tab
    tab_ref[0, 7] = pi_tab
    tab_ref[0, 8] = bcast(a8[0])
    tab_ref[0, 9] = bcast(a8[1])


def _s5_prep(lam_re, lam_im, log_dt, b_re_emb, b_im_emb, c_re_emb, c_im_emb):
    nb = b_re_emb.shape[0]
    ns = S5_BLOCK_STATE
    k = S5_L * LANES
    vec = pl.BlockSpec((1, 1, ns), lambda j: (j, 0, 0))
    emb = pl.BlockSpec((1, LANES, ns), lambda j: (j, 0, 0))
    wspec = pl.BlockSpec((1, k, 2 * ns), lambda j: (j, 0, 0))
    return pl.pallas_call(
        _s5_prep_kernel,
        grid=(nb,),
        in_specs=[vec, vec, vec, emb, emb, emb, emb],
        out_specs=[
            pl.BlockSpec((1, k, k), lambda j: (j, 0, 0)),
            wspec,
            wspec,
            pl.BlockSpec((1, TAB_ROWS, SUBLANES, ns), lambda j: (j, 0, 0, 0)),
        ],
        out_shape=[
            jax.ShapeDtypeStruct((nb, k, k), BF16),
            jax.ShapeDtypeStruct((nb, k, 2 * ns), BF16),
            jax.ShapeDtypeStruct((nb, k, 2 * ns), BF16),
            jax.ShapeDtypeStruct((nb, TAB_ROWS, SUBLANES, ns), F32),
        ],
        compiler_params=pltpu.CompilerParams(
            dimension_semantics=("arbitrary",), vmem_limit_bytes=VMEM_LIMIT_BYTES),
        name="s5_prep",
    )(lam_re, lam_im, log_dt, b_re_emb, b_im_emb, c_re_emb, c_im_emb)


def _s5_core_kernel(n_prompt_groups, n_streams,
                    h_ref, wt_ref, we_ref, wst_ref, tab_ref, s0r_ref, s0i_ref, d_ref,
                    y_ref, sfr_ref, sfi_ref,
                    u2f_ref, u2_ref, e_ref):
    ns = S5_BLOCK_STATE
    n_rows = h_ref.shape[0] // S5_L

    for l in range(S5_L):
        u2f_ref[:, l * LANES:(l + 1) * LANES] = h_ref[pl.ds(l, n_rows, stride=S5_L), :]
    u2_ref[...] = u2f_ref[...].astype(BF16)
    e_ref[...] = jnp.dot(u2_ref[...], we_ref[0], preferred_element_type=F32)

    m1r, m1i, m2r, m2i, m4r, m4i, pr, pi, a8r, a8i = [tab_ref[0, t] for t in range(TAB_ROWS)]
    row = lax.broadcasted_iota(jnp.int32, (SUBLANES, ns), 0)

    def group_step(g8, cr, ci):
        xr = e_ref[pl.ds(g8, SUBLANES), 0:ns]
        xi = e_ref[pl.ds(g8, SUBLANES), ns:2 * ns]
        for shift, tr, ti in ((1, m1r, m1i), (2, m2r, m2i), (4, m4r, m4i)):
            sr = pltpu.roll(xr, shift, axis=0)
            si = pltpu.roll(xi, shift, axis=0)
            xr, xi = xr + tr * sr - ti * si, xi + tr * si + ti * sr
        last_r = jnp.broadcast_to(xr[SUBLANES - 1:SUBLANES], (SUBLANES, ns))
        last_i = jnp.broadcast_to(xi[SUBLANES - 1:SUBLANES], (SUBLANES, ns))
        st_r = xr + pr * cr - pi * ci
        st_i = xi + pr * ci + pi * cr
        e_ref[pl.ds(g8, SUBLANES), 0:ns] = jnp.where(row == 0, cr, pltpu.roll(st_r, 1, axis=0))
        e_ref[pl.ds(g8, SUBLANES), ns:2 * ns] = jnp.where(row == 0, ci, pltpu.roll(st_i, 1, axis=0))
        return a8r * cr - a8i * ci + last_r, a8r * ci + a8i * cr + last_i

    def prompt_body(g, carry):
        return group_step(pl.multiple_of(g * SUBLANES, SUBLANES), *carry)

    zero = jnp.zeros((SUBLANES, ns), F32)
    fr, fi = lax.fori_loop(0, n_prompt_groups, prompt_body, (zero, zero))
    sfr_ref[...] = jnp.zeros_like(sfr_ref)
    sfi_ref[...] = jnp.zeros_like(sfi_ref)
    sfr_ref[n_streams:n_streams + 1, :] = fr[0:1]
    sfi_ref[n_streams:n_streams + 1, :] = fi[0:1]
    for b in range(n_streams):
        cr = jnp.broadcast_to(s0r_ref[b:b + 1, :], (SUBLANES, ns))
        ci = jnp.broadcast_to(s0i_ref[b:b + 1, :], (SUBLANES, ns))
        fr, fi = group_step((n_prompt_groups + b) * SUBLANES, cr, ci)
        sfr_ref[b:b + 1, :] = fr[0:1]
        sfi_ref[b:b + 1, :] = fi[0:1]

    y2 = jnp.dot(u2_ref[...], wt_ref[0], preferred_element_type=F32)
    y2 = y2 + lax.dot_general(e_ref[...].astype(BF16), wst_ref[0], (((1,), (1,)), ((), ())),
                              preferred_element_type=F32)
    u2f_ref[...] = y2 + jnp.concatenate([d_ref[...]] * S5_L, axis=1) * u2f_ref[...]
    for l in range(S5_L):
        y_ref[pl.ds(l, n_rows, stride=S5_L), :] = u2f_ref[:, l * LANES:(l + 1) * LANES]


def _s5_core(h, wt, we, wst, tab, s0_re, s0_im, d_skip, *, n_prompt_rows):
    m_rows, d = h.shape
    nb = d // LANES
    ns = S5_BLOCK_STATE
    k = S5_L * LANES
    n_rows = m_rows // S5_L
    group_frames = S5_L * SUBLANES
    n_prompt_groups = n_prompt_rows // group_frames
    n_streams = s0_re.shape[0]
    assert (m_rows - n_prompt_rows) == n_streams * group_frames
    sf_rows = n_streams + SUBLANES
    lane_blk = pl.BlockSpec((m_rows, LANES), lambda j: (0, j))
    wspec = pl.BlockSpec((1, k, 2 * ns), lambda j: (j, 0, 0))
    st = pl.BlockSpec((n_streams, ns), lambda j: (0, j))
    sf = pl.BlockSpec((sf_rows, ns), lambda j: (0, j))
    return pl.pallas_call(
        functools.partial(_s5_core_kernel, n_prompt_groups, n_streams),
        grid=(nb,),
        in_specs=[
            lane_blk,
            pl.BlockSpec((1, k, k), lambda j: (j, 0, 0)),
            wspec,
            wspec,
            pl.BlockSpec((1, TAB_ROWS, SUBLANES, ns), lambda j: (j, 0, 0, 0)),
            st, st,
            pl.BlockSpec((1, LANES), lambda j: (0, j)),
        ],
        out_specs=[lane_blk, sf, sf],
        out_shape=[
            jax.ShapeDtypeStruct((m_rows, d), F32),
            jax.ShapeDtypeStruct((sf_rows, nb * ns), F32),
            jax.ShapeDtypeStruct((sf_rows, nb * ns), F32),
        ],
        scratch_shapes=[
            pltpu.VMEM((n_rows, k), F32),
            pltpu.VMEM((n_rows, k), BF16),
            pltpu.VMEM((n_rows, 2 * ns), F32),
        ],
        compiler_params=pltpu.CompilerParams(
            dimension_semantics=("arbitrary",), vmem_limit_bytes=VMEM_LIMIT_BYTES),
        name="s5_core",
    )(h, wt, we, wst, tab, s0_re, s0_im, d_skip)


def _glu_kernel(n_col_blocks, x_ref, y_ref, w_ref, out_ref):
    d = x_ref.shape[1]
    a = _gelu(y_ref[...]).astype(BF16)
    cb = d // n_col_blocks
    for n in range(n_col_blocks):
        cols = slice(n * cb, (n + 1) * cb)
        val = jnp.dot(a, w_ref[:, n * cb:(n + 1) * cb], preferred_element_type=F32)
        gate = jnp.dot(a, w_ref[:, d + n * cb:d + (n + 1) * cb], preferred_element_type=F32)
        out_ref[:, cols] = x_ref[:, cols] + val * jax.nn.sigmoid(gate)


def _glu(x, y, w, *, tm, n_col_blocks=4):
    m_rows, d = x.shape
    assert m_rows % tm == 0 and d % n_col_blocks == 0
    tile = pl.BlockSpec((tm, d), lambda i: (i, 0))
    return pl.pallas_call(
        functools.partial(_glu_kernel, n_col_blocks),
        grid=(m_rows // tm,),
        in_specs=[tile, tile, _const_spec(w.shape)],
        out_specs=tile,
        out_shape=jax.ShapeDtypeStruct((m_rows, d), F32),
        compiler_params=pltpu.CompilerParams(
            dimension_semantics=("arbitrary",), vmem_limit_bytes=VMEM_LIMIT_BYTES),
        name="glu",
    )(x, y, w)


def _attention_tables(sinks):
    slopes = jnp.exp2(-8.0 * jnp.arange(1, N_HEADS + 1, dtype=F32) / N_HEADS)
    frame = jnp.arange(CHUNK, dtype=F32)[None, :]
    band = jnp.arange(BAND, dtype=F32)[:, None]
    dist = jnp.abs(frame - (band - WINDOW))
    bias = (dist[:, None, :] * slopes[None, :, None]).reshape(BAND, N_HEADS * CHUNK)
    sink_row = jnp.repeat(sinks.astype(F32), CHUNK).reshape(1, N_HEADS * CHUNK)
    return bias, sink_row


def _gm_tables(gm_ws, gm_b, d_gm):
    blk = jnp.arange(GM_CHUNK) // CHUNK
    w_prompt = jnp.where((blk[:, None] >= blk[None, :])[None], gm_ws, 0.0)
    top = gm_ws[:, :CHUNK, :CHUNK]
    zeros = jnp.zeros_like(top)
    w_sample = jnp.concatenate(
        [jnp.concatenate([top, zeros], axis=2), jnp.concatenate([zeros, top], axis=2)], axis=1)
    wsp = jnp.stack([w_prompt, w_sample])
    wsp = wsp.reshape(2, N_GM_GROUPS // 2, 2, GM_CHUNK, GM_CHUNK).transpose(0, 1, 3, 2, 4)
    wsp = wsp.reshape(2, N_GM_GROUPS // 2, GM_CHUNK, 2 * GM_CHUNK).astype(BF16)
    b_prompt = gm_b.T
    b_sample = jnp.concatenate([gm_b[:, :CHUNK].T, gm_b[:, :CHUNK].T], axis=0)
    gmb = jnp.stack([b_prompt, b_sample]).astype(F32)
    gmb = jnp.repeat(gmb, d_gm // N_GM_GROUPS, axis=2)
    return wsp, gmb


def _s5_embed(b_re, b_im, c_re, c_im):
    n_groups = b_re.shape[0]
    nb = n_groups // S5_LANE_GROUPS
    eye = jnp.eye(S5_LANE_GROUPS, dtype=F32)

    def emb_b(b):
        b = b.astype(F32).reshape(nb, S5_LANE_GROUPS, S5_STATE, S5_GROUP)
        return jnp.einsum('jgpi,gh->jgihp', b, eye).reshape(nb, LANES, S5_BLOCK_STATE)

    def emb_c(c):
        c = c.astype(F32).reshape(nb, S5_LANE_GROUPS, S5_GROUP, S5_STATE)
        return jnp.einsum('jgip,gh->jgihp', c, eye).reshape(nb, LANES, S5_BLOCK_STATE)

    return emb_b(b_re), emb_b(b_im), emb_c(c_re), emb_c(c_im)


def kernel(x_prompt, x_sample, cache_swa_k, cache_swa_v, state_s5_re, state_s5_im, norm_mix, norm_ffn, norm_final, w_in0, attn_sinks, gm_norm, gm_ws, gm_b, w_out0, s5_lam_re, s5_lam_im, s5_log_dt, s5_b_re, s5_b_im, s5_c_re, s5_c_im, s5_d, s5_w_glu, ffn_w_gate, ffn_w_up, ffn_w_down):
    batch, seq, d = x_prompt.shape
    dec_batch, dec_seq, _ = x_sample.shape
    assert batch == 1 and dec_seq == CHUNK and norm_mix.shape[0] == 2
    n_prompt = batch * seq
    n_sample = dec_batch * dec_seq
    d_gm = gm_norm.shape[-1]

    bias_tbl, sink_row = _attention_tables(attn_sinks[0])
    wsp, gmb = _gm_tables(gm_ws[0], gm_b[0], d_gm)
    x1, k_all, v_all, gvn = _mix0(
        x_prompt.reshape(n_prompt, d), x_sample.reshape(n_sample, d),
        cache_swa_k[0].reshape(dec_batch * WINDOW, D_KV), cache_swa_v[0].reshape(dec_batch * WINDOW, D_KV),
        norm_mix[0].reshape(1, d), w_in0[0].astype(BF16), bias_tbl, sink_row,
        gm_norm[0].reshape(1, d_gm), wsp, gmb, w_out0[0].astype(BF16), units=2)

    x2, h2 = _ffn(x1, norm_ffn[0].reshape(1, d), ffn_w_gate[0].astype(BF16), ffn_w_up[0].astype(BF16),
                  ffn_w_down[0].astype(BF16), norm_mix[1].reshape(1, d), tm=512, tf=512)

    n_groups = s5_lam_re.shape[1]
    nb = n_groups // S5_LANE_GROUPS
    bre, bim, cre, cim = _s5_embed(s5_b_re[0], s5_b_im[0], s5_c_re[0], s5_c_im[0])
    wt, we, wst, tab = _s5_prep(
        s5_lam_re[0].reshape(nb, 1, S5_BLOCK_STATE), s5_lam_im[0].reshape(nb, 1, S5_BLOCK_STATE),
        jnp.repeat(s5_log_dt[0], S5_STATE).reshape(nb, 1, S5_BLOCK_STATE), bre, bim, cre, cim)
    ys5, sf_re, sf_im = _s5_core(
        h2, wt, we, wst, tab, state_s5_re[0].reshape(dec_batch, n_groups * S5_STATE),
        state_s5_im[0].reshape(dec_batch, n_groups * S5_STATE), s5_d[0].reshape(1, d),
        n_prompt_rows=n_prompt)
    x3 = _glu(x2, ys5, s5_w_glu[0].astype(BF16), tm=256)

    y_prompt, y_sample = _ffn(
        x3, norm_ffn[1].reshape(1, d), ffn_w_gate[1].astype(BF16), ffn_w_up[1].astype(BF16),
        ffn_w_down[1].astype(BF16), norm_final.reshape(1, d), tm=512, tf=512, n_prompt_rows=n_prompt)

    keep = min(WINDOW, seq)
    y_prompt = y_prompt.reshape(batch, seq, d)
    y_sample = y_sample.reshape(dec_batch, dec_seq, d)
    kv_shape_p = (1, batch, keep, N_KV_HEADS, HEAD_DIM)
    kv_shape_s = (1, dec_batch, dec_seq, N_KV_HEADS, HEAD_DIM)
    st_p = (1, batch, n_groups, S5_STATE)
    st_s = (1, dec_batch, n_groups, S5_STATE)
    return (y_prompt, y_sample,
            k_all[n_prompt - keep:n_prompt].reshape(kv_shape_p),
            v_all[n_prompt - keep:n_prompt].reshape(kv_shape_p),
            k_all[n_prompt:].reshape(kv_shape_s),
            v_all[n_prompt:].reshape(kv_shape_s),
            gvn.reshape(1, dec_batch, dec_seq, d_gm),
            sf_re[dec_batch].reshape(st_p), sf_im[dec_batch].reshape(st_p),
            sf_re[:dec_batch].reshape(st_s), sf_im[:dec_batch].reshape(st_s))
```

```python
import functools
import math

import jax
import jax.numpy as jnp
from jax import lax
from jax.experimental import pallas as pl
from jax.experimental.pallas import tpu as pltpu

F32 = jnp.float32
BF16 = jnp.bfloat16

CHUNK = 64
HEAD_DIM = 64
N_HEADS = 16
N_KV_HEADS = 2
Q_PER_KV = N_HEADS // N_KV_HEADS
WINDOW = 128
BAND = WINDOW + CHUNK
D_ATTN = N_HEADS * HEAD_DIM
D_KV = N_KV_HEADS * HEAD_DIM
GM_CHUNK = 128
N_GM_GROUPS = 16
S5_GROUP = 16
S5_STATE = 64
RMS_EPS = 1e-5
NEG_INF = -1e30

LANES = 128
SUBLANES = 8
VMEM_LIMIT_BYTES = 56 * 1024 * 1024

FFN_TM = 1024
FFN_TF = 256

S5_L = SUBLANES
S5_LANE_GROUPS = LANES // S5_GROUP
S5_BLOCK_STATE = S5_LANE_GROUPS * S5_STATE


def _gelu(x):
    return 0.5 * x * (1.0 + lax.erf(x * math.sqrt(0.5)))


def _rms_scale(x):
    return x * lax.rsqrt(jnp.mean(x * x, axis=-1, keepdims=True) + RMS_EPS)


def _const_spec(shape):
    zeros = (0,) * len(shape)
    return pl.BlockSpec(shape, lambda *_: zeros, pipeline_mode=pl.Buffered(1))


def _mix0_kernel(n_prompt_tiles, units,
                 xp_ref, xs_ref, ck_ref, cv_ref, g_ref, win_ref, bias_ref, sink_ref, gmn_ref,
                 wsp_ref, gmb_ref, wout_ref,
                 x1_ref, k_ref, v_ref, gvn_ref,
                 z_ref, q_ref, ocat_ref, kprev_ref, vprev_ref):
    i = pl.program_id(0)
    is_sample = i >= n_prompt_tiles
    tm = units * GM_CHUNK
    d_gm = gmn_ref.shape[-1]
    off_k = D_ATTN
    off_v = D_ATTN + D_KV
    off_gu = D_ATTN + 2 * D_KV
    off_gv = off_gu + d_gm
    n_q = Q_PER_KV * CHUNK

    @pl.when(i == 0)
    def _():
        kprev_ref[...] = jnp.zeros_like(kprev_ref)
        vprev_ref[...] = jnp.zeros_like(vprev_ref)

    x = jnp.where(is_sample, xs_ref[...], xp_ref[...])
    h = (_rms_scale(x) * g_ref[...]).astype(BF16)
    z_ref[...] = jnp.dot(h, win_ref[...], preferred_element_type=F32)

    q_ref[...] = (z_ref[:, 0:D_ATTN] * (HEAD_DIM ** -0.5)).astype(BF16)
    k = z_ref[:, off_k:off_k + D_KV]
    v = z_ref[:, off_v:off_v + D_KV]
    k_ref[...] = k
    v_ref[...] = v

    def lane_lo(rows):
        return lax.broadcasted_iota(jnp.int32, (rows, LANES), 1) < HEAD_DIM

    def replicate(a):
        r = pltpu.roll(a, HEAD_DIM, axis=1)
        lo = lane_lo(a.shape[0])
        return jnp.where(lo, a, r).astype(BF16), jnp.where(lo, r, a).astype(BF16)

    k_rep = replicate(k)
    v_rep = replicate(v)
    ck_rep = replicate(ck_ref[...])
    cv_rep = replicate(cv_ref[...])

    lo64 = lane_lo(CHUNK)
    band_pos = lax.broadcasted_iota(jnp.int32, (BAND, n_q), 0)
    chunks_per_tile = tm // CHUNK

    for u in range(units):
        for c2 in range(GM_CHUNK // CHUNK):
            r0 = u * GM_CHUNK + c2 * CHUNK
            stream = r0 // CHUNK
            chunk_index = i * chunks_per_tile + stream
            valid_from = jnp.where(is_sample, 0, jnp.maximum(WINDOW - CHUNK * chunk_index, 0))
            valid = band_pos >= valid_from
            for kv in range(N_KV_HEADS):
                def band(cur, prev_ref, cached):
                    if u == 0:
                        prev_unit = prev_ref[kv]
                    else:
                        prev_unit = cur[(u - 1) * GM_CHUNK:u * GM_CHUNK]
                    if c2 == 0:
                        prompt_prev = prev_unit
                    else:
                        prompt_prev = jnp.concatenate(
                            [prev_unit[CHUNK:], cur[u * GM_CHUNK:u * GM_CHUNK + CHUNK]], axis=0)
                    sample_prev = cached[stream * WINDOW:(stream + 1) * WINDOW]
                    prev = jnp.where(is_sample, sample_prev, prompt_prev)
                    return jnp.concatenate([prev, cur[r0:r0 + CHUNK]], axis=0)

                kb = band(k_rep[kv], kprev_ref, ck_rep[kv])
                vb = band(v_rep[kv], vprev_ref, cv_rep[kv])

                pieces = []
                for m in range(Q_PER_KV // 2):
                    c0 = kv * Q_PER_KV * HEAD_DIM + m * LANES
                    qp = q_ref[r0:r0 + CHUNK, c0:c0 + LANES]
                    pieces.append(jnp.where(lo64, qp, jnp.zeros_like(qp)))
                    pieces.append(jnp.where(lo64, jnp.zeros_like(qp), qp))
                qs = jnp.concatenate(pieces, axis=0)
                cols = slice(kv * n_q, (kv + 1) * n_q)
                st = lax.dot_general(kb, qs, (((1,), (1,)), ((), ())),
                                     preferred_element_type=F32)
                st = jnp.where(valid, st - bias_ref[:, cols], NEG_INF)
                sink = sink_ref[:, cols]
                mx = jnp.maximum(jnp.max(st, axis=0, keepdims=True), sink)
                p = jnp.exp(st - mx)
                denom = jnp.sum(p, axis=0, keepdims=True) + jnp.exp(sink - mx)
                pn = (p * (1.0 / denom)).astype(BF16)
                o = lax.dot_general(pn, vb, (((0,), (0,)), ((), ())),
                                    preferred_element_type=F32)
                for m in range(Q_PER_KV // 2):
                    o_pair = jnp.where(lo64, o[(2 * m) * CHUNK:(2 * m + 1) * CHUNK],
                                       o[(2 * m + 1) * CHUNK:(2 * m + 2) * CHUNK])
                    c0 = kv * Q_PER_KV * HEAD_DIM + m * LANES
                    ocat_ref[r0:r0 + CHUNK, c0:c0 + LANES] = o_pair.astype(BF16)

    lo128 = lane_lo(GM_CHUNK)
    for u in range(units):
        rows = slice(u * GM_CHUNK, (u + 1) * GM_CHUNK)
        ua = _gelu(z_ref[rows, off_gu:off_gu + d_gm])
        gvn = _rms_scale(_gelu(z_ref[rows, off_gv:off_gv + d_gm])) * gmn_ref[...]
        gvn_ref[rows, :] = gvn
        gb = gvn.astype(BF16)
        for m in range(N_GM_GROUPS // 2):
            cols = slice(m * LANES, (m + 1) * LANES)
            rhs = gb[:, cols]
            rhs2 = jnp.concatenate([jnp.where(lo128, rhs, jnp.zeros_like(rhs)),
                                    jnp.where(lo128, jnp.zeros_like(rhs), rhs)], axis=0)
            sp = jnp.dot(wsp_ref[0, m], rhs2, preferred_element_type=F32) + gmb_ref[0, :, cols]
            ocat_ref[rows, D_ATTN + m * LANES:D_ATTN + (m + 1) * LANES] = (ua[:, cols] * sp).astype(BF16)

    x1_ref[...] = x + jnp.dot(ocat_ref[...], wout_ref[...], preferred_element_type=F32)

    for kv in range(N_KV_HEADS):
        kprev_ref[kv] = k_rep[kv][tm - GM_CHUNK:]
        vprev_ref[kv] = v_rep[kv][tm - GM_CHUNK:]


def _mix0(xp, xs, cache_k, cache_v, g, w_in, bias_tbl, sink_row, gm_norm, wsp, gmb, w_out, *, units):
    n_prompt_rows, d = xp.shape
    n_sample_rows = xs.shape[0]
    m_rows = n_prompt_rows + n_sample_rows
    tm = units * GM_CHUNK
    assert n_prompt_rows % tm == 0 and n_sample_rows % tm == 0
    n_tiles = m_rows // tm
    n_prompt_tiles = n_prompt_rows // tm
    d_in = w_in.shape[1]
    d_gm = gm_norm.shape[-1]
    cache_rows = (tm // CHUNK) * WINDOW

    def prompt_block(i):
        return jnp.minimum(i, n_prompt_tiles - 1)

    def sample_block(i):
        return jnp.maximum(i - n_prompt_tiles, 0)

    def kind(i):
        return jnp.where(i >= n_prompt_tiles, 1, 0)

    in_specs = [
        pl.BlockSpec((tm, d), lambda i: (prompt_block(i), 0)),
        pl.BlockSpec((tm, d), lambda i: (sample_block(i), 0)),
        pl.BlockSpec((cache_rows, D_KV), lambda i: (sample_block(i), 0)),
        pl.BlockSpec((cache_rows, D_KV), lambda i: (sample_block(i), 0)),
        _const_spec((1, d)),
        _const_spec((d, d_in)),
        _const_spec(bias_tbl.shape),
        _const_spec(sink_row.shape),
        _const_spec((1, d_gm)),
        pl.BlockSpec((1,) + wsp.shape[1:], lambda i: (kind(i), 0, 0, 0)),
        pl.BlockSpec((1, GM_CHUNK, d_gm), lambda i: (kind(i), 0, 0)),
        _const_spec(w_out.shape),
    ]
    out_specs = [
        pl.BlockSpec((tm, d), lambda i: (i, 0)),
        pl.BlockSpec((tm, D_KV), lambda i: (i, 0)),
        pl.BlockSpec((tm, D_KV), lambda i: (i, 0)),
        pl.BlockSpec((tm, d_gm), lambda i: (sample_block(i), 0)),
    ]
    out_shape = [
        jax.ShapeDtypeStruct((m_rows, d), F32),
        jax.ShapeDtypeStruct((m_rows, D_KV), F32),
        jax.ShapeDtypeStruct((m_rows, D_KV), F32),
        jax.ShapeDtypeStruct((n_sample_rows, d_gm), F32),
    ]
    scratch = [
        pltpu.VMEM((tm, d_in), F32),
        pltpu.VMEM((tm, D_ATTN), BF16),
        pltpu.VMEM((tm, D_ATTN + d_gm), BF16),
        pltpu.VMEM((N_KV_HEADS, GM_CHUNK, LANES), BF16),
        pltpu.VMEM((N_KV_HEADS, GM_CHUNK, LANES), BF16),
    ]
    return pl.pallas_call(
        functools.partial(_mix0_kernel, n_prompt_tiles, units),
        grid=(n_tiles,),
        in_specs=in_specs,
        out_specs=out_specs,
        out_shape=out_shape,
        scratch_shapes=scratch,
        compiler_params=pltpu.CompilerParams(
            dimension_semantics=("arbitrary",), vmem_limit_bytes=VMEM_LIMIT_BYTES),
        name="mix0",
    )(xp, xs, cache_k, cache_v, g, w_in, bias_tbl, sink_row, gm_norm, wsp, gmb, w_out)


def _ffn_kernel(final, x_ref, g_ref, wg_ref, wu_ref, wd_ref, gnext_ref, out_ref, *rest):
    h_ref = rest[-1]
    j = pl.program_id(1)

    @pl.when(j == 0)
    def _():
        x = x_ref[...]
        h_ref[...] = (_rms_scale(x) * g_ref[...]).astype(BF16)
        out_ref[...] = x

    h = h_ref[...]
    gate = jnp.dot(h, wg_ref[0].astype(BF16), preferred_element_type=F32)
    up = jnp.dot(h, wu_ref[0].astype(BF16), preferred_element_type=F32)
    act = (gate * jax.nn.sigmoid(gate) * up).astype(BF16)
    out_ref[...] += jnp.dot(act, wd_ref[0].astype(BF16), preferred_element_type=F32)

    @pl.when(j == pl.num_programs(1) - 1)
    def _():
        out = out_ref[...]
        scale = lax.rsqrt(jnp.mean(out * out, axis=-1, keepdims=True) + RMS_EPS)
        if final:
            out_ref[...] = out * scale * gnext_ref[...]
        else:
            rest[0][...] = jnp.broadcast_to(scale, rest[0].shape)


def _ffn(x, g, wg, wu, wd, gnext, *, layer, tm, tf, final, row0=0, n_rows=None):
    d = x.shape[1]
    n_rows = x.shape[0] if n_rows is None else n_rows
    f = wg.shape[2]
    assert n_rows % tm == 0 and row0 % tm == 0 and f % tf == 0
    tile0 = row0 // tm
    out_specs = [pl.BlockSpec((tm, d), lambda i, j: (i, 0))]
    out_shape = [jax.ShapeDtypeStruct((n_rows, d), F32)]
    if not final:
        out_specs.append(pl.BlockSpec((tm, LANES), lambda i, j: (i, 0)))
        out_shape.append(jax.ShapeDtypeStruct((n_rows, LANES), F32))
    return pl.pallas_call(
        functools.partial(_ffn_kernel, final),
        grid=(n_rows // tm, f // tf),
        in_specs=[
            pl.BlockSpec((tm, d), lambda i, j: (i + tile0, 0)),
            pl.BlockSpec((1, d), lambda i, j: (0, 0)),
            pl.BlockSpec((1, d, tf), lambda i, j: (layer, 0, j)),
            pl.BlockSpec((1, d, tf), lambda i, j: (layer, 0, j)),
            pl.BlockSpec((1, tf, d), lambda i, j: (layer, j, 0)),
            pl.BlockSpec((1, d), lambda i, j: (0, 0)),
        ],
        out_specs=out_specs,
        out_shape=out_shape,
        scratch_shapes=[pltpu.VMEM((tm, d), BF16)],
        compiler_params=pltpu.CompilerParams(
            dimension_semantics=("arbitrary", "arbitrary"), vmem_limit_bytes=VMEM_LIMIT_BYTES),
        name="ffn",
    )(x, g, wg, wu, wd, gnext)


TAB_ROWS = 10


def _s5_prep_kernel(lre_ref, lim_ref, ldt_ref, bre_ref, bim_ref, cre_ref, cim_ref,
                    wt_ref, we_ref, wst_ref, tab_ref):
    ns = S5_BLOCK_STATE
    lr = lre_ref[0]
    li = lim_ref[0]
    dt = jnp.exp(ldt_ref[0])
    mag = jnp.exp(lr * dt)
    ar = mag * jnp.cos(li * dt)
    ai = mag * jnp.sin(li * dt)
    den = lr * lr + li * li
    nr = ar - 1.0
    fr = (nr * lr + ai * li) / den
    fi = (ai * lr - nr * li) / den
    b_re = bre_ref[0]
    b_im = bim_ref[0]
    bbr = fr * b_re - fi * b_im
    bbi = fr * b_im + fi * b_re
    c_re = cre_ref[0]
    c_im = cim_ref[0]

    def cmul(pr, pi, qr, qi):
        return pr * qr - pi * qi, pr * qi + pi * qr

    powers = [(jnp.ones_like(ar), jnp.zeros_like(ar))]
    for _ in range(S5_L):
        powers.append(cmul(*powers[-1], ar, ai))

    dmats = []
    for l in range(S5_L):
        pr, pi = powers[l]
        bkr, bki = cmul(pr, pi, bbr, bbi)
        dk = (lax.dot_general(bkr, c_re, (((1,), (1,)), ((), ())), precision=lax.Precision.HIGHEST,
                              preferred_element_type=F32)
              - lax.dot_general(bki, c_im, (((1,), (1,)), ((), ())), precision=lax.Precision.HIGHEST,
                                preferred_element_type=F32))
        dmats.append(dk.astype(BF16))
        we_ref[0, (S5_L - 1 - l) * LANES:(S5_L - l) * LANES, 0:ns] = bkr.astype(BF16)
        we_ref[0, (S5_L - 1 - l) * LANES:(S5_L - l) * LANES, ns:2 * ns] = bki.astype(BF16)
        qr, qi = powers[l + 1]
        wst_ref[0, l * LANES:(l + 1) * LANES, 0:ns] = (c_re * qr - c_im * qi).astype(BF16)
        wst_ref[0, l * LANES:(l + 1) * LANES, ns:2 * ns] = (-c_re * qi - c_im * qr).astype(BF16)

    zero_blk = jnp.zeros((LANES, LANES), BF16)
    for l in range(S5_L):
        for l2 in range(S5_L):
            blk = dmats[l2 - l] if l2 >= l else zero_blk
            wt_ref[0, l * LANES:(l + 1) * LANES, l2 * LANES:(l2 + 1) * LANES] = blk

    row = lax.broadcasted_iota(jnp.int32, (SUBLANES, ns), 0)

    def bcast(a):
        return jnp.broadcast_to(a, (SUBLANES, ns))

    a1 = powers[S5_L]
    a2 = cmul(*a1, *a1)
    a4 = cmul(*a2, *a2)
    a8 = cmul(*a4, *a4)
    t = 0
    for shift, (pr, pi) in ((1, a1), (2, a2), (4, a4)):
        tab_ref[0, t] = jnp.where(row >= shift, bcast(pr), 0.0)
        tab_ref[0, t + 1] = jnp.where(row >= shift, bcast(pi), 0.0)
        t += 2
    pr_tab = jnp.zeros((SUBLANES, ns), F32)
    pi_tab = jnp.zeros((SUBLANES, ns), F32)
    cur = a1
    for r in range(SUBLANES):
        pr_tab = jnp.where(row == r, bcast(cur[0]), pr_tab)
        pi_tab = jnp.where(row == r, bcast(cur[1]), pi_tab)
        cur = cmul(*cur, *a1)
    tab_ref[0, 6] = pr_tab
    tab_ref[0, 7] = pi_tab
    tab_ref[0, 8] = bcast(a8[0])
    tab_ref[0, 9] = bcast(a8[1])


def _s5_prep(lam_re, lam_im, log_dt, b_re_emb, b_im_emb, c_re_emb, c_im_emb):
    nb = b_re_emb.shape[0]
    ns = S5_BLOCK_STATE
    k = S5_L * LANES
    vec = pl.BlockSpec((1, 1, ns), lambda j: (j, 0, 0))
    emb = pl.BlockSpec((1, LANES, ns), lambda j: (j, 0, 0))
    wspec = pl.BlockSpec((1, k, 2 * ns), lambda j: (j, 0, 0))
    return pl.pallas_call(
        _s5_prep_kernel,
        grid=(nb,),
        in_specs=[vec, vec, vec, emb, emb, emb, emb],
        out_specs=[
            pl.BlockSpec((1, k, k), lambda j: (j, 0, 0)),
            wspec,
            wspec,
            pl.BlockSpec((1, TAB_ROWS, SUBLANES, ns), lambda j: (j, 0, 0, 0)),
        ],
        out_shape=[
            jax.ShapeDtypeStruct((nb, k, k), BF16),
            jax.ShapeDtypeStruct((nb, k, 2 * ns), BF16),
            jax.ShapeDtypeStruct((nb, k, 2 * ns), BF16),
            jax.ShapeDtypeStruct((nb, TAB_ROWS, SUBLANES, ns), F32),
        ],
        compiler_params=pltpu.CompilerParams(
            dimension_semantics=("arbitrary",), vmem_limit_bytes=VMEM_LIMIT_BYTES),
        name="s5_prep",
    )(lam_re, lam_im, log_dt, b_re_emb, b_im_emb, c_re_emb, c_im_emb)


def _s5_core_kernel(n_prompt_groups, n_streams,
                    x_ref, rstd_ref, g_ref, wt_ref, we_ref, wst_ref, tab_ref, s0r_ref, s0i_ref, d_ref,
                    y_ref, sfr_ref, sfi_ref,
                    u2f_ref, u2_ref, e_ref):
    ns = S5_BLOCK_STATE
    n_rows = x_ref.shape[0] // S5_L

    for l in range(S5_L):
        frames = pl.ds(l, n_rows, stride=S5_L)
        u2f_ref[:, l * LANES:(l + 1) * LANES] = x_ref[frames, :] * rstd_ref[frames, :] * g_ref[...]
    u2_ref[...] = u2f_ref[...].astype(BF16)
    e_ref[...] = jnp.dot(u2_ref[...], we_ref[0], preferred_element_type=F32)

    m1r, m1i, m2r, m2i, m4r, m4i, pr, pi, a8r, a8i = [tab_ref[0, t] for t in range(TAB_ROWS)]
    row = lax.broadcasted_iota(jnp.int32, (SUBLANES, ns), 0)

    def group_step(g8, cr, ci):
        xr = e_ref[pl.ds(g8, SUBLANES), 0:ns]
        xi = e_ref[pl.ds(g8, SUBLANES), ns:2 * ns]
        for shift, tr, ti in ((1, m1r, m1i), (2, m2r, m2i), (4, m4r, m4i)):
            sr = pltpu.roll(xr, shift, axis=0)
            si = pltpu.roll(xi, shift, axis=0)
            xr, xi = xr + tr * sr - ti * si, xi + tr * si + ti * sr
        last_r = jnp.broadcast_to(xr[SUBLANES - 1:SUBLANES], (SUBLANES, ns))
        last_i = jnp.broadcast_to(xi[SUBLANES - 1:SUBLANES], (SUBLANES, ns))
        st_r = xr + pr * cr - pi * ci
        st_i = xi + pr * ci + pi * cr
        e_ref[pl.ds(g8, SUBLANES), 0:ns] = jnp.where(row == 0, cr, pltpu.roll(st_r, 1, axis=0))
        e_ref[pl.ds(g8, SUBLANES), ns:2 * ns] = jnp.where(row == 0, ci, pltpu.roll(st_i, 1, axis=0))
        return a8r * cr - a8i * ci + last_r, a8r * ci + a8i * cr + last_i

    def prompt_body(g, carry):
        return group_step(pl.multiple_of(g * SUBLANES, SUBLANES), *carry)

    zero = jnp.zeros((SUBLANES, ns), F32)
    fr, fi = lax.fori_loop(0, n_prompt_groups, prompt_body, (zero, zero))
    sfr_ref[...] = jnp.zeros_like(sfr_ref)
    sfi_ref[...] = jnp.zeros_like(sfi_ref)
    sfr_ref[n_streams:n_streams + 1, :] = fr[0:1]
    sfi_ref[n_streams:n_streams + 1, :] = fi[0:1]
    for b in range(n_streams):
        cr = jnp.broadcast_to(s0r_ref[b:b + 1, :], (SUBLANES, ns))
        ci = jnp.broadcast_to(s0i_ref[b:b + 1, :], (SUBLANES, ns))
        fr, fi = group_step((n_prompt_groups + b) * SUBLANES, cr, ci)
        sfr_ref[b:b + 1, :] = fr[0:1]
        sfi_ref[b:b + 1, :] = fi[0:1]

    y2 = jnp.dot(u2_ref[...], wt_ref[0], preferred_element_type=F32)
    y2 = y2 + lax.dot_general(e_ref[...].astype(BF16), wst_ref[0], (((1,), (1,)), ((), ())),
                              preferred_element_type=F32)
    u2f_ref[...] = y2 + jnp.concatenate([d_ref[...]] * S5_L, axis=1) * u2f_ref[...]
    for l in range(S5_L):
        y_ref[pl.ds(l, n_rows, stride=S5_L), :] = u2f_ref[:, l * LANES:(l + 1) * LANES]


def _s5_core(x, rstd, g, wt, we, wst, tab, s0_re, s0_im, d_skip, *, n_prompt_rows):
    m_rows, d = x.shape
    nb = d // LANES
    ns = S5_BLOCK_STATE
    k = S5_L * LANES
    n_rows = m_rows // S5_L
    group_frames = S5_L * SUBLANES
    n_prompt_groups = n_prompt_rows // group_frames
    n_streams = s0_re.shape[0]
    assert (m_rows - n_prompt_rows) == n_streams * group_frames
    sf_rows = n_streams + SUBLANES
    lane_blk = pl.BlockSpec((m_rows, LANES), lambda j: (0, j))
    wspec = pl.BlockSpec((1, k, 2 * ns), lambda j: (j, 0, 0))
    st = pl.BlockSpec((n_streams, ns), lambda j: (0, j))
    sf = pl.BlockSpec((sf_rows, ns), lambda j: (0, j))
    return pl.pallas_call(
        functools.partial(_s5_core_kernel, n_prompt_groups, n_streams),
        grid=(nb,),
        in_specs=[
            lane_blk,
            _const_spec((m_rows, LANES)),
            pl.BlockSpec((1, LANES), lambda j: (0, j)),
            pl.BlockSpec((1, k, k), lambda j: (j, 0, 0)),
            wspec,
            wspec,
            pl.BlockSpec((1, TAB_ROWS, SUBLANES, ns), lambda j: (j, 0, 0, 0)),
            st, st,
            pl.BlockSpec((1, LANES), lambda j: (0, j)),
        ],
        out_specs=[lane_blk, sf, sf],
        out_shape=[
            jax.ShapeDtypeStruct((m_rows, d), F32),
            jax.ShapeDtypeStruct((sf_rows, nb * ns), F32),
            jax.ShapeDtypeStruct((sf_rows, nb * ns), F32),
        ],
        scratch_shapes=[
            pltpu.VMEM((n_rows, k), F32),
            pltpu.VMEM((n_rows, k), BF16),
            pltpu.VMEM((n_rows, 2 * ns), F32),
        ],
        compiler_params=pltpu.CompilerParams(
            dimension_semantics=("arbitrary",), vmem_limit_bytes=VMEM_LIMIT_BYTES),
        name="s5_core",
    )(x, rstd, g, wt, we, wst, tab, s0_re, s0_im, d_skip)


def _glu_kernel(n_col_blocks, x_ref, y_ref, w_ref, out_ref):
    d = x_ref.shape[1]
    a = _gelu(y_ref[...]).astype(BF16)
    cb = d // n_col_blocks
    for n in range(n_col_blocks):
        cols = slice(n * cb, (n + 1) * cb)
        val = jnp.dot(a, w_ref[:, n * cb:(n + 1) * cb], preferred_element_type=F32)
        gate = jnp.dot(a, w_ref[:, d + n * cb:d + (n + 1) * cb], preferred_element_type=F32)
        out_ref[:, cols] = x_ref[:, cols] + val * jax.nn.sigmoid(gate)


def _glu(x, y, w, *, tm, n_col_blocks=4):
    m_rows, d = x.shape
    assert m_rows % tm == 0 and d % n_col_blocks == 0
    tile = pl.BlockSpec((tm, d), lambda i: (i, 0))
    return pl.pallas_call(
        functools.partial(_glu_kernel, n_col_blocks),
        grid=(m_rows // tm,),
        in_specs=[tile, tile, _const_spec(w.shape)],
        out_specs=tile,
        out_shape=jax.ShapeDtypeStruct((m_rows, d), F32),
        compiler_params=pltpu.CompilerParams(
            dimension_semantics=("arbitrary",), vmem_limit_bytes=VMEM_LIMIT_BYTES),
        name="glu",
    )(x, y, w)


def _attention_tables(sinks):
    slopes = jnp.exp2(-8.0 * jnp.arange(1, N_HEADS + 1, dtype=F32) / N_HEADS)
    frame = jnp.arange(CHUNK, dtype=F32)[None, :]
    band = jnp.arange(BAND, dtype=F32)[:, None]
    dist = jnp.abs(frame - (band - WINDOW))
    bias = (dist[:, None, :] * slopes[None, :, None]).reshape(BAND, N_HEADS * CHUNK)
    sink_row = jnp.repeat(sinks.astype(F32), CHUNK).reshape(1, N_HEADS * CHUNK)
    return bias, sink_row


def _gm_tables(gm_ws, gm_b, d_gm):
    blk = jnp.arange(GM_CHUNK) // CHUNK
    w_prompt = jnp.where((blk[:, None] >= blk[None, :])[None], gm_ws, 0.0)
    top = gm_ws[:, :CHUNK, :CHUNK]
    zeros = jnp.zeros_like(top)
    w_sample = jnp.concatenate(
        [jnp.concatenate([top, zeros], axis=2), jnp.concatenate([zeros, top], axis=2)], axis=1)
    wsp = jnp.stack([w_prompt, w_sample])
    wsp = wsp.reshape(2, N_GM_GROUPS // 2, 2, GM_CHUNK, GM_CHUNK).transpose(0, 1, 3, 2, 4)
    wsp = wsp.reshape(2, N_GM_GROUPS // 2, GM_CHUNK, 2 * GM_CHUNK).astype(BF16)
    b_prompt = gm_b.T
    b_sample = jnp.concatenate([gm_b[:, :CHUNK].T, gm_b[:, :CHUNK].T], axis=0)
    gmb = jnp.stack([b_prompt, b_sample]).astype(F32)
    gmb = jnp.repeat(gmb, d_gm // N_GM_GROUPS, axis=2)
    return wsp, gmb


def _s5_embed(b_re, b_im, c_re, c_im):
    n_groups = b_re.shape[0]
    nb = n_groups // S5_LANE_GROUPS
    eye = jnp.eye(S5_LANE_GROUPS, dtype=F32)

    def emb_b(b):
        b = b.astype(F32).reshape(nb, S5_LANE_GROUPS, S5_STATE, S5_GROUP)
        return jnp.einsum('jgpi,gh->jgihp', b, eye).reshape(nb, LANES, S5_BLOCK_STATE)

    def emb_c(c):
        c = c.astype(F32).reshape(nb, S5_LANE_GROUPS, S5_GROUP, S5_STATE)
        return jnp.einsum('jgip,gh->jgihp', c, eye).reshape(nb, LANES, S5_BLOCK_STATE)

    return emb_b(b_re), emb_b(b_im), emb_c(c_re), emb_c(c_im)


def kernel(x_prompt, x_sample, cache_swa_k, cache_swa_v, state_s5_re, state_s5_im, norm_mix, norm_ffn, norm_final, w_in0, attn_sinks, gm_norm, gm_ws, gm_b, w_out0, s5_lam_re, s5_lam_im, s5_log_dt, s5_b_re, s5_b_im, s5_c_re, s5_c_im, s5_d, s5_w_glu, ffn_w_gate, ffn_w_up, ffn_w_down):
    batch, seq, d = x_prompt.shape
    dec_batch, dec_seq, _ = x_sample.shape
    assert batch == 1 and dec_seq == CHUNK and norm_mix.shape[0] == 2
    n_prompt = batch * seq
    n_sample = dec_batch * dec_seq
    d_gm = gm_norm.shape[-1]

    bias_tbl, sink_row = _attention_tables(attn_sinks[0])
    wsp, gmb = _gm_tables(gm_ws[0], gm_b[0], d_gm)
    x1, k_all, v_all, gvn = _mix0(
        x_prompt.reshape(n_prompt, d), x_sample.reshape(n_sample, d),
        cache_swa_k[0].reshape(dec_batch * WINDOW, D_KV), cache_swa_v[0].reshape(dec_batch * WINDOW, D_KV),
        norm_mix[0].reshape(1, d), w_in0[0].astype(BF16), bias_tbl, sink_row,
        gm_norm[0].reshape(1, d_gm), wsp, gmb, w_out0[0].astype(BF16), units=2)

    x2, rstd2 = _ffn(x1, norm_ffn[0].reshape(1, d), ffn_w_gate, ffn_w_up, ffn_w_down,
                     norm_mix[1].reshape(1, d), layer=0, tm=FFN_TM, tf=FFN_TF, final=False)

    n_groups = s5_lam_re.shape[1]
    nb = n_groups // S5_LANE_GROUPS
    bre, bim, cre, cim = _s5_embed(s5_b_re[0], s5_b_im[0], s5_c_re[0], s5_c_im[0])
    wt, we, wst, tab = _s5_prep(
        s5_lam_re[0].reshape(nb, 1, S5_BLOCK_STATE), s5_lam_im[0].reshape(nb, 1, S5_BLOCK_STATE),
        jnp.repeat(s5_log_dt[0], S5_STATE).reshape(nb, 1, S5_BLOCK_STATE), bre, bim, cre, cim)
    ys5, sf_re, sf_im = _s5_core(
        x2, rstd2, norm_mix[1].reshape(1, d), wt, we, wst, tab, state_s5_re[0].reshape(dec_batch, n_groups * S5_STATE),
        state_s5_im[0].reshape(dec_batch, n_groups * S5_STATE), s5_d[0].reshape(1, d),
        n_prompt_rows=n_prompt)
    x3 = _glu(x2, ys5, s5_w_glu[0].astype(BF16), tm=256)

    def last_ffn(row0, n_rows):
        (y,) = _ffn(x3, norm_ffn[1].reshape(1, d), ffn_w_gate, ffn_w_up, ffn_w_down,
                    norm_final.reshape(1, d), layer=1, tm=FFN_TM, tf=FFN_TF, final=True,
                    row0=row0, n_rows=n_rows)
        return y

    y_prompt = last_ffn(0, n_prompt)
    y_sample = last_ffn(n_prompt, n_sample)

    keep = min(WINDOW, seq)
    y_prompt = y_prompt.reshape(batch, seq, d)
    y_sample = y_sample.reshape(dec_batch, dec_seq, d)
    kv_shape_p = (1, batch, keep, N_KV_HEADS, HEAD_DIM)
    kv_shape_s = (1, dec_batch, dec_seq, N_KV_HEADS, HEAD_DIM)
    st_p = (1, batch, n_groups, S5_STATE)
    st_s = (1, dec_batch, n_groups, S5_STATE)
    return (y_prompt, y_sample,
            k_all[n_prompt - keep:n_prompt].reshape(kv_shape_p),
            v_all[n_prompt - keep:n_prompt].reshape(kv_shape_p),
            k_all[n_prompt:].reshape(kv_shape_s),
            v_all[n_prompt:].reshape(kv_shape_s),
            gvn.reshape(1, dec_batch, dec_seq, d_gm),
            sf_re[dec_batch].reshape(st_p), sf_im[dec_batch].reshape(st_p),
            sf_re[:dec_batch].reshape(st_s), sf_im[:dec_batch].reshape(st_s))
```

```python
import functools
import math

import jax
import jax.numpy as jnp
from jax import lax
from jax.experimental import pallas as pl
from jax.experimental.pallas import tpu as pltpu

F32 = jnp.float32
BF16 = jnp.bfloat16

CHUNK = 64
HEAD_DIM = 64
N_HEADS = 16
N_KV_HEADS = 2
Q_PER_KV = N_HEADS // N_KV_HEADS
WINDOW = 128
BAND = WINDOW + CHUNK
D_ATTN = N_HEADS * HEAD_DIM
D_KV = N_KV_HEADS * HEAD_DIM
GM_CHUNK = 128
N_GM_GROUPS = 16
S5_GROUP = 16
S5_STATE = 64
RMS_EPS = 1e-5
NEG_INF = -1e30

LANES = 128
SUBLANES = 8
VMEM_LIMIT_BYTES = 56 * 1024 * 1024

FFN_TM = 1024
FFN_TF = 256

S5_L = SUBLANES
S5_LANE_GROUPS = LANES // S5_GROUP
S5_BLOCK_STATE = S5_LANE_GROUPS * S5_STATE
S5_WT_BLOCK = 256


def _gelu(x):
    return 0.5 * x * (1.0 + lax.erf(x * math.sqrt(0.5)))


def _rms_scale(x):
    return x * lax.rsqrt(jnp.mean(x * x, axis=-1, keepdims=True) + RMS_EPS)


def _const_spec(shape):
    zeros = (0,) * len(shape)
    return pl.BlockSpec(shape, lambda *_: zeros, pipeline_mode=pl.Buffered(1))


def _mix0_kernel(n_prompt_tiles, units,
                 xp_ref, xs_ref, ck_ref, cv_ref, g_ref, win_ref, bias_ref, sink_ref, gmn_ref,
                 wsp_ref, gmb_ref, wout_ref,
                 x1_ref, k_ref, v_ref, gvn_ref,
                 z_ref, q_ref, ocat_ref, kprev_ref, vprev_ref):
    i = pl.program_id(0)
    is_sample = i >= n_prompt_tiles
    tm = units * GM_CHUNK
    d_gm = gmn_ref.shape[-1]
    off_k = D_ATTN
    off_v = D_ATTN + D_KV
    off_gu = D_ATTN + 2 * D_KV
    off_gv = off_gu + d_gm
    n_q = Q_PER_KV * CHUNK

    @pl.when(i == 0)
    def _():
        kprev_ref[...] = jnp.zeros_like(kprev_ref)
        vprev_ref[...] = jnp.zeros_like(vprev_ref)

    x = jnp.where(is_sample, xs_ref[...], xp_ref[...])
    h = (_rms_scale(x) * g_ref[...]).astype(BF16)
    z_ref[...] = jnp.dot(h, win_ref[...], preferred_element_type=F32)

    q_ref[...] = (z_ref[:, 0:D_ATTN] * (HEAD_DIM ** -0.5)).astype(BF16)
    k = z_ref[:, off_k:off_k + D_KV]
    v = z_ref[:, off_v:off_v + D_KV]
    k_ref[...] = k
    v_ref[...] = v

    def lane_lo(rows):
        return lax.broadcasted_iota(jnp.int32, (rows, LANES), 1) < HEAD_DIM

    def replicate(a):
        r = pltpu.roll(a, HEAD_DIM, axis=1)
        lo = lane_lo(a.shape[0])
        return jnp.where(lo, a, r).astype(BF16), jnp.where(lo, r, a).astype(BF16)

    k_rep = replicate(k)
    v_rep = replicate(v)
    ck_rep = replicate(ck_ref[...])
    cv_rep = replicate(cv_ref[...])

    lo64 = lane_lo(CHUNK)
    band_pos = lax.broadcasted_iota(jnp.int32, (BAND, n_q), 0)
    chunks_per_tile = tm // CHUNK

    for u in range(units):
        for c2 in range(GM_CHUNK // CHUNK):
            r0 = u * GM_CHUNK + c2 * CHUNK
            stream = r0 // CHUNK
            chunk_index = i * chunks_per_tile + stream
            valid_from = jnp.where(is_sample, 0, jnp.maximum(WINDOW - CHUNK * chunk_index, 0))
            valid = band_pos >= valid_from
            for kv in range(N_KV_HEADS):
                def band(cur, prev_ref, cached):
                    if u == 0:
                        prev_unit = prev_ref[kv]
                    else:
                        prev_unit = cur[(u - 1) * GM_CHUNK:u * GM_CHUNK]
                    if c2 == 0:
                        prompt_prev = prev_unit
                    else:
                        prompt_prev = jnp.concatenate(
                            [prev_unit[CHUNK:], cur[u * GM_CHUNK:u * GM_CHUNK + CHUNK]], axis=0)
                    sample_prev = cached[stream * WINDOW:(stream + 1) * WINDOW]
                    prev = jnp.where(is_sample, sample_prev, prompt_prev)
                    return jnp.concatenate([prev, cur[r0:r0 + CHUNK]], axis=0)

                kb = band(k_rep[kv], kprev_ref, ck_rep[kv])
                vb = band(v_rep[kv], vprev_ref, cv_rep[kv])

                pieces = []
                for m in range(Q_PER_KV // 2):
                    c0 = kv * Q_PER_KV * HEAD_DIM + m * LANES
                    qp = q_ref[r0:r0 + CHUNK, c0:c0 + LANES]
                    pieces.append(jnp.where(lo64, qp, jnp.zeros_like(qp)))
                    pieces.append(jnp.where(lo64, jnp.zeros_like(qp), qp))
                qs = jnp.concatenate(pieces, axis=0)
                cols = slice(kv * n_q, (kv + 1) * n_q)
                st = lax.dot_general(kb, qs, (((1,), (1,)), ((), ())),
                                     preferred_element_type=F32)
                st = jnp.where(valid, st - bias_ref[:, cols], NEG_INF)
                sink = sink_ref[:, cols]
                mx = jnp.maximum(jnp.max(st, axis=0, keepdims=True), sink)
                p = jnp.exp(st - mx)
                denom = jnp.sum(p, axis=0, keepdims=True) + jnp.exp(sink - mx)
                pn = (p * (1.0 / denom)).astype(BF16)
                o = lax.dot_general(pn, vb, (((0,), (0,)), ((), ())),
                                    preferred_element_type=F32)
                for m in range(Q_PER_KV // 2):
                    o_pair = jnp.where(lo64, o[(2 * m) * CHUNK:(2 * m + 1) * CHUNK],
                                       o[(2 * m + 1) * CHUNK:(2 * m + 2) * CHUNK])
                    c0 = kv * Q_PER_KV * HEAD_DIM + m * LANES
                    ocat_ref[r0:r0 + CHUNK, c0:c0 + LANES] = o_pair.astype(BF16)

    lo128 = lane_lo(GM_CHUNK)
    for u in range(units):
        rows = slice(u * GM_CHUNK, (u + 1) * GM_CHUNK)
        ua = _gelu(z_ref[rows, off_gu:off_gu + d_gm])
        gvn = _rms_scale(_gelu(z_ref[rows, off_gv:off_gv + d_gm])) * gmn_ref[...]
        gvn_ref[rows, :] = gvn
        gb = gvn.astype(BF16)
        for m in range(N_GM_GROUPS // 2):
            cols = slice(m * LANES, (m + 1) * LANES)
            rhs = gb[:, cols]
            rhs2 = jnp.concatenate([jnp.where(lo128, rhs, jnp.zeros_like(rhs)),
                                    jnp.where(lo128, jnp.zeros_like(rhs), rhs)], axis=0)
            sp = jnp.dot(wsp_ref[0, m], rhs2, preferred_element_type=F32) + gmb_ref[0, :, cols]
            ocat_ref[rows, D_ATTN + m * LANES:D_ATTN + (m + 1) * LANES] = (ua[:, cols] * sp).astype(BF16)

    x1_ref[...] = x + jnp.dot(ocat_ref[...], wout_ref[...], preferred_element_type=F32)

    for kv in range(N_KV_HEADS):
        kprev_ref[kv] = k_rep[kv][tm - GM_CHUNK:]
        vprev_ref[kv] = v_rep[kv][tm - GM_CHUNK:]


def _mix0(xp, xs, cache_k, cache_v, g, w_in, bias_tbl, sink_row, gm_norm, wsp, gmb, w_out, *, units):
    n_prompt_rows, d = xp.shape
    n_sample_rows = xs.shape[0]
    m_rows = n_prompt_rows + n_sample_rows
    tm = units * GM_CHUNK
    assert n_prompt_rows % tm == 0 and n_sample_rows % tm == 0
    n_tiles = m_rows // tm
    n_prompt_tiles = n_prompt_rows // tm
    d_in = w_in.shape[1]
    d_gm = gm_norm.shape[-1]
    cache_rows = (tm // CHUNK) * WINDOW

    def prompt_block(i):
        return jnp.minimum(i, n_prompt_tiles - 1)

    def sample_block(i):
        return jnp.maximum(i - n_prompt_tiles, 0)

    def kind(i):
        return jnp.where(i >= n_prompt_tiles, 1, 0)

    in_specs = [
        pl.BlockSpec((tm, d), lambda i: (prompt_block(i), 0)),
        pl.BlockSpec((tm, d), lambda i: (sample_block(i), 0)),
        pl.BlockSpec((cache_rows, D_KV), lambda i: (sample_block(i), 0)),
        pl.BlockSpec((cache_rows, D_KV), lambda i: (sample_block(i), 0)),
        _const_spec((1, d)),
        _const_spec((d, d_in)),
        _const_spec(bias_tbl.shape),
        _const_spec(sink_row.shape),
        _const_spec((1, d_gm)),
        pl.BlockSpec((1,) + wsp.shape[1:], lambda i: (kind(i), 0, 0, 0)),
        pl.BlockSpec((1, GM_CHUNK, d_gm), lambda i: (kind(i), 0, 0)),
        _const_spec(w_out.shape),
    ]
    out_specs = [
        pl.BlockSpec((tm, d), lambda i: (i, 0)),
        pl.BlockSpec((tm, D_KV), lambda i: (i, 0)),
        pl.BlockSpec((tm, D_KV), lambda i: (i, 0)),
        pl.BlockSpec((tm, d_gm), lambda i: (sample_block(i), 0)),
    ]
    out_shape = [
        jax.ShapeDtypeStruct((m_rows, d), F32),
        jax.ShapeDtypeStruct((m_rows, D_KV), F32),
        jax.ShapeDtypeStruct((m_rows, D_KV), F32),
        jax.ShapeDtypeStruct((n_sample_rows, d_gm), F32),
    ]
    scratch = [
        pltpu.VMEM((tm, d_in), F32),
        pltpu.VMEM((tm, D_ATTN), BF16),
        pltpu.VMEM((tm, D_ATTN + d_gm), BF16),
        pltpu.VMEM((N_KV_HEADS, GM_CHUNK, LANES), BF16),
        pltpu.VMEM((N_KV_HEADS, GM_CHUNK, LANES), BF16),
    ]
    return pl.pallas_call(
        functools.partial(_mix0_kernel, n_prompt_tiles, units),
        grid=(n_tiles,),
        in_specs=in_specs,
        out_specs=out_specs,
        out_shape=out_shape,
        scratch_shapes=scratch,
        compiler_params=pltpu.CompilerParams(
            dimension_semantics=("arbitrary",), vmem_limit_bytes=VMEM_LIMIT_BYTES),
        name="mix0",
    )(xp, xs, cache_k, cache_v, g, w_in, bias_tbl, sink_row, gm_norm, wsp, gmb, w_out)


def _ffn_kernel(final, x_ref, g_ref, wg_ref, wu_ref, wd_ref, gnext_ref, out_ref, *rest):
    h_ref = rest[-1]
    j = pl.program_id(1)

    @pl.when(j == 0)
    def _():
        x = x_ref[...]
        h_ref[...] = (_rms_scale(x) * g_ref[...]).astype(BF16)
        out_ref[...] = x

    h = h_ref[...]
    gate = jnp.dot(h, wg_ref[0].astype(BF16), preferred_element_type=F32)
    up = jnp.dot(h, wu_ref[0].astype(BF16), preferred_element_type=F32)
    act = (gate * jax.nn.sigmoid(gate) * up).astype(BF16)
    out_ref[...] += jnp.dot(act, wd_ref[0].astype(BF16), preferred_element_type=F32)

    @pl.when(j == pl.num_programs(1) - 1)
    def _():
        out = out_ref[...]
        scale = lax.rsqrt(jnp.mean(out * out, axis=-1, keepdims=True) + RMS_EPS)
        if final:
            out_ref[...] = out * scale * gnext_ref[...]
        else:
            rest[0][...] = jnp.broadcast_to(scale, rest[0].shape)


def _ffn(x, g, wg, wu, wd, gnext, *, layer, tm, tf, final, row0=0, n_rows=None):
    d = x.shape[1]
    n_rows = x.shape[0] if n_rows is None else n_rows
    f = wg.shape[2]
    assert n_rows % tm == 0 and row0 % tm == 0 and f % tf == 0
    tile0 = row0 // tm
    out_specs = [pl.BlockSpec((tm, d), lambda i, j: (i, 0))]
    out_shape = [jax.ShapeDtypeStruct((n_rows, d), F32)]
    if not final:
        out_specs.append(pl.BlockSpec((tm, LANES), lambda i, j: (i, 0)))
        out_shape.append(jax.ShapeDtypeStruct((n_rows, LANES), F32))
    return pl.pallas_call(
        functools.partial(_ffn_kernel, final),
        grid=(n_rows // tm, f // tf),
        in_specs=[
            pl.BlockSpec((tm, d), lambda i, j: (i + tile0, 0)),
            pl.BlockSpec((1, d), lambda i, j: (0, 0)),
            pl.BlockSpec((1, d, tf), lambda i, j: (layer, 0, j)),
            pl.BlockSpec((1, d, tf), lambda i, j: (layer, 0, j)),
            pl.BlockSpec((1, tf, d), lambda i, j: (layer, j, 0)),
            pl.BlockSpec((1, d), lambda i, j: (0, 0)),
        ],
        out_specs=out_specs,
        out_shape=out_shape,
        scratch_shapes=[pltpu.VMEM((tm, d), BF16)],
        compiler_params=pltpu.CompilerParams(
            dimension_semantics=("arbitrary", "arbitrary"), vmem_limit_bytes=VMEM_LIMIT_BYTES),
        name="ffn",
    )(x, g, wg, wu, wd, gnext)


TAB_ROWS = 18


def _s5_prep_kernel(seg_rows, lre_ref, lim_ref, ldt_ref, bre_ref, bim_ref, cre_ref, cim_ref,
                    wt_ref, we_ref, wst_ref, tab_ref, bhi_ref, blo_ref):
    ns = S5_BLOCK_STATE
    lr = lre_ref[0]
    li = lim_ref[0]
    dt = jnp.exp(ldt_ref[0])
    mag = jnp.exp(lr * dt)
    ar = mag * jnp.cos(li * dt)
    ai = mag * jnp.sin(li * dt)
    den = lr * lr + li * li
    nr = ar - 1.0
    fr = (nr * lr + ai * li) / den
    fi = (ai * lr - nr * li) / den
    b_re = bre_ref[0]
    b_im = bim_ref[0]
    bbr = fr * b_re - fi * b_im
    bbi = fr * b_im + fi * b_re
    c_re = cre_ref[0]
    c_im = cim_ref[0]

    def cmul(pr, pi, qr, qi):
        return pr * qr - pi * qi, pr * qi + pi * qr

    powers = [(jnp.ones_like(ar), jnp.zeros_like(ar))]
    for _ in range(S5_L):
        powers.append(cmul(*powers[-1], ar, ai))

    def split(a):
        hi = a.astype(BF16)
        return hi, (a - hi.astype(F32)).astype(BF16)

    for l in range(S5_L):
        pr, pi = powers[l]
        bkr, bki = cmul(pr, pi, bbr, bbi)
        rows = slice(l * LANES, (l + 1) * LANES)
        for part, (hi, lo) in ((0, split(bkr)), (1, split(bki))):
            bhi_ref[rows, part * ns:(part + 1) * ns] = hi
            blo_ref[rows, part * ns:(part + 1) * ns] = lo
        we_ref[0, (S5_L - 1 - l) * LANES:(S5_L - l) * LANES, 0:ns] = bkr.astype(BF16)
        we_ref[0, (S5_L - 1 - l) * LANES:(S5_L - l) * LANES, ns:2 * ns] = bki.astype(BF16)
        qr, qi = powers[l + 1]
        wst_ref[0, rows, 0:ns] = (c_re * qr - c_im * qi).astype(BF16)
        wst_ref[0, rows, ns:2 * ns] = (-c_re * qi - c_im * qr).astype(BF16)

    c_cat = jnp.concatenate([c_re, -c_im], axis=1)
    c_hi, c_lo = split(c_cat)
    c_parts = jnp.concatenate([c_hi, c_lo], axis=0)
    dims = (((1,), (1,)), ((), ()))
    d_hi = lax.dot_general(bhi_ref[...], c_parts, dims, preferred_element_type=F32)
    d_lo = lax.dot_general(blo_ref[...], c_parts, dims, preferred_element_type=F32)
    d_all = (d_hi[:, 0:LANES] + d_hi[:, LANES:2 * LANES] + d_lo[:, 0:LANES]).astype(BF16)

    zero_blk = jnp.zeros((LANES, LANES), BF16)
    for l in range(S5_L):
        for l2 in range(S5_L):
            blk = d_all[(l2 - l) * LANES:(l2 - l + 1) * LANES] if l2 >= l else zero_blk
            wt_ref[0, l * LANES:(l + 1) * LANES, l2 * LANES:(l2 + 1) * LANES] = blk

    row = lax.broadcasted_iota(jnp.int32, (SUBLANES, ns), 0)

    def bcast(a):
        return jnp.broadcast_to(a, (SUBLANES, ns))

    def log_step_tables(t0, base):
        cur = base
        for n, shift in enumerate((1, 2, 4)):
            tab_ref[0, t0 + 2 * n] = jnp.where(row >= shift, bcast(cur[0]), 0.0)
            tab_ref[0, t0 + 2 * n + 1] = jnp.where(row >= shift, bcast(cur[1]), 0.0)
            cur = cmul(*cur, *cur)
        return cur

    a1 = powers[S5_L]
    a8 = log_step_tables(0, a1)
    pr_tab = jnp.zeros((SUBLANES, ns), F32)
    pi_tab = jnp.zeros((SUBLANES, ns), F32)
    cur = a1
    for r in range(SUBLANES):
        pr_tab = jnp.where(row == r, bcast(cur[0]), pr_tab)
        pi_tab = jnp.where(row == r, bcast(cur[1]), pi_tab)
        cur = cmul(*cur, *a1)
    tab_ref[0, 6] = pr_tab
    tab_ref[0, 7] = pi_tab
    tab_ref[0, 8] = bcast(a8[0])
    tab_ref[0, 9] = bcast(a8[1])
    tab_ref[0, 10] = bcast(a1[0])
    tab_ref[0, 11] = bcast(a1[1])
    seg = a1
    for _ in range(seg_rows.bit_length() - 1):
        seg = cmul(*seg, *seg)
    log_step_tables(12, seg)


def _s5_prep(lam_re, lam_im, log_dt, b_re_emb, b_im_emb, c_re_emb, c_im_emb, *, seg_rows):
    assert seg_rows & (seg_rows - 1) == 0
    nb = b_re_emb.shape[0]
    ns = S5_BLOCK_STATE
    k = S5_L * LANES
    vec = pl.BlockSpec((1, 1, ns), lambda j: (j, 0, 0))
    emb = pl.BlockSpec((1, LANES, ns), lambda j: (j, 0, 0))
    wspec = pl.BlockSpec((1, k, 2 * ns), lambda j: (j, 0, 0))
    return pl.pallas_call(
        functools.partial(_s5_prep_kernel, seg_rows),
        grid=(nb,),
        in_specs=[vec, vec, vec, emb, emb, emb, emb],
        out_specs=[
            pl.BlockSpec((1, k, k), lambda j: (j, 0, 0)),
            wspec,
            wspec,
            pl.BlockSpec((1, TAB_ROWS, SUBLANES, ns), lambda j: (j, 0, 0, 0)),
        ],
        out_shape=[
            jax.ShapeDtypeStruct((nb, k, k), BF16),
            jax.ShapeDtypeStruct((nb, k, 2 * ns), BF16),
            jax.ShapeDtypeStruct((nb, k, 2 * ns), BF16),
            jax.ShapeDtypeStruct((nb, TAB_ROWS, SUBLANES, ns), F32),
        ],
        scratch_shapes=[pltpu.VMEM((k, 2 * ns), BF16), pltpu.VMEM((k, 2 * ns), BF16)],
        compiler_params=pltpu.CompilerParams(
            dimension_semantics=("arbitrary",), vmem_limit_bytes=VMEM_LIMIT_BYTES),
        name="s5_prep",
    )(lam_re, lam_im, log_dt, b_re_emb, b_im_emb, c_re_emb, c_im_emb)


def _s5_core_kernel(n_prompt_rows, n_streams,
                    x_ref, rstd_ref, g_ref, wt_ref, we_ref, wst_ref, tab_ref, s0r_ref, s0i_ref, d_ref,
                    y_ref, sfr_ref, sfi_ref,
                    u2f_ref, u2_ref, e_ref):
    ns = S5_BLOCK_STATE
    n_rows = x_ref.shape[0] // S5_L
    n_sample_rows = n_rows - n_prompt_rows
    seg_rows = n_prompt_rows // SUBLANES
    n_prompt_frames = n_prompt_rows * S5_L

    def frame_slices():
        for l in range(S5_L):
            for seg in range(SUBLANES):
                yield l, pl.ds(seg * seg_rows * S5_L + l, seg_rows, stride=S5_L), \
                    pl.ds(seg, seg_rows, stride=SUBLANES)
            yield l, pl.ds(n_prompt_frames + l, n_sample_rows, stride=S5_L), \
                pl.ds(n_prompt_rows, n_sample_rows)

    for l, frames, rows in frame_slices():
        u2f_ref[l, rows, :] = x_ref[frames, :] * rstd_ref[frames, :] * g_ref[...]
    for l in range(S5_L):
        u2_ref[:, l * LANES:(l + 1) * LANES] = u2f_ref[l].astype(BF16)
    e_ref[...] = jnp.dot(u2_ref[...], we_ref[0], preferred_element_type=F32)

    tabs = [tab_ref[0, t] for t in range(TAB_ROWS)]
    m_tabs, (pr, pi, a8r, a8i, ar, ai), b_tabs = tabs[0:6], tabs[6:12], tabs[12:18]
    row = lax.broadcasted_iota(jnp.int32, (SUBLANES, ns), 0)

    def log_step_scan(xr, xi, t):
        for n, shift in enumerate((1, 2, 4)):
            tr, ti = t[2 * n], t[2 * n + 1]
            sr = pltpu.roll(xr, shift, axis=0)
            si = pltpu.roll(xi, shift, axis=0)
            xr, xi = xr + tr * sr - ti * si, xi + tr * si + ti * sr
        return xr, xi

    def shift_down(xr, xi, fr, fi):
        first = row == 0
        return (jnp.where(first, fr, pltpu.roll(xr, 1, axis=0)),
                jnp.where(first, fi, pltpu.roll(xi, 1, axis=0)))

    def rows_of(k):
        return pl.ds(pl.multiple_of(k * SUBLANES, SUBLANES), SUBLANES)

    def local_step(k, carry):
        sr, si = carry
        er = e_ref[rows_of(k), 0:ns]
        ei = e_ref[rows_of(k), ns:2 * ns]
        e_ref[rows_of(k), 0:ns] = sr
        e_ref[rows_of(k), ns:2 * ns] = si
        return ar * sr - ai * si + er, ar * si + ai * sr + ei

    zero = jnp.zeros((SUBLANES, ns), F32)
    ends = lax.fori_loop(0, seg_rows, local_step, (zero, zero), unroll=4)
    ends = log_step_scan(*ends, b_tabs)
    sfr_ref[...] = jnp.zeros_like(sfr_ref)
    sfi_ref[...] = jnp.zeros_like(sfi_ref)
    sfr_ref[n_streams:n_streams + 1, :] = ends[0][SUBLANES - 1:SUBLANES]
    sfi_ref[n_streams:n_streams + 1, :] = ends[1][SUBLANES - 1:SUBLANES]

    def correct_step(k, carry):
        cr, ci = carry
        e_ref[rows_of(k), 0:ns] += cr
        e_ref[rows_of(k), ns:2 * ns] += ci
        return ar * cr - ai * ci, ar * ci + ai * cr

    lax.fori_loop(0, seg_rows, correct_step, shift_down(*ends, zero, zero), unroll=4)

    for b in range(n_streams):
        rows = pl.ds(n_prompt_rows + b * SUBLANES, SUBLANES)
        cr = jnp.broadcast_to(s0r_ref[b:b + 1, :], (SUBLANES, ns))
        ci = jnp.broadcast_to(s0i_ref[b:b + 1, :], (SUBLANES, ns))
        xr, xi = log_step_scan(e_ref[rows, 0:ns], e_ref[rows, ns:2 * ns], m_tabs)
        st_r = xr + pr * cr - pi * ci
        st_i = xi + pr * ci + pi * cr
        e_ref[rows, 0:ns], e_ref[rows, ns:2 * ns] = shift_down(st_r, st_i, cr, ci)
        sfr_ref[b:b + 1, :] = st_r[SUBLANES - 1:SUBLANES]
        sfi_ref[b:b + 1, :] = st_i[SUBLANES - 1:SUBLANES]

    sprev = e_ref[...].astype(BF16)
    slabs_per_block = S5_WT_BLOCK // LANES
    for cb in range(S5_L * LANES // S5_WT_BLOCK):
        cols = slice(cb * S5_WT_BLOCK, (cb + 1) * S5_WT_BLOCK)
        k_end = (cb + 1) * S5_WT_BLOCK
        y2 = jnp.dot(u2_ref[:, 0:k_end], wt_ref[0, 0:k_end, cols], preferred_element_type=F32)
        y2 = y2 + lax.dot_general(sprev, wst_ref[0, cols, :], (((1,), (1,)), ((), ())),
                                  preferred_element_type=F32)
        for n in range(slabs_per_block):
            l = cb * slabs_per_block + n
            u2f_ref[l] = y2[:, n * LANES:(n + 1) * LANES] + d_ref[...] * u2f_ref[l]
    for l, frames, rows in frame_slices():
        y_ref[frames, :] = u2f_ref[l, rows, :]


def _s5_core(x, rstd, g, wt, we, wst, tab, s0_re, s0_im, d_skip, *, n_prompt_rows):
    m_rows, d = x.shape
    nb = d // LANES
    ns = S5_BLOCK_STATE
    k = S5_L * LANES
    n_rows = m_rows // S5_L
    n_streams = s0_re.shape[0]
    assert n_prompt_rows % (S5_L * SUBLANES) == 0
    assert (m_rows - n_prompt_rows) == n_streams * S5_L * SUBLANES
    sf_rows = n_streams + SUBLANES
    lane_blk = pl.BlockSpec((m_rows, LANES), lambda j: (0, j))
    wspec = pl.BlockSpec((1, k, 2 * ns), lambda j: (j, 0, 0))
    st = pl.BlockSpec((n_streams, ns), lambda j: (0, j))
    sf = pl.BlockSpec((sf_rows, ns), lambda j: (0, j))
    return pl.pallas_call(
        functools.partial(_s5_core_kernel, n_prompt_rows // S5_L, n_streams),
        grid=(nb,),
        in_specs=[
            lane_blk,
            _const_spec((m_rows, LANES)),
            pl.BlockSpec((1, LANES), lambda j: (0, j)),
            pl.BlockSpec((1, k, k), lambda j: (j, 0, 0)),
            wspec,
            wspec,
            pl.BlockSpec((1, TAB_ROWS, SUBLANES, ns), lambda j: (j, 0, 0, 0)),
            st, st,
            pl.BlockSpec((1, LANES), lambda j: (0, j)),
        ],
        out_specs=[lane_blk, sf, sf],
        out_shape=[
            jax.ShapeDtypeStruct((m_rows, d), F32),
            jax.ShapeDtypeStruct((sf_rows, nb * ns), F32),
            jax.ShapeDtypeStruct((sf_rows, nb * ns), F32),
        ],
        scratch_shapes=[
            pltpu.VMEM((S5_L, n_rows, LANES), F32),
            pltpu.VMEM((n_rows, k), BF16),
            pltpu.VMEM((n_rows, 2 * ns), F32),
        ],
        compiler_params=pltpu.CompilerParams(
            dimension_semantics=("arbitrary",), vmem_limit_bytes=VMEM_LIMIT_BYTES),
        name="s5_core",
    )(x, rstd, g, wt, we, wst, tab, s0_re, s0_im, d_skip)


def _glu_kernel(n_col_blocks, x_ref, y_ref, w_ref, out_ref):
    d = x_ref.shape[1]
    a = _gelu(y_ref[...]).astype(BF16)
    cb = d // n_col_blocks
    for n in range(n_col_blocks):
        cols = slice(n * cb, (n + 1) * cb)
        val = jnp.dot(a, w_ref[:, n * cb:(n + 1) * cb], preferred_element_type=F32)
        gate = jnp.dot(a, w_ref[:, d + n * cb:d + (n + 1) * cb], preferred_element_type=F32)
        out_ref[:, cols] = x_ref[:, cols] + val * jax.nn.sigmoid(gate)


def _glu(x, y, w, *, tm, n_col_blocks=4):
    m_rows, d = x.shape
    assert m_rows % tm == 0 and d % n_col_blocks == 0
    tile = pl.BlockSpec((tm, d), lambda i: (i, 0))
    return pl.pallas_call(
        functools.partial(_glu_kernel, n_col_blocks),
        grid=(m_rows // tm,),
        in_specs=[tile, tile, _const_spec(w.shape)],
        out_specs=tile,
        out_shape=jax.ShapeDtypeStruct((m_rows, d), F32),
        compiler_params=pltpu.CompilerParams(
            dimension_semantics=("arbitrary",), vmem_limit_bytes=VMEM_LIMIT_BYTES),
        name="glu",
    )(x, y, w)


def _attention_tables(sinks):
    slopes = jnp.exp2(-8.0 * jnp.arange(1, N_HEADS + 1, dtype=F32) / N_HEADS)
    frame = jnp.arange(CHUNK, dtype=F32)[None, :]
    band = jnp.arange(BAND, dtype=F32)[:, None]
    dist = jnp.abs(frame - (band - WINDOW))
    bias = (dist[:, None, :] * slopes[None, :, None]).reshape(BAND, N_HEADS * CHUNK)
    sink_row = jnp.repeat(sinks.astype(F32), CHUNK).reshape(1, N_HEADS * CHUNK)
    return bias, sink_row


def _gm_tables(gm_ws, gm_b, d_gm):
    blk = jnp.arange(GM_CHUNK) // CHUNK
    w_prompt = jnp.where((blk[:, None] >= blk[None, :])[None], gm_ws, 0.0)
    top = gm_ws[:, :CHUNK, :CHUNK]
    zeros = jnp.zeros_like(top)
    w_sample = jnp.concatenate(
        [jnp.concatenate([top, zeros], axis=2), jnp.concatenate([zeros, top], axis=2)], axis=1)
    wsp = jnp.stack([w_prompt, w_sample])
    wsp = wsp.reshape(2, N_GM_GROUPS // 2, 2, GM_CHUNK, GM_CHUNK).transpose(0, 1, 3, 2, 4)
    wsp = wsp.reshape(2, N_GM_GROUPS // 2, GM_CHUNK, 2 * GM_CHUNK).astype(BF16)
    b_prompt = gm_b.T
    b_sample = jnp.concatenate([gm_b[:, :CHUNK].T, gm_b[:, :CHUNK].T], axis=0)
    gmb = jnp.stack([b_prompt, b_sample]).astype(F32)
    gmb = jnp.repeat(gmb, d_gm // N_GM_GROUPS, axis=2)
    return wsp, gmb


def _s5_embed(b_re, b_im, c_re, c_im):
    n_groups = b_re.shape[0]
    nb = n_groups // S5_LANE_GROUPS
    state_group = jnp.arange(S5_BLOCK_STATE) // S5_STATE
    own = (state_group[None, :] == jnp.arange(S5_LANE_GROUPS)[:, None])[None, :, None, :]

    def place(t):
        t = jnp.broadcast_to(t[:, None], (nb, S5_LANE_GROUPS, S5_GROUP, S5_BLOCK_STATE))
        return jnp.where(own, t, 0.0).reshape(nb, LANES, S5_BLOCK_STATE)

    def emb_b(b):
        b = b.astype(F32).reshape(nb, S5_LANE_GROUPS, S5_STATE, S5_GROUP)
        return place(b.transpose(0, 3, 1, 2).reshape(nb, S5_GROUP, S5_BLOCK_STATE))

    def emb_c(c):
        c = c.astype(F32).reshape(nb, S5_LANE_GROUPS, S5_GROUP, S5_STATE)
        return place(c.transpose(0, 2, 1, 3).reshape(nb, S5_GROUP, S5_BLOCK_STATE))

    return emb_b(b_re), emb_b(b_im), emb_c(c_re), emb_c(c_im)


def kernel(x_prompt, x_sample, cache_swa_k, cache_swa_v, state_s5_re, state_s5_im, norm_mix, norm_ffn, norm_final, w_in0, attn_sinks, gm_norm, gm_ws, gm_b, w_out0, s5_lam_re, s5_lam_im, s5_log_dt, s5_b_re, s5_b_im, s5_c_re, s5_c_im, s5_d, s5_w_glu, ffn_w_gate, ffn_w_up, ffn_w_down):
    batch, seq, d = x_prompt.shape
    dec_batch, dec_seq, _ = x_sample.shape
    assert batch == 1 and dec_seq == CHUNK and norm_mix.shape[0] == 2
    n_prompt = batch * seq
    n_sample = dec_batch * dec_seq
    d_gm = gm_norm.shape[-1]

    bias_tbl, sink_row = _attention_tables(attn_sinks[0])
    wsp, gmb = _gm_tables(gm_ws[0], gm_b[0], d_gm)
    x1, k_all, v_all, gvn = _mix0(
        x_prompt.reshape(n_prompt, d), x_sample.reshape(n_sample, d),
        cache_swa_k[0].reshape(dec_batch * WINDOW, D_KV), cache_swa_v[0].reshape(dec_batch * WINDOW, D_KV),
        norm_mix[0].reshape(1, d), w_in0[0].astype(BF16), bias_tbl, sink_row,
        gm_norm[0].reshape(1, d_gm), wsp, gmb, w_out0[0].astype(BF16), units=2)

    x2, rstd2 = _ffn(x1, norm_ffn[0].reshape(1, d), ffn_w_gate, ffn_w_up, ffn_w_down,
                     norm_mix[1].reshape(1, d), layer=0, tm=FFN_TM, tf=FFN_TF, final=False)

    n_groups = s5_lam_re.shape[1]
    nb = n_groups // S5_LANE_GROUPS
    bre, bim, cre, cim = _s5_embed(s5_b_re[0], s5_b_im[0], s5_c_re[0], s5_c_im[0])
    wt, we, wst, tab = _s5_prep(
        s5_lam_re[0].reshape(nb, 1, S5_BLOCK_STATE), s5_lam_im[0].reshape(nb, 1, S5_BLOCK_STATE),
        jnp.repeat(s5_log_dt[0], S5_STATE).reshape(nb, 1, S5_BLOCK_STATE), bre, bim, cre, cim,
        seg_rows=n_prompt // (S5_L * SUBLANES))
    ys5, sf_re, sf_im = _s5_core(
        x2, rstd2, norm_mix[1].reshape(1, d), wt, we, wst, tab, state_s5_re[0].reshape(dec_batch, n_groups * S5_STATE),
        state_s5_im[0].reshape(dec_batch, n_groups * S5_STATE), s5_d[0].reshape(1, d),
        n_prompt_rows=n_prompt)
    x3 = _glu(x2, ys5, s5_w_glu[0].astype(BF16), tm=256)

    def last_ffn(row0, n_rows):
        (y,) = _ffn(x3, norm_ffn[1].reshape(1, d), ffn_w_gate, ffn_w_up, ffn_w_down,
                    norm_final.reshape(1, d), layer=1, tm=FFN_TM, tf=FFN_TF, final=True,
                    row0=row0, n_rows=n_rows)
        return y

    y_prompt = last_ffn(0, n_prompt)
    y_sample = last_ffn(n_prompt, n_sample)

    keep = min(WINDOW, seq)
    y_prompt = y_prompt.reshape(batch, seq, d)
    y_sample = y_sample.reshape(dec_batch, dec_seq, d)
    kv_shape_p = (1, batch, keep, N_KV_HEADS, HEAD_DIM)
    kv_shape_s = (1, dec_batch, dec_seq, N_KV_HEADS, HEAD_DIM)
    st_p = (1, batch, n_groups, S5_STATE)
    st_s = (1, dec_batch, n_groups, S5_STATE)
    return (y_prompt, y_sample,
            k_all[n_prompt - keep:n_prompt].reshape(kv_shape_p),
            v_all[n_prompt - keep:n_prompt].reshape(kv_shape_p),
            k_all[n_prompt:].reshape(kv_shape_s),
            v_all[n_prompt:].reshape(kv_shape_s),
            gvn.reshape(1, dec_batch, dec_seq, d_gm),
            sf_re[dec_batch].reshape(st_p), sf_im[dec_batch].reshape(st_p),
            sf_re[:dec_batch].reshape(st_s), sf_im[:dec_batch].reshape(st_s))
```

```python
import functools
import math

import jax
import jax.numpy as jnp
from jax import lax
from jax.experimental import pallas as pl
from jax.experimental.pallas import tpu as pltpu

F32 = jnp.float32
BF16 = jnp.bfloat16

CHUNK = 64
HEAD_DIM = 64
N_HEADS = 16
N_KV_HEADS = 2
Q_PER_KV = N_HEADS // N_KV_HEADS
WINDOW = 128
BAND = WINDOW + CHUNK
D_ATTN = N_HEADS * HEAD_DIM
D_KV = N_KV_HEADS * HEAD_DIM
GM_CHUNK = 128
N_GM_GROUPS = 16
S5_GROUP = 16
S5_STATE = 64
RMS_EPS = 1e-5
NEG_INF = -1e30

LANES = 128
SUBLANES = 8
VMEM_LIMIT_BYTES = 56 * 1024 * 1024

FFN_TM = 1024
FFN_TF = 256

S5_L = SUBLANES
S5_LANE_GROUPS = LANES // S5_GROUP
S5_BLOCK_STATE = S5_LANE_GROUPS * S5_STATE
S5_WT_BLOCK = 256


def _gelu(x):
    return 0.5 * x * (1.0 + lax.erf(x * math.sqrt(0.5)))


def _rms_scale(x):
    return x * lax.rsqrt(jnp.mean(x * x, axis=-1, keepdims=True) + RMS_EPS)


def _const_spec(shape):
    zeros = (0,) * len(shape)
    return pl.BlockSpec(shape, lambda *_: zeros, pipeline_mode=pl.Buffered(1))


def _mix0_kernel(n_prompt_tiles, units,
                 xp_ref, xs_ref, ck_ref, cv_ref, g_ref, win_ref, bias_ref, sink_ref, gmn_ref,
                 wsp_ref, gmb_ref, wout_ref,
                 x1_ref, k_ref, v_ref, gvn_ref,
                 z_ref, q_ref, ocat_ref, kprev_ref, vprev_ref):
    i = pl.program_id(0)
    is_sample = i >= n_prompt_tiles
    tm = units * GM_CHUNK
    d_gm = gmn_ref.shape[-1]
    off_k = D_ATTN
    off_v = D_ATTN + D_KV
    off_gu = D_ATTN + 2 * D_KV
    off_gv = off_gu + d_gm
    n_q = Q_PER_KV * CHUNK

    @pl.when(i == 0)
    def _():
        kprev_ref[...] = jnp.zeros_like(kprev_ref)
        vprev_ref[...] = jnp.zeros_like(vprev_ref)

    x = jnp.where(is_sample, xs_ref[...], xp_ref[...])
    h = (_rms_scale(x) * g_ref[...]).astype(BF16)
    z_ref[...] = jnp.dot(h, win_ref[...], preferred_element_type=F32)

    q_ref[...] = (z_ref[:, 0:D_ATTN] * (HEAD_DIM ** -0.5)).astype(BF16)
    k = z_ref[:, off_k:off_k + D_KV]
    v = z_ref[:, off_v:off_v + D_KV]
    k_ref[...] = k
    v_ref[...] = v

    def lane_lo(rows):
        return lax.broadcasted_iota(jnp.int32, (rows, LANES), 1) < HEAD_DIM

    def replicate(a):
        r = pltpu.roll(a, HEAD_DIM, axis=1)
        lo = lane_lo(a.shape[0])
        return jnp.where(lo, a, r).astype(BF16), jnp.where(lo, r, a).astype(BF16)

    k_rep = replicate(k)
    v_rep = replicate(v)
    ck_rep = replicate(ck_ref[...])
    cv_rep = replicate(cv_ref[...])

    lo64 = lane_lo(CHUNK)
    band_pos = lax.broadcasted_iota(jnp.int32, (BAND, n_q), 0)
    chunks_per_tile = tm // CHUNK

    for u in range(units):
        for c2 in range(GM_CHUNK // CHUNK):
            r0 = u * GM_CHUNK + c2 * CHUNK
            stream = r0 // CHUNK
            chunk_index = i * chunks_per_tile + stream
            valid_from = jnp.where(is_sample, 0, jnp.maximum(WINDOW - CHUNK * chunk_index, 0))
            valid = band_pos >= valid_from
            for kv in range(N_KV_HEADS):
                def band(cur, prev_ref, cached):
                    if u == 0:
                        prev_unit = prev_ref[kv]
                    else:
                        prev_unit = cur[(u - 1) * GM_CHUNK:u * GM_CHUNK]
                    if c2 == 0:
                        prompt_prev = prev_unit
                    else:
                        prompt_prev = jnp.concatenate(
                            [prev_unit[CHUNK:], cur[u * GM_CHUNK:u * GM_CHUNK + CHUNK]], axis=0)
                    sample_prev = cached[stream * WINDOW:(stream + 1) * WINDOW]
                    prev = jnp.where(is_sample, sample_prev, prompt_prev)
                    return jnp.concatenate([prev, cur[r0:r0 + CHUNK]], axis=0)

                kb = band(k_rep[kv], kprev_ref, ck_rep[kv])
                vb = band(v_rep[kv], vprev_ref, cv_rep[kv])

                pieces = []
                for m in range(Q_PER_KV // 2):
                    c0 = kv * Q_PER_KV * HEAD_DIM + m * LANES
                    qp = q_ref[r0:r0 + CHUNK, c0:c0 + LANES]
                    pieces.append(jnp.where(lo64, qp, jnp.zeros_like(qp)))
                    pieces.append(jnp.where(lo64, jnp.zeros_like(qp), qp))
                qs = jnp.concatenate(pieces, axis=0)
                cols = slice(kv * n_q, (kv + 1) * n_q)
                st = lax.dot_general(kb, qs, (((1,), (1,)), ((), ())),
                                     preferred_element_type=F32)
                st = jnp.where(valid, st - bias_ref[:, cols], NEG_INF)
                sink = sink_ref[:, cols]
                mx = jnp.maximum(jnp.max(st, axis=0, keepdims=True), sink)
                p = jnp.exp(st - mx)
                denom = jnp.sum(p, axis=0, keepdims=True) + jnp.exp(sink - mx)
                pn = (p * (1.0 / denom)).astype(BF16)
                o = lax.dot_general(pn, vb, (((0,), (0,)), ((), ())),
                                    preferred_element_type=F32)
                for m in range(Q_PER_KV // 2):
                    o_pair = jnp.where(lo64, o[(2 * m) * CHUNK:(2 * m + 1) * CHUNK],
                                       o[(2 * m + 1) * CHUNK:(2 * m + 2) * CHUNK])
                    c0 = kv * Q_PER_KV * HEAD_DIM + m * LANES
                    ocat_ref[r0:r0 + CHUNK, c0:c0 + LANES] = o_pair.astype(BF16)

    lo128 = lane_lo(GM_CHUNK)
    for u in range(units):
        rows = slice(u * GM_CHUNK, (u + 1) * GM_CHUNK)
        ua = _gelu(z_ref[rows, off_gu:off_gu + d_gm])
        gvn = _rms_scale(_gelu(z_ref[rows, off_gv:off_gv + d_gm])) * gmn_ref[...]
        gvn_ref[rows, :] = gvn
        gb = gvn.astype(BF16)
        for m in range(N_GM_GROUPS // 2):
            cols = slice(m * LANES, (m + 1) * LANES)
            rhs = gb[:, cols]
            rhs2 = jnp.concatenate([jnp.where(lo128, rhs, jnp.zeros_like(rhs)),
                                    jnp.where(lo128, jnp.zeros_like(rhs), rhs)], axis=0)
            sp = jnp.dot(wsp_ref[0, m], rhs2, preferred_element_type=F32) + gmb_ref[0, :, cols]
            ocat_ref[rows, D_ATTN + m * LANES:D_ATTN + (m + 1) * LANES] = (ua[:, cols] * sp).astype(BF16)

    x1_ref[...] = x + jnp.dot(ocat_ref[...], wout_ref[...], preferred_element_type=F32)

    for kv in range(N_KV_HEADS):
        kprev_ref[kv] = k_rep[kv][tm - GM_CHUNK:]
        vprev_ref[kv] = v_rep[kv][tm - GM_CHUNK:]


def _mix0(xp, xs, cache_k, cache_v, g, w_in, bias_tbl, sink_row, gm_norm, wsp, gmb, w_out, *, units):
    n_prompt_rows, d = xp.shape
    n_sample_rows = xs.shape[0]
    m_rows = n_prompt_rows + n_sample_rows
    tm = units * GM_CHUNK
    assert n_prompt_rows % tm == 0 and n_sample_rows % tm == 0
    n_tiles = m_rows // tm
    n_prompt_tiles = n_prompt_rows // tm
    d_in = w_in.shape[1]
    d_gm = gm_norm.shape[-1]
    cache_rows = (tm // CHUNK) * WINDOW

    def prompt_block(i):
        return jnp.minimum(i, n_prompt_tiles - 1)

    def sample_block(i):
        return jnp.maximum(i - n_prompt_tiles, 0)

    def kind(i):
        return jnp.where(i >= n_prompt_tiles, 1, 0)

    in_specs = [
        pl.BlockSpec((tm, d), lambda i: (prompt_block(i), 0)),
        pl.BlockSpec((tm, d), lambda i: (sample_block(i), 0)),
        pl.BlockSpec((cache_rows, D_KV), lambda i: (sample_block(i), 0)),
        pl.BlockSpec((cache_rows, D_KV), lambda i: (sample_block(i), 0)),
        _const_spec((1, d)),
        _const_spec((d, d_in)),
        _const_spec(bias_tbl.shape),
        _const_spec(sink_row.shape),
        _const_spec((1, d_gm)),
        pl.BlockSpec((1,) + wsp.shape[1:], lambda i: (kind(i), 0, 0, 0)),
        pl.BlockSpec((1, GM_CHUNK, d_gm), lambda i: (kind(i), 0, 0)),
        _const_spec(w_out.shape),
    ]
    out_specs = [
        pl.BlockSpec((tm, d), lambda i: (i, 0)),
        pl.BlockSpec((tm, D_KV), lambda i: (i, 0)),
        pl.BlockSpec((tm, D_KV), lambda i: (i, 0)),
        pl.BlockSpec((tm, d_gm), lambda i: (sample_block(i), 0)),
    ]
    out_shape = [
        jax.ShapeDtypeStruct((m_rows, d), F32),
        jax.ShapeDtypeStruct((m_rows, D_KV), F32),
        jax.ShapeDtypeStruct((m_rows, D_KV), F32),
        jax.ShapeDtypeStruct((n_sample_rows, d_gm), F32),
    ]
    scratch = [
        pltpu.VMEM((tm, d_in), F32),
        pltpu.VMEM((tm, D_ATTN), BF16),
        pltpu.VMEM((tm, D_ATTN + d_gm), BF16),
        pltpu.VMEM((N_KV_HEADS, GM_CHUNK, LANES), BF16),
        pltpu.VMEM((N_KV_HEADS, GM_CHUNK, LANES), BF16),
    ]
    return pl.pallas_call(
        functools.partial(_mix0_kernel, n_prompt_tiles, units),
        grid=(n_tiles,),
        in_specs=in_specs,
        out_specs=out_specs,
        out_shape=out_shape,
        scratch_shapes=scratch,
        compiler_params=pltpu.CompilerParams(
            dimension_semantics=("arbitrary",), vmem_limit_bytes=VMEM_LIMIT_BYTES),
        name="mix0",
    )(xp, xs, cache_k, cache_v, g, w_in, bias_tbl, sink_row, gm_norm, wsp, gmb, w_out)


def _ffn_kernel(final, x_ref, g_ref, wg_ref, wu_ref, wd_ref, gnext_ref, out_ref, *rest):
    h_ref = rest[-1]
    j = pl.program_id(1)

    @pl.when(j == 0)
    def _():
        x = x_ref[...]
        h_ref[...] = (_rms_scale(x) * g_ref[...]).astype(BF16)
        out_ref[...] = x

    h = h_ref[...]
    gate = jnp.dot(h, wg_ref[0].astype(BF16), preferred_element_type=F32)
    up = jnp.dot(h, wu_ref[0].astype(BF16), preferred_element_type=F32)
    act = (gate * jax.nn.sigmoid(gate) * up).astype(BF16)
    out_ref[...] += jnp.dot(act, wd_ref[0].astype(BF16), preferred_element_type=F32)

    @pl.when(j == pl.num_programs(1) - 1)
    def _():
        out = out_ref[...]
        scale = lax.rsqrt(jnp.mean(out * out, axis=-1, keepdims=True) + RMS_EPS)
        if final:
            out_ref[...] = out * scale * gnext_ref[...]
        else:
            rest[0][...] = jnp.broadcast_to(scale, rest[0].shape)


def _ffn(x, g, wg, wu, wd, gnext, *, layer, tm, tf, final, row0=0, n_rows=None):
    d = x.shape[1]
    n_rows = x.shape[0] if n_rows is None else n_rows
    f = wg.shape[2]
    assert n_rows % tm == 0 and row0 % tm == 0 and f % tf == 0
    tile0 = row0 // tm
    out_specs = [pl.BlockSpec((tm, d), lambda i, j: (i, 0))]
    out_shape = [jax.ShapeDtypeStruct((n_rows, d), F32)]
    if not final:
        out_specs.append(pl.BlockSpec((tm, LANES), lambda i, j: (i, 0)))
        out_shape.append(jax.ShapeDtypeStruct((n_rows, LANES), F32))
    return pl.pallas_call(
        functools.partial(_ffn_kernel, final),
        grid=(n_rows // tm, f // tf),
        in_specs=[
            pl.BlockSpec((tm, d), lambda i, j: (i + tile0, 0)),
            pl.BlockSpec((1, d), lambda i, j: (0, 0)),
            pl.BlockSpec((1, d, tf), lambda i, j: (layer, 0, j)),
            pl.BlockSpec((1, d, tf), lambda i, j: (layer, 0, j)),
            pl.BlockSpec((1, tf, d), lambda i, j: (layer, j, 0)),
            pl.BlockSpec((1, d), lambda i, j: (0, 0)),
        ],
        out_specs=out_specs,
        out_shape=out_shape,
        scratch_shapes=[pltpu.VMEM((tm, d), BF16)],
        compiler_params=pltpu.CompilerParams(
            dimension_semantics=("arbitrary", "arbitrary"), vmem_limit_bytes=VMEM_LIMIT_BYTES),
        name="ffn",
    )(x, g, wg, wu, wd, gnext)


TAB_ROWS = 18


def _s5_prep_kernel(seg_rows, lre_ref, lim_ref, ldt_ref, bre_ref, bim_ref, cre_ref, cim_ref,
                    wt_ref, we_ref, wst_ref, tab_ref, bhi_ref, blo_ref):
    ns = S5_BLOCK_STATE
    lr = lre_ref[0]
    li = lim_ref[0]
    dt = jnp.exp(ldt_ref[0])
    mag = jnp.exp(lr * dt)
    ar = mag * jnp.cos(li * dt)
    ai = mag * jnp.sin(li * dt)
    den = lr * lr + li * li
    nr = ar - 1.0
    fr = (nr * lr + ai * li) / den
    fi = (ai * lr - nr * li) / den
    b_re = bre_ref[0]
    b_im = bim_ref[0]
    bbr = fr * b_re - fi * b_im
    bbi = fr * b_im + fi * b_re
    c_re = cre_ref[0]
    c_im = cim_ref[0]

    def cmul(pr, pi, qr, qi):
        return pr * qr - pi * qi, pr * qi + pi * qr

    powers = [(jnp.ones_like(ar), jnp.zeros_like(ar))]
    for _ in range(S5_L):
        powers.append(cmul(*powers[-1], ar, ai))

    def split(a):
        hi = a.astype(BF16)
        return hi, (a - hi.astype(F32)).astype(BF16)

    for l in range(S5_L):
        pr, pi = powers[l]
        bkr, bki = cmul(pr, pi, bbr, bbi)
        rows = slice(l * LANES, (l + 1) * LANES)
        for part, (hi, lo) in ((0, split(bkr)), (1, split(bki))):
            bhi_ref[rows, part * ns:(part + 1) * ns] = hi
            blo_ref[rows, part * ns:(part + 1) * ns] = lo
        we_ref[0, (S5_L - 1 - l) * LANES:(S5_L - l) * LANES, 0:ns] = bkr.astype(BF16)
        we_ref[0, (S5_L - 1 - l) * LANES:(S5_L - l) * LANES, ns:2 * ns] = bki.astype(BF16)
        qr, qi = powers[l + 1]
        wst_ref[0, rows, 0:ns] = (c_re * qr - c_im * qi).astype(BF16)
        wst_ref[0, rows, ns:2 * ns] = (-c_re * qi - c_im * qr).astype(BF16)

    c_cat = jnp.concatenate([c_re, -c_im], axis=1)
    c_hi, c_lo = split(c_cat)
    c_parts = jnp.concatenate([c_hi, c_lo], axis=0)
    dims = (((1,), (1,)), ((), ()))
    d_hi = lax.dot_general(bhi_ref[...], c_parts, dims, preferred_element_type=F32)
    d_lo = lax.dot_general(blo_ref[...], c_parts, dims, preferred_element_type=F32)
    d_all = (d_hi[:, 0:LANES] + d_hi[:, LANES:2 * LANES] + d_lo[:, 0:LANES]).astype(BF16)

    zero_blk = jnp.zeros((LANES, LANES), BF16)
    for l in range(S5_L):
        for l2 in range(S5_L):
            blk = d_all[(l2 - l) * LANES:(l2 - l + 1) * LANES] if l2 >= l else zero_blk
            wt_ref[0, l * LANES:(l + 1) * LANES, l2 * LANES:(l2 + 1) * LANES] = blk

    row = lax.broadcasted_iota(jnp.int32, (SUBLANES, ns), 0)

    def bcast(a):
        return jnp.broadcast_to(a, (SUBLANES, ns))

    def log_step_tables(t0, base):
        cur = base
        for n, shift in enumerate((1, 2, 4)):
            tab_ref[0, t0 + 2 * n] = jnp.where(row >= shift, bcast(cur[0]), 0.0)
            tab_ref[0, t0 + 2 * n + 1] = jnp.where(row >= shift, bcast(cur[1]), 0.0)
            cur = cmul(*cur, *cur)
        return cur

    a1 = powers[S5_L]
    a8 = log_step_tables(0, a1)
    pr_tab = jnp.zeros((SUBLANES, ns), F32)
    pi_tab = jnp.zeros((SUBLANES, ns), F32)
    cur = a1
    for r in range(SUBLANES):
        pr_tab = jnp.where(row == r, bcast(cur[0]), pr_tab)
        pi_tab = jnp.where(row == r, bcast(cur[1]), pi_tab)
        cur = cmul(*cur, *a1)
    tab_ref[0, 6] = pr_tab
    tab_ref[0, 7] = pi_tab
    tab_ref[0, 8] = bcast(a8[0])
    tab_ref[0, 9] = bcast(a8[1])
    tab_ref[0, 10] = bcast(a1[0])
    tab_ref[0, 11] = bcast(a1[1])
    seg = a1
    for _ in range(seg_rows.bit_length() - 1):
        seg = cmul(*seg, *seg)
    log_step_tables(12, seg)


def _s5_prep(lam_re, lam_im, log_dt, b_re_emb, b_im_emb, c_re_emb, c_im_emb, *, seg_rows):
    assert seg_rows & (seg_rows - 1) == 0
    nb = b_re_emb.shape[0]
    ns = S5_BLOCK_STATE
    k = S5_L * LANES
    vec = pl.BlockSpec((1, 1, ns), lambda j: (j, 0, 0))
    emb = pl.BlockSpec((1, LANES, ns), lambda j: (j, 0, 0))
    wspec = pl.BlockSpec((1, k, 2 * ns), lambda j: (j, 0, 0))
    return pl.pallas_call(
        functools.partial(_s5_prep_kernel, seg_rows),
        grid=(nb,),
        in_specs=[vec, vec, vec, emb, emb, emb, emb],
        out_specs=[
            pl.BlockSpec((1, k, k), lambda j: (j, 0, 0)),
            wspec,
            wspec,
            pl.BlockSpec((1, TAB_ROWS, SUBLANES, ns), lambda j: (j, 0, 0, 0)),
        ],
        out_shape=[
            jax.ShapeDtypeStruct((nb, k, k), BF16),
            jax.ShapeDtypeStruct((nb, k, 2 * ns), BF16),
            jax.ShapeDtypeStruct((nb, k, 2 * ns), BF16),
            jax.ShapeDtypeStruct((nb, TAB_ROWS, SUBLANES, ns), F32),
        ],
        scratch_shapes=[pltpu.VMEM((k, 2 * ns), BF16), pltpu.VMEM((k, 2 * ns), BF16)],
        compiler_params=pltpu.CompilerParams(
            dimension_semantics=("arbitrary",), vmem_limit_bytes=VMEM_LIMIT_BYTES),
        name="s5_prep",
    )(lam_re, lam_im, log_dt, b_re_emb, b_im_emb, c_re_emb, c_im_emb)


def _s5_core_kernel(n_prompt_rows, n_streams,
                    x_ref, rstd_ref, g_ref, wt_ref, we_ref, wst_ref, tab_ref, s0r_ref, s0i_ref, d_ref,
                    y_ref, sfr_ref, sfi_ref,
                    u2f_ref, u2_ref, e_ref):
    ns = S5_BLOCK_STATE
    n_rows = x_ref.shape[0] // S5_L
    n_sample_rows = n_rows - n_prompt_rows
    seg_rows = n_prompt_rows // SUBLANES
    n_prompt_frames = n_prompt_rows * S5_L

    def frame_slices():
        for l in range(S5_L):
            for seg in range(SUBLANES):
                yield l, pl.ds(seg * seg_rows * S5_L + l, seg_rows, stride=S5_L), \
                    pl.ds(seg, seg_rows, stride=SUBLANES)
            yield l, pl.ds(n_prompt_frames + l, n_sample_rows, stride=S5_L), \
                pl.ds(n_prompt_rows, n_sample_rows)

    for l, frames, rows in frame_slices():
        u2f_ref[l, rows, :] = x_ref[frames, :] * rstd_ref[frames, :] * g_ref[...]
    for l in range(S5_L):
        u2_ref[:, l * LANES:(l + 1) * LANES] = u2f_ref[l].astype(BF16)
    e_ref[...] = jnp.dot(u2_ref[...], we_ref[0], preferred_element_type=F32)

    tabs = [tab_ref[0, t] for t in range(TAB_ROWS)]
    m_tabs, (pr, pi, a8r, a8i, ar, ai), b_tabs = tabs[0:6], tabs[6:12], tabs[12:18]
    row = lax.broadcasted_iota(jnp.int32, (SUBLANES, ns), 0)

    def log_step_scan(xr, xi, t):
        for n, shift in enumerate((1, 2, 4)):
            tr, ti = t[2 * n], t[2 * n + 1]
            sr = pltpu.roll(xr, shift, axis=0)
            si = pltpu.roll(xi, shift, axis=0)
            xr, xi = xr + tr * sr - ti * si, xi + tr * si + ti * sr
        return xr, xi

    def shift_down(xr, xi, fr, fi):
        first = row == 0
        return (jnp.where(first, fr, pltpu.roll(xr, 1, axis=0)),
                jnp.where(first, fi, pltpu.roll(xi, 1, axis=0)))

    def rows_of(k):
        return pl.ds(pl.multiple_of(k * SUBLANES, SUBLANES), SUBLANES)

    def local_step(k, carry):
        sr, si = carry
        er = e_ref[rows_of(k), 0:ns]
        ei = e_ref[rows_of(k), ns:2 * ns]
        e_ref[rows_of(k), 0:ns] = sr
        e_ref[rows_of(k), ns:2 * ns] = si
        return ar * sr - ai * si + er, ar * si + ai * sr + ei

    zero = jnp.zeros((SUBLANES, ns), F32)
    ends = lax.fori_loop(0, seg_rows, local_step, (zero, zero), unroll=True)
    ends = log_step_scan(*ends, b_tabs)
    sfr_ref[...] = jnp.zeros_like(sfr_ref)
    sfi_ref[...] = jnp.zeros_like(sfi_ref)
    sfr_ref[n_streams:n_streams + 1, :] = ends[0][SUBLANES - 1:SUBLANES]
    sfi_ref[n_streams:n_streams + 1, :] = ends[1][SUBLANES - 1:SUBLANES]

    def correct_step(k, carry):
        cr, ci = carry
        e_ref[rows_of(k), 0:ns] += cr
        e_ref[rows_of(k), ns:2 * ns] += ci
        return ar * cr - ai * ci, ar * ci + ai * cr

    lax.fori_loop(0, seg_rows, correct_step, shift_down(*ends, zero, zero), unroll=True)

    for b in range(n_streams):
        rows = pl.ds(n_prompt_rows + b * SUBLANES, SUBLANES)
        cr = jnp.broadcast_to(s0r_ref[b:b + 1, :], (SUBLANES, ns))
        ci = jnp.broadcast_to(s0i_ref[b:b + 1, :], (SUBLANES, ns))
        xr, xi = log_step_scan(e_ref[rows, 0:ns], e_ref[rows, ns:2 * ns], m_tabs)
        st_r = xr + pr * cr - pi * ci
        st_i = xi + pr * ci + pi * cr
        e_ref[rows, 0:ns], e_ref[rows, ns:2 * ns] = shift_down(st_r, st_i, cr, ci)
        sfr_ref[b:b + 1, :] = st_r[SUBLANES - 1:SUBLANES]
        sfi_ref[b:b + 1, :] = st_i[SUBLANES - 1:SUBLANES]

    sprev = e_ref[...].astype(BF16)
    slabs_per_block = S5_WT_BLOCK // LANES
    for cb in range(S5_L * LANES // S5_WT_BLOCK):
        cols = slice(cb * S5_WT_BLOCK, (cb + 1) * S5_WT_BLOCK)
        k_end = (cb + 1) * S5_WT_BLOCK
        y2 = jnp.dot(u2_ref[:, 0:k_end], wt_ref[0, 0:k_end, cols], preferred_element_type=F32)
        y2 = y2 + lax.dot_general(sprev, wst_ref[0, cols, :], (((1,), (1,)), ((), ())),
                                  preferred_element_type=F32)
        for n in range(slabs_per_block):
            l = cb * slabs_per_block + n
            u2f_ref[l] = y2[:, n * LANES:(n + 1) * LANES] + d_ref[...] * u2f_ref[l]
    for l, frames, rows in frame_slices():
        y_ref[frames, :] = u2f_ref[l, rows, :]


def _s5_core(x, rstd, g, wt, we, wst, tab, s0_re, s0_im, d_skip, *, n_prompt_rows):
    m_rows, d = x.shape
    nb = d // LANES
    ns = S5_BLOCK_STATE
    k = S5_L * LANES
    n_rows = m_rows // S5_L
    n_streams = s0_re.shape[0]
    assert n_prompt_rows % (S5_L * SUBLANES) == 0
    assert (m_rows - n_prompt_rows) == n_streams * S5_L * SUBLANES
    sf_rows = n_streams + SUBLANES
    lane_blk = pl.BlockSpec((m_rows, LANES), lambda j: (0, j))
    wspec = pl.BlockSpec((1, k, 2 * ns), lambda j: (j, 0, 0))
    st = pl.BlockSpec((n_streams, ns), lambda j: (0, j))
    sf = pl.BlockSpec((sf_rows, ns), lambda j: (0, j))
    return pl.pallas_call(
        functools.partial(_s5_core_kernel, n_prompt_rows // S5_L, n_streams),
        grid=(nb,),
        in_specs=[
            lane_blk,
            _const_spec((m_rows, LANES)),
            pl.BlockSpec((1, LANES), lambda j: (0, j)),
            pl.BlockSpec((1, k, k), lambda j: (j, 0, 0)),
            wspec,
            wspec,
            pl.BlockSpec((1, TAB_ROWS, SUBLANES, ns), lambda j: (j, 0, 0, 0)),
            st, st,
            pl.BlockSpec((1, LANES), lambda j: (0, j)),
        ],
        out_specs=[lane_blk, sf, sf],
        out_shape=[
            jax.ShapeDtypeStruct((m_rows, d), F32),
            jax.ShapeDtypeStruct((sf_rows, nb * ns), F32),
            jax.ShapeDtypeStruct((sf_rows, nb * ns), F32),
        ],
        scratch_shapes=[
            pltpu.VMEM((S5_L, n_rows, LANES), F32),
            pltpu.VMEM((n_rows, k), BF16),
            pltpu.VMEM((n_rows, 2 * ns), F32),
        ],
        compiler_params=pltpu.CompilerParams(
            dimension_semantics=("arbitrary",), vmem_limit_bytes=VMEM_LIMIT_BYTES),
        name="s5_core",
    )(x, rstd, g, wt, we, wst, tab, s0_re, s0_im, d_skip)


def _glu_kernel(n_col_blocks, x_ref, y_ref, w_ref, out_ref):
    d = x_ref.shape[1]
    a = _gelu(y_ref[...]).astype(BF16)
    cb = d // n_col_blocks
    for n in range(n_col_blocks):
        cols = slice(n * cb, (n + 1) * cb)
        val = jnp.dot(a, w_ref[:, n * cb:(n + 1) * cb], preferred_element_type=F32)
        gate = jnp.dot(a, w_ref[:, d + n * cb:d + (n + 1) * cb], preferred_element_type=F32)
        out_ref[:, cols] = x_ref[:, cols] + val * jax.nn.sigmoid(gate)


def _glu(x, y, w, *, tm, n_col_blocks=4):
    m_rows, d = x.shape
    assert m_rows % tm == 0 and d % n_col_blocks == 0
    tile = pl.BlockSpec((tm, d), lambda i: (i, 0))
    return pl.pallas_call(
        functools.partial(_glu_kernel, n_col_blocks),
        grid=(m_rows // tm,),
        in_specs=[tile, tile, _const_spec(w.shape)],
        out_specs=tile,
        out_shape=jax.ShapeDtypeStruct((m_rows, d), F32),
        compiler_params=pltpu.CompilerParams(
            dimension_semantics=("arbitrary",), vmem_limit_bytes=VMEM_LIMIT_BYTES),
        name="glu",
    )(x, y, w)


def _attention_tables(sinks):
    slopes = jnp.exp2(-8.0 * jnp.arange(1, N_HEADS + 1, dtype=F32) / N_HEADS)
    frame = jnp.arange(CHUNK, dtype=F32)[None, :]
    band = jnp.arange(BAND, dtype=F32)[:, None]
    dist = jnp.abs(frame - (band - WINDOW))
    bias = (dist[:, None, :] * slopes[None, :, None]).reshape(BAND, N_HEADS * CHUNK)
    sink_row = jnp.repeat(sinks.astype(F32), CHUNK).reshape(1, N_HEADS * CHUNK)
    return bias, sink_row


def _gm_tables(gm_ws, gm_b, d_gm):
    blk = jnp.arange(GM_CHUNK) // CHUNK
    w_prompt = jnp.where((blk[:, None] >= blk[None, :])[None], gm_ws, 0.0)
    top = gm_ws[:, :CHUNK, :CHUNK]
    zeros = jnp.zeros_like(top)
    w_sample = jnp.concatenate(
        [jnp.concatenate([top, zeros], axis=2), jnp.concatenate([zeros, top], axis=2)], axis=1)
    wsp = jnp.stack([w_prompt, w_sample])
    wsp = wsp.reshape(2, N_GM_GROUPS // 2, 2, GM_CHUNK, GM_CHUNK).transpose(0, 1, 3, 2, 4)
    wsp = wsp.reshape(2, N_GM_GROUPS // 2, GM_CHUNK, 2 * GM_CHUNK).astype(BF16)
    b_prompt = gm_b.T
    b_sample = jnp.concatenate([gm_b[:, :CHUNK].T, gm_b[:, :CHUNK].T], axis=0)
    gmb = jnp.stack([b_prompt, b_sample]).astype(F32)
    gmb = jnp.repeat(gmb, d_gm // N_GM_GROUPS, axis=2)
    return wsp, gmb


def _s5_embed(b_re, b_im, c_re, c_im):
    n_groups = b_re.shape[0]
    nb = n_groups // S5_LANE_GROUPS
    state_group = jnp.arange(S5_BLOCK_STATE) // S5_STATE
    own = (state_group[None, :] == jnp.arange(S5_LANE_GROUPS)[:, None])[None, :, None, :]

    def place(t):
        t = jnp.broadcast_to(t[:, None], (nb, S5_LANE_GROUPS, S5_GROUP, S5_BLOCK_STATE))
        return jnp.where(own, t, 0.0).reshape(nb, LANES, S5_BLOCK_STATE)

    def emb_b(b):
        b = b.astype(F32).reshape(nb, S5_LANE_GROUPS, S5_STATE, S5_GROUP)
        return place(b.transpose(0, 3, 1, 2).reshape(nb, S5_GROUP, S5_BLOCK_STATE))

    def emb_c(c):
        c = c.astype(F32).reshape(nb, S5_LANE_GROUPS, S5_GROUP, S5_STATE)
        return place(c.transpose(0, 2, 1, 3).reshape(nb, S5_GROUP, S5_BLOCK_STATE))

    return emb_b(b_re), emb_b(b_im), emb_c(c_re), emb_c(c_im)


def kernel(x_prompt, x_sample, cache_swa_k, cache_swa_v, state_s5_re, state_s5_im, norm_mix, norm_ffn, norm_final, w_in0, attn_sinks, gm_norm, gm_ws, gm_b, w_out0, s5_lam_re, s5_lam_im, s5_log_dt, s5_b_re, s5_b_im, s5_c_re, s5_c_im, s5_d, s5_w_glu, ffn_w_gate, ffn_w_up, ffn_w_down):
    batch, seq, d = x_prompt.shape
    dec_batch, dec_seq, _ = x_sample.shape
    assert batch == 1 and dec_seq == CHUNK and norm_mix.shape[0] == 2
    n_prompt = batch * seq
    n_sample = dec_batch * dec_seq
    d_gm = gm_norm.shape[-1]

    bias_tbl, sink_row = _attention_tables(attn_sinks[0])
    wsp, gmb = _gm_tables(gm_ws[0], gm_b[0], d_gm)
    x1, k_all, v_all, gvn = _mix0(
        x_prompt.reshape(n_prompt, d), x_sample.reshape(n_sample, d),
        cache_swa_k[0].reshape(dec_batch * WINDOW, D_KV), cache_swa_v[0].reshape(dec_batch * WINDOW, D_KV),
        norm_mix[0].reshape(1, d), w_in0[0].astype(BF16), bias_tbl, sink_row,
        gm_norm[0].reshape(1, d_gm), wsp, gmb, w_out0[0].astype(BF16), units=2)

    x2, rstd2 = _ffn(x1, norm_ffn[0].reshape(1, d), ffn_w_gate, ffn_w_up, ffn_w_down,
                     norm_mix[1].reshape(1, d), layer=0, tm=FFN_TM, tf=FFN_TF, final=False)

    n_groups = s5_lam_re.shape[1]
    nb = n_groups // S5_LANE_GROUPS
    bre, bim, cre, cim = _s5_embed(s5_b_re[0], s5_b_im[0], s5_c_re[0], s5_c_im[0])
    wt, we, wst, tab = _s5_prep(
        s5_lam_re[0].reshape(nb, 1, S5_BLOCK_STATE), s5_lam_im[0].reshape(nb, 1, S5_BLOCK_STATE),
        jnp.repeat(s5_log_dt[0], S5_STATE).reshape(nb, 1, S5_BLOCK_STATE), bre, bim, cre, cim,
        seg_rows=n_prompt // (S5_L * SUBLANES))
    ys5, sf_re, sf_im = _s5_core(
        x2, rstd2, norm_mix[1].reshape(1, d), wt, we, wst, tab, state_s5_re[0].reshape(dec_batch, n_groups * S5_STATE),
        state_s5_im[0].reshape(dec_batch, n_groups * S5_STATE), s5_d[0].reshape(1, d),
        n_prompt_rows=n_prompt)
    x3 = _glu(x2, ys5, s5_w_glu[0].astype(BF16), tm=256)

    def last_ffn(row0, n_rows):
        (y,) = _ffn(x3, norm_ffn[1].reshape(1, d), ffn_w_gate, ffn_w_up, ffn_w_down,
                    norm_final.reshape(1, d), layer=1, tm=FFN_TM, tf=FFN_TF, final=True,
                    row0=row0, n_rows=n_rows)
        return y

    y_prompt = last_ffn(0, n_prompt)
    y_sample = last_ffn(n_prompt, n_sample)

    keep = min(WINDOW, seq)
    y_prompt = y_prompt.reshape(batch, seq, d)
    y_sample = y_sample.reshape(dec_batch, dec_seq, d)
    kv_shape_p = (1, batch, keep, N_KV_HEADS, HEAD_DIM)
    kv_shape_s = (1, dec_batch, dec_seq, N_KV_HEADS, HEAD_DIM)
    st_p = (1, batch, n_groups, S5_STATE)
    st_s = (1, dec_batch, n_groups, S5_STATE)
    return (y_prompt, y_sample,
            k_all[n_prompt - keep:n_prompt].reshape(kv_shape_p),
            v_all[n_prompt - keep:n_prompt].reshape(kv_shape_p),
            k_all[n_prompt:].reshape(kv_shape_s),
            v_all[n_prompt:].reshape(kv_shape_s),
            gvn.reshape(1, dec_batch, dec_seq, d_gm),
            sf_re[dec_batch].reshape(st_p), sf_im[dec_batch].reshape(st_p),
            sf_re[:dec_batch].reshape(st_s), sf_im[:dec_batch].reshape(st_s))
```

```python
import functools
import math

import jax
import jax.numpy as jnp
from jax import lax
from jax.experimental import pallas as pl
from jax.experimental.pallas import tpu as pltpu

F32 = jnp.float32
BF16 = jnp.bfloat16

CHUNK = 64
HEAD_DIM = 64
N_HEADS = 16
N_KV_HEADS = 2
Q_PER_KV = N_HEADS // N_KV_HEADS
WINDOW = 128
BAND = WINDOW + CHUNK
D_ATTN = N_HEADS * HEAD_DIM
D_KV = N_KV_HEADS * HEAD_DIM
GM_CHUNK = 128
N_GM_GROUPS = 16
S5_GROUP = 16
S5_STATE = 64
RMS_EPS = 1e-5
NEG_INF = -1e30

LANES = 128
SUBLANES = 8
VMEM_LIMIT_BYTES = 56 * 1024 * 1024

FFN_TM = 1024
FFN_TF = 256

S5_L = SUBLANES
S5_LANE_GROUPS = LANES // S5_GROUP
S5_BLOCK_STATE = S5_LANE_GROUPS * S5_STATE
S5_WT_BLOCK = 256


def _gelu(x):
    return 0.5 * x * (1.0 + lax.erf(x * math.sqrt(0.5)))


def _rms_scale(x):
    return x * lax.rsqrt(jnp.mean(x * x, axis=-1, keepdims=True) + RMS_EPS)


def _const_spec(shape):
    zeros = (0,) * len(shape)
    return pl.BlockSpec(shape, lambda *_: zeros, pipeline_mode=pl.Buffered(1))


def _mix0_kernel(n_prompt_tiles, units,
                 xp_ref, xs_ref, ck_ref, cv_ref, g_ref, win_ref, bias_ref, sink_ref, gmn_ref,
                 wsp_ref, gmb_ref, wout_ref,
                 x1_ref, k_ref, v_ref, gvn_ref,
                 z_ref, q_ref, ocat_ref, kprev_ref, vprev_ref):
    i = pl.program_id(0)
    is_sample = i >= n_prompt_tiles
    tm = units * GM_CHUNK
    d_gm = gmn_ref.shape[-1]
    off_k = D_ATTN
    off_v = D_ATTN + D_KV
    off_gu = D_ATTN + 2 * D_KV
    off_gv = off_gu + d_gm
    n_q = Q_PER_KV * CHUNK

    @pl.when(i == 0)
    def _():
        kprev_ref[...] = jnp.zeros_like(kprev_ref)
        vprev_ref[...] = jnp.zeros_like(vprev_ref)

    x = jnp.where(is_sample, xs_ref[...], xp_ref[...])
    h = (_rms_scale(x) * g_ref[...]).astype(BF16)
    z_ref[...] = jnp.dot(h, win_ref[...], preferred_element_type=F32)

    q_ref[...] = (z_ref[:, 0:D_ATTN] * (HEAD_DIM ** -0.5)).astype(BF16)
    k = z_ref[:, off_k:off_k + D_KV]
    v = z_ref[:, off_v:off_v + D_KV]
    k_ref[...] = k
    v_ref[...] = v

    def lane_lo(rows):
        return lax.broadcasted_iota(jnp.int32, (rows, LANES), 1) < HEAD_DIM

    def replicate(a):
        r = pltpu.roll(a, HEAD_DIM, axis=1)
        lo = lane_lo(a.shape[0])
        return jnp.where(lo, a, r).astype(BF16), jnp.where(lo, r, a).astype(BF16)

    k_rep = replicate(k)
    v_rep = replicate(v)
    ck_rep = replicate(ck_ref[...])
    cv_rep = replicate(cv_ref[...])

    lo64 = lane_lo(CHUNK)
    band_pos = lax.broadcasted_iota(jnp.int32, (BAND, n_q), 0)
    chunks_per_tile = tm // CHUNK

    for u in range(units):
        for c2 in range(GM_CHUNK // CHUNK):
            r0 = u * GM_CHUNK + c2 * CHUNK
            stream = r0 // CHUNK
            chunk_index = i * chunks_per_tile + stream
            valid_from = jnp.where(is_sample, 0, jnp.maximum(WINDOW - CHUNK * chunk_index, 0))
            valid = band_pos >= valid_from
            for kv in range(N_KV_HEADS):
                def band(cur, prev_ref, cached):
                    if u == 0:
                        prev_unit = prev_ref[kv]
                    else:
                        prev_unit = cur[(u - 1) * GM_CHUNK:u * GM_CHUNK]
                    if c2 == 0:
                        prompt_prev = prev_unit
                    else:
                        prompt_prev = jnp.concatenate(
                            [prev_unit[CHUNK:], cur[u * GM_CHUNK:u * GM_CHUNK + CHUNK]], axis=0)
                    sample_prev = cached[stream * WINDOW:(stream + 1) * WINDOW]
                    prev = jnp.where(is_sample, sample_prev, prompt_prev)
                    return jnp.concatenate([prev, cur[r0:r0 + CHUNK]], axis=0)

                kb = band(k_rep[kv], kprev_ref, ck_rep[kv])
                vb = band(v_rep[kv], vprev_ref, cv_rep[kv])

                pieces = []
                for m in range(Q_PER_KV // 2):
                    c0 = kv * Q_PER_KV * HEAD_DIM + m * LANES
                    qp = q_ref[r0:r0 + CHUNK, c0:c0 + LANES]
                    pieces.append(jnp.where(lo64, qp, jnp.zeros_like(qp)))
                    pieces.append(jnp.where(lo64, jnp.zeros_like(qp), qp))
                qs = jnp.concatenate(pieces, axis=0)
                cols = slice(kv * n_q, (kv + 1) * n_q)
                st = lax.dot_general(kb, qs, (((1,), (1,)), ((), ())),
                                     preferred_element_type=F32)
                st = jnp.where(valid, st - bias_ref[:, cols], NEG_INF)
                sink = sink_ref[:, cols]
                mx = jnp.maximum(jnp.max(st, axis=0, keepdims=True), sink)
                p = jnp.exp(st - mx)
                denom = jnp.sum(p, axis=0, keepdims=True) + jnp.exp(sink - mx)
                pn = (p * (1.0 / denom)).astype(BF16)
                o = lax.dot_general(pn, vb, (((0,), (0,)), ((), ())),
                                    preferred_element_type=F32)
                for m in range(Q_PER_KV // 2):
                    o_pair = jnp.where(lo64, o[(2 * m) * CHUNK:(2 * m + 1) * CHUNK],
                                       o[(2 * m + 1) * CHUNK:(2 * m + 2) * CHUNK])
                    c0 = kv * Q_PER_KV * HEAD_DIM + m * LANES
                    ocat_ref[r0:r0 + CHUNK, c0:c0 + LANES] = o_pair.astype(BF16)

    lo128 = lane_lo(GM_CHUNK)
    for u in range(units):
        rows = slice(u * GM_CHUNK, (u + 1) * GM_CHUNK)
        ua = _gelu(z_ref[rows, off_gu:off_gu + d_gm])
        gvn = _rms_scale(_gelu(z_ref[rows, off_gv:off_gv + d_gm])) * gmn_ref[...]
        gvn_ref[rows, :] = gvn
        gb = gvn.astype(BF16)
        for m in range(N_GM_GROUPS // 2):
            cols = slice(m * LANES, (m + 1) * LANES)
            rhs = gb[:, cols]
            rhs2 = jnp.concatenate([jnp.where(lo128, rhs, jnp.zeros_like(rhs)),
                                    jnp.where(lo128, jnp.zeros_like(rhs), rhs)], axis=0)
            sp = jnp.dot(wsp_ref[0, m], rhs2, preferred_element_type=F32) + gmb_ref[0, :, cols]
            ocat_ref[rows, D_ATTN + m * LANES:D_ATTN + (m + 1) * LANES] = (ua[:, cols] * sp).astype(BF16)

    x1_ref[...] = x + jnp.dot(ocat_ref[...], wout_ref[...], preferred_element_type=F32)

    for kv in range(N_KV_HEADS):
        kprev_ref[kv] = k_rep[kv][tm - GM_CHUNK:]
        vprev_ref[kv] = v_rep[kv][tm - GM_CHUNK:]


def _mix0(xp, xs, cache_k, cache_v, g, w_in, bias_tbl, sink_row, gm_norm, wsp, gmb, w_out, *, units):
    n_prompt_rows, d = xp.shape
    n_sample_rows = xs.shape[0]
    m_rows = n_prompt_rows + n_sample_rows
    tm = units * GM_CHUNK
    assert n_prompt_rows % tm == 0 and n_sample_rows % tm == 0
    n_tiles = m_rows // tm
    n_prompt_tiles = n_prompt_rows // tm
    d_in = w_in.shape[1]
    d_gm = gm_norm.shape[-1]
    cache_rows = (tm // CHUNK) * WINDOW

    def prompt_block(i):
        return jnp.minimum(i, n_prompt_tiles - 1)

    def sample_block(i):
        return jnp.maximum(i - n_prompt_tiles, 0)

    def kind(i):
        return jnp.where(i >= n_prompt_tiles, 1, 0)

    in_specs = [
        pl.BlockSpec((tm, d), lambda i: (prompt_block(i), 0)),
        pl.BlockSpec((tm, d), lambda i: (sample_block(i), 0)),
        pl.BlockSpec((cache_rows, D_KV), lambda i: (sample_block(i), 0)),
        pl.BlockSpec((cache_rows, D_KV), lambda i: (sample_block(i), 0)),
        _const_spec((1, d)),
        _const_spec((d, d_in)),
        _const_spec(bias_tbl.shape),
        _const_spec(sink_row.shape),
        _const_spec((1, d_gm)),
        pl.BlockSpec((1,) + wsp.shape[1:], lambda i: (kind(i), 0, 0, 0)),
        pl.BlockSpec((1, GM_CHUNK, d_gm), lambda i: (kind(i), 0, 0)),
        _const_spec(w_out.shape),
    ]
    out_specs = [
        pl.BlockSpec((tm, d), lambda i: (i, 0)),
        pl.BlockSpec((tm, D_KV), lambda i: (i, 0)),
        pl.BlockSpec((tm, D_KV), lambda i: (i, 0)),
        pl.BlockSpec((tm, d_gm), lambda i: (sample_block(i), 0)),
    ]
    out_shape = [
        jax.ShapeDtypeStruct((m_rows, d), F32),
        jax.ShapeDtypeStruct((m_rows, D_KV), F32),
        jax.ShapeDtypeStruct((m_rows, D_KV), F32),
        jax.ShapeDtypeStruct((n_sample_rows, d_gm), F32),
    ]
    scratch = [
        pltpu.VMEM((tm, d_in), F32),
        pltpu.VMEM((tm, D_ATTN), BF16),
        pltpu.VMEM((tm, D_ATTN + d_gm), BF16),
        pltpu.VMEM((N_KV_HEADS, GM_CHUNK, LANES), BF16),
        pltpu.VMEM((N_KV_HEADS, GM_CHUNK, LANES), BF16),
    ]
    return pl.pallas_call(
        functools.partial(_mix0_kernel, n_prompt_tiles, units),
        grid=(n_tiles,),
        in_specs=in_specs,
        out_specs=out_specs,
        out_shape=out_shape,
        scratch_shapes=scratch,
        compiler_params=pltpu.CompilerParams(
            dimension_semantics=("arbitrary",), vmem_limit_bytes=VMEM_LIMIT_BYTES),
        name="mix0",
    )(xp, xs, cache_k, cache_v, g, w_in, bias_tbl, sink_row, gm_norm, wsp, gmb, w_out)


def _ffn_kernel(final, x_ref, g_ref, wg_ref, wu_ref, wd_ref, gnext_ref, out_ref, *rest):
    h_ref = rest[-1]
    j = pl.program_id(1)
    last = pl.num_programs(1) - 1

    def hidden_step(h, base):
        gate = jnp.dot(h, wg_ref[0].astype(BF16), preferred_element_type=F32)
        up = jnp.dot(h, wu_ref[0].astype(BF16), preferred_element_type=F32)
        act = (gate * jax.nn.sigmoid(gate) * up).astype(BF16)
        return base + jnp.dot(act, wd_ref[0].astype(BF16), preferred_element_type=F32)

    @pl.when(j == 0)
    def _():
        x = x_ref[...]
        h = (_rms_scale(x) * g_ref[...]).astype(BF16)
        h_ref[...] = h
        out_ref[...] = hidden_step(h, x)

    @pl.when(jnp.logical_and(j > 0, j < last))
    def _():
        out_ref[...] = hidden_step(h_ref[...], out_ref[...])

    @pl.when(j == last)
    def _():
        out = hidden_step(h_ref[...], out_ref[...])
        scale = lax.rsqrt(jnp.mean(out * out, axis=-1, keepdims=True) + RMS_EPS)
        if final:
            out_ref[...] = out * scale * gnext_ref[...]
        else:
            out_ref[...] = out
            rest[0][...] = jnp.broadcast_to(scale, rest[0].shape)


def _ffn(x, g, wg, wu, wd, gnext, *, layer, tm, tf, final, row0=0, n_rows=None):
    d = x.shape[1]
    n_rows = x.shape[0] if n_rows is None else n_rows
    f = wg.shape[2]
    assert n_rows % tm == 0 and row0 % tm == 0 and f % tf == 0
    tile0 = row0 // tm
    out_specs = [pl.BlockSpec((tm, d), lambda i, j: (i, 0))]
    out_shape = [jax.ShapeDtypeStruct((n_rows, d), F32)]
    if not final:
        out_specs.append(pl.BlockSpec((tm, LANES), lambda i, j: (i, 0)))
        out_shape.append(jax.ShapeDtypeStruct((n_rows, LANES), F32))
    return pl.pallas_call(
        functools.partial(_ffn_kernel, final),
        grid=(n_rows // tm, f // tf),
        in_specs=[
            pl.BlockSpec((tm, d), lambda i, j: (i + tile0, 0)),
            pl.BlockSpec((1, d), lambda i, j: (0, 0)),
            pl.BlockSpec((1, d, tf), lambda i, j: (layer, 0, j)),
            pl.BlockSpec((1, d, tf), lambda i, j: (layer, 0, j)),
            pl.BlockSpec((1, tf, d), lambda i, j: (layer, j, 0)),
            pl.BlockSpec((1, d), lambda i, j: (0, 0)),
        ],
        out_specs=out_specs,
        out_shape=out_shape,
        scratch_shapes=[pltpu.VMEM((tm, d), BF16)],
        compiler_params=pltpu.CompilerParams(
            dimension_semantics=("arbitrary", "arbitrary"), vmem_limit_bytes=VMEM_LIMIT_BYTES),
        name="ffn",
    )(x, g, wg, wu, wd, gnext)


TAB_ROWS = 18


def _s5_prep_kernel(seg_rows, lre_ref, lim_ref, ldt_ref, bre_ref, bim_ref, cre_ref, cim_ref,
                    wt_ref, we_ref, wst_ref, tab_ref, bhi_ref, blo_ref):
    ns = S5_BLOCK_STATE
    lr = lre_ref[0]
    li = lim_ref[0]
    dt = jnp.exp(ldt_ref[0])
    mag = jnp.exp(lr * dt)
    ar = mag * jnp.cos(li * dt)
    ai = mag * jnp.sin(li * dt)
    den = lr * lr + li * li
    nr = ar - 1.0
    fr = (nr * lr + ai * li) / den
    fi = (ai * lr - nr * li) / den
    b_re = bre_ref[0]
    b_im = bim_ref[0]
    bbr = fr * b_re - fi * b_im
    bbi = fr * b_im + fi * b_re
    c_re = cre_ref[0]
    c_im = cim_ref[0]

    def cmul(pr, pi, qr, qi):
        return pr * qr - pi * qi, pr * qi + pi * qr

    powers = [(jnp.ones_like(ar), jnp.zeros_like(ar))]
    for _ in range(S5_L):
        powers.append(cmul(*powers[-1], ar, ai))

    def split(a):
        hi = a.astype(BF16)
        return hi, (a - hi.astype(F32)).astype(BF16)

    for l in range(S5_L):
        pr, pi = powers[l]
        bkr, bki = cmul(pr, pi, bbr, bbi)
        rows = slice(l * LANES, (l + 1) * LANES)
        for part, (hi, lo) in ((0, split(bkr)), (1, split(bki))):
            bhi_ref[rows, part * ns:(part + 1) * ns] = hi
            blo_ref[rows, part * ns:(part + 1) * ns] = lo
        we_ref[0, (S5_L - 1 - l) * LANES:(S5_L - l) * LANES, 0:ns] = bkr.astype(BF16)
        we_ref[0, (S5_L - 1 - l) * LANES:(S5_L - l) * LANES, ns:2 * ns] = bki.astype(BF16)
        qr, qi = powers[l + 1]
        wst_ref[0, rows, 0:ns] = (c_re * qr - c_im * qi).astype(BF16)
        wst_ref[0, rows, ns:2 * ns] = (-c_re * qi - c_im * qr).astype(BF16)

    c_cat = jnp.concatenate([c_re, -c_im], axis=1)
    c_hi, c_lo = split(c_cat)
    c_parts = jnp.concatenate([c_hi, c_lo], axis=0)
    dims = (((1,), (1,)), ((), ()))
    d_hi = lax.dot_general(bhi_ref[...], c_parts, dims, preferred_element_type=F32)
    d_lo = lax.dot_general(blo_ref[...], c_parts, dims, preferred_element_type=F32)
    d_all = (d_hi[:, 0:LANES] + d_hi[:, LANES:2 * LANES] + d_lo[:, 0:LANES]).astype(BF16)

    zero_blk = jnp.zeros((LANES, LANES), BF16)
    for l in range(S5_L):
        for l2 in range(S5_L):
            blk = d_all[(l2 - l) * LANES:(l2 - l + 1) * LANES] if l2 >= l else zero_blk
            wt_ref[0, l * LANES:(l + 1) * LANES, l2 * LANES:(l2 + 1) * LANES] = blk

    row = lax.broadcasted_iota(jnp.int32, (SUBLANES, ns), 0)

    def bcast(a):
        return jnp.broadcast_to(a, (SUBLANES, ns))

    def log_step_tables(t0, base):
        cur = base
        for n, shift in enumerate((1, 2, 4)):
            tab_ref[0, t0 + 2 * n] = jnp.where(row >= shift, bcast(cur[0]), 0.0)
            tab_ref[0, t0 + 2 * n + 1] = jnp.where(row >= shift, bcast(cur[1]), 0.0)
            cur = cmul(*cur, *cur)
        return cur

    a1 = powers[S5_L]
    a8 = log_step_tables(0, a1)
    pr_tab = jnp.zeros((SUBLANES, ns), F32)
    pi_tab = jnp.zeros((SUBLANES, ns), F32)
    cur = a1
    for r in range(SUBLANES):
        pr_tab = jnp.where(row == r, bcast(cur[0]), pr_tab)
        pi_tab = jnp.where(row == r, bcast(cur[1]), pi_tab)
        cur = cmul(*cur, *a1)
    tab_ref[0, 6] = pr_tab
    tab_ref[0, 7] = pi_tab
    tab_ref[0, 8] = bcast(a8[0])
    tab_ref[0, 9] = bcast(a8[1])
    tab_ref[0, 10] = bcast(a1[0])
    tab_ref[0, 11] = bcast(a1[1])
    seg = a1
    for _ in range(seg_rows.bit_length() - 1):
        seg = cmul(*seg, *seg)
    log_step_tables(12, seg)


def _s5_prep(lam_re, lam_im, log_dt, b_re_emb, b_im_emb, c_re_emb, c_im_emb, *, seg_rows):
    assert seg_rows & (seg_rows - 1) == 0
    nb = b_re_emb.shape[0]
    ns = S5_BLOCK_STATE
    k = S5_L * LANES
    vec = pl.BlockSpec((1, 1, ns), lambda j: (j, 0, 0))
    emb = pl.BlockSpec((1, LANES, ns), lambda j: (j, 0, 0))
    wspec = pl.BlockSpec((1, k, 2 * ns), lambda j: (j, 0, 0))
    return pl.pallas_call(
        functools.partial(_s5_prep_kernel, seg_rows),
        grid=(nb,),
        in_specs=[vec, vec, vec, emb, emb, emb, emb],
        out_specs=[
            pl.BlockSpec((1, k, k), lambda j: (j, 0, 0)),
            wspec,
            wspec,
            pl.BlockSpec((1, TAB_ROWS, SUBLANES, ns), lambda j: (j, 0, 0, 0)),
        ],
        out_shape=[
            jax.ShapeDtypeStruct((nb, k, k), BF16),
            jax.ShapeDtypeStruct((nb, k, 2 * ns), BF16),
            jax.ShapeDtypeStruct((nb, k, 2 * ns), BF16),
            jax.ShapeDtypeStruct((nb, TAB_ROWS, SUBLANES, ns), F32),
        ],
        scratch_shapes=[pltpu.VMEM((k, 2 * ns), BF16), pltpu.VMEM((k, 2 * ns), BF16)],
        compiler_params=pltpu.CompilerParams(
            dimension_semantics=("arbitrary",), vmem_limit_bytes=VMEM_LIMIT_BYTES),
        name="s5_prep",
    )(lam_re, lam_im, log_dt, b_re_emb, b_im_emb, c_re_emb, c_im_emb)


def _s5_core_kernel(n_prompt_rows, n_streams,
                    x_ref, rstd_ref, g_ref, wt_ref, we_ref, wst_ref, tab_ref, s0r_ref, s0i_ref, d_ref,
                    y_ref, sfr_ref, sfi_ref,
                    u2f_ref, u2_ref, e_ref):
    ns = S5_BLOCK_STATE
    n_rows = x_ref.shape[0] // S5_L
    n_sample_rows = n_rows - n_prompt_rows
    seg_rows = n_prompt_rows // SUBLANES
    n_prompt_frames = n_prompt_rows * S5_L

    def frame_slices():
        for l in range(S5_L):
            for seg in range(SUBLANES):
                yield l, pl.ds(seg * seg_rows * S5_L + l, seg_rows, stride=S5_L), \
                    pl.ds(seg, seg_rows, stride=SUBLANES)
            yield l, pl.ds(n_prompt_frames + l, n_sample_rows, stride=S5_L), \
                pl.ds(n_prompt_rows, n_sample_rows)

    for l, frames, rows in frame_slices():
        u2f_ref[l, rows, :] = x_ref[frames, :] * rstd_ref[frames, :] * g_ref[...]
    for l in range(S5_L):
        u2_ref[:, l * LANES:(l + 1) * LANES] = u2f_ref[l].astype(BF16)
    e_ref[...] = jnp.dot(u2_ref[...], we_ref[0], preferred_element_type=F32)

    tabs = [tab_ref[0, t] for t in range(TAB_ROWS)]
    m_tabs, (pr, pi, a8r, a8i, ar, ai), b_tabs = tabs[0:6], tabs[6:12], tabs[12:18]
    row = lax.broadcasted_iota(jnp.int32, (SUBLANES, ns), 0)

    def log_step_scan(xr, xi, t):
        for n, shift in enumerate((1, 2, 4)):
            tr, ti = t[2 * n], t[2 * n + 1]
            sr = pltpu.roll(xr, shift, axis=0)
            si = pltpu.roll(xi, shift, axis=0)
            xr, xi = xr + tr * sr - ti * si, xi + tr * si + ti * sr
        return xr, xi

    def shift_down(xr, xi, fr, fi):
        first = row == 0
        return (jnp.where(first, fr, pltpu.roll(xr, 1, axis=0)),
                jnp.where(first, fi, pltpu.roll(xi, 1, axis=0)))

    def rows_of(k):
        return pl.ds(pl.multiple_of(k * SUBLANES, SUBLANES), SUBLANES)

    def local_step(k, carry):
        sr, si = carry
        er = e_ref[rows_of(k), 0:ns]
        ei = e_ref[rows_of(k), ns:2 * ns]
        e_ref[rows_of(k), 0:ns] = sr
        e_ref[rows_of(k), ns:2 * ns] = si
        return ar * sr - ai * si + er, ar * si + ai * sr + ei

    zero = jnp.zeros((SUBLANES, ns), F32)
    ends = lax.fori_loop(0, seg_rows, local_step, (zero, zero), unroll=True)
    ends = log_step_scan(*ends, b_tabs)
    sfr_ref[...] = jnp.zeros_like(sfr_ref)
    sfi_ref[...] = jnp.zeros_like(sfi_ref)
    sfr_ref[n_streams:n_streams + 1, :] = ends[0][SUBLANES - 1:SUBLANES]
    sfi_ref[n_streams:n_streams + 1, :] = ends[1][SUBLANES - 1:SUBLANES]

    def correct_step(k, carry):
        cr, ci = carry
        e_ref[rows_of(k), 0:ns] += cr
        e_ref[rows_of(k), ns:2 * ns] += ci
        return ar * cr - ai * ci, ar * ci + ai * cr

    lax.fori_loop(0, seg_rows, correct_step, shift_down(*ends, zero, zero), unroll=True)

    for b in range(n_streams):
        rows = pl.ds(n_prompt_rows + b * SUBLANES, SUBLANES)
        cr = jnp.broadcast_to(s0r_ref[b:b + 1, :], (SUBLANES, ns))
        ci = jnp.broadcast_to(s0i_ref[b:b + 1, :], (SUBLANES, ns))
        xr, xi = log_step_scan(e_ref[rows, 0:ns], e_ref[rows, ns:2 * ns], m_tabs)
        st_r = xr + pr * cr - pi * ci
        st_i = xi + pr * ci + pi * cr
        e_ref[rows, 0:ns], e_ref[rows, ns:2 * ns] = shift_down(st_r, st_i, cr, ci)
        sfr_ref[b:b + 1, :] = st_r[SUBLANES - 1:SUBLANES]
        sfi_ref[b:b + 1, :] = st_i[SUBLANES - 1:SUBLANES]

    sprev = e_ref[...].astype(BF16)
    slabs_per_block = S5_WT_BLOCK // LANES
    for cb in range(S5_L * LANES // S5_WT_BLOCK):
        cols = slice(cb * S5_WT_BLOCK, (cb + 1) * S5_WT_BLOCK)
        k_end = (cb + 1) * S5_WT_BLOCK
        y2 = jnp.dot(u2_ref[:, 0:k_end], wt_ref[0, 0:k_end, cols], preferred_element_type=F32)
        y2 = y2 + lax.dot_general(sprev, wst_ref[0, cols, :], (((1,), (1,)), ((), ())),
                                  preferred_element_type=F32)
        for n in range(slabs_per_block):
            l = cb * slabs_per_block + n
            u2f_ref[l] = y2[:, n * LANES:(n + 1) * LANES] + d_ref[...] * u2f_ref[l]
    for l, frames, rows in frame_slices():
        y_ref[frames, :] = u2f_ref[l, rows, :]


def _s5_core(x, rstd, g, wt, we, wst, tab, s0_re, s0_im, d_skip, *, n_prompt_rows):
    m_rows, d = x.shape
    nb = d // LANES
    ns = S5_BLOCK_STATE
    k = S5_L * LANES
    n_rows = m_rows // S5_L
    n_streams = s0_re.shape[0]
    assert n_prompt_rows % (S5_L * SUBLANES) == 0
    assert (m_rows - n_prompt_rows) == n_streams * S5_L * SUBLANES
    sf_rows = n_streams + SUBLANES
    lane_blk = pl.BlockSpec((m_rows, LANES), lambda j: (0, j))
    wspec = pl.BlockSpec((1, k, 2 * ns), lambda j: (j, 0, 0))
    st = pl.BlockSpec((n_streams, ns), lambda j: (0, j))
    sf = pl.BlockSpec((sf_rows, ns), lambda j: (0, j))
    return pl.pallas_call(
        functools.partial(_s5_core_kernel, n_prompt_rows // S5_L, n_streams),
        grid=(nb,),
        in_specs=[
            lane_blk,
            _const_spec((m_rows, LANES)),
            pl.BlockSpec((1, LANES), lambda j: (0, j)),
            pl.BlockSpec((1, k, k), lambda j: (j, 0, 0)),
            wspec,
            wspec,
            pl.BlockSpec((1, TAB_ROWS, SUBLANES, ns), lambda j: (j, 0, 0, 0)),
            st, st,
            pl.BlockSpec((1, LANES), lambda j: (0, j)),
        ],
        out_specs=[lane_blk, sf, sf],
        out_shape=[
            jax.ShapeDtypeStruct((m_rows, d), F32),
            jax.ShapeDtypeStruct((sf_rows, nb * ns), F32),
            jax.ShapeDtypeStruct((sf_rows, nb * ns), F32),
        ],
        scratch_shapes=[
            pltpu.VMEM((S5_L, n_rows, LANES), F32),
            pltpu.VMEM((n_rows, k), BF16),
            pltpu.VMEM((n_rows, 2 * ns), F32),
        ],
        compiler_params=pltpu.CompilerParams(
            dimension_semantics=("arbitrary",), vmem_limit_bytes=VMEM_LIMIT_BYTES),
        name="s5_core",
    )(x, rstd, g, wt, we, wst, tab, s0_re, s0_im, d_skip)


def _glu_kernel(n_col_blocks, x_ref, y_ref, w_ref, out_ref):
    d = x_ref.shape[1]
    a = _gelu(y_ref[...]).astype(BF16)
    cb = d // n_col_blocks
    for n in range(n_col_blocks):
        cols = slice(n * cb, (n + 1) * cb)
        val = jnp.dot(a, w_ref[:, n * cb:(n + 1) * cb], preferred_element_type=F32)
        gate = jnp.dot(a, w_ref[:, d + n * cb:d + (n + 1) * cb], preferred_element_type=F32)
        out_ref[:, cols] = x_ref[:, cols] + val * jax.nn.sigmoid(gate)


def _glu(x, y, w, *, tm, n_col_blocks=4):
    m_rows, d = x.shape
    assert m_rows % tm == 0 and d % n_col_blocks == 0
    tile = pl.BlockSpec((tm, d), lambda i: (i, 0))
    return pl.pallas_call(
        functools.partial(_glu_kernel, n_col_blocks),
        grid=(m_rows // tm,),
        in_specs=[tile, tile, _const_spec(w.shape)],
        out_specs=tile,
        out_shape=jax.ShapeDtypeStruct((m_rows, d), F32),
        compiler_params=pltpu.CompilerParams(
            dimension_semantics=("arbitrary",), vmem_limit_bytes=VMEM_LIMIT_BYTES),
        name="glu",
    )(x, y, w)


def _attention_tables(sinks):
    slopes = jnp.exp2(-8.0 * jnp.arange(1, N_HEADS + 1, dtype=F32) / N_HEADS)
    frame = jnp.arange(CHUNK, dtype=F32)[None, :]
    band = jnp.arange(BAND, dtype=F32)[:, None]
    dist = jnp.abs(frame - (band - WINDOW))
    bias = (dist[:, None, :] * slopes[None, :, None]).reshape(BAND, N_HEADS * CHUNK)
    sink_row = jnp.repeat(sinks.astype(F32), CHUNK).reshape(1, N_HEADS * CHUNK)
    return bias, sink_row


def _gm_tables(gm_ws, gm_b, d_gm):
    blk = jnp.arange(GM_CHUNK) // CHUNK
    w_prompt = jnp.where((blk[:, None] >= blk[None, :])[None], gm_ws, 0.0)
    top = gm_ws[:, :CHUNK, :CHUNK]
    zeros = jnp.zeros_like(top)
    w_sample = jnp.concatenate(
        [jnp.concatenate([top, zeros], axis=2), jnp.concatenate([zeros, top], axis=2)], axis=1)
    wsp = jnp.stack([w_prompt, w_sample])
    wsp = wsp.reshape(2, N_GM_GROUPS // 2, 2, GM_CHUNK, GM_CHUNK).transpose(0, 1, 3, 2, 4)
    wsp = wsp.reshape(2, N_GM_GROUPS // 2, GM_CHUNK, 2 * GM_CHUNK).astype(BF16)
    b_prompt = gm_b.T
    b_sample = jnp.concatenate([gm_b[:, :CHUNK].T, gm_b[:, :CHUNK].T], axis=0)
    gmb = jnp.stack([b_prompt, b_sample]).astype(F32)
    gmb = jnp.repeat(gmb, d_gm // N_GM_GROUPS, axis=2)
    return wsp, gmb


def _s5_embed(b_re, b_im, c_re, c_im):
    n_groups = b_re.shape[0]
    nb = n_groups // S5_LANE_GROUPS
    state_group = jnp.arange(S5_BLOCK_STATE) // S5_STATE
    own = (state_group[None, :] == jnp.arange(S5_LANE_GROUPS)[:, None])[None, :, None, :]

    def place(t):
        t = jnp.broadcast_to(t[:, None], (nb, S5_LANE_GROUPS, S5_GROUP, S5_BLOCK_STATE))
        return jnp.where(own, t, 0.0).reshape(nb, LANES, S5_BLOCK_STATE)

    def emb_b(b):
        b = b.astype(F32).reshape(nb, S5_LANE_GROUPS, S5_STATE, S5_GROUP)
        return place(b.transpose(0, 3, 1, 2).reshape(nb, S5_GROUP, S5_BLOCK_STATE))

    def emb_c(c):
        c = c.astype(F32).reshape(nb, S5_LANE_GROUPS, S5_GROUP, S5_STATE)
        return place(c.transpose(0, 2, 1, 3).reshape(nb, S5_GROUP, S5_BLOCK_STATE))

    return emb_b(b_re), emb_b(b_im), emb_c(c_re), emb_c(c_im)


def kernel(x_prompt, x_sample, cache_swa_k, cache_swa_v, state_s5_re, state_s5_im, norm_mix, norm_ffn, norm_final, w_in0, attn_sinks, gm_norm, gm_ws, gm_b, w_out0, s5_lam_re, s5_lam_im, s5_log_dt, s5_b_re, s5_b_im, s5_c_re, s5_c_im, s5_d, s5_w_glu, ffn_w_gate, ffn_w_up, ffn_w_down):
    batch, seq, d = x_prompt.shape
    dec_batch, dec_seq, _ = x_sample.shape
    assert batch == 1 and dec_seq == CHUNK and norm_mix.shape[0] == 2
    n_prompt = batch * seq
    n_sample = dec_batch * dec_seq
    d_gm = gm_norm.shape[-1]

    bias_tbl, sink_row = _attention_tables(attn_sinks[0])
    wsp, gmb = _gm_tables(gm_ws[0], gm_b[0], d_gm)
    x1, k_all, v_all, gvn = _mix0(
        x_prompt.reshape(n_prompt, d), x_sample.reshape(n_sample, d),
        cache_swa_k[0].reshape(dec_batch * WINDOW, D_KV), cache_swa_v[0].reshape(dec_batch * WINDOW, D_KV),
        norm_mix[0].reshape(1, d), w_in0[0].astype(BF16), bias_tbl, sink_row,
        gm_norm[0].reshape(1, d_gm), wsp, gmb, w_out0[0].astype(BF16), units=2)

    x2, rstd2 = _ffn(x1, norm_ffn[0].reshape(1, d), ffn_w_gate, ffn_w_up, ffn_w_down,
                     norm_mix[1].reshape(1, d), layer=0, tm=FFN_TM, tf=FFN_TF, final=False)

    n_groups = s5_lam_re.shape[1]
    nb = n_groups // S5_LANE_GROUPS
    bre, bim, cre, cim = _s5_embed(s5_b_re[0], s5_b_im[0], s5_c_re[0], s5_c_im[0])
    wt, we, wst, tab = _s5_prep(
        s5_lam_re[0].reshape(nb, 1, S5_BLOCK_STATE), s5_lam_im[0].reshape(nb, 1, S5_BLOCK_STATE),
        jnp.repeat(s5_log_dt[0], S5_STATE).reshape(nb, 1, S5_BLOCK_STATE), bre, bim, cre, cim,
        seg_rows=n_prompt // (S5_L * SUBLANES))
    ys5, sf_re, sf_im = _s5_core(
        x2, rstd2, norm_mix[1].reshape(1, d), wt, we, wst, tab, state_s5_re[0].reshape(dec_batch, n_groups * S5_STATE),
        state_s5_im[0].reshape(dec_batch, n_groups * S5_STATE), s5_d[0].reshape(1, d),
        n_prompt_rows=n_prompt)
    x3 = _glu(x2, ys5, s5_w_glu[0].astype(BF16), tm=256)

    def last_ffn(row0, n_rows):
        (y,) = _ffn(x3, norm_ffn[1].reshape(1, d), ffn_w_gate, ffn_w_up, ffn_w_down,
                    norm_final.reshape(1, d), layer=1, tm=FFN_TM, tf=FFN_TF, final=True,
                    row0=row0, n_rows=n_rows)
        return y

    y_prompt = last_ffn(0, n_prompt)
    y_sample = last_ffn(n_prompt, n_sample)

    keep = min(WINDOW, seq)
    y_prompt = y_prompt.reshape(batch, seq, d)
    y_sample = y_sample.reshape(dec_batch, dec_seq, d)
    kv_shape_p = (1, batch, keep, N_KV_HEADS, HEAD_DIM)
    kv_shape_s = (1, dec_batch, dec_seq, N_KV_HEADS, HEAD_DIM)
    st_p = (1, batch, n_groups, S5_STATE)
    st_s = (1, dec_batch, n_groups, S5_STATE)
    return (y_prompt, y_sample,
            k_all[n_prompt - keep:n_prompt].reshape(kv_shape_p),
            v_all[n_prompt - keep:n_prompt].reshape(kv_shape_p),
            k_all[n_prompt:].reshape(kv_shape_s),
            v_all[n_prompt:].reshape(kv_shape_s),
            gvn.reshape(1, dec_batch, dec_seq, d_gm),
            sf_re[dec_batch].reshape(st_p), sf_im[dec_batch].reshape(st_p),
            sf_re[:dec_batch].reshape(st_s), sf_im[:dec_batch].reshape(st_s))
```

```python
import functools
import math

import jax
import jax.numpy as jnp
from jax import lax
from jax.experimental import pallas as pl
from jax.experimental.pallas import tpu as pltpu

F32 = jnp.float32
BF16 = jnp.bfloat16

CHUNK = 64
HEAD_DIM = 64
N_HEADS = 16
N_KV_HEADS = 2
Q_PER_KV = N_HEADS // N_KV_HEADS
WINDOW = 128
BAND = WINDOW + CHUNK
D_ATTN = N_HEADS * HEAD_DIM
D_KV = N_KV_HEADS * HEAD_DIM
GM_CHUNK = 128
N_GM_GROUPS = 16
S5_GROUP = 16
S5_STATE = 64
RMS_EPS = 1e-5
NEG_INF = -1e30

LANES = 128
SUBLANES = 8
VMEM_LIMIT_BYTES = 56 * 1024 * 1024

FFN_TM = 1024
FFN_TF = 256

S5_L = SUBLANES
S5_LANE_GROUPS = LANES // S5_GROUP
S5_BLOCK_STATE = S5_LANE_GROUPS * S5_STATE
S5_WT_BLOCK = 256


def _gelu(x):
    return 0.5 * x * (1.0 + lax.erf(x * math.sqrt(0.5)))


def _rms_scale(x):
    return x * lax.rsqrt(jnp.mean(x * x, axis=-1, keepdims=True) + RMS_EPS)


def _const_spec(shape):
    zeros = (0,) * len(shape)
    return pl.BlockSpec(shape, lambda *_: zeros, pipeline_mode=pl.Buffered(1))


def _mix0_kernel(n_prompt_tiles, units,
                 xp_ref, xs_ref, ck_ref, cv_ref, g_ref, win_ref, bias_ref, sink_ref, gmn_ref,
                 wsp_ref, gmb_ref, wout_ref,
                 x1_ref, k_ref, v_ref, gvn_ref,
                 z_ref, q_ref, ocat_ref, kprev_ref, vprev_ref):
    i = pl.program_id(0)
    is_sample = i >= n_prompt_tiles
    tm = units * GM_CHUNK
    d_gm = gmn_ref.shape[-1]
    off_k = D_ATTN
    off_v = D_ATTN + D_KV
    off_gu = D_ATTN + 2 * D_KV
    off_gv = off_gu + d_gm
    n_q = Q_PER_KV * CHUNK

    @pl.when(i == 0)
    def _():
        kprev_ref[...] = jnp.zeros_like(kprev_ref)
        vprev_ref[...] = jnp.zeros_like(vprev_ref)

    x = jnp.where(is_sample, xs_ref[...], xp_ref[...])
    h = (_rms_scale(x) * g_ref[...]).astype(BF16)
    z_ref[...] = jnp.dot(h, win_ref[...], preferred_element_type=F32)

    q_ref[...] = (z_ref[:, 0:D_ATTN] * (HEAD_DIM ** -0.5)).astype(BF16)
    k = z_ref[:, off_k:off_k + D_KV]
    v = z_ref[:, off_v:off_v + D_KV]
    k_ref[...] = k
    v_ref[...] = v

    def lane_lo(rows):
        return lax.broadcasted_iota(jnp.int32, (rows, LANES), 1) < HEAD_DIM

    def replicate(a):
        r = pltpu.roll(a, HEAD_DIM, axis=1)
        lo = lane_lo(a.shape[0])
        return jnp.where(lo, a, r).astype(BF16), jnp.where(lo, r, a).astype(BF16)

    k_rep = replicate(k)
    v_rep = replicate(v)
    ck_rep = replicate(ck_ref[...])
    cv_rep = replicate(cv_ref[...])

    lo64 = lane_lo(CHUNK)
    band_pos = lax.broadcasted_iota(jnp.int32, (BAND, n_q), 0)
    chunks_per_tile = tm // CHUNK

    for u in range(units):
        for c2 in range(GM_CHUNK // CHUNK):
            r0 = u * GM_CHUNK + c2 * CHUNK
            stream = r0 // CHUNK
            chunk_index = i * chunks_per_tile + stream
            valid_from = jnp.where(is_sample, 0, jnp.maximum(WINDOW - CHUNK * chunk_index, 0))
            valid = band_pos >= valid_from
            for kv in range(N_KV_HEADS):
                def band(cur, prev_ref, cached):
                    if u == 0:
                        prev_unit = prev_ref[kv]
                    else:
                        prev_unit = cur[(u - 1) * GM_CHUNK:u * GM_CHUNK]
                    if c2 == 0:
                        prompt_prev = prev_unit
                    else:
                        prompt_prev = jnp.concatenate(
                            [prev_unit[CHUNK:], cur[u * GM_CHUNK:u * GM_CHUNK + CHUNK]], axis=0)
                    sample_prev = cached[stream * WINDOW:(stream + 1) * WINDOW]
                    prev = jnp.where(is_sample, sample_prev, prompt_prev)
                    return jnp.concatenate([prev, cur[r0:r0 + CHUNK]], axis=0)

                kb = band(k_rep[kv], kprev_ref, ck_rep[kv])
                vb = band(v_rep[kv], vprev_ref, cv_rep[kv])

                pieces = []
                for m in range(Q_PER_KV // 2):
                    c0 = kv * Q_PER_KV * HEAD_DIM + m * LANES
                    qp = q_ref[r0:r0 + CHUNK, c0:c0 + LANES]
                    pieces.append(jnp.where(lo64, qp, jnp.zeros_like(qp)))
                    pieces.append(jnp.where(lo64, jnp.zeros_like(qp), qp))
                qs = jnp.concatenate(pieces, axis=0)
                cols = slice(kv * n_q, (kv + 1) * n_q)
                st = lax.dot_general(kb, qs, (((1,), (1,)), ((), ())),
                                     preferred_element_type=F32)
                st = jnp.where(valid, st - bias_ref[:, cols], NEG_INF)
                sink = sink_ref[:, cols]
                mx = jnp.maximum(jnp.max(st, axis=0, keepdims=True), sink)
                p = jnp.exp(st - mx)
                denom = jnp.sum(p, axis=0, keepdims=True) + jnp.exp(sink - mx)
                pn = (p * (1.0 / denom)).astype(BF16)
                o = lax.dot_general(pn, vb, (((0,), (0,)), ((), ())),
                                    preferred_element_type=F32)
                for m in range(Q_PER_KV // 2):
                    o_pair = jnp.where(lo64, o[(2 * m) * CHUNK:(2 * m + 1) * CHUNK],
                                       o[(2 * m + 1) * CHUNK:(2 * m + 2) * CHUNK])
                    c0 = kv * Q_PER_KV * HEAD_DIM + m * LANES
                    ocat_ref[r0:r0 + CHUNK, c0:c0 + LANES] = o_pair.astype(BF16)

    lo128 = lane_lo(GM_CHUNK)
    for u in range(units):
        rows = slice(u * GM_CHUNK, (u + 1) * GM_CHUNK)
        ua = _gelu(z_ref[rows, off_gu:off_gu + d_gm])
        gvn = _rms_scale(_gelu(z_ref[rows, off_gv:off_gv + d_gm])) * gmn_ref[...]
        gvn_ref[rows, :] = gvn
        gb = gvn.astype(BF16)
        for m in range(N_GM_GROUPS // 2):
            cols = slice(m * LANES, (m + 1) * LANES)
            rhs = gb[:, cols]
            rhs2 = jnp.concatenate([jnp.where(lo128, rhs, jnp.zeros_like(rhs)),
                                    jnp.where(lo128, jnp.zeros_like(rhs), rhs)], axis=0)
            sp = jnp.dot(wsp_ref[0, m], rhs2, preferred_element_type=F32) + gmb_ref[0, :, cols]
            ocat_ref[rows, D_ATTN + m * LANES:D_ATTN + (m + 1) * LANES] = (ua[:, cols] * sp).astype(BF16)

    x1_ref[...] = x + jnp.dot(ocat_ref[...], wout_ref[...], preferred_element_type=F32)

    for kv in range(N_KV_HEADS):
        kprev_ref[kv] = k_rep[kv][tm - GM_CHUNK:]
        vprev_ref[kv] = v_rep[kv][tm - GM_CHUNK:]


def _mix0(xp, xs, cache_k, cache_v, g, w_in, bias_tbl, sink_row, gm_norm, wsp, gmb, w_out, *, units):
    n_prompt_rows, d = xp.shape
    n_sample_rows = xs.shape[0]
    m_rows = n_prompt_rows + n_sample_rows
    tm = units * GM_CHUNK
    assert n_prompt_rows % tm == 0 and n_sample_rows % tm == 0
    n_tiles = m_rows // tm
    n_prompt_tiles = n_prompt_rows // tm
    d_in = w_in.shape[1]
    d_gm = gm_norm.shape[-1]
    cache_rows = (tm // CHUNK) * WINDOW

    def prompt_block(i):
        return jnp.minimum(i, n_prompt_tiles - 1)

    def sample_block(i):
        return jnp.maximum(i - n_prompt_tiles, 0)

    def kind(i):
        return jnp.where(i >= n_prompt_tiles, 1, 0)

    in_specs = [
        pl.BlockSpec((tm, d), lambda i: (prompt_block(i), 0)),
        pl.BlockSpec((tm, d), lambda i: (sample_block(i), 0)),
        pl.BlockSpec((cache_rows, D_KV), lambda i: (sample_block(i), 0)),
        pl.BlockSpec((cache_rows, D_KV), lambda i: (sample_block(i), 0)),
        _const_spec((1, d)),
        _const_spec((d, d_in)),
        _const_spec(bias_tbl.shape),
        _const_spec(sink_row.shape),
        _const_spec((1, d_gm)),
        pl.BlockSpec((1,) + wsp.shape[1:], lambda i: (kind(i), 0, 0, 0)),
        pl.BlockSpec((1, GM_CHUNK, d_gm), lambda i: (kind(i), 0, 0)),
        _const_spec(w_out.shape),
    ]
    out_specs = [
        pl.BlockSpec((tm, d), lambda i: (i, 0)),
        pl.BlockSpec((tm, D_KV), lambda i: (i, 0)),
        pl.BlockSpec((tm, D_KV), lambda i: (i, 0)),
        pl.BlockSpec((tm, d_gm), lambda i: (sample_block(i), 0)),
    ]
    out_shape = [
        jax.ShapeDtypeStruct((m_rows, d), F32),
        jax.ShapeDtypeStruct((m_rows, D_KV), F32),
        jax.ShapeDtypeStruct((m_rows, D_KV), F32),
        jax.ShapeDtypeStruct((n_sample_rows, d_gm), F32),
    ]
    scratch = [
        pltpu.VMEM((tm, d_in), F32),
        pltpu.VMEM((tm, D_ATTN), BF16),
        pltpu.VMEM((tm, D_ATTN + d_gm), BF16),
        pltpu.VMEM((N_KV_HEADS, GM_CHUNK, LANES), BF16),
        pltpu.VMEM((N_KV_HEADS, GM_CHUNK, LANES), BF16),
    ]
    return pl.pallas_call(
        functools.partial(_mix0_kernel, n_prompt_tiles, units),
        grid=(n_tiles,),
        in_specs=in_specs,
        out_specs=out_specs,
        out_shape=out_shape,
        scratch_shapes=scratch,
        compiler_params=pltpu.CompilerParams(
            dimension_semantics=("arbitrary",), vmem_limit_bytes=VMEM_LIMIT_BYTES),
        name="mix0",
    )(xp, xs, cache_k, cache_v, g, w_in, bias_tbl, sink_row, gm_norm, wsp, gmb, w_out)


def _ffn_kernel(final, x_ref, g_ref, wg_ref, wu_ref, wd_ref, gnext_ref, out_ref, *rest):
    h_ref = rest[-1]
    j = pl.program_id(1)
    last = pl.num_programs(1) - 1

    def hidden_step(h, base):
        gate = jnp.dot(h, wg_ref[0].astype(BF16), preferred_element_type=F32)
        up = jnp.dot(h, wu_ref[0].astype(BF16), preferred_element_type=F32)
        act = (gate * jax.nn.sigmoid(gate) * up).astype(BF16)
        return base + jnp.dot(act, wd_ref[0].astype(BF16), preferred_element_type=F32)

    @pl.when(j == 0)
    def _():
        x = x_ref[...]
        h = (_rms_scale(x) * g_ref[...]).astype(BF16)
        h_ref[...] = h
        out_ref[...] = hidden_step(h, x)

    @pl.when(jnp.logical_and(j > 0, j < last))
    def _():
        out_ref[...] = hidden_step(h_ref[...], out_ref[...])

    @pl.when(j == last)
    def _():
        out = hidden_step(h_ref[...], out_ref[...])
        scale = lax.rsqrt(jnp.mean(out * out, axis=-1, keepdims=True) + RMS_EPS)
        if final:
            out_ref[...] = out * scale * gnext_ref[...]
        else:
            out_ref[...] = out
            rest[0][...] = jnp.broadcast_to(scale, rest[0].shape)


def _ffn(x, g, wg, wu, wd, gnext, *, layer, tm, tf, final, row0=0, n_rows=None):
    d = x.shape[1]
    n_rows = x.shape[0] if n_rows is None else n_rows
    f = wg.shape[2]
    assert n_rows % tm == 0 and row0 % tm == 0 and f % tf == 0
    tile0 = row0 // tm
    out_specs = [pl.BlockSpec((tm, d), lambda i, j: (i, 0))]
    out_shape = [jax.ShapeDtypeStruct((n_rows, d), F32)]
    if not final:
        out_specs.append(pl.BlockSpec((tm, LANES), lambda i, j: (i, 0)))
        out_shape.append(jax.ShapeDtypeStruct((n_rows, LANES), F32))
    return pl.pallas_call(
        functools.partial(_ffn_kernel, final),
        grid=(n_rows // tm, f // tf),
        in_specs=[
            pl.BlockSpec((tm, d), lambda i, j: (i + tile0, 0)),
            pl.BlockSpec((1, d), lambda i, j: (0, 0)),
            pl.BlockSpec((1, d, tf), lambda i, j: (layer, 0, j)),
            pl.BlockSpec((1, d, tf), lambda i, j: (layer, 0, j)),
            pl.BlockSpec((1, tf, d), lambda i, j: (layer, j, 0)),
            pl.BlockSpec((1, d), lambda i, j: (0, 0)),
        ],
        out_specs=out_specs,
        out_shape=out_shape,
        scratch_shapes=[pltpu.VMEM((tm, d), BF16)],
        compiler_params=pltpu.CompilerParams(
            dimension_semantics=("arbitrary", "arbitrary"), vmem_limit_bytes=VMEM_LIMIT_BYTES),
        name="ffn",
    )(x, g, wg, wu, wd, gnext)


TAB_ROWS = 18


def _s5_prep_kernel(seg_rows, lre_ref, lim_ref, ldt_ref, bre_ref, bim_ref, cre_ref, cim_ref,
                    wt_ref, we_ref, wst_ref, tab_ref, bhi_ref, blo_ref):
    ns = S5_BLOCK_STATE
    lr = lre_ref[0]
    li = lim_ref[0]
    dt = jnp.exp(ldt_ref[0])
    mag = jnp.exp(lr * dt)
    ar = mag * jnp.cos(li * dt)
    ai = mag * jnp.sin(li * dt)
    den = lr * lr + li * li
    nr = ar - 1.0
    fr = (nr * lr + ai * li) / den
    fi = (ai * lr - nr * li) / den
    b_re = bre_ref[0]
    b_im = bim_ref[0]
    bbr = fr * b_re - fi * b_im
    bbi = fr * b_im + fi * b_re
    c_re = cre_ref[0]
    c_im = cim_ref[0]

    def cmul(pr, pi, qr, qi):
        return pr * qr - pi * qi, pr * qi + pi * qr

    powers = [(jnp.ones_like(ar), jnp.zeros_like(ar))]
    for _ in range(S5_L):
        powers.append(cmul(*powers[-1], ar, ai))

    def split(a):
        hi = a.astype(BF16)
        return hi, (a - hi.astype(F32)).astype(BF16)

    for l in range(S5_L):
        pr, pi = powers[l]
        bkr, bki = cmul(pr, pi, bbr, bbi)
        rows = slice(l * LANES, (l + 1) * LANES)
        for part, (hi, lo) in ((0, split(bkr)), (1, split(bki))):
            bhi_ref[rows, part * ns:(part + 1) * ns] = hi
            blo_ref[rows, part * ns:(part + 1) * ns] = lo
        we_ref[0, (S5_L - 1 - l) * LANES:(S5_L - l) * LANES, 0:ns] = bkr.astype(BF16)
        we_ref[0, (S5_L - 1 - l) * LANES:(S5_L - l) * LANES, ns:2 * ns] = bki.astype(BF16)
        qr, qi = powers[l + 1]
        wst_ref[0, rows, 0:ns] = (c_re * qr - c_im * qi).astype(BF16)
        wst_ref[0, rows, ns:2 * ns] = (-c_re * qi - c_im * qr).astype(BF16)

    c_cat = jnp.concatenate([c_re, -c_im], axis=1)
    c_hi, c_lo = split(c_cat)
    c_parts = jnp.concatenate([c_hi, c_lo], axis=0)
    dims = (((1,), (1,)), ((), ()))
    d_hi = lax.dot_general(bhi_ref[...], c_parts, dims, preferred_element_type=F32)
    d_lo = lax.dot_general(blo_ref[...], c_parts, dims, preferred_element_type=F32)
    d_all = (d_hi[:, 0:LANES] + d_hi[:, LANES:2 * LANES] + d_lo[:, 0:LANES]).astype(BF16)

    zero_blk = jnp.zeros((LANES, LANES), BF16)
    for l in range(S5_L):
        for l2 in range(S5_L):
            blk = d_all[(l2 - l) * LANES:(l2 - l + 1) * LANES] if l2 >= l else zero_blk
            wt_ref[0, l * LANES:(l + 1) * LANES, l2 * LANES:(l2 + 1) * LANES] = blk

    row = lax.broadcasted_iota(jnp.int32, (SUBLANES, ns), 0)

    def bcast(a):
        return jnp.broadcast_to(a, (SUBLANES, ns))

    def log_step_tables(t0, base):
        cur = base
        for n, shift in enumerate((1, 2, 4)):
            tab_ref[0, t0 + 2 * n] = jnp.where(row >= shift, bcast(cur[0]), 0.0)
            tab_ref[0, t0 + 2 * n + 1] = jnp.where(row >= shift, bcast(cur[1]), 0.0)
            cur = cmul(*cur, *cur)
        return cur

    a1 = powers[S5_L]
    a8 = log_step_tables(0, a1)
    pr_tab = jnp.zeros((SUBLANES, ns), F32)
    pi_tab = jnp.zeros((SUBLANES, ns), F32)
    cur = a1
    for r in range(SUBLANES):
        pr_tab = jnp.where(row == r, bcast(cur[0]), pr_tab)
        pi_tab = jnp.where(row == r, bcast(cur[1]), pi_tab)
        cur = cmul(*cur, *a1)
    tab_ref[0, 6] = pr_tab
    tab_ref[0, 7] = pi_tab
    tab_ref[0, 8] = bcast(a8[0])
    tab_ref[0, 9] = bcast(a8[1])
    tab_ref[0, 10] = bcast(a1[0])
    tab_ref[0, 11] = bcast(a1[1])
    seg = a1
    for _ in range(seg_rows.bit_length() - 1):
        seg = cmul(*seg, *seg)
    log_step_tables(12, seg)


def _s5_prep(lam_re, lam_im, log_dt, b_re_emb, b_im_emb, c_re_emb, c_im_emb, *, seg_rows):
    assert seg_rows & (seg_rows - 1) == 0
    nb = b_re_emb.shape[0]
    ns = S5_BLOCK_STATE
    k = S5_L * LANES
    vec = pl.BlockSpec((1, 1, ns), lambda j: (j, 0, 0))
    emb = pl.BlockSpec((1, LANES, ns), lambda j: (j, 0, 0))
    wspec = pl.BlockSpec((1, k, 2 * ns), lambda j: (j, 0, 0))
    return pl.pallas_call(
        functools.partial(_s5_prep_kernel, seg_rows),
        grid=(nb,),
        in_specs=[vec, vec, vec, emb, emb, emb, emb],
        out_specs=[
            pl.BlockSpec((1, k, k), lambda j: (j, 0, 0)),
            wspec,
            wspec,
            pl.BlockSpec((1, TAB_ROWS, SUBLANES, ns), lambda j: (j, 0, 0, 0)),
        ],
        out_shape=[
            jax.ShapeDtypeStruct((nb, k, k), BF16),
            jax.ShapeDtypeStruct((nb, k, 2 * ns), BF16),
            jax.ShapeDtypeStruct((nb, k, 2 * ns), BF16),
            jax.ShapeDtypeStruct((nb, TAB_ROWS, SUBLANES, ns), F32),
        ],
        scratch_shapes=[pltpu.VMEM((k, 2 * ns), BF16), pltpu.VMEM((k, 2 * ns), BF16)],
        compiler_params=pltpu.CompilerParams(
            dimension_semantics=("arbitrary",), vmem_limit_bytes=VMEM_LIMIT_BYTES),
        name="s5_prep",
    )(lam_re, lam_im, log_dt, b_re_emb, b_im_emb, c_re_emb, c_im_emb)


def _s5_core_kernel(n_prompt_rows, n_streams,
                    x_ref, rstd_ref, g_ref, wt_ref, we_ref, wst_ref, tab_ref, s0r_ref, s0i_ref, d_ref,
                    y_ref, sfr_ref, sfi_ref,
                    u2f_ref, u2_ref, e_ref):
    ns = S5_BLOCK_STATE
    n_rows = x_ref.shape[0] // S5_L
    n_sample_rows = n_rows - n_prompt_rows
    seg_rows = n_prompt_rows // SUBLANES
    n_prompt_frames = n_prompt_rows * S5_L

    def frame_slices():
        for l in range(S5_L):
            for seg in range(SUBLANES):
                yield l, pl.ds(seg * seg_rows * S5_L + l, seg_rows, stride=S5_L), \
                    pl.ds(seg, seg_rows, stride=SUBLANES)
            yield l, pl.ds(n_prompt_frames + l, n_sample_rows, stride=S5_L), \
                pl.ds(n_prompt_rows, n_sample_rows)

    for l, frames, rows in frame_slices():
        u2f_ref[l, rows, :] = x_ref[frames, :] * rstd_ref[frames, :] * g_ref[...]
    for l in range(S5_L):
        u2_ref[:, l * LANES:(l + 1) * LANES] = u2f_ref[l].astype(BF16)
    e_ref[...] = jnp.dot(u2_ref[...], we_ref[0], preferred_element_type=F32)

    tabs = [tab_ref[0, t] for t in range(TAB_ROWS)]
    m_tabs, (pr, pi, a8r, a8i, ar, ai), b_tabs = tabs[0:6], tabs[6:12], tabs[12:18]
    row = lax.broadcasted_iota(jnp.int32, (SUBLANES, ns), 0)

    def log_step_scan(xr, xi, t):
        for n, shift in enumerate((1, 2, 4)):
            tr, ti = t[2 * n], t[2 * n + 1]
            sr = pltpu.roll(xr, shift, axis=0)
            si = pltpu.roll(xi, shift, axis=0)
            xr, xi = xr + tr * sr - ti * si, xi + tr * si + ti * sr
        return xr, xi

    def shift_down(xr, xi, fr, fi):
        first = row == 0
        return (jnp.where(first, fr, pltpu.roll(xr, 1, axis=0)),
                jnp.where(first, fi, pltpu.roll(xi, 1, axis=0)))

    def rows_of(k):
        return pl.ds(pl.multiple_of(k * SUBLANES, SUBLANES), SUBLANES)

    def local_step(k, carry):
        sr, si = carry
        er = e_ref[rows_of(k), 0:ns]
        ei = e_ref[rows_of(k), ns:2 * ns]
        e_ref[rows_of(k), 0:ns] = sr
        e_ref[rows_of(k), ns:2 * ns] = si
        return ar * sr - ai * si + er, ar * si + ai * sr + ei

    zero = jnp.zeros((SUBLANES, ns), F32)
    ends = lax.fori_loop(0, seg_rows, local_step, (zero, zero), unroll=True)
    ends = log_step_scan(*ends, b_tabs)
    sfr_ref[...] = jnp.zeros_like(sfr_ref)
    sfi_ref[...] = jnp.zeros_like(sfi_ref)
    sfr_ref[n_streams:n_streams + 1, :] = ends[0][SUBLANES - 1:SUBLANES]
    sfi_ref[n_streams:n_streams + 1, :] = ends[1][SUBLANES - 1:SUBLANES]

    def correct_step(k, carry):
        cr, ci = carry
        e_ref[rows_of(k), 0:ns] += cr
        e_ref[rows_of(k), ns:2 * ns] += ci
        return ar * cr - ai * ci, ar * ci + ai * cr

    lax.fori_loop(0, seg_rows, correct_step, shift_down(*ends, zero, zero), unroll=True)

    for b in range(n_streams):
        rows = pl.ds(n_prompt_rows + b * SUBLANES, SUBLANES)
        cr = jnp.broadcast_to(s0r_ref[b:b + 1, :], (SUBLANES, ns))
        ci = jnp.broadcast_to(s0i_ref[b:b + 1, :], (SUBLANES, ns))
        xr, xi = log_step_scan(e_ref[rows, 0:ns], e_ref[rows, ns:2 * ns], m_tabs)
        st_r = xr + pr * cr - pi * ci
        st_i = xi + pr * ci + pi * cr
        e_ref[rows, 0:ns], e_ref[rows, ns:2 * ns] = shift_down(st_r, st_i, cr, ci)
        sfr_ref[b:b + 1, :] = st_r[SUBLANES - 1:SUBLANES]
        sfi_ref[b:b + 1, :] = st_i[SUBLANES - 1:SUBLANES]

    sprev = e_ref[...].astype(BF16)
    slabs_per_block = S5_WT_BLOCK // LANES
    for cb in range(S5_L * LANES // S5_WT_BLOCK):
        cols = slice(cb * S5_WT_BLOCK, (cb + 1) * S5_WT_BLOCK)
        k_end = (cb + 1) * S5_WT_BLOCK
        y2 = jnp.dot(u2_ref[:, 0:k_end], wt_ref[0, 0:k_end, cols], preferred_element_type=F32)
        y2 = y2 + lax.dot_general(sprev, wst_ref[0, cols, :], (((1,), (1,)), ((), ())),
                                  preferred_element_type=F32)
        for n in range(slabs_per_block):
            l = cb * slabs_per_block + n
            u2f_ref[l] = y2[:, n * LANES:(n + 1) * LANES] + d_ref[...] * u2f_ref[l]
    for l, frames, rows in frame_slices():
        y_ref[frames, :] = u2f_ref[l, rows, :]


def _s5_core(x, rstd, g, wt, we, wst, tab, s0_re, s0_im, d_skip, *, n_prompt_rows):
    m_rows, d = x.shape
    nb = d // LANES
    ns = S5_BLOCK_STATE
    k = S5_L * LANES
    n_rows = m_rows // S5_L
    n_streams = s0_re.shape[0]
    assert n_prompt_rows % (S5_L * SUBLANES) == 0
    assert (m_rows - n_prompt_rows) == n_streams * S5_L * SUBLANES
    sf_rows = n_streams + SUBLANES
    lane_blk = pl.BlockSpec((m_rows, LANES), lambda j: (0, j))
    wspec = pl.BlockSpec((1, k, 2 * ns), lambda j: (j, 0, 0))
    st = pl.BlockSpec((n_streams, ns), lambda j: (0, j))
    sf = pl.BlockSpec((sf_rows, ns), lambda j: (0, j))
    return pl.pallas_call(
        functools.partial(_s5_core_kernel, n_prompt_rows // S5_L, n_streams),
        grid=(nb,),
        in_specs=[
            lane_blk,
            _const_spec((m_rows, LANES)),
            pl.BlockSpec((1, LANES), lambda j: (0, j)),
            pl.BlockSpec((1, k, k), lambda j: (j, 0, 0)),
            wspec,
            wspec,
            pl.BlockSpec((1, TAB_ROWS, SUBLANES, ns), lambda j: (j, 0, 0, 0)),
            st, st,
            pl.BlockSpec((1, LANES), lambda j: (0, j)),
        ],
        out_specs=[lane_blk, sf, sf],
        out_shape=[
            jax.ShapeDtypeStruct((m_rows, d), F32),
            jax.ShapeDtypeStruct((sf_rows, nb * ns), F32),
            jax.ShapeDtypeStruct((sf_rows, nb * ns), F32),
        ],
        scratch_shapes=[
            pltpu.VMEM((S5_L, n_rows, LANES), F32),
            pltpu.VMEM((n_rows, k), BF16),
            pltpu.VMEM((n_rows, 2 * ns), F32),
        ],
        compiler_params=pltpu.CompilerParams(
            dimension_semantics=("arbitrary",), vmem_limit_bytes=VMEM_LIMIT_BYTES),
        name="s5_core",
    )(x, rstd, g, wt, we, wst, tab, s0_re, s0_im, d_skip)


def _glu_kernel(n_col_blocks, x_ref, y_ref, w_ref, out_ref):
    d = x_ref.shape[1]
    a = _gelu(y_ref[...]).astype(BF16)
    cb = d // n_col_blocks
    for n in range(n_col_blocks):
        cols = slice(n * cb, (n + 1) * cb)
        val = jnp.dot(a, w_ref[:, n * cb:(n + 1) * cb], preferred_element_type=F32)
        gate = jnp.dot(a, w_ref[:, d + n * cb:d + (n + 1) * cb], preferred_element_type=F32)
        out_ref[:, cols] = x_ref[:, cols] + val * jax.nn.sigmoid(gate)


def _glu(x, y, w, *, tm, n_col_blocks=4):
    m_rows, d = x.shape
    assert m_rows % tm == 0 and d % n_col_blocks == 0
    tile = pl.BlockSpec((tm, d), lambda i: (i, 0))
    return pl.pallas_call(
        functools.partial(_glu_kernel, n_col_blocks),
        grid=(m_rows // tm,),
        in_specs=[tile, tile, _const_spec(w.shape)],
        out_specs=tile,
        out_shape=jax.ShapeDtypeStruct((m_rows, d), F32),
        compiler_params=pltpu.CompilerParams(
            dimension_semantics=("arbitrary",), vmem_limit_bytes=VMEM_LIMIT_BYTES),
        name="glu",
    )(x, y, w)


def _attention_tables(sinks):
    slopes = jnp.exp2(-8.0 * jnp.arange(1, N_HEADS + 1, dtype=F32) / N_HEADS)
    frame = jnp.arange(CHUNK, dtype=F32)[None, :]
    band = jnp.arange(BAND, dtype=F32)[:, None]
    dist = jnp.abs(frame - (band - WINDOW))
    bias = (dist[:, None, :] * slopes[None, :, None]).reshape(BAND, N_HEADS * CHUNK)
    sink_row = jnp.repeat(sinks.astype(F32), CHUNK).reshape(1, N_HEADS * CHUNK)
    return bias, sink_row


def _gm_tables(gm_ws, gm_b, d_gm):
    blk = jnp.arange(GM_CHUNK) // CHUNK
    w_prompt = jnp.where((blk[:, None] >= blk[None, :])[None], gm_ws, 0.0)
    top = gm_ws[:, :CHUNK, :CHUNK]
    zeros = jnp.zeros_like(top)
    w_sample = jnp.concatenate(
        [jnp.concatenate([top, zeros], axis=2), jnp.concatenate([zeros, top], axis=2)], axis=1)
    wsp = jnp.stack([w_prompt, w_sample])
    wsp = wsp.reshape(2, N_GM_GROUPS // 2, 2, GM_CHUNK, GM_CHUNK).transpose(0, 1, 3, 2, 4)
    wsp = wsp.reshape(2, N_GM_GROUPS // 2, GM_CHUNK, 2 * GM_CHUNK).astype(BF16)
    b_prompt = gm_b.T
    b_sample = jnp.concatenate([gm_b[:, :CHUNK].T, gm_b[:, :CHUNK].T], axis=0)
    gmb = jnp.stack([b_prompt, b_sample]).astype(F32)
    gmb = jnp.repeat(gmb, d_gm // N_GM_GROUPS, axis=2)
    return wsp, gmb


def _s5_embed(b_re, b_im, c_re, c_im):
    n_groups = b_re.shape[0]
    nb = n_groups // S5_LANE_GROUPS
    state_group = jnp.arange(S5_BLOCK_STATE) // S5_STATE
    own = (state_group[None, :] == jnp.arange(S5_LANE_GROUPS)[:, None])[None, :, None, :]

    def place(t):
        t = jnp.broadcast_to(t[:, None], (nb, S5_LANE_GROUPS, S5_GROUP, S5_BLOCK_STATE))
        return jnp.where(own, t, 0.0).reshape(nb, LANES, S5_BLOCK_STATE)

    def emb_b(b):
        b = b.astype(F32).reshape(nb, S5_LANE_GROUPS, S5_STATE, S5_GROUP)
        return place(b.transpose(0, 3, 1, 2).reshape(nb, S5_GROUP, S5_BLOCK_STATE))

    def emb_c(c):
        c = c.astype(F32).reshape(nb, S5_LANE_GROUPS, S5_GROUP, S5_STATE)
        return place(c.transpose(0, 2, 1, 3).reshape(nb, S5_GROUP, S5_BLOCK_STATE))

    return emb_b(b_re), emb_b(b_im), emb_c(c_re), emb_c(c_im)


def kernel(x_prompt, x_sample, cache_swa_k, cache_swa_v, state_s5_re, state_s5_im, norm_mix, norm_ffn, norm_final, w_in0, attn_sinks, gm_norm, gm_ws, gm_b, w_out0, s5_lam_re, s5_lam_im, s5_log_dt, s5_b_re, s5_b_im, s5_c_re, s5_c_im, s5_d, s5_w_glu, ffn_w_gate, ffn_w_up, ffn_w_down):
    batch, seq, d = x_prompt.shape
    dec_batch, dec_seq, _ = x_sample.shape
    assert batch == 1 and dec_seq == CHUNK and norm_mix.shape[0] == 2
    n_prompt = batch * seq
    n_sample = dec_batch * dec_seq
    d_gm = gm_norm.shape[-1]

    bias_tbl, sink_row = _attention_tables(attn_sinks[0])
    wsp, gmb = _gm_tables(gm_ws[0], gm_b[0], d_gm)
    x1, k_all, v_all, gvn = _mix0(
        x_prompt.reshape(n_prompt, d), x_sample.reshape(n_sample, d),
        cache_swa_k[0].reshape(dec_batch * WINDOW, D_KV), cache_swa_v[0].reshape(dec_batch * WINDOW, D_KV),
        norm_mix[0].reshape(1, d), w_in0[0].astype(BF16), bias_tbl, sink_row,
        gm_norm[0].reshape(1, d_gm), wsp, gmb, w_out0[0].astype(BF16), units=2)

    ffn_w_gate, ffn_w_up, ffn_w_down = (w.astype(BF16) for w in (ffn_w_gate, ffn_w_up, ffn_w_down))
    x2, rstd2 = _ffn(x1, norm_ffn[0].reshape(1, d), ffn_w_gate, ffn_w_up, ffn_w_down,
                     norm_mix[1].reshape(1, d), layer=0, tm=FFN_TM, tf=FFN_TF, final=False)

    n_groups = s5_lam_re.shape[1]
    nb = n_groups // S5_LANE_GROUPS
    bre, bim, cre, cim = _s5_embed(s5_b_re[0], s5_b_im[0], s5_c_re[0], s5_c_im[0])
    wt, we, wst, tab = _s5_prep(
        s5_lam_re[0].reshape(nb, 1, S5_BLOCK_STATE), s5_lam_im[0].reshape(nb, 1, S5_BLOCK_STATE),
        jnp.repeat(s5_log_dt[0], S5_STATE).reshape(nb, 1, S5_BLOCK_STATE), bre, bim, cre, cim,
        seg_rows=n_prompt // (S5_L * SUBLANES))
    ys5, sf_re, sf_im = _s5_core(
        x2, rstd2, norm_mix[1].reshape(1, d), wt, we, wst, tab, state_s5_re[0].reshape(dec_batch, n_groups * S5_STATE),
        state_s5_im[0].reshape(dec_batch, n_groups * S5_STATE), s5_d[0].reshape(1, d),
        n_prompt_rows=n_prompt)
    x3 = _glu(x2, ys5, s5_w_glu[0].astype(BF16), tm=256)

    def last_ffn(row0, n_rows):
        (y,) = _ffn(x3, norm_ffn[1].reshape(1, d), ffn_w_gate, ffn_w_up, ffn_w_down,
                    norm_final.reshape(1, d), layer=1, tm=FFN_TM, tf=FFN_TF, final=True,
                    row0=row0, n_rows=n_rows)
        return y

    y_prompt = last_ffn(0, n_prompt)
    y_sample = last_ffn(n_prompt, n_sample)

    keep = min(WINDOW, seq)
    y_prompt = y_prompt.reshape(batch, seq, d)
    y_sample = y_sample.reshape(dec_batch, dec_seq, d)
    kv_shape_p = (1, batch, keep, N_KV_HEADS, HEAD_DIM)
    kv_shape_s = (1, dec_batch, dec_seq, N_KV_HEADS, HEAD_DIM)
    st_p = (1, batch, n_groups, S5_STATE)
    st_s = (1, dec_batch, n_groups, S5_STATE)
    return (y_prompt, y_sample,
            k_all[n_prompt - keep:n_prompt].reshape(kv_shape_p),
            v_all[n_prompt - keep:n_prompt].reshape(kv_shape_p),
            k_all[n_prompt:].reshape(kv_shape_s),
            v_all[n_prompt:].reshape(kv_shape_s),
            gvn.reshape(1, dec_batch, dec_seq, d_gm),
            sf_re[dec_batch].reshape(st_p), sf_im[dec_batch].reshape(st_p),
            sf_re[:dec_batch].reshape(st_s), sf_im[:dec_batch].reshape(st_s))
```

```python
import functools
import math

import jax
import jax.numpy as jnp
from jax import lax
from jax.experimental import pallas as pl
from jax.experimental.pallas import tpu as pltpu

F32 = jnp.float32
BF16 = jnp.bfloat16

CHUNK = 64
HEAD_DIM = 64
N_HEADS = 16
N_KV_HEADS = 2
Q_PER_KV = N_HEADS // N_KV_HEADS
WINDOW = 128
BAND = WINDOW + CHUNK
D_ATTN = N_HEADS * HEAD_DIM
D_KV = N_KV_HEADS * HEAD_DIM
GM_CHUNK = 128
N_GM_GROUPS = 16
S5_GROUP = 16
S5_STATE = 64
RMS_EPS = 1e-5
NEG_INF = -1e30

LANES = 128
SUBLANES = 8
VMEM_LIMIT_BYTES = 56 * 1024 * 1024

FFN_TM = 1024
FFN_TF = 256

S5_L = SUBLANES
S5_LANE_GROUPS = LANES // S5_GROUP
S5_BLOCK_STATE = S5_LANE_GROUPS * S5_STATE
S5_WT_BLOCK = 256


def _gelu(x):
    return 0.5 * x * (1.0 + lax.erf(x * math.sqrt(0.5)))


def _rms_scale(x):
    return x * lax.rsqrt(jnp.mean(x * x, axis=-1, keepdims=True) + RMS_EPS)


def _const_spec(shape):
    zeros = (0,) * len(shape)
    return pl.BlockSpec(shape, lambda *_: zeros, pipeline_mode=pl.Buffered(1))


def _mix0_kernel(n_prompt_tiles, units,
                 xp_ref, xs_ref, ck_ref, cv_ref, g_ref, win_ref, bias_ref, sink_ref, gmn_ref,
                 wsp_ref, gmb_ref, wout_ref,
                 x1_ref, k_ref, v_ref, gvn_ref,
                 z_ref, q_ref, ocat_ref, kprev_ref, vprev_ref):
    i = pl.program_id(0)
    is_sample = i >= n_prompt_tiles
    tm = units * GM_CHUNK
    d_gm = gmn_ref.shape[-1]
    off_k = D_ATTN
    off_v = D_ATTN + D_KV
    off_gu = D_ATTN + 2 * D_KV
    off_gv = off_gu + d_gm
    n_q = Q_PER_KV * CHUNK

    @pl.when(i == 0)
    def _():
        kprev_ref[...] = jnp.zeros_like(kprev_ref)
        vprev_ref[...] = jnp.zeros_like(vprev_ref)

    x = jnp.where(is_sample, xs_ref[...], xp_ref[...])
    h = (_rms_scale(x) * g_ref[...]).astype(BF16)
    z_ref[...] = jnp.dot(h, win_ref[...], preferred_element_type=F32)

    q_ref[...] = (z_ref[:, 0:D_ATTN] * (HEAD_DIM ** -0.5)).astype(BF16)
    k = z_ref[:, off_k:off_k + D_KV]
    v = z_ref[:, off_v:off_v + D_KV]
    k_ref[...] = k
    v_ref[...] = v

    def lane_lo(rows):
        return lax.broadcasted_iota(jnp.int32, (rows, LANES), 1) < HEAD_DIM

    def replicate(a):
        r = pltpu.roll(a, HEAD_DIM, axis=1)
        lo = lane_lo(a.shape[0])
        return jnp.where(lo, a, r).astype(BF16), jnp.where(lo, r, a).astype(BF16)

    k_rep = replicate(k)
    v_rep = replicate(v)
    ck_rep = replicate(ck_ref[...])
    cv_rep = replicate(cv_ref[...])

    lo64 = lane_lo(CHUNK)
    band_pos = lax.broadcasted_iota(jnp.int32, (BAND, n_q), 0)
    chunks_per_tile = tm // CHUNK

    for u in range(units):
        for c2 in range(GM_CHUNK // CHUNK):
            r0 = u * GM_CHUNK + c2 * CHUNK
            stream = r0 // CHUNK
            chunk_index = i * chunks_per_tile + stream
            valid_from = jnp.where(is_sample, 0, jnp.maximum(WINDOW - CHUNK * chunk_index, 0))
            valid = band_pos >= valid_from
            for kv in range(N_KV_HEADS):
                def band(cur, prev_ref, cached):
                    if u == 0:
                        prev_unit = prev_ref[kv]
                    else:
                        prev_unit = cur[(u - 1) * GM_CHUNK:u * GM_CHUNK]
                    if c2 == 0:
                        prompt_prev = prev_unit
                    else:
                        prompt_prev = jnp.concatenate(
                            [prev_unit[CHUNK:], cur[u * GM_CHUNK:u * GM_CHUNK + CHUNK]], axis=0)
                    sample_prev = cached[stream * WINDOW:(stream + 1) * WINDOW]
                    prev = jnp.where(is_sample, sample_prev, prompt_prev)
                    return jnp.concatenate([prev, cur[r0:r0 + CHUNK]], axis=0)

                kb = band(k_rep[kv], kprev_ref, ck_rep[kv])
                vb = band(v_rep[kv], vprev_ref, cv_rep[kv])

                pieces = []
                for m in range(Q_PER_KV // 2):
                    c0 = kv * Q_PER_KV * HEAD_DIM + m * LANES
                    qp = q_ref[r0:r0 + CHUNK, c0:c0 + LANES]
                    pieces.append(jnp.where(lo64, qp, jnp.zeros_like(qp)))
                    pieces.append(jnp.where(lo64, jnp.zeros_like(qp), qp))
                qs = jnp.concatenate(pieces, axis=0)
                cols = slice(kv * n_q, (kv + 1) * n_q)
                st = lax.dot_general(kb, qs, (((1,), (1,)), ((), ())),
                                     preferred_element_type=F32)
                st = jnp.where(valid, st - bias_ref[:, cols], NEG_INF)
                sink = sink_ref[:, cols]
                mx = jnp.maximum(jnp.max(st, axis=0, keepdims=True), sink)
                p = jnp.exp(st - mx)
                denom = jnp.sum(p, axis=0, keepdims=True) + jnp.exp(sink - mx)
                pn = (p * (1.0 / denom)).astype(BF16)
                o = lax.dot_general(pn, vb, (((0,), (0,)), ((), ())),
                                    preferred_element_type=F32)
                for m in range(Q_PER_KV // 2):
                    o_pair = jnp.where(lo64, o[(2 * m) * CHUNK:(2 * m + 1) * CHUNK],
                                       o[(2 * m + 1) * CHUNK:(2 * m + 2) * CHUNK])
                    c0 = kv * Q_PER_KV * HEAD_DIM + m * LANES
                    ocat_ref[r0:r0 + CHUNK, c0:c0 + LANES] = o_pair.astype(BF16)

    lo128 = lane_lo(GM_CHUNK)
    for u in range(units):
        rows = slice(u * GM_CHUNK, (u + 1) * GM_CHUNK)
        ua = _gelu(z_ref[rows, off_gu:off_gu + d_gm])
        gvn = _rms_scale(_gelu(z_ref[rows, off_gv:off_gv + d_gm])) * gmn_ref[...]
        gvn_ref[rows, :] = gvn
        gb = gvn.astype(BF16)
        for m in range(N_GM_GROUPS // 2):
            cols = slice(m * LANES, (m + 1) * LANES)
            rhs = gb[:, cols]
            rhs2 = jnp.concatenate([jnp.where(lo128, rhs, jnp.zeros_like(rhs)),
                                    jnp.where(lo128, jnp.zeros_like(rhs), rhs)], axis=0)
            sp = jnp.dot(wsp_ref[0, m], rhs2, preferred_element_type=F32) + gmb_ref[0, :, cols]
            ocat_ref[rows, D_ATTN + m * LANES:D_ATTN + (m + 1) * LANES] = (ua[:, cols] * sp).astype(BF16)

    x1_ref[...] = x + jnp.dot(ocat_ref[...], wout_ref[...], preferred_element_type=F32)

    for kv in range(N_KV_HEADS):
        kprev_ref[kv] = k_rep[kv][tm - GM_CHUNK:]
        vprev_ref[kv] = v_rep[kv][tm - GM_CHUNK:]


def _mix0(xp, xs, cache_k, cache_v, g, w_in, bias_tbl, sink_row, gm_norm, wsp, gmb, w_out, *, units):
    n_prompt_rows, d = xp.shape
    n_sample_rows = xs.shape[0]
    m_rows = n_prompt_rows + n_sample_rows
    tm = units * GM_CHUNK
    assert n_prompt_rows % tm == 0 and n_sample_rows % tm == 0
    n_tiles = m_rows // tm
    n_prompt_tiles = n_prompt_rows // tm
    d_in = w_in.shape[1]
    d_gm = gm_norm.shape[-1]
    cache_rows = (tm // CHUNK) * WINDOW

    def prompt_block(i):
        return jnp.minimum(i, n_prompt_tiles - 1)

    def sample_block(i):
        return jnp.maximum(i - n_prompt_tiles, 0)

    def kind(i):
        return jnp.where(i >= n_prompt_tiles, 1, 0)

    in_specs = [
        pl.BlockSpec((tm, d), lambda i: (prompt_block(i), 0)),
        pl.BlockSpec((tm, d), lambda i: (sample_block(i), 0)),
        pl.BlockSpec((cache_rows, D_KV), lambda i: (sample_block(i), 0)),
        pl.BlockSpec((cache_rows, D_KV), lambda i: (sample_block(i), 0)),
        _const_spec((1, d)),
        _const_spec((d, d_in)),
        _const_spec(bias_tbl.shape),
        _const_spec(sink_row.shape),
        _const_spec((1, d_gm)),
        pl.BlockSpec((1,) + wsp.shape[1:], lambda i: (kind(i), 0, 0, 0)),
        pl.BlockSpec((1, GM_CHUNK, d_gm), lambda i: (kind(i), 0, 0)),
        _const_spec(w_out.shape),
    ]
    out_specs = [
        pl.BlockSpec((tm, d), lambda i: (i, 0)),
        pl.BlockSpec((tm, D_KV), lambda i: (i, 0)),
        pl.BlockSpec((tm, D_KV), lambda i: (i, 0)),
        pl.BlockSpec((tm, d_gm), lambda i: (sample_block(i), 0)),
    ]
    out_shape = [
        jax.ShapeDtypeStruct((m_rows, d), F32),
        jax.ShapeDtypeStruct((m_rows, D_KV), F32),
        jax.ShapeDtypeStruct((m_rows, D_KV), F32),
        jax.ShapeDtypeStruct((n_sample_rows, d_gm), F32),
    ]
    scratch = [
        pltpu.VMEM((tm, d_in), F32),
        pltpu.VMEM((tm, D_ATTN), BF16),
        pltpu.VMEM((tm, D_ATTN + d_gm), BF16),
        pltpu.VMEM((N_KV_HEADS, GM_CHUNK, LANES), BF16),
        pltpu.VMEM((N_KV_HEADS, GM_CHUNK, LANES), BF16),
    ]
    return pl.pallas_call(
        functools.partial(_mix0_kernel, n_prompt_tiles, units),
        grid=(n_tiles,),
        in_specs=in_specs,
        out_specs=out_specs,
        out_shape=out_shape,
        scratch_shapes=scratch,
        compiler_params=pltpu.CompilerParams(
            dimension_semantics=("arbitrary",), vmem_limit_bytes=VMEM_LIMIT_BYTES),
        name="mix0",
    )(xp, xs, cache_k, cache_v, g, w_in, bias_tbl, sink_row, gm_norm, wsp, gmb, w_out)


def _ffn_kernel(final, x_ref, g_ref, wg_ref, wu_ref, wd_ref, gnext_ref, out_ref, *rest):
    h_ref = rest[-1]
    j = pl.program_id(1)

    @pl.when(j == 0)
    def _():
        x = x_ref[...]
        h_ref[...] = (_rms_scale(x) * g_ref[...]).astype(BF16)
        out_ref[...] = x

    h = h_ref[...]
    gate = jnp.dot(h, wg_ref[0].astype(BF16), preferred_element_type=F32)
    up = jnp.dot(h, wu_ref[0].astype(BF16), preferred_element_type=F32)
    act = (gate * jax.nn.sigmoid(gate) * up).astype(BF16)
    out_ref[...] += jnp.dot(act, wd_ref[0].astype(BF16), preferred_element_type=F32)

    @pl.when(j == pl.num_programs(1) - 1)
    def _():
        out = out_ref[...]
        scale = lax.rsqrt(jnp.mean(out * out, axis=-1, keepdims=True) + RMS_EPS)
        if final:
            out_ref[...] = out * scale * gnext_ref[...]
        else:
            rest[0][...] = jnp.broadcast_to(scale, rest[0].shape)


def _ffn(x, g, wg, wu, wd, gnext, *, layer, tm, tf, final, row0=0, n_rows=None):
    d = x.shape[1]
    n_rows = x.shape[0] if n_rows is None else n_rows
    f = wg.shape[2]
    assert n_rows % tm == 0 and row0 % tm == 0 and f % tf == 0
    tile0 = row0 // tm
    out_specs = [pl.BlockSpec((tm, d), lambda i, j: (i, 0))]
    out_shape = [jax.ShapeDtypeStruct((n_rows, d), F32)]
    if not final:
        out_specs.append(pl.BlockSpec((tm, LANES), lambda i, j: (i, 0)))
        out_shape.append(jax.ShapeDtypeStruct((n_rows, LANES), F32))
    return pl.pallas_call(
        functools.partial(_ffn_kernel, final),
        grid=(n_rows // tm, f // tf),
        in_specs=[
            pl.BlockSpec((tm, d), lambda i, j: (i + tile0, 0)),
            pl.BlockSpec((1, d), lambda i, j: (0, 0)),
            pl.BlockSpec((1, d, tf), lambda i, j: (layer, 0, j)),
            pl.BlockSpec((1, d, tf), lambda i, j: (layer, 0, j)),
            pl.BlockSpec((1, tf, d), lambda i, j: (layer, j, 0)),
            pl.BlockSpec((1, d), lambda i, j: (0, 0)),
        ],
        out_specs=out_specs,
        out_shape=out_shape,
        scratch_shapes=[pltpu.VMEM((tm, d), BF16)],
        compiler_params=pltpu.CompilerParams(
            dimension_semantics=("arbitrary", "arbitrary"), vmem_limit_bytes=VMEM_LIMIT_BYTES),
        name="ffn",
    )(x, g, wg, wu, wd, gnext)


TAB_ROWS = 18


def _discretize(lr, li, log_dt):
    dt = jnp.exp(log_dt)
    mag = jnp.exp(lr * dt)
    ar = mag * jnp.cos(li * dt)
    ai = mag * jnp.sin(li * dt)
    den = lr * lr + li * li
    nr = ar - 1.0
    return ar, ai, (nr * lr + ai * li) / den, (ai * lr - nr * li) / den


def _cmul(pr, pi, qr, qi):
    return pr * qr - pi * qi, pr * qi + pi * qr


def _s5_prep_kernel(seg_rows, lam_ref, lamc_ref, b_ref, c_ref, dall_ref, wec_ref, wsc_ref, tab_ref,
                    bhi_ref, blo_ref):
    ns = S5_BLOCK_STATE

    ar, ai, fr, fi = _discretize(lamc_ref[0, 0], lamc_ref[0, 1], lamc_ref[0, 2])
    bbr, bbi = _cmul(fr, fi, b_ref[0, 0], b_ref[0, 1])
    c_re = c_ref[0, 0]
    c_im = c_ref[0, 1]

    def split(a):
        hi = a.astype(BF16)
        return hi, (a - hi.astype(F32)).astype(BF16)

    power = (jnp.ones_like(ar), jnp.zeros_like(ar))
    for l in range(S5_L):
        bkr, bki = _cmul(*power, bbr, bbi)
        power = _cmul(*power, ar, ai)
        rows = slice(l * LANES, (l + 1) * LANES)
        bk = jnp.concatenate([bkr, bki], axis=1)
        bhi_ref[rows, :], blo_ref[rows, :] = split(bk)
        wec_ref[0, (S5_L - 1 - l) * LANES:(S5_L - l) * LANES, :] = bk
        qr, qi = power
        wsc_ref[0, rows, :] = jnp.concatenate([c_re * qr - c_im * qi, -c_re * qi - c_im * qr], axis=1)

    c_hi, c_lo = split(jnp.concatenate([c_re, -c_im], axis=1))
    c_parts = jnp.concatenate([c_hi, c_lo], axis=0)
    dims = (((1,), (1,)), ((), ()))
    d_hi = lax.dot_general(bhi_ref[...], c_parts, dims, preferred_element_type=F32)
    d_lo = lax.dot_general(blo_ref[...], c_parts, dims, preferred_element_type=F32)
    d_all = d_hi[:, 0:LANES] + d_hi[:, LANES:2 * LANES] + d_lo[:, 0:LANES]
    shape = (S5_L * LANES, LANES)
    in_group = (lax.broadcasted_iota(jnp.int32, shape, 0) % LANES) // S5_GROUP
    out_group = lax.broadcasted_iota(jnp.int32, shape, 1) // S5_GROUP
    dall_ref[0] = jnp.where(in_group == out_group, d_all, 0.0).astype(BF16)

    ar, ai, _, _ = _discretize(lam_ref[0, 0], lam_ref[0, 1], lam_ref[0, 2])
    row = lax.broadcasted_iota(jnp.int32, (SUBLANES, ns), 0)

    def bcast(a):
        return jnp.broadcast_to(a, (SUBLANES, ns))

    def log_step_tables(t0, base):
        cur = base
        for n, shift in enumerate((1, 2, 4)):
            tab_ref[0, t0 + 2 * n] = jnp.where(row >= shift, bcast(cur[0]), 0.0)
            tab_ref[0, t0 + 2 * n + 1] = jnp.where(row >= shift, bcast(cur[1]), 0.0)
            cur = _cmul(*cur, *cur)
        return cur

    a1 = (ar, ai)
    for _ in range(S5_L.bit_length() - 1):
        a1 = _cmul(*a1, *a1)
    a8 = log_step_tables(0, a1)
    pr_tab = jnp.zeros((SUBLANES, ns), F32)
    pi_tab = jnp.zeros((SUBLANES, ns), F32)
    cur = a1
    for r in range(SUBLANES):
        pr_tab = jnp.where(row == r, bcast(cur[0]), pr_tab)
        pi_tab = jnp.where(row == r, bcast(cur[1]), pi_tab)
        cur = _cmul(*cur, *a1)
    tab_ref[0, 6] = pr_tab
    tab_ref[0, 7] = pi_tab
    tab_ref[0, 8] = bcast(a8[0])
    tab_ref[0, 9] = bcast(a8[1])
    tab_ref[0, 10] = bcast(a1[0])
    tab_ref[0, 11] = bcast(a1[1])
    seg = a1
    for _ in range(seg_rows.bit_length() - 1):
        seg = _cmul(*seg, *seg)
    log_step_tables(12, seg)


def _s5_prep(lam, lam_c, b_c, c_c, *, seg_rows):
    assert seg_rows & (seg_rows - 1) == 0 and S5_L & (S5_L - 1) == 0
    nb = lam.shape[0]
    ns = S5_BLOCK_STATE
    k = S5_L * LANES
    cspec = pl.BlockSpec((1, k, LANES), lambda j: (j, 0, 0))

    def whole(a):
        return pl.BlockSpec((1,) + a.shape[1:], lambda j: (j,) + (0,) * (a.ndim - 1))

    return pl.pallas_call(
        functools.partial(_s5_prep_kernel, seg_rows),
        grid=(nb,),
        in_specs=[whole(lam), whole(lam_c), whole(b_c), whole(c_c)],
        out_specs=[cspec, cspec, cspec,
                   pl.BlockSpec((1, TAB_ROWS, SUBLANES, ns), lambda j: (j, 0, 0, 0))],
        out_shape=[
            jax.ShapeDtypeStruct((nb, k, LANES), BF16),
            jax.ShapeDtypeStruct((nb, k, LANES), F32),
            jax.ShapeDtypeStruct((nb, k, LANES), F32),
            jax.ShapeDtypeStruct((nb, TAB_ROWS, SUBLANES, ns), F32),
        ],
        scratch_shapes=[pltpu.VMEM((k, LANES), BF16), pltpu.VMEM((k, LANES), BF16)],
        compiler_params=pltpu.CompilerParams(
            dimension_semantics=("arbitrary",), vmem_limit_bytes=VMEM_LIMIT_BYTES),
        name="s5_prep",
    )(lam, lam_c, b_c, c_c)


def _s5_core_kernel(n_prompt_rows, n_streams,
                    x_ref, rstd_ref, g_ref, dall_ref, wec_ref, wsc_ref, tab_ref, s0r_ref, s0i_ref,
                    d_ref,
                    y_ref, sfr_ref, sfi_ref,
                    u2f_ref, u2_ref, e_ref, wt_ref, we_ref, wst_ref):
    ns = S5_BLOCK_STATE
    n_rows = x_ref.shape[0] // S5_L
    n_sample_rows = n_rows - n_prompt_rows
    seg_rows = n_prompt_rows // SUBLANES
    n_prompt_frames = n_prompt_rows * S5_L

    shape = (S5_L * LANES, LANES)
    lane = lax.broadcasted_iota(jnp.int32, shape, 1)
    lane_lo = lane < S5_STATE
    row_group = (lax.broadcasted_iota(jnp.int32, shape, 0) % LANES) // S5_GROUP
    groups_per_vreg = LANES // S5_STATE

    def expand(compact_ref, full_ref):
        x = compact_ref[0]
        r = pltpu.roll(x, S5_STATE, axis=1)
        for part, rep in enumerate((jnp.where(lane_lo, x, r), jnp.where(lane_lo, r, x))):
            for v in range(S5_LANE_GROUPS // groups_per_vreg):
                own = row_group == groups_per_vreg * v + (lane // S5_STATE)
                col = part * S5_BLOCK_STATE + v * LANES
                full_ref[:, col:col + LANES] = jnp.where(own, rep, 0.0).astype(BF16)

    expand(wec_ref, we_ref)
    expand(wsc_ref, wst_ref)
    zero_blk = jnp.zeros((LANES, LANES), BF16)
    for l in range(S5_L):
        for l2 in range(S5_L):
            blk = dall_ref[0, (l2 - l) * LANES:(l2 - l + 1) * LANES, :] if l2 >= l else zero_blk
            wt_ref[l * LANES:(l + 1) * LANES, l2 * LANES:(l2 + 1) * LANES] = blk

    def frame_slices():
        for l in range(S5_L):
            for seg in range(SUBLANES):
                yield l, pl.ds(seg * seg_rows * S5_L + l, seg_rows, stride=S5_L), \
                    pl.ds(seg, seg_rows, stride=SUBLANES)
            yield l, pl.ds(n_prompt_frames + l, n_sample_rows, stride=S5_L), \
                pl.ds(n_prompt_rows, n_sample_rows)

    for l, frames, rows in frame_slices():
        u2f_ref[l, rows, :] = x_ref[frames, :] * rstd_ref[frames, :] * g_ref[...]
    for l in range(S5_L):
        u2_ref[:, l * LANES:(l + 1) * LANES] = u2f_ref[l].astype(BF16)
    e_ref[...] = jnp.dot(u2_ref[...], we_ref[...], preferred_element_type=F32)

    tabs = [tab_ref[0, t] for t in range(TAB_ROWS)]
    m_tabs, (pr, pi, a8r, a8i, ar, ai), b_tabs = tabs[0:6], tabs[6:12], tabs[12:18]
    row = lax.broadcasted_iota(jnp.int32, (SUBLANES, ns), 0)

    def log_step_scan(xr, xi, t):
        for n, shift in enumerate((1, 2, 4)):
            tr, ti = t[2 * n], t[2 * n + 1]
            sr = pltpu.roll(xr, shift, axis=0)
            si = pltpu.roll(xi, shift, axis=0)
            xr, xi = xr + tr * sr - ti * si, xi + tr * si + ti * sr
        return xr, xi

    def shift_down(xr, xi, fr, fi):
        first = row == 0
        return (jnp.where(first, fr, pltpu.roll(xr, 1, axis=0)),
                jnp.where(first, fi, pltpu.roll(xi, 1, axis=0)))

    def rows_of(k):
        return pl.ds(pl.multiple_of(k * SUBLANES, SUBLANES), SUBLANES)

    def local_step(k, carry):
        sr, si = carry
        er = e_ref[rows_of(k), 0:ns]
        ei = e_ref[rows_of(k), ns:2 * ns]
        e_ref[rows_of(k), 0:ns] = sr
        e_ref[rows_of(k), ns:2 * ns] = si
        return ar * sr - ai * si + er, ar * si + ai * sr + ei

    zero = jnp.zeros((SUBLANES, ns), F32)
    ends = lax.fori_loop(0, seg_rows, local_step, (zero, zero), unroll=True)
    ends = log_step_scan(*ends, b_tabs)
    sfr_ref[...] = jnp.zeros_like(sfr_ref)
    sfi_ref[...] = jnp.zeros_like(sfi_ref)
    sfr_ref[n_streams:n_streams + 1, :] = ends[0][SUBLANES - 1:SUBLANES]
    sfi_ref[n_streams:n_streams + 1, :] = ends[1][SUBLANES - 1:SUBLANES]

    def correct_step(k, carry):
        cr, ci = carry
        e_ref[rows_of(k), 0:ns] += cr
        e_ref[rows_of(k), ns:2 * ns] += ci
        return ar * cr - ai * ci, ar * ci + ai * cr

    lax.fori_loop(0, seg_rows, correct_step, shift_down(*ends, zero, zero), unroll=True)

    for b in range(n_streams):
        rows = pl.ds(n_prompt_rows + b * SUBLANES, SUBLANES)
        cr = jnp.broadcast_to(s0r_ref[b:b + 1, :], (SUBLANES, ns))
        ci = jnp.broadcast_to(s0i_ref[b:b + 1, :], (SUBLANES, ns))
        xr, xi = log_step_scan(e_ref[rows, 0:ns], e_ref[rows, ns:2 * ns], m_tabs)
        st_r = xr + pr * cr - pi * ci
        st_i = xi + pr * ci + pi * cr
        e_ref[rows, 0:ns], e_ref[rows, ns:2 * ns] = shift_down(st_r, st_i, cr, ci)
        sfr_ref[b:b + 1, :] = st_r[SUBLANES - 1:SUBLANES]
        sfi_ref[b:b + 1, :] = st_i[SUBLANES - 1:SUBLANES]

    sprev = e_ref[...].astype(BF16)
    slabs_per_block = S5_WT_BLOCK // LANES
    for cb in range(S5_L * LANES // S5_WT_BLOCK):
        cols = slice(cb * S5_WT_BLOCK, (cb + 1) * S5_WT_BLOCK)
        k_end = (cb + 1) * S5_WT_BLOCK
        y2 = jnp.dot(u2_ref[:, 0:k_end], wt_ref[0:k_end, cols], preferred_element_type=F32)
        y2 = y2 + lax.dot_general(sprev, wst_ref[cols, :], (((1,), (1,)), ((), ())),
                                  preferred_element_type=F32)
        for n in range(slabs_per_block):
            l = cb * slabs_per_block + n
            u2f_ref[l] = y2[:, n * LANES:(n + 1) * LANES] + d_ref[...] * u2f_ref[l]
    for l, frames, rows in frame_slices():
        y_ref[frames, :] = u2f_ref[l, rows, :]


def _s5_core(x, rstd, g, dall, wec, wsc, tab, s0_re, s0_im, d_skip, *, n_prompt_rows):
    m_rows, d = x.shape
    nb = d // LANES
    ns = S5_BLOCK_STATE
    k = S5_L * LANES
    n_rows = m_rows // S5_L
    n_streams = s0_re.shape[0]
    assert n_prompt_rows % (S5_L * SUBLANES) == 0
    assert (m_rows - n_prompt_rows) == n_streams * S5_L * SUBLANES
    sf_rows = n_streams + SUBLANES
    lane_blk = pl.BlockSpec((m_rows, LANES), lambda j: (0, j))
    cspec = pl.BlockSpec((1, k, LANES), lambda j: (j, 0, 0))
    st = pl.BlockSpec((n_streams, ns), lambda j: (0, j))
    sf = pl.BlockSpec((sf_rows, ns), lambda j: (0, j))
    return pl.pallas_call(
        functools.partial(_s5_core_kernel, n_prompt_rows // S5_L, n_streams),
        grid=(nb,),
        in_specs=[
            lane_blk,
            _const_spec((m_rows, LANES)),
            pl.BlockSpec((1, LANES), lambda j: (0, j)),
            cspec, cspec, cspec,
            pl.BlockSpec((1, TAB_ROWS, SUBLANES, ns), lambda j: (j, 0, 0, 0)),
            st, st,
            pl.BlockSpec((1, LANES), lambda j: (0, j)),
        ],
        out_specs=[lane_blk, sf, sf],
        out_shape=[
            jax.ShapeDtypeStruct((m_rows, d), F32),
            jax.ShapeDtypeStruct((sf_rows, nb * ns), F32),
            jax.ShapeDtypeStruct((sf_rows, nb * ns), F32),
        ],
        scratch_shapes=[
            pltpu.VMEM((S5_L, n_rows, LANES), F32),
            pltpu.VMEM((n_rows, k), BF16),
            pltpu.VMEM((n_rows, 2 * ns), F32),
            pltpu.VMEM((k, k), BF16),
            pltpu.VMEM((k, 2 * ns), BF16),
            pltpu.VMEM((k, 2 * ns), BF16),
        ],
        compiler_params=pltpu.CompilerParams(
            dimension_semantics=("arbitrary",), vmem_limit_bytes=VMEM_LIMIT_BYTES),
        name="s5_core",
    )(x, rstd, g, dall, wec, wsc, tab, s0_re, s0_im, d_skip)


def _glu_kernel(n_col_blocks, x_ref, y_ref, w_ref, out_ref):
    d = x_ref.shape[1]
    a = _gelu(y_ref[...]).astype(BF16)
    cb = d // n_col_blocks
    for n in range(n_col_blocks):
        cols = slice(n * cb, (n + 1) * cb)
        val = jnp.dot(a, w_ref[:, n * cb:(n + 1) * cb], preferred_element_type=F32)
        gate = jnp.dot(a, w_ref[:, d + n * cb:d + (n + 1) * cb], preferred_element_type=F32)
        out_ref[:, cols] = x_ref[:, cols] + val * jax.nn.sigmoid(gate)


def _glu(x, y, w, *, tm, n_col_blocks=4):
    m_rows, d = x.shape
    assert m_rows % tm == 0 and d % n_col_blocks == 0
    tile = pl.BlockSpec((tm, d), lambda i: (i, 0))
    return pl.pallas_call(
        functools.partial(_glu_kernel, n_col_blocks),
        grid=(m_rows // tm,),
        in_specs=[tile, tile, _const_spec(w.shape)],
        out_specs=tile,
        out_shape=jax.ShapeDtypeStruct((m_rows, d), F32),
        compiler_params=pltpu.CompilerParams(
            dimension_semantics=("arbitrary",), vmem_limit_bytes=VMEM_LIMIT_BYTES),
        name="glu",
    )(x, y, w)


def _attention_tables(sinks):
    slopes = jnp.exp2(-8.0 * jnp.arange(1, N_HEADS + 1, dtype=F32) / N_HEADS)
    frame = jnp.arange(CHUNK, dtype=F32)[None, :]
    band = jnp.arange(BAND, dtype=F32)[:, None]
    dist = jnp.abs(frame - (band - WINDOW))
    bias = (dist[:, None, :] * slopes[None, :, None]).reshape(BAND, N_HEADS * CHUNK)
    sink_row = jnp.repeat(sinks.astype(F32), CHUNK).reshape(1, N_HEADS * CHUNK)
    return bias, sink_row


def _gm_tables(gm_ws, gm_b, d_gm):
    blk = jnp.arange(GM_CHUNK) // CHUNK
    w_prompt = jnp.where((blk[:, None] >= blk[None, :])[None], gm_ws, 0.0)
    top = gm_ws[:, :CHUNK, :CHUNK]
    zeros = jnp.zeros_like(top)
    w_sample = jnp.concatenate(
        [jnp.concatenate([top, zeros], axis=2), jnp.concatenate([zeros, top], axis=2)], axis=1)
    wsp = jnp.stack([w_prompt, w_sample])
    wsp = wsp.reshape(2, N_GM_GROUPS // 2, 2, GM_CHUNK, GM_CHUNK).transpose(0, 1, 3, 2, 4)
    wsp = wsp.reshape(2, N_GM_GROUPS // 2, GM_CHUNK, 2 * GM_CHUNK).astype(BF16)
    b_prompt = gm_b.T
    b_sample = jnp.concatenate([gm_b[:, :CHUNK].T, gm_b[:, :CHUNK].T], axis=0)
    gmb = jnp.stack([b_prompt, b_sample]).astype(F32)
    gmb = jnp.repeat(gmb, d_gm // N_GM_GROUPS, axis=2)
    return wsp, gmb


def _s5_compact(lam_re, lam_im, log_dt, b_re, b_im, c_re, c_im):
    n_groups = lam_re.shape[0]
    nb = n_groups // S5_LANE_GROUPS
    per_state = jnp.stack([lam_re, lam_im, jnp.broadcast_to(log_dt[:, None], lam_re.shape)]).astype(F32)
    per_state = per_state.reshape(3, nb, S5_LANE_GROUPS, S5_STATE).transpose(1, 0, 2, 3)
    lam = per_state.reshape(nb, 3, 1, S5_BLOCK_STATE)
    lam_c = jnp.broadcast_to(per_state[:, :, :, None, :],
                             (nb, 3, S5_LANE_GROUPS, S5_GROUP, S5_STATE)).reshape(nb, 3, LANES, S5_STATE)
    b_c = jnp.stack([b_re, b_im]).astype(F32).transpose(1, 0, 3, 2)
    b_c = b_c.reshape(nb, S5_LANE_GROUPS, 2, S5_GROUP, S5_STATE).transpose(0, 2, 1, 3, 4)
    c_c = jnp.stack([c_re, c_im]).astype(F32).transpose(1, 0, 2, 3)
    c_c = c_c.reshape(nb, S5_LANE_GROUPS, 2, S5_GROUP, S5_STATE).transpose(0, 2, 1, 3, 4)
    return (lam, lam_c, b_c.reshape(nb, 2, LANES, S5_STATE), c_c.reshape(nb, 2, LANES, S5_STATE))


def kernel(x_prompt, x_sample, cache_swa_k, cache_swa_v, state_s5_re, state_s5_im, norm_mix, norm_ffn, norm_final, w_in0, attn_sinks, gm_norm, gm_ws, gm_b, w_out0, s5_lam_re, s5_lam_im, s5_log_dt, s5_b_re, s5_b_im, s5_c_re, s5_c_im, s5_d, s5_w_glu, ffn_w_gate, ffn_w_up, ffn_w_down):
    batch, seq, d = x_prompt.shape
    dec_batch, dec_seq, _ = x_sample.shape
    assert batch == 1 and dec_seq == CHUNK and norm_mix.shape[0] == 2
    n_prompt = batch * seq
    n_sample = dec_batch * dec_seq
    d_gm = gm_norm.shape[-1]

    bias_tbl, sink_row = _attention_tables(attn_sinks[0])
    wsp, gmb = _gm_tables(gm_ws[0], gm_b[0], d_gm)
    x1, k_all, v_all, gvn = _mix0(
        x_prompt.reshape(n_prompt, d), x_sample.reshape(n_sample, d),
        cache_swa_k[0].reshape(dec_batch * WINDOW, D_KV), cache_swa_v[0].reshape(dec_batch * WINDOW, D_KV),
        norm_mix[0].reshape(1, d), w_in0[0].astype(BF16), bias_tbl, sink_row,
        gm_norm[0].reshape(1, d_gm), wsp, gmb, w_out0[0].astype(BF16), units=2)

    x2, rstd2 = _ffn(x1, norm_ffn[0].reshape(1, d), ffn_w_gate, ffn_w_up, ffn_w_down,
                     norm_mix[1].reshape(1, d), layer=0, tm=FFN_TM, tf=FFN_TF, final=False)

    n_groups = s5_lam_re.shape[1]
    dall, wec, wsc, tab = _s5_prep(
        *_s5_compact(s5_lam_re[0], s5_lam_im[0], s5_log_dt[0], s5_b_re[0], s5_b_im[0], s5_c_re[0],
                     s5_c_im[0]),
        seg_rows=n_prompt // (S5_L * SUBLANES))
    ys5, sf_re, sf_im = _s5_core(
        x2, rstd2, norm_mix[1].reshape(1, d), dall, wec, wsc, tab, state_s5_re[0].reshape(dec_batch, n_groups * S5_STATE),
        state_s5_im[0].reshape(dec_batch, n_groups * S5_STATE), s5_d[0].reshape(1, d),
        n_prompt_rows=n_prompt)
    x3 = _glu(x2, ys5, s5_w_glu[0].astype(BF16), tm=256)

    def last_ffn(row0, n_rows):
        (y,) = _ffn(x3, norm_ffn[1].reshape(1, d), ffn_w_gate, ffn_w_up, ffn_w_down,
                    norm_final.reshape(1, d), layer=1, tm=FFN_TM, tf=FFN_TF, final=True,
                    row0=row0, n_rows=n_rows)
        return y

    y_prompt = last_ffn(0, n_prompt)
    y_sample = last_ffn(n_prompt, n_sample)

    keep = min(WINDOW, seq)
    y_prompt = y_prompt.reshape(batch, seq, d)
    y_sample = y_sample.reshape(dec_batch, dec_seq, d)
    kv_shape_p = (1, batch, keep, N_KV_HEADS, HEAD_DIM)
    kv_shape_s = (1, dec_batch, dec_seq, N_KV_HEADS, HEAD_DIM)
    st_p = (1, batch, n_groups, S5_STATE)
    st_s = (1, dec_batch, n_groups, S5_STATE)
    return (y_prompt, y_sample,
            k_all[n_prompt - keep:n_prompt].reshape(kv_shape_p),
            v_all[n_prompt - keep:n_prompt].reshape(kv_shape_p),
            k_all[n_prompt:].reshape(kv_shape_s),
            v_all[n_prompt:].reshape(kv_shape_s),
            gvn.reshape(1, dec_batch, dec_seq, d_gm),
            sf_re[dec_batch].reshape(st_p), sf_im[dec_batch].reshape(st_p),
            sf_re[:dec_batch].reshape(st_s), sf_im[:dec_batch].reshape(st_s))
```

```python
import functools
import math

import jax
import jax.numpy as jnp
from jax import lax
from jax.experimental import pallas as pl
from jax.experimental.pallas import tpu as pltpu

F32 = jnp.float32
BF16 = jnp.bfloat16

CHUNK = 64
HEAD_DIM = 64
N_HEADS = 16
N_KV_HEADS = 2
Q_PER_KV = N_HEADS // N_KV_HEADS
WINDOW = 128
BAND = WINDOW + CHUNK
D_ATTN = N_HEADS * HEAD_DIM
D_KV = N_KV_HEADS * HEAD_DIM
GM_CHUNK = 128
N_GM_GROUPS = 16
S5_GROUP = 16
S5_STATE = 64
RMS_EPS = 1e-5
NEG_INF = -1e30

LANES = 128
SUBLANES = 8
VMEM_LIMIT_BYTES = 56 * 1024 * 1024

FFN_TM = 1024
FFN_TF = 256

S5_L = SUBLANES
S5_LANE_GROUPS = LANES // S5_GROUP
S5_BLOCK_STATE = S5_LANE_GROUPS * S5_STATE
S5_WT_BLOCK = 256


def _gelu(x):
    return 0.5 * x * (1.0 + lax.erf(x * math.sqrt(0.5)))


def _rms_scale(x):
    return x * lax.rsqrt(jnp.mean(x * x, axis=-1, keepdims=True) + RMS_EPS)


def _const_spec(shape):
    zeros = (0,) * len(shape)
    return pl.BlockSpec(shape, lambda *_: zeros, pipeline_mode=pl.Buffered(1))


def _mix0_kernel(n_prompt_tiles, units,
                 xp_ref, xs_ref, ck_ref, cv_ref, g_ref, win_ref, bias_ref, sink_ref, gmn_ref,
                 wsp_ref, gmb_ref, wout_ref,
                 x1_ref, k_ref, v_ref, gvn_ref,
                 z_ref, q_ref, ocat_ref, kprev_ref, vprev_ref):
    i = pl.program_id(0)
    is_sample = i >= n_prompt_tiles
    tm = units * GM_CHUNK
    d_gm = gmn_ref.shape[-1]
    off_k = D_ATTN
    off_v = D_ATTN + D_KV
    off_gu = D_ATTN + 2 * D_KV
    off_gv = off_gu + d_gm
    n_q = Q_PER_KV * CHUNK

    @pl.when(i == 0)
    def _():
        kprev_ref[...] = jnp.zeros_like(kprev_ref)
        vprev_ref[...] = jnp.zeros_like(vprev_ref)

    x = jnp.where(is_sample, xs_ref[...], xp_ref[...])
    h = (_rms_scale(x) * g_ref[...]).astype(BF16)
    z_ref[...] = jnp.dot(h, win_ref[...], preferred_element_type=F32)

    q_ref[...] = (z_ref[:, 0:D_ATTN] * (HEAD_DIM ** -0.5)).astype(BF16)
    k = z_ref[:, off_k:off_k + D_KV]
    v = z_ref[:, off_v:off_v + D_KV]
    k_ref[...] = k
    v_ref[...] = v

    def lane_lo(rows):
        return lax.broadcasted_iota(jnp.int32, (rows, LANES), 1) < HEAD_DIM

    def replicate(a):
        r = pltpu.roll(a, HEAD_DIM, axis=1)
        lo = lane_lo(a.shape[0])
        return jnp.where(lo, a, r).astype(BF16), jnp.where(lo, r, a).astype(BF16)

    k_rep = replicate(k)
    v_rep = replicate(v)
    ck_rep = replicate(ck_ref[...])
    cv_rep = replicate(cv_ref[...])

    lo64 = lane_lo(CHUNK)
    band_pos = lax.broadcasted_iota(jnp.int32, (BAND, n_q), 0)
    chunks_per_tile = tm // CHUNK

    for u in range(units):
        for c2 in range(GM_CHUNK // CHUNK):
            r0 = u * GM_CHUNK + c2 * CHUNK
            stream = r0 // CHUNK
            chunk_index = i * chunks_per_tile + stream
            valid_from = jnp.where(is_sample, 0, jnp.maximum(WINDOW - CHUNK * chunk_index, 0))
            valid = band_pos >= valid_from
            for kv in range(N_KV_HEADS):
                def band(cur, prev_ref, cached):
                    if u == 0:
                        prev_unit = prev_ref[kv]
                    else:
                        prev_unit = cur[(u - 1) * GM_CHUNK:u * GM_CHUNK]
                    if c2 == 0:
                        prompt_prev = prev_unit
                    else:
                        prompt_prev = jnp.concatenate(
                            [prev_unit[CHUNK:], cur[u * GM_CHUNK:u * GM_CHUNK + CHUNK]], axis=0)
                    sample_prev = cached[stream * WINDOW:(stream + 1) * WINDOW]
                    prev = jnp.where(is_sample, sample_prev, prompt_prev)
                    return jnp.concatenate([prev, cur[r0:r0 + CHUNK]], axis=0)

                kb = band(k_rep[kv], kprev_ref, ck_rep[kv])
                vb = band(v_rep[kv], vprev_ref, cv_rep[kv])

                pieces = []
                for m in range(Q_PER_KV // 2):
                    c0 = kv * Q_PER_KV * HEAD_DIM + m * LANES
                    qp = q_ref[r0:r0 + CHUNK, c0:c0 + LANES]
                    pieces.append(jnp.where(lo64, qp, jnp.zeros_like(qp)))
                    pieces.append(jnp.where(lo64, jnp.zeros_like(qp), qp))
                qs = jnp.concatenate(pieces, axis=0)
                cols = slice(kv * n_q, (kv + 1) * n_q)
                st = lax.dot_general(kb, qs, (((1,), (1,)), ((), ())),
                                     preferred_element_type=F32)
                st = jnp.where(valid, st - bias_ref[:, cols], NEG_INF)
                sink = sink_ref[:, cols]
                mx = jnp.maximum(jnp.max(st, axis=0, keepdims=True), sink)
                p = jnp.exp(st - mx)
                denom = jnp.sum(p, axis=0, keepdims=True) + jnp.exp(sink - mx)
                pn = (p * (1.0 / denom)).astype(BF16)
                o = lax.dot_general(pn, vb, (((0,), (0,)), ((), ())),
                                    preferred_element_type=F32)
                for m in range(Q_PER_KV // 2):
                    o_pair = jnp.where(lo64, o[(2 * m) * CHUNK:(2 * m + 1) * CHUNK],
                                       o[(2 * m + 1) * CHUNK:(2 * m + 2) * CHUNK])
                    c0 = kv * Q_PER_KV * HEAD_DIM + m * LANES
                    ocat_ref[r0:r0 + CHUNK, c0:c0 + LANES] = o_pair.astype(BF16)

    lo128 = lane_lo(GM_CHUNK)
    for u in range(units):
        rows = slice(u * GM_CHUNK, (u + 1) * GM_CHUNK)
        ua = _gelu(z_ref[rows, off_gu:off_gu + d_gm])
        gvn = _rms_scale(_gelu(z_ref[rows, off_gv:off_gv + d_gm])) * gmn_ref[...]
        gvn_ref[rows, :] = gvn
        gb = gvn.astype(BF16)
        for m in range(N_GM_GROUPS // 2):
            cols = slice(m * LANES, (m + 1) * LANES)
            rhs = gb[:, cols]
            rhs2 = jnp.concatenate([jnp.where(lo128, rhs, jnp.zeros_like(rhs)),
                                    jnp.where(lo128, jnp.zeros_like(rhs), rhs)], axis=0)
            sp = jnp.dot(wsp_ref[0, m], rhs2, preferred_element_type=F32) + gmb_ref[0, :, cols]
            ocat_ref[rows, D_ATTN + m * LANES:D_ATTN + (m + 1) * LANES] = (ua[:, cols] * sp).astype(BF16)

    x1_ref[...] = x + jnp.dot(ocat_ref[...], wout_ref[...], preferred_element_type=F32)

    for kv in range(N_KV_HEADS):
        kprev_ref[kv] = k_rep[kv][tm - GM_CHUNK:]
        vprev_ref[kv] = v_rep[kv][tm - GM_CHUNK:]


def _mix0(xp, xs, cache_k, cache_v, g, w_in, bias_tbl, sink_row, gm_norm, wsp, gmb, w_out, *, units):
    n_prompt_rows, d = xp.shape
    n_sample_rows = xs.shape[0]
    m_rows = n_prompt_rows + n_sample_rows
    tm = units * GM_CHUNK
    assert n_prompt_rows % tm == 0 and n_sample_rows % tm == 0
    n_tiles = m_rows // tm
    n_prompt_tiles = n_prompt_rows // tm
    d_in = w_in.shape[1]
    d_gm = gm_norm.shape[-1]
    cache_rows = (tm // CHUNK) * WINDOW

    def prompt_block(i):
        return jnp.minimum(i, n_prompt_tiles - 1)

    def sample_block(i):
        return jnp.maximum(i - n_prompt_tiles, 0)

    def kind(i):
        return jnp.where(i >= n_prompt_tiles, 1, 0)

    in_specs = [
        pl.BlockSpec((tm, d), lambda i: (prompt_block(i), 0)),
        pl.BlockSpec((tm, d), lambda i: (sample_block(i), 0)),
        pl.BlockSpec((cache_rows, D_KV), lambda i: (sample_block(i), 0)),
        pl.BlockSpec((cache_rows, D_KV), lambda i: (sample_block(i), 0)),
        _const_spec((1, d)),
        _const_spec((d, d_in)),
        _const_spec(bias_tbl.shape),
        _const_spec(sink_row.shape),
        _const_spec((1, d_gm)),
        pl.BlockSpec((1,) + wsp.shape[1:], lambda i: (kind(i), 0, 0, 0)),
        pl.BlockSpec((1, GM_CHUNK, d_gm), lambda i: (kind(i), 0, 0)),
        _const_spec(w_out.shape),
    ]
    out_specs = [
        pl.BlockSpec((tm, d), lambda i: (i, 0)),
        pl.BlockSpec((tm, D_KV), lambda i: (i, 0)),
        pl.BlockSpec((tm, D_KV), lambda i: (i, 0)),
        pl.BlockSpec((tm, d_gm), lambda i: (sample_block(i), 0)),
    ]
    out_shape = [
        jax.ShapeDtypeStruct((m_rows, d), F32),
        jax.ShapeDtypeStruct((m_rows, D_KV), F32),
        jax.ShapeDtypeStruct((m_rows, D_KV), F32),
        jax.ShapeDtypeStruct((n_sample_rows, d_gm), F32),
    ]
    scratch = [
        pltpu.VMEM((tm, d_in), F32),
        pltpu.VMEM((tm, D_ATTN), BF16),
        pltpu.VMEM((tm, D_ATTN + d_gm), BF16),
        pltpu.VMEM((N_KV_HEADS, GM_CHUNK, LANES), BF16),
        pltpu.VMEM((N_KV_HEADS, GM_CHUNK, LANES), BF16),
    ]
    return pl.pallas_call(
        functools.partial(_mix0_kernel, n_prompt_tiles, units),
        grid=(n_tiles,),
        in_specs=in_specs,
        out_specs=out_specs,
        out_shape=out_shape,
        scratch_shapes=scratch,
        compiler_params=pltpu.CompilerParams(
            dimension_semantics=("arbitrary",), vmem_limit_bytes=VMEM_LIMIT_BYTES),
        name="mix0",
    )(xp, xs, cache_k, cache_v, g, w_in, bias_tbl, sink_row, gm_norm, wsp, gmb, w_out)


def _ffn_kernel(final, x_ref, g_ref, wg_ref, wu_ref, wd_ref, gnext_ref, out_ref, *rest):
    h_ref = rest[-1]
    j = pl.program_id(1)

    @pl.when(j == 0)
    def _():
        x = x_ref[...]
        h_ref[...] = (_rms_scale(x) * g_ref[...]).astype(BF16)
        out_ref[...] = x

    h = h_ref[...]
    gate = jnp.dot(h, wg_ref[0].astype(BF16), preferred_element_type=F32)
    up = jnp.dot(h, wu_ref[0].astype(BF16), preferred_element_type=F32)
    act = (gate * jax.nn.sigmoid(gate) * up).astype(BF16)
    out_ref[...] += jnp.dot(act, wd_ref[0].astype(BF16), preferred_element_type=F32)

    @pl.when(j == pl.num_programs(1) - 1)
    def _():
        out = out_ref[...]
        scale = lax.rsqrt(jnp.mean(out * out, axis=-1, keepdims=True) + RMS_EPS)
        if final:
            out_ref[...] = out * scale * gnext_ref[...]
        else:
            rest[0][...] = jnp.broadcast_to(scale, rest[0].shape)


def _ffn(x, g, wg, wu, wd, gnext, *, layer, tm, tf, final, row0=0, n_rows=None):
    d = x.shape[1]
    n_rows = x.shape[0] if n_rows is None else n_rows
    f = wg.shape[2]
    assert n_rows % tm == 0 and row0 % tm == 0 and f % tf == 0
    tile0 = row0 // tm
    out_specs = [pl.BlockSpec((tm, d), lambda i, j: (i, 0))]
    out_shape = [jax.ShapeDtypeStruct((n_rows, d), F32)]
    if not final:
        out_specs.append(pl.BlockSpec((tm, LANES), lambda i, j: (i, 0)))
        out_shape.append(jax.ShapeDtypeStruct((n_rows, LANES), F32))
    return pl.pallas_call(
        functools.partial(_ffn_kernel, final),
        grid=(n_rows // tm, f // tf),
        in_specs=[
            pl.BlockSpec((tm, d), lambda i, j: (i + tile0, 0)),
            pl.BlockSpec((1, d), lambda i, j: (0, 0)),
            pl.BlockSpec((1, d, tf), lambda i, j: (layer, 0, j)),
            pl.BlockSpec((1, d, tf), lambda i, j: (layer, 0, j)),
            pl.BlockSpec((1, tf, d), lambda i, j: (layer, j, 0)),
            pl.BlockSpec((1, d), lambda i, j: (0, 0)),
        ],
        out_specs=out_specs,
        out_shape=out_shape,
        scratch_shapes=[pltpu.VMEM((tm, d), BF16)],
        compiler_params=pltpu.CompilerParams(
            dimension_semantics=("arbitrary", "arbitrary"), vmem_limit_bytes=VMEM_LIMIT_BYTES),
        name="ffn",
    )(x, g, wg, wu, wd, gnext)


TAB_ROWS = 18


def _discretize(lr, li, log_dt):
    dt = jnp.exp(log_dt)
    mag = jnp.exp(lr * dt)
    ar = mag * jnp.cos(li * dt)
    ai = mag * jnp.sin(li * dt)
    den = lr * lr + li * li
    nr = ar - 1.0
    return ar, ai, (nr * lr + ai * li) / den, (ai * lr - nr * li) / den


def _cmul(pr, pi, qr, qi):
    return pr * qr - pi * qi, pr * qi + pi * qr


def _s5_prep_kernel(seg_rows, lam_ref, lamc_ref, b_ref, c_ref, dall_ref, wec_ref, wsc_ref, tab_ref,
                    bhi_ref, blo_ref):
    ns = S5_BLOCK_STATE

    ar, ai, fr, fi = _discretize(lamc_ref[0, 0], lamc_ref[0, 1], lamc_ref[0, 2])
    bbr, bbi = _cmul(fr, fi, b_ref[0, 0], b_ref[0, 1])
    c_re = c_ref[0, 0]
    c_im = c_ref[0, 1]

    def split(a):
        hi = a.astype(BF16)
        return hi, (a - hi.astype(F32)).astype(BF16)

    power = (jnp.ones_like(ar), jnp.zeros_like(ar))
    for l in range(S5_L):
        bkr, bki = _cmul(*power, bbr, bbi)
        power = _cmul(*power, ar, ai)
        rows = slice(l * LANES, (l + 1) * LANES)
        bk = jnp.concatenate([bkr, bki], axis=1)
        bhi_ref[rows, :], blo_ref[rows, :] = split(bk)
        wec_ref[0, (S5_L - 1 - l) * LANES:(S5_L - l) * LANES, :] = bk
        qr, qi = power
        wsc_ref[0, rows, :] = jnp.concatenate([c_re * qr - c_im * qi, -c_re * qi - c_im * qr], axis=1)

    c_hi, c_lo = split(jnp.concatenate([c_re, -c_im], axis=1))
    c_parts = jnp.concatenate([c_hi, c_lo], axis=0)
    dims = (((1,), (1,)), ((), ()))
    d_hi = lax.dot_general(bhi_ref[...], c_parts, dims, preferred_element_type=F32)
    d_lo = lax.dot_general(blo_ref[...], c_parts, dims, preferred_element_type=F32)
    d_all = d_hi[:, 0:LANES] + d_hi[:, LANES:2 * LANES] + d_lo[:, 0:LANES]
    shape = (S5_L * LANES, LANES)
    in_group = (lax.broadcasted_iota(jnp.int32, shape, 0) % LANES) // S5_GROUP
    out_group = lax.broadcasted_iota(jnp.int32, shape, 1) // S5_GROUP
    dall_ref[0] = jnp.where(in_group == out_group, d_all, 0.0).astype(BF16)

    ar, ai, _, _ = _discretize(lam_ref[0, 0], lam_ref[0, 1], lam_ref[0, 2])
    row = lax.broadcasted_iota(jnp.int32, (SUBLANES, ns), 0)

    def bcast(a):
        return jnp.broadcast_to(a, (SUBLANES, ns))

    def log_step_tables(t0, base):
        cur = base
        for n, shift in enumerate((1, 2, 4)):
            tab_ref[0, t0 + 2 * n] = jnp.where(row >= shift, bcast(cur[0]), 0.0)
            tab_ref[0, t0 + 2 * n + 1] = jnp.where(row >= shift, bcast(cur[1]), 0.0)
            cur = _cmul(*cur, *cur)
        return cur

    a1 = (ar, ai)
    for _ in range(S5_L.bit_length() - 1):
        a1 = _cmul(*a1, *a1)
    a8 = log_step_tables(0, a1)
    pr_tab = jnp.zeros((SUBLANES, ns), F32)
    pi_tab = jnp.zeros((SUBLANES, ns), F32)
    cur = a1
    for r in range(SUBLANES):
        pr_tab = jnp.where(row == r, bcast(cur[0]), pr_tab)
        pi_tab = jnp.where(row == r, bcast(cur[1]), pi_tab)
        cur = _cmul(*cur, *a1)
    tab_ref[0, 6] = pr_tab
    tab_ref[0, 7] = pi_tab
    tab_ref[0, 8] = bcast(a8[0])
    tab_ref[0, 9] = bcast(a8[1])
    tab_ref[0, 10] = bcast(a1[0])
    tab_ref[0, 11] = bcast(a1[1])
    seg = a1
    for _ in range(seg_rows.bit_length() - 1):
        seg = _cmul(*seg, *seg)
    log_step_tables(12, seg)


def _s5_prep(lam, lam_c, b_c, c_c, *, seg_rows):
    assert seg_rows & (seg_rows - 1) == 0 and S5_L & (S5_L - 1) == 0
    nb = lam.shape[0]
    ns = S5_BLOCK_STATE
    k = S5_L * LANES
    cspec = pl.BlockSpec((1, k, LANES), lambda j: (j, 0, 0))

    def whole(a):
        return pl.BlockSpec((1,) + a.shape[1:], lambda j: (j,) + (0,) * (a.ndim - 1))

    return pl.pallas_call(
        functools.partial(_s5_prep_kernel, seg_rows),
        grid=(nb,),
        in_specs=[whole(lam), whole(lam_c), whole(b_c), whole(c_c)],
        out_specs=[cspec, cspec, cspec,
                   pl.BlockSpec((1, TAB_ROWS, SUBLANES, ns), lambda j: (j, 0, 0, 0))],
        out_shape=[
            jax.ShapeDtypeStruct((nb, k, LANES), BF16),
            jax.ShapeDtypeStruct((nb, k, LANES), F32),
            jax.ShapeDtypeStruct((nb, k, LANES), F32),
            jax.ShapeDtypeStruct((nb, TAB_ROWS, SUBLANES, ns), F32),
        ],
        scratch_shapes=[pltpu.VMEM((k, LANES), BF16), pltpu.VMEM((k, LANES), BF16)],
        compiler_params=pltpu.CompilerParams(
            dimension_semantics=("arbitrary",), vmem_limit_bytes=VMEM_LIMIT_BYTES),
        name="s5_prep",
    )(lam, lam_c, b_c, c_c)


def _s5_core_kernel(n_prompt_rows, n_streams,
                    x_ref, rstd_ref, g_ref, dall_ref, wec_ref, wsc_ref, tab_ref, s0r_ref, s0i_ref,
                    d_ref, wglu_ref,
                    y_ref, sfr_ref, sfi_ref, wglu_bf16_ref,
                    u2f_ref, u2_ref, e_ref, wt_ref, we_ref, wst_ref):
    wglu_bf16_ref[...] = wglu_ref[...].astype(BF16)
    ns = S5_BLOCK_STATE
    n_rows = x_ref.shape[0] // S5_L
    n_sample_rows = n_rows - n_prompt_rows
    seg_rows = n_prompt_rows // SUBLANES
    n_prompt_frames = n_prompt_rows * S5_L

    shape = (S5_L * LANES, LANES)
    lane = lax.broadcasted_iota(jnp.int32, shape, 1)
    lane_lo = lane < S5_STATE
    row_group = (lax.broadcasted_iota(jnp.int32, shape, 0) % LANES) // S5_GROUP
    groups_per_vreg = LANES // S5_STATE

    def expand(compact_ref, full_ref):
        x = compact_ref[0]
        r = pltpu.roll(x, S5_STATE, axis=1)
        for part, rep in enumerate((jnp.where(lane_lo, x, r), jnp.where(lane_lo, r, x))):
            for v in range(S5_LANE_GROUPS // groups_per_vreg):
                own = row_group == groups_per_vreg * v + (lane // S5_STATE)
                col = part * S5_BLOCK_STATE + v * LANES
                full_ref[:, col:col + LANES] = jnp.where(own, rep, 0.0).astype(BF16)

    expand(wec_ref, we_ref)
    expand(wsc_ref, wst_ref)
    zero_blk = jnp.zeros((LANES, LANES), BF16)
    for l in range(S5_L):
        for l2 in range(S5_L):
            blk = dall_ref[0, (l2 - l) * LANES:(l2 - l + 1) * LANES, :] if l2 >= l else zero_blk
            wt_ref[l * LANES:(l + 1) * LANES, l2 * LANES:(l2 + 1) * LANES] = blk

    def frame_slices():
        for l in range(S5_L):
            for seg in range(SUBLANES):
                yield l, pl.ds(seg * seg_rows * S5_L + l, seg_rows, stride=S5_L), \
                    pl.ds(seg, seg_rows, stride=SUBLANES)
            yield l, pl.ds(n_prompt_frames + l, n_sample_rows, stride=S5_L), \
                pl.ds(n_prompt_rows, n_sample_rows)

    for l, frames, rows in frame_slices():
        u2f_ref[l, rows, :] = x_ref[frames, :] * rstd_ref[frames, :] * g_ref[...]
    for l in range(S5_L):
        u2_ref[:, l * LANES:(l + 1) * LANES] = u2f_ref[l].astype(BF16)
    e_ref[...] = jnp.dot(u2_ref[...], we_ref[...], preferred_element_type=F32)

    tabs = [tab_ref[0, t] for t in range(TAB_ROWS)]
    m_tabs, (pr, pi, a8r, a8i, ar, ai), b_tabs = tabs[0:6], tabs[6:12], tabs[12:18]
    row = lax.broadcasted_iota(jnp.int32, (SUBLANES, ns), 0)

    def log_step_scan(xr, xi, t):
        for n, shift in enumerate((1, 2, 4)):
            tr, ti = t[2 * n], t[2 * n + 1]
            sr = pltpu.roll(xr, shift, axis=0)
            si = pltpu.roll(xi, shift, axis=0)
            xr, xi = xr + tr * sr - ti * si, xi + tr * si + ti * sr
        return xr, xi

    def shift_down(xr, xi, fr, fi):
        first = row == 0
        return (jnp.where(first, fr, pltpu.roll(xr, 1, axis=0)),
                jnp.where(first, fi, pltpu.roll(xi, 1, axis=0)))

    def rows_of(k):
        return pl.ds(pl.multiple_of(k * SUBLANES, SUBLANES), SUBLANES)

    def local_step(k, carry):
        sr, si = carry
        er = e_ref[rows_of(k), 0:ns]
        ei = e_ref[rows_of(k), ns:2 * ns]
        e_ref[rows_of(k), 0:ns] = sr
        e_ref[rows_of(k), ns:2 * ns] = si
        return ar * sr - ai * si + er, ar * si + ai * sr + ei

    zero = jnp.zeros((SUBLANES, ns), F32)
    ends = lax.fori_loop(0, seg_rows, local_step, (zero, zero), unroll=True)
    ends = log_step_scan(*ends, b_tabs)
    sfr_ref[...] = jnp.zeros_like(sfr_ref)
    sfi_ref[...] = jnp.zeros_like(sfi_ref)
    sfr_ref[n_streams:n_streams + 1, :] = ends[0][SUBLANES - 1:SUBLANES]
    sfi_ref[n_streams:n_streams + 1, :] = ends[1][SUBLANES - 1:SUBLANES]

    def correct_step(k, carry):
        cr, ci = carry
        e_ref[rows_of(k), 0:ns] += cr
        e_ref[rows_of(k), ns:2 * ns] += ci
        return ar * cr - ai * ci, ar * ci + ai * cr

    lax.fori_loop(0, seg_rows, correct_step, shift_down(*ends, zero, zero), unroll=True)

    for b in range(n_streams):
        rows = pl.ds(n_prompt_rows + b * SUBLANES, SUBLANES)
        cr = jnp.broadcast_to(s0r_ref[b:b + 1, :], (SUBLANES, ns))
        ci = jnp.broadcast_to(s0i_ref[b:b + 1, :], (SUBLANES, ns))
        xr, xi = log_step_scan(e_ref[rows, 0:ns], e_ref[rows, ns:2 * ns], m_tabs)
        st_r = xr + pr * cr - pi * ci
        st_i = xi + pr * ci + pi * cr
        e_ref[rows, 0:ns], e_ref[rows, ns:2 * ns] = shift_down(st_r, st_i, cr, ci)
        sfr_ref[b:b + 1, :] = st_r[SUBLANES - 1:SUBLANES]
        sfi_ref[b:b + 1, :] = st_i[SUBLANES - 1:SUBLANES]

    sprev = e_ref[...].astype(BF16)
    slabs_per_block = S5_WT_BLOCK // LANES
    for cb in range(S5_L * LANES // S5_WT_BLOCK):
        cols = slice(cb * S5_WT_BLOCK, (cb + 1) * S5_WT_BLOCK)
        k_end = (cb + 1) * S5_WT_BLOCK
        y2 = jnp.dot(u2_ref[:, 0:k_end], wt_ref[0:k_end, cols], preferred_element_type=F32)
        y2 = y2 + lax.dot_general(sprev, wst_ref[cols, :], (((1,), (1,)), ((), ())),
                                  preferred_element_type=F32)
        for n in range(slabs_per_block):
            l = cb * slabs_per_block + n
            u2f_ref[l] = y2[:, n * LANES:(n + 1) * LANES] + d_ref[...] * u2f_ref[l]
    for l, frames, rows in frame_slices():
        y_ref[frames, :] = u2f_ref[l, rows, :]


def _s5_core(x, rstd, g, dall, wec, wsc, tab, s0_re, s0_im, d_skip, w_glu, *, n_prompt_rows):
    m_rows, d = x.shape
    nb = d // LANES
    glu_cols = w_glu.shape[1] // nb
    assert glu_cols % LANES == 0
    glu_blk = pl.BlockSpec((w_glu.shape[0], glu_cols), lambda j: (0, j))
    ns = S5_BLOCK_STATE
    k = S5_L * LANES
    n_rows = m_rows // S5_L
    n_streams = s0_re.shape[0]
    assert n_prompt_rows % (S5_L * SUBLANES) == 0
    assert (m_rows - n_prompt_rows) == n_streams * S5_L * SUBLANES
    sf_rows = n_streams + SUBLANES
    lane_blk = pl.BlockSpec((m_rows, LANES), lambda j: (0, j))
    cspec = pl.BlockSpec((1, k, LANES), lambda j: (j, 0, 0))
    st = pl.BlockSpec((n_streams, ns), lambda j: (0, j))
    sf = pl.BlockSpec((sf_rows, ns), lambda j: (0, j))
    return pl.pallas_call(
        functools.partial(_s5_core_kernel, n_prompt_rows // S5_L, n_streams),
        grid=(nb,),
        in_specs=[
            lane_blk,
            _const_spec((m_rows, LANES)),
            pl.BlockSpec((1, LANES), lambda j: (0, j)),
            cspec, cspec, cspec,
            pl.BlockSpec((1, TAB_ROWS, SUBLANES, ns), lambda j: (j, 0, 0, 0)),
            st, st,
            pl.BlockSpec((1, LANES), lambda j: (0, j)),
            glu_blk,
        ],
        out_specs=[lane_blk, sf, sf, glu_blk],
        out_shape=[
            jax.ShapeDtypeStruct((m_rows, d), F32),
            jax.ShapeDtypeStruct((sf_rows, nb * ns), F32),
            jax.ShapeDtypeStruct((sf_rows, nb * ns), F32),
            jax.ShapeDtypeStruct(w_glu.shape, BF16),
        ],
        scratch_shapes=[
            pltpu.VMEM((S5_L, n_rows, LANES), F32),
            pltpu.VMEM((n_rows, k), BF16),
            pltpu.VMEM((n_rows, 2 * ns), F32),
            pltpu.VMEM((k, k), BF16),
            pltpu.VMEM((k, 2 * ns), BF16),
            pltpu.VMEM((k, 2 * ns), BF16),
        ],
        compiler_params=pltpu.CompilerParams(
            dimension_semantics=("arbitrary",), vmem_limit_bytes=VMEM_LIMIT_BYTES),
        name="s5_core",
    )(x, rstd, g, dall, wec, wsc, tab, s0_re, s0_im, d_skip, w_glu)


def _glu_kernel(n_col_blocks, x_ref, y_ref, w_ref, out_ref):
    d = x_ref.shape[1]
    a = _gelu(y_ref[...]).astype(BF16)
    cb = d // n_col_blocks
    for n in range(n_col_blocks):
        cols = slice(n * cb, (n + 1) * cb)
        val = jnp.dot(a, w_ref[:, n * cb:(n + 1) * cb], preferred_element_type=F32)
        gate = jnp.dot(a, w_ref[:, d + n * cb:d + (n + 1) * cb], preferred_element_type=F32)
        out_ref[:, cols] = x_ref[:, cols] + val * jax.nn.sigmoid(gate)


def _glu(x, y, w, *, tm, n_col_blocks=4):
    m_rows, d = x.shape
    assert m_rows % tm == 0 and d % n_col_blocks == 0
    tile = pl.BlockSpec((tm, d), lambda i: (i, 0))
    return pl.pallas_call(
        functools.partial(_glu_kernel, n_col_blocks),
        grid=(m_rows // tm,),
        in_specs=[tile, tile, _const_spec(w.shape)],
        out_specs=tile,
        out_shape=jax.ShapeDtypeStruct((m_rows, d), F32),
        compiler_params=pltpu.CompilerParams(
            dimension_semantics=("arbitrary",), vmem_limit_bytes=VMEM_LIMIT_BYTES),
        name="glu",
    )(x, y, w)


def _attention_tables(sinks):
    slopes = jnp.exp2(-8.0 * jnp.arange(1, N_HEADS + 1, dtype=F32) / N_HEADS)
    frame = jnp.arange(CHUNK, dtype=F32)[None, :]
    band = jnp.arange(BAND, dtype=F32)[:, None]
    dist = jnp.abs(frame - (band - WINDOW))
    bias = (dist[:, None, :] * slopes[None, :, None]).reshape(BAND, N_HEADS * CHUNK)
    sink_row = jnp.repeat(sinks.astype(F32), CHUNK).reshape(1, N_HEADS * CHUNK)
    return bias, sink_row


def _gm_tables(gm_ws, gm_b, d_gm):
    blk = jnp.arange(GM_CHUNK) // CHUNK
    w_prompt = jnp.where((blk[:, None] >= blk[None, :])[None], gm_ws, 0.0)
    top = gm_ws[:, :CHUNK, :CHUNK]
    zeros = jnp.zeros_like(top)
    w_sample = jnp.concatenate(
        [jnp.concatenate([top, zeros], axis=2), jnp.concatenate([zeros, top], axis=2)], axis=1)
    wsp = jnp.stack([w_prompt, w_sample])
    wsp = wsp.reshape(2, N_GM_GROUPS // 2, 2, GM_CHUNK, GM_CHUNK).transpose(0, 1, 3, 2, 4)
    wsp = wsp.reshape(2, N_GM_GROUPS // 2, GM_CHUNK, 2 * GM_CHUNK).astype(BF16)
    b_prompt = gm_b.T
    b_sample = jnp.concatenate([gm_b[:, :CHUNK].T, gm_b[:, :CHUNK].T], axis=0)
    gmb = jnp.stack([b_prompt, b_sample]).astype(F32)
    gmb = jnp.repeat(gmb, d_gm // N_GM_GROUPS, axis=2)
    return wsp, gmb


def _s5_compact(lam_re, lam_im, log_dt, b_re, b_im, c_re, c_im):
    n_groups = lam_re.shape[0]
    nb = n_groups // S5_LANE_GROUPS
    per_state = jnp.stack([lam_re, lam_im, jnp.broadcast_to(log_dt[:, None], lam_re.shape)]).astype(F32)
    per_state = per_state.reshape(3, nb, S5_LANE_GROUPS, S5_STATE).transpose(1, 0, 2, 3)
    lam = per_state.reshape(nb, 3, 1, S5_BLOCK_STATE)
    lam_c = jnp.broadcast_to(per_state[:, :, :, None, :],
                             (nb, 3, S5_LANE_GROUPS, S5_GROUP, S5_STATE)).reshape(nb, 3, LANES, S5_STATE)
    b_c = jnp.stack([b_re, b_im]).astype(F32).transpose(1, 0, 3, 2)
    b_c = b_c.reshape(nb, S5_LANE_GROUPS, 2, S5_GROUP, S5_STATE).transpose(0, 2, 1, 3, 4)
    c_c = jnp.stack([c_re, c_im]).astype(F32).transpose(1, 0, 2, 3)
    c_c = c_c.reshape(nb, S5_LANE_GROUPS, 2, S5_GROUP, S5_STATE).transpose(0, 2, 1, 3, 4)
    return (lam, lam_c, b_c.reshape(nb, 2, LANES, S5_STATE), c_c.reshape(nb, 2, LANES, S5_STATE))


def kernel(x_prompt, x_sample, cache_swa_k, cache_swa_v, state_s5_re, state_s5_im, norm_mix, norm_ffn, norm_final, w_in0, attn_sinks, gm_norm, gm_ws, gm_b, w_out0, s5_lam_re, s5_lam_im, s5_log_dt, s5_b_re, s5_b_im, s5_c_re, s5_c_im, s5_d, s5_w_glu, ffn_w_gate, ffn_w_up, ffn_w_down):
    batch, seq, d = x_prompt.shape
    dec_batch, dec_seq, _ = x_sample.shape
    assert batch == 1 and dec_seq == CHUNK and norm_mix.shape[0] == 2
    n_prompt = batch * seq
    n_sample = dec_batch * dec_seq
    d_gm = gm_norm.shape[-1]

    bias_tbl, sink_row = _attention_tables(attn_sinks[0])
    wsp, gmb = _gm_tables(gm_ws[0], gm_b[0], d_gm)
    x1, k_all, v_all, gvn = _mix0(
        x_prompt.reshape(n_prompt, d), x_sample.reshape(n_sample, d),
        cache_swa_k[0].reshape(dec_batch * WINDOW, D_KV), cache_swa_v[0].reshape(dec_batch * WINDOW, D_KV),
        norm_mix[0].reshape(1, d), w_in0[0].astype(BF16), bias_tbl, sink_row,
        gm_norm[0].reshape(1, d_gm), wsp, gmb, w_out0[0].astype(BF16), units=2)

    x2, rstd2 = _ffn(x1, norm_ffn[0].reshape(1, d), ffn_w_gate, ffn_w_up, ffn_w_down,
                     norm_mix[1].reshape(1, d), layer=0, tm=FFN_TM, tf=FFN_TF, final=False)

    n_groups = s5_lam_re.shape[1]
    dall, wec, wsc, tab = _s5_prep(
        *_s5_compact(s5_lam_re[0], s5_lam_im[0], s5_log_dt[0], s5_b_re[0], s5_b_im[0], s5_c_re[0],
                     s5_c_im[0]),
        seg_rows=n_prompt // (S5_L * SUBLANES))
    ys5, sf_re, sf_im, w_glu_bf16 = _s5_core(
        x2, rstd2, norm_mix[1].reshape(1, d), dall, wec, wsc, tab,
        state_s5_re[0].reshape(dec_batch, n_groups * S5_STATE),
        state_s5_im[0].reshape(dec_batch, n_groups * S5_STATE), s5_d[0].reshape(1, d), s5_w_glu[0],
        n_prompt_rows=n_prompt)
    x3 = _glu(x2, ys5, w_glu_bf16, tm=256)

    def last_ffn(row0, n_rows):
        (y,) = _ffn(x3, norm_ffn[1].reshape(1, d), ffn_w_gate, ffn_w_up, ffn_w_down,
                    norm_final.reshape(1, d), layer=1, tm=FFN_TM, tf=FFN_TF, final=True,
                    row0=row0, n_rows=n_rows)
        return y

    y_prompt = last_ffn(0, n_prompt)
    y_sample = last_ffn(n_prompt, n_sample)

    keep = min(WINDOW, seq)
    y_prompt = y_prompt.reshape(batch, seq, d)
    y_sample = y_sample.reshape(dec_batch, dec_seq, d)
    kv_shape_p = (1, batch, keep, N_KV_HEADS, HEAD_DIM)
    kv_shape_s = (1, dec_batch, dec_seq, N_KV_HEADS, HEAD_DIM)
    st_p = (1, batch, n_groups, S5_STATE)
    st_s = (1, dec_batch, n_groups, S5_STATE)
    return (y_prompt, y_sample,
            k_all[n_prompt - keep:n_prompt].reshape(kv_shape_p),
            v_all[n_prompt - keep:n_prompt].reshape(kv_shape_p),
            k_all[n_prompt:].reshape(kv_shape_s),
            v_all[n_prompt:].reshape(kv_shape_s),
            gvn.reshape(1, dec_batch, dec_seq, d_gm),
            sf_re[dec_batch].reshape(st_p), sf_im[dec_batch].reshape(st_p),
            sf_re[:dec_batch].reshape(st_s), sf_im[:dec_batch].reshape(st_s))
```

```python
import functools
import math

import jax
import jax.numpy as jnp
from jax import lax
from jax.experimental import pallas as pl
from jax.experimental.pallas import tpu as pltpu

F32 = jnp.float32
BF16 = jnp.bfloat16

CHUNK = 64
HEAD_DIM = 64
N_HEADS = 16
N_KV_HEADS = 2
Q_PER_KV = N_HEADS // N_KV_HEADS
WINDOW = 128
BAND = WINDOW + CHUNK
D_ATTN = N_HEADS * HEAD_DIM
D_KV = N_KV_HEADS * HEAD_DIM
GM_CHUNK = 128
N_GM_GROUPS = 16
S5_GROUP = 16
S5_STATE = 64
RMS_EPS = 1e-5
NEG_INF = -1e30

LANES = 128
SUBLANES = 8
VMEM_LIMIT_BYTES = 56 * 1024 * 1024

FFN_TM = 1024
FFN_TF = 256

S5_L = SUBLANES
S5_LANE_GROUPS = LANES // S5_GROUP
S5_BLOCK_STATE = S5_LANE_GROUPS * S5_STATE
S5_WT_BLOCK = 256


def _gelu(x):
    return 0.5 * x * (1.0 + lax.erf(x * math.sqrt(0.5)))


def _rms_scale(x):
    return x * lax.rsqrt(jnp.mean(x * x, axis=-1, keepdims=True) + RMS_EPS)


def _const_spec(shape):
    zeros = (0,) * len(shape)
    return pl.BlockSpec(shape, lambda *_: zeros, pipeline_mode=pl.Buffered(1))


def _mix0_kernel(n_prompt_tiles, units,
                 xp_ref, xs_ref, ck_ref, cv_ref, g_ref, win_ref, bias_ref, sink_ref, gmn_ref,
                 wsp_ref, gmb_ref, wout_ref,
                 x1_ref, k_ref, v_ref, gvn_ref,
                 z_ref, q_ref, ocat_ref, kprev_ref, vprev_ref):
    i = pl.program_id(0)
    is_sample = i >= n_prompt_tiles
    tm = units * GM_CHUNK
    d_gm = gmn_ref.shape[-1]
    off_k = D_ATTN
    off_v = D_ATTN + D_KV
    off_gu = D_ATTN + 2 * D_KV
    off_gv = off_gu + d_gm
    n_q = Q_PER_KV * CHUNK

    @pl.when(i == 0)
    def _():
        kprev_ref[...] = jnp.zeros_like(kprev_ref)
        vprev_ref[...] = jnp.zeros_like(vprev_ref)

    x = jnp.where(is_sample, xs_ref[...], xp_ref[...])
    h = (_rms_scale(x) * g_ref[...]).astype(BF16)
    z_ref[...] = jnp.dot(h, win_ref[...], preferred_element_type=F32)

    q_ref[...] = (z_ref[:, 0:D_ATTN] * (HEAD_DIM ** -0.5)).astype(BF16)
    k = z_ref[:, off_k:off_k + D_KV]
    v = z_ref[:, off_v:off_v + D_KV]
    k_ref[...] = k
    v_ref[...] = v

    def lane_lo(rows):
        return lax.broadcasted_iota(jnp.int32, (rows, LANES), 1) < HEAD_DIM

    def replicate(a):
        r = pltpu.roll(a, HEAD_DIM, axis=1)
        lo = lane_lo(a.shape[0])
        return jnp.where(lo, a, r).astype(BF16), jnp.where(lo, r, a).astype(BF16)

    k_rep = replicate(k)
    v_rep = replicate(v)
    ck_rep = replicate(ck_ref[...])
    cv_rep = replicate(cv_ref[...])

    lo64 = lane_lo(CHUNK)
    band_pos = lax.broadcasted_iota(jnp.int32, (BAND, n_q), 0)
    chunks_per_tile = tm // CHUNK

    def scores(u, c2, kv):
        r0 = u * GM_CHUNK + c2 * CHUNK
        stream = r0 // CHUNK
        chunk_index = i * chunks_per_tile + stream
        valid_from = jnp.where(is_sample, 0, jnp.maximum(WINDOW - CHUNK * chunk_index, 0))
        valid = band_pos >= valid_from

        def band(cur, prev_ref, cached):
            if u == 0:
                prev_unit = prev_ref[kv]
            else:
                prev_unit = cur[(u - 1) * GM_CHUNK:u * GM_CHUNK]
            if c2 == 0:
                prompt_prev = prev_unit
            else:
                prompt_prev = jnp.concatenate(
                    [prev_unit[CHUNK:], cur[u * GM_CHUNK:u * GM_CHUNK + CHUNK]], axis=0)
            sample_prev = cached[stream * WINDOW:(stream + 1) * WINDOW]
            prev = jnp.where(is_sample, sample_prev, prompt_prev)
            return jnp.concatenate([prev, cur[r0:r0 + CHUNK]], axis=0)

        kb = band(k_rep[kv], kprev_ref, ck_rep[kv])
        vb = band(v_rep[kv], vprev_ref, cv_rep[kv])

        pieces = []
        for m in range(Q_PER_KV // 2):
            c0 = kv * Q_PER_KV * HEAD_DIM + m * LANES
            qp = q_ref[r0:r0 + CHUNK, c0:c0 + LANES]
            pieces.append(jnp.where(lo64, qp, jnp.zeros_like(qp)))
            pieces.append(jnp.where(lo64, jnp.zeros_like(qp), qp))
        qs = jnp.concatenate(pieces, axis=0)
        cols = slice(kv * n_q, (kv + 1) * n_q)
        st = lax.dot_general(kb, qs, (((1,), (1,)), ((), ())),
                             preferred_element_type=F32)
        return jnp.where(valid, st - bias_ref[:, cols], NEG_INF), vb

    def attend(u, c2, kv, st, vb):
        r0 = u * GM_CHUNK + c2 * CHUNK
        cols = slice(kv * n_q, (kv + 1) * n_q)
        sink = sink_ref[:, cols]
        mx = jnp.maximum(jnp.max(st, axis=0, keepdims=True), sink)
        p = jnp.exp(st - mx)
        denom = jnp.sum(p, axis=0, keepdims=True) + jnp.exp(sink - mx)
        pn = (p * (1.0 / denom)).astype(BF16)
        o = lax.dot_general(pn, vb, (((0,), (0,)), ((), ())),
                            preferred_element_type=F32)
        for m in range(Q_PER_KV // 2):
            o_pair = jnp.where(lo64, o[(2 * m) * CHUNK:(2 * m + 1) * CHUNK],
                               o[(2 * m + 1) * CHUNK:(2 * m + 2) * CHUNK])
            c0 = kv * Q_PER_KV * HEAD_DIM + m * LANES
            ocat_ref[r0:r0 + CHUNK, c0:c0 + LANES] = o_pair.astype(BF16)

    blocks = [(u, c2, kv) for u in range(units) for c2 in range(GM_CHUNK // CHUNK)
              for kv in range(N_KV_HEADS)]
    scored = [scores(*blk) for blk in blocks]
    for blk, (st, vb) in zip(blocks, scored):
        attend(*blk, st, vb)

    lo128 = lane_lo(GM_CHUNK)
    for u in range(units):
        rows = slice(u * GM_CHUNK, (u + 1) * GM_CHUNK)
        ua = _gelu(z_ref[rows, off_gu:off_gu + d_gm])
        gvn = _rms_scale(_gelu(z_ref[rows, off_gv:off_gv + d_gm])) * gmn_ref[...]
        gvn_ref[rows, :] = gvn
        gb = gvn.astype(BF16)
        for m in range(N_GM_GROUPS // 2):
            cols = slice(m * LANES, (m + 1) * LANES)
            rhs = gb[:, cols]
            rhs2 = jnp.concatenate([jnp.where(lo128, rhs, jnp.zeros_like(rhs)),
                                    jnp.where(lo128, jnp.zeros_like(rhs), rhs)], axis=0)
            sp = jnp.dot(wsp_ref[0, m], rhs2, preferred_element_type=F32) + gmb_ref[0, :, cols]
            ocat_ref[rows, D_ATTN + m * LANES:D_ATTN + (m + 1) * LANES] = (ua[:, cols] * sp).astype(BF16)

    x1_ref[...] = x + jnp.dot(ocat_ref[...], wout_ref[...], preferred_element_type=F32)

    for kv in range(N_KV_HEADS):
        kprev_ref[kv] = k_rep[kv][tm - GM_CHUNK:]
        vprev_ref[kv] = v_rep[kv][tm - GM_CHUNK:]


def _mix0(xp, xs, cache_k, cache_v, g, w_in, bias_tbl, sink_row, gm_norm, wsp, gmb, w_out, *, units):
    n_prompt_rows, d = xp.shape
    n_sample_rows = xs.shape[0]
    m_rows = n_prompt_rows + n_sample_rows
    tm = units * GM_CHUNK
    assert n_prompt_rows % tm == 0 and n_sample_rows % tm == 0
    n_tiles = m_rows // tm
    n_prompt_tiles = n_prompt_rows // tm
    d_in = w_in.shape[1]
    d_gm = gm_norm.shape[-1]
    cache_rows = (tm // CHUNK) * WINDOW

    def prompt_block(i):
        return jnp.minimum(i, n_prompt_tiles - 1)

    def sample_block(i):
        return jnp.maximum(i - n_prompt_tiles, 0)

    def kind(i):
        return jnp.where(i >= n_prompt_tiles, 1, 0)

    in_specs = [
        pl.BlockSpec((tm, d), lambda i: (prompt_block(i), 0)),
        pl.BlockSpec((tm, d), lambda i: (sample_block(i), 0)),
        pl.BlockSpec((cache_rows, D_KV), lambda i: (sample_block(i), 0)),
        pl.BlockSpec((cache_rows, D_KV), lambda i: (sample_block(i), 0)),
        _const_spec((1, d)),
        _const_spec((d, d_in)),
        _const_spec(bias_tbl.shape),
        _const_spec(sink_row.shape),
        _const_spec((1, d_gm)),
        pl.BlockSpec((1,) + wsp.shape[1:], lambda i: (kind(i), 0, 0, 0)),
        pl.BlockSpec((1, GM_CHUNK, d_gm), lambda i: (kind(i), 0, 0)),
        _const_spec(w_out.shape),
    ]
    out_specs = [
        pl.BlockSpec((tm, d), lambda i: (i, 0)),
        pl.BlockSpec((tm, D_KV), lambda i: (i, 0)),
        pl.BlockSpec((tm, D_KV), lambda i: (i, 0)),
        pl.BlockSpec((tm, d_gm), lambda i: (sample_block(i), 0)),
    ]
    out_shape = [
        jax.ShapeDtypeStruct((m_rows, d), F32),
        jax.ShapeDtypeStruct((m_rows, D_KV), F32),
        jax.ShapeDtypeStruct((m_rows, D_KV), F32),
        jax.ShapeDtypeStruct((n_sample_rows, d_gm), F32),
    ]
    scratch = [
        pltpu.VMEM((tm, d_in), F32),
        pltpu.VMEM((tm, D_ATTN), BF16),
        pltpu.VMEM((tm, D_ATTN + d_gm), BF16),
        pltpu.VMEM((N_KV_HEADS, GM_CHUNK, LANES), BF16),
        pltpu.VMEM((N_KV_HEADS, GM_CHUNK, LANES), BF16),
    ]
    return pl.pallas_call(
        functools.partial(_mix0_kernel, n_prompt_tiles, units),
        grid=(n_tiles,),
        in_specs=in_specs,
        out_specs=out_specs,
        out_shape=out_shape,
        scratch_shapes=scratch,
        compiler_params=pltpu.CompilerParams(
            dimension_semantics=("arbitrary",), vmem_limit_bytes=VMEM_LIMIT_BYTES),
        name="mix0",
    )(xp, xs, cache_k, cache_v, g, w_in, bias_tbl, sink_row, gm_norm, wsp, gmb, w_out)


def _ffn_kernel(final, x_ref, g_ref, wg_ref, wu_ref, wd_ref, gnext_ref, out_ref, *rest):
    h_ref = rest[-1]
    j = pl.program_id(1)

    @pl.when(j == 0)
    def _():
        x = x_ref[...]
        h_ref[...] = (_rms_scale(x) * g_ref[...]).astype(BF16)
        out_ref[...] = x

    h = h_ref[...]
    gate = jnp.dot(h, wg_ref[0].astype(BF16), preferred_element_type=F32)
    up = jnp.dot(h, wu_ref[0].astype(BF16), preferred_element_type=F32)
    act = (gate * jax.nn.sigmoid(gate) * up).astype(BF16)
    out_ref[...] += jnp.dot(act, wd_ref[0].astype(BF16), preferred_element_type=F32)

    @pl.when(j == pl.num_programs(1) - 1)
    def _():
        out = out_ref[...]
        scale = lax.rsqrt(jnp.mean(out * out, axis=-1, keepdims=True) + RMS_EPS)
        if final:
            out_ref[...] = out * scale * gnext_ref[...]
        else:
            rest[0][...] = jnp.broadcast_to(scale, rest[0].shape)


def _ffn(x, g, wg, wu, wd, gnext, *, layer, tm, tf, final, row0=0, n_rows=None):
    d = x.shape[1]
    n_rows = x.shape[0] if n_rows is None else n_rows
    f = wg.shape[2]
    assert n_rows % tm == 0 and row0 % tm == 0 and f % tf == 0
    tile0 = row0 // tm
    out_specs = [pl.BlockSpec((tm, d), lambda i, j: (i, 0))]
    out_shape = [jax.ShapeDtypeStruct((n_rows, d), F32)]
    if not final:
        out_specs.append(pl.BlockSpec((tm, LANES), lambda i, j: (i, 0)))
        out_shape.append(jax.ShapeDtypeStruct((n_rows, LANES), F32))
    return pl.pallas_call(
        functools.partial(_ffn_kernel, final),
        grid=(n_rows // tm, f // tf),
        in_specs=[
            pl.BlockSpec((tm, d), lambda i, j: (i + tile0, 0)),
            pl.BlockSpec((1, d), lambda i, j: (0, 0)),
            pl.BlockSpec((1, d, tf), lambda i, j: (layer, 0, j)),
            pl.BlockSpec((1, d, tf), lambda i, j: (layer, 0, j)),
            pl.BlockSpec((1, tf, d), lambda i, j: (layer, j, 0)),
            pl.BlockSpec((1, d), lambda i, j: (0, 0)),
        ],
        out_specs=out_specs,
        out_shape=out_shape,
        scratch_shapes=[pltpu.VMEM((tm, d), BF16)],
        compiler_params=pltpu.CompilerParams(
            dimension_semantics=("arbitrary", "arbitrary"), vmem_limit_bytes=VMEM_LIMIT_BYTES),
        name="ffn",
    )(x, g, wg, wu, wd, gnext)


TAB_ROWS = 18


def _discretize(lr, li, log_dt):
    dt = jnp.exp(log_dt)
    mag = jnp.exp(lr * dt)
    ar = mag * jnp.cos(li * dt)
    ai = mag * jnp.sin(li * dt)
    den = lr * lr + li * li
    nr = ar - 1.0
    return ar, ai, (nr * lr + ai * li) / den, (ai * lr - nr * li) / den


def _cmul(pr, pi, qr, qi):
    return pr * qr - pi * qi, pr * qi + pi * qr


def _s5_prep_kernel(seg_rows, lam_ref, lamc_ref, b_ref, c_ref, dall_ref, wec_ref, wsc_ref, tab_ref,
                    bhi_ref, blo_ref):
    ns = S5_BLOCK_STATE

    ar, ai, fr, fi = _discretize(lamc_ref[0, 0], lamc_ref[0, 1], lamc_ref[0, 2])
    bbr, bbi = _cmul(fr, fi, b_ref[0, 0], b_ref[0, 1])
    c_re = c_ref[0, 0]
    c_im = c_ref[0, 1]

    def split(a):
        hi = a.astype(BF16)
        return hi, (a - hi.astype(F32)).astype(BF16)

    power = (jnp.ones_like(ar), jnp.zeros_like(ar))
    for l in range(S5_L):
        bkr, bki = _cmul(*power, bbr, bbi)
        power = _cmul(*power, ar, ai)
        rows = slice(l * LANES, (l + 1) * LANES)
        bk = jnp.concatenate([bkr, bki], axis=1)
        bhi_ref[rows, :], blo_ref[rows, :] = split(bk)
        wec_ref[0, (S5_L - 1 - l) * LANES:(S5_L - l) * LANES, :] = bk
        qr, qi = power
        wsc_ref[0, rows, :] = jnp.concatenate([c_re * qr - c_im * qi, -c_re * qi - c_im * qr], axis=1)

    c_hi, c_lo = split(jnp.concatenate([c_re, -c_im], axis=1))
    c_parts = jnp.concatenate([c_hi, c_lo], axis=0)
    dims = (((1,), (1,)), ((), ()))
    d_hi = lax.dot_general(bhi_ref[...], c_parts, dims, preferred_element_type=F32)
    d_lo = lax.dot_general(blo_ref[...], c_parts, dims, preferred_element_type=F32)
    d_all = d_hi[:, 0:LANES] + d_hi[:, LANES:2 * LANES] + d_lo[:, 0:LANES]
    shape = (S5_L * LANES, LANES)
    in_group = (lax.broadcasted_iota(jnp.int32, shape, 0) % LANES) // S5_GROUP
    out_group = lax.broadcasted_iota(jnp.int32, shape, 1) // S5_GROUP
    dall_ref[0] = jnp.where(in_group == out_group, d_all, 0.0).astype(BF16)

    ar, ai, _, _ = _discretize(lam_ref[0, 0], lam_ref[0, 1], lam_ref[0, 2])
    row = lax.broadcasted_iota(jnp.int32, (SUBLANES, ns), 0)

    def bcast(a):
        return jnp.broadcast_to(a, (SUBLANES, ns))

    def log_step_tables(t0, base):
        cur = base
        for n, shift in enumerate((1, 2, 4)):
            tab_ref[0, t0 + 2 * n] = jnp.where(row >= shift, bcast(cur[0]), 0.0)
            tab_ref[0, t0 + 2 * n + 1] = jnp.where(row >= shift, bcast(cur[1]), 0.0)
            cur = _cmul(*cur, *cur)
        return cur

    a1 = (ar, ai)
    for _ in range(S5_L.bit_length() - 1):
        a1 = _cmul(*a1, *a1)
    a8 = log_step_tables(0, a1)
    pr_tab = jnp.zeros((SUBLANES, ns), F32)
    pi_tab = jnp.zeros((SUBLANES, ns), F32)
    cur = a1
    for r in range(SUBLANES):
        pr_tab = jnp.where(row == r, bcast(cur[0]), pr_tab)
        pi_tab = jnp.where(row == r, bcast(cur[1]), pi_tab)
        cur = _cmul(*cur, *a1)
    tab_ref[0, 6] = pr_tab
    tab_ref[0, 7] = pi_tab
    tab_ref[0, 8] = bcast(a8[0])
    tab_ref[0, 9] = bcast(a8[1])
    tab_ref[0, 10] = bcast(a1[0])
    tab_ref[0, 11] = bcast(a1[1])
    seg = a1
    for _ in range(seg_rows.bit_length() - 1):
        seg = _cmul(*seg, *seg)
    log_step_tables(12, seg)


def _s5_prep(lam, lam_c, b_c, c_c, *, seg_rows):
    assert seg_rows & (seg_rows - 1) == 0 and S5_L & (S5_L - 1) == 0
    nb = lam.shape[0]
    ns = S5_BLOCK_STATE
    k = S5_L * LANES
    cspec = pl.BlockSpec((1, k, LANES), lambda j: (j, 0, 0))

    def whole(a):
        return pl.BlockSpec((1,) + a.shape[1:], lambda j: (j,) + (0,) * (a.ndim - 1))

    return pl.pallas_call(
        functools.partial(_s5_prep_kernel, seg_rows),
        grid=(nb,),
        in_specs=[whole(lam), whole(lam_c), whole(b_c), whole(c_c)],
        out_specs=[cspec, cspec, cspec,
                   pl.BlockSpec((1, TAB_ROWS, SUBLANES, ns), lambda j: (j, 0, 0, 0))],
        out_shape=[
            jax.ShapeDtypeStruct((nb, k, LANES), BF16),
            jax.ShapeDtypeStruct((nb, k, LANES), F32),
            jax.ShapeDtypeStruct((nb, k, LANES), F32),
            jax.ShapeDtypeStruct((nb, TAB_ROWS, SUBLANES, ns), F32),
        ],
        scratch_shapes=[pltpu.VMEM((k, LANES), BF16), pltpu.VMEM((k, LANES), BF16)],
        compiler_params=pltpu.CompilerParams(
            dimension_semantics=("arbitrary",), vmem_limit_bytes=VMEM_LIMIT_BYTES),
        name="s5_prep",
    )(lam, lam_c, b_c, c_c)


def _s5_core_kernel(n_prompt_rows, n_streams,
                    x_ref, rstd_ref, g_ref, dall_ref, wec_ref, wsc_ref, tab_ref, s0r_ref, s0i_ref,
                    d_ref, wglu_ref,
                    y_ref, sfr_ref, sfi_ref, wglu_bf16_ref,
                    u2f_ref, u2_ref, e_ref, wt_ref, we_ref, wst_ref):
    wglu_bf16_ref[...] = wglu_ref[...].astype(BF16)
    ns = S5_BLOCK_STATE
    n_rows = x_ref.shape[0] // S5_L
    n_sample_rows = n_rows - n_prompt_rows
    seg_rows = n_prompt_rows // SUBLANES
    n_prompt_frames = n_prompt_rows * S5_L

    shape = (S5_L * LANES, LANES)
    lane = lax.broadcasted_iota(jnp.int32, shape, 1)
    lane_lo = lane < S5_STATE
    row_group = (lax.broadcasted_iota(jnp.int32, shape, 0) % LANES) // S5_GROUP
    groups_per_vreg = LANES // S5_STATE

    def expand(compact_ref, full_ref):
        x = compact_ref[0]
        r = pltpu.roll(x, S5_STATE, axis=1)
        for part, rep in enumerate((jnp.where(lane_lo, x, r), jnp.where(lane_lo, r, x))):
            for v in range(S5_LANE_GROUPS // groups_per_vreg):
                own = row_group == groups_per_vreg * v + (lane // S5_STATE)
                col = part * S5_BLOCK_STATE + v * LANES
                full_ref[:, col:col + LANES] = jnp.where(own, rep, 0.0).astype(BF16)

    expand(wec_ref, we_ref)

    def frame_slices():
        for l in range(S5_L):
            for seg in range(SUBLANES):
                yield l, pl.ds(seg * seg_rows * S5_L + l, seg_rows, stride=S5_L), \
                    pl.ds(seg, seg_rows, stride=SUBLANES)
            yield l, pl.ds(n_prompt_frames + l, n_sample_rows, stride=S5_L), \
                pl.ds(n_prompt_rows, n_sample_rows)

    for l, frames, rows in frame_slices():
        u2f_ref[l, rows, :] = x_ref[frames, :] * rstd_ref[frames, :] * g_ref[...]
    for l in range(S5_L):
        u2_ref[:, l * LANES:(l + 1) * LANES] = u2f_ref[l].astype(BF16)
    e_ref[...] = jnp.dot(u2_ref[...], we_ref[...], preferred_element_type=F32)

    zero_blk = jnp.zeros((LANES, LANES), BF16)
    for l in range(S5_L):
        for l2 in range(S5_L):
            blk = dall_ref[0, (l2 - l) * LANES:(l2 - l + 1) * LANES, :] if l2 >= l else zero_blk
            wt_ref[l * LANES:(l + 1) * LANES, l2 * LANES:(l2 + 1) * LANES] = blk
    slabs_per_block = S5_WT_BLOCK // LANES
    n_col_blocks = S5_L * LANES // S5_WT_BLOCK
    for cb in range(n_col_blocks):
        k_end = (cb + 1) * S5_WT_BLOCK
        y2 = jnp.dot(u2_ref[:, 0:k_end], wt_ref[0:k_end, cb * S5_WT_BLOCK:k_end],
                     preferred_element_type=F32)
        for n in range(slabs_per_block):
            l = cb * slabs_per_block + n
            u2f_ref[l] = y2[:, n * LANES:(n + 1) * LANES] + d_ref[...] * u2f_ref[l]
    expand(wsc_ref, wst_ref)

    tabs = [tab_ref[0, t] for t in range(TAB_ROWS)]
    m_tabs, (pr, pi, a8r, a8i, ar, ai), b_tabs = tabs[0:6], tabs[6:12], tabs[12:18]
    row = lax.broadcasted_iota(jnp.int32, (SUBLANES, ns), 0)

    def log_step_scan(xr, xi, t):
        for n, shift in enumerate((1, 2, 4)):
            tr, ti = t[2 * n], t[2 * n + 1]
            sr = pltpu.roll(xr, shift, axis=0)
            si = pltpu.roll(xi, shift, axis=0)
            xr, xi = xr + tr * sr - ti * si, xi + tr * si + ti * sr
        return xr, xi

    def shift_down(xr, xi, fr, fi):
        first = row == 0
        return (jnp.where(first, fr, pltpu.roll(xr, 1, axis=0)),
                jnp.where(first, fi, pltpu.roll(xi, 1, axis=0)))

    def rows_of(k):
        return pl.ds(pl.multiple_of(k * SUBLANES, SUBLANES), SUBLANES)

    def local_step(k, carry):
        sr, si = carry
        er = e_ref[rows_of(k), 0:ns]
        ei = e_ref[rows_of(k), ns:2 * ns]
        e_ref[rows_of(k), 0:ns] = sr
        e_ref[rows_of(k), ns:2 * ns] = si
        return ar * sr - ai * si + er, ar * si + ai * sr + ei

    zero = jnp.zeros((SUBLANES, ns), F32)
    ends = lax.fori_loop(0, seg_rows, local_step, (zero, zero), unroll=True)
    ends = log_step_scan(*ends, b_tabs)
    sfr_ref[...] = jnp.zeros_like(sfr_ref)
    sfi_ref[...] = jnp.zeros_like(sfi_ref)
    sfr_ref[n_streams:n_streams + 1, :] = ends[0][SUBLANES - 1:SUBLANES]
    sfi_ref[n_streams:n_streams + 1, :] = ends[1][SUBLANES - 1:SUBLANES]

    def correct_step(k, carry):
        cr, ci = carry
        e_ref[rows_of(k), 0:ns] += cr
        e_ref[rows_of(k), ns:2 * ns] += ci
        return ar * cr - ai * ci, ar * ci + ai * cr

    lax.fori_loop(0, seg_rows, correct_step, shift_down(*ends, zero, zero), unroll=True)

    for b in range(n_streams):
        rows = pl.ds(n_prompt_rows + b * SUBLANES, SUBLANES)
        cr = jnp.broadcast_to(s0r_ref[b:b + 1, :], (SUBLANES, ns))
        ci = jnp.broadcast_to(s0i_ref[b:b + 1, :], (SUBLANES, ns))
        xr, xi = log_step_scan(e_ref[rows, 0:ns], e_ref[rows, ns:2 * ns], m_tabs)
        st_r = xr + pr * cr - pi * ci
        st_i = xi + pr * ci + pi * cr
        e_ref[rows, 0:ns], e_ref[rows, ns:2 * ns] = shift_down(st_r, st_i, cr, ci)
        sfr_ref[b:b + 1, :] = st_r[SUBLANES - 1:SUBLANES]
        sfi_ref[b:b + 1, :] = st_i[SUBLANES - 1:SUBLANES]

    sprev = e_ref[...].astype(BF16)
    for cb in range(n_col_blocks):
        cols = slice(cb * S5_WT_BLOCK, (cb + 1) * S5_WT_BLOCK)
        y2 = lax.dot_general(sprev, wst_ref[cols, :], (((1,), (1,)), ((), ())),
                             preferred_element_type=F32)
        for n in range(slabs_per_block):
            l = cb * slabs_per_block + n
            u2f_ref[l] += y2[:, n * LANES:(n + 1) * LANES]
    for l, frames, rows in frame_slices():
        y_ref[frames, :] = u2f_ref[l, rows, :]


def _s5_core(x, rstd, g, dall, wec, wsc, tab, s0_re, s0_im, d_skip, w_glu, *, n_prompt_rows):
    m_rows, d = x.shape
    nb = d // LANES
    glu_cols = w_glu.shape[1] // nb
    assert glu_cols % LANES == 0
    glu_blk = pl.BlockSpec((w_glu.shape[0], glu_cols), lambda j: (0, j))
    ns = S5_BLOCK_STATE
    k = S5_L * LANES
    n_rows = m_rows // S5_L
    n_streams = s0_re.shape[0]
    assert n_prompt_rows % (S5_L * SUBLANES) == 0
    assert (m_rows - n_prompt_rows) == n_streams * S5_L * SUBLANES
    sf_rows = n_streams + SUBLANES
    lane_blk = pl.BlockSpec((m_rows, LANES), lambda j: (0, j))
    cspec = pl.BlockSpec((1, k, LANES), lambda j: (j, 0, 0))
    st = pl.BlockSpec((n_streams, ns), lambda j: (0, j))
    sf = pl.BlockSpec((sf_rows, ns), lambda j: (0, j))
    return pl.pallas_call(
        functools.partial(_s5_core_kernel, n_prompt_rows // S5_L, n_streams),
        grid=(nb,),
        in_specs=[
            lane_blk,
            _const_spec((m_rows, LANES)),
            pl.BlockSpec((1, LANES), lambda j: (0, j)),
            cspec, cspec, cspec,
            pl.BlockSpec((1, TAB_ROWS, SUBLANES, ns), lambda j: (j, 0, 0, 0)),
            st, st,
            pl.BlockSpec((1, LANES), lambda j: (0, j)),
            glu_blk,
        ],
        out_specs=[lane_blk, sf, sf, glu_blk],
        out_shape=[
            jax.ShapeDtypeStruct((m_rows, d), F32),
            jax.ShapeDtypeStruct((sf_rows, nb * ns), F32),
            jax.ShapeDtypeStruct((sf_rows, nb * ns), F32),
            jax.ShapeDtypeStruct(w_glu.shape, BF16),
        ],
        scratch_shapes=[
            pltpu.VMEM((S5_L, n_rows, LANES), F32),
            pltpu.VMEM((n_rows, k), BF16),
            pltpu.VMEM((n_rows, 2 * ns), F32),
            pltpu.VMEM((k, k), BF16),
            pltpu.VMEM((k, 2 * ns), BF16),
            pltpu.VMEM((k, 2 * ns), BF16),
        ],
        compiler_params=pltpu.CompilerParams(
            dimension_semantics=("arbitrary",), vmem_limit_bytes=VMEM_LIMIT_BYTES),
        name="s5_core",
    )(x, rstd, g, dall, wec, wsc, tab, s0_re, s0_im, d_skip, w_glu)


def _glu_kernel(n_col_blocks, x_ref, y_ref, w_ref, out_ref):
    d = x_ref.shape[1]
    a = _gelu(y_ref[...]).astype(BF16)
    cb = d // n_col_blocks
    for n in range(n_col_blocks):
        cols = slice(n * cb, (n + 1) * cb)
        val = jnp.dot(a, w_ref[:, n * cb:(n + 1) * cb], preferred_element_type=F32)
        gate = jnp.dot(a, w_ref[:, d + n * cb:d + (n + 1) * cb], preferred_element_type=F32)
        out_ref[:, cols] = x_ref[:, cols] + val * jax.nn.sigmoid(gate)


def _glu(x, y, w, *, tm, n_col_blocks=4):
    m_rows, d = x.shape
    assert m_rows % tm == 0 and d % n_col_blocks == 0
    tile = pl.BlockSpec((tm, d), lambda i: (i, 0))
    return pl.pallas_call(
        functools.partial(_glu_kernel, n_col_blocks),
        grid=(m_rows // tm,),
        in_specs=[tile, tile, _const_spec(w.shape)],
        out_specs=tile,
        out_shape=jax.ShapeDtypeStruct((m_rows, d), F32),
        compiler_params=pltpu.CompilerParams(
            dimension_semantics=("arbitrary",), vmem_limit_bytes=VMEM_LIMIT_BYTES),
        name="glu",
    )(x, y, w)


def _attention_tables(sinks):
    slopes = jnp.exp2(-8.0 * jnp.arange(1, N_HEADS + 1, dtype=F32) / N_HEADS)
    frame = jnp.arange(CHUNK, dtype=F32)[None, :]
    band = jnp.arange(BAND, dtype=F32)[:, None]
    dist = jnp.abs(frame - (band - WINDOW))
    bias = (dist[:, None, :] * slopes[None, :, None]).reshape(BAND, N_HEADS * CHUNK)
    sink_row = jnp.repeat(sinks.astype(F32), CHUNK).reshape(1, N_HEADS * CHUNK)
    return bias, sink_row


def _gm_tables(gm_ws, gm_b, d_gm):
    blk = jnp.arange(GM_CHUNK) // CHUNK
    w_prompt = jnp.where((blk[:, None] >= blk[None, :])[None], gm_ws, 0.0)
    top = gm_ws[:, :CHUNK, :CHUNK]
    zeros = jnp.zeros_like(top)
    w_sample = jnp.concatenate(
        [jnp.concatenate([top, zeros], axis=2), jnp.concatenate([zeros, top], axis=2)], axis=1)
    wsp = jnp.stack([w_prompt, w_sample])
    wsp = wsp.reshape(2, N_GM_GROUPS // 2, 2, GM_CHUNK, GM_CHUNK).transpose(0, 1, 3, 2, 4)
    wsp = wsp.reshape(2, N_GM_GROUPS // 2, GM_CHUNK, 2 * GM_CHUNK).astype(BF16)
    b_prompt = gm_b.T
    b_sample = jnp.concatenate([gm_b[:, :CHUNK].T, gm_b[:, :CHUNK].T], axis=0)
    gmb = jnp.stack([b_prompt, b_sample]).astype(F32)
    gmb = jnp.repeat(gmb, d_gm // N_GM_GROUPS, axis=2)
    return wsp, gmb


def _s5_compact(lam_re, lam_im, log_dt, b_re, b_im, c_re, c_im):
    n_groups = lam_re.shape[0]
    nb = n_groups // S5_LANE_GROUPS
    per_state = jnp.stack([lam_re, lam_im, jnp.broadcast_to(log_dt[:, None], lam_re.shape)]).astype(F32)
    per_state = per_state.reshape(3, nb, S5_LANE_GROUPS, S5_STATE).transpose(1, 0, 2, 3)
    lam = per_state.reshape(nb, 3, 1, S5_BLOCK_STATE)
    lam_c = jnp.broadcast_to(per_state[:, :, :, None, :],
                             (nb, 3, S5_LANE_GROUPS, S5_GROUP, S5_STATE)).reshape(nb, 3, LANES, S5_STATE)
    b_c = jnp.stack([b_re, b_im]).astype(F32).transpose(1, 0, 3, 2)
    b_c = b_c.reshape(nb, S5_LANE_GROUPS, 2, S5_GROUP, S5_STATE).transpose(0, 2, 1, 3, 4)
    c_c = jnp.stack([c_re, c_im]).astype(F32).transpose(1, 0, 2, 3)
    c_c = c_c.reshape(nb, S5_LANE_GROUPS, 2, S5_GROUP, S5_STATE).transpose(0, 2, 1, 3, 4)
    return (lam, lam_c, b_c.reshape(nb, 2, LANES, S5_STATE), c_c.reshape(nb, 2, LANES, S5_STATE))


def kernel(x_prompt, x_sample, cache_swa_k, cache_swa_v, state_s5_re, state_s5_im, norm_mix, norm_ffn, norm_final, w_in0, attn_sinks, gm_norm, gm_ws, gm_b, w_out0, s5_lam_re, s5_lam_im, s5_log_dt, s5_b_re, s5_b_im, s5_c_re, s5_c_im, s5_d, s5_w_glu, ffn_w_gate, ffn_w_up, ffn_w_down):
    batch, seq, d = x_prompt.shape
    dec_batch, dec_seq, _ = x_sample.shape
    assert batch == 1 and dec_seq == CHUNK and norm_mix.shape[0] == 2
    n_prompt = batch * seq
    n_sample = dec_batch * dec_seq
    d_gm = gm_norm.shape[-1]

    bias_tbl, sink_row = _attention_tables(attn_sinks[0])
    wsp, gmb = _gm_tables(gm_ws[0], gm_b[0], d_gm)
    x1, k_all, v_all, gvn = _mix0(
        x_prompt.reshape(n_prompt, d), x_sample.reshape(n_sample, d),
        cache_swa_k[0].reshape(dec_batch * WINDOW, D_KV), cache_swa_v[0].reshape(dec_batch * WINDOW, D_KV),
        norm_mix[0].reshape(1, d), w_in0[0].astype(BF16), bias_tbl, sink_row,
        gm_norm[0].reshape(1, d_gm), wsp, gmb, w_out0[0].astype(BF16), units=2)

    x2, rstd2 = _ffn(x1, norm_ffn[0].reshape(1, d), ffn_w_gate, ffn_w_up, ffn_w_down,
                     norm_mix[1].reshape(1, d), layer=0, tm=FFN_TM, tf=FFN_TF, final=False)

    n_groups = s5_lam_re.shape[1]
    dall, wec, wsc, tab = _s5_prep(
        *_s5_compact(s5_lam_re[0], s5_lam_im[0], s5_log_dt[0], s5_b_re[0], s5_b_im[0], s5_c_re[0],
                     s5_c_im[0]),
        seg_rows=n_prompt // (S5_L * SUBLANES))
    ys5, sf_re, sf_im, w_glu_bf16 = _s5_core(
        x2, rstd2, norm_mix[1].reshape(1, d), dall, wec, wsc, tab,
        state_s5_re[0].reshape(dec_batch, n_groups * S5_STATE),
        state_s5_im[0].reshape(dec_batch, n_groups * S5_STATE), s5_d[0].reshape(1, d), s5_w_glu[0],
        n_prompt_rows=n_prompt)
    x3 = _glu(x2, ys5, w_glu_bf16, tm=256)

    def last_ffn(row0, n_rows):
        (y,) = _ffn(x3, norm_ffn[1].reshape(1, d), ffn_w_gate, ffn_w_up, ffn_w_down,
                    norm_final.reshape(1, d), layer=1, tm=FFN_TM, tf=FFN_TF, final=True,
                    row0=row0, n_rows=n_rows)
        return y

    y_prompt = last_ffn(0, n_prompt)
    y_sample = last_ffn(n_prompt, n_sample)

    keep = min(WINDOW, seq)
    y_prompt = y_prompt.reshape(batch, seq, d)
    y_sample = y_sample.reshape(dec_batch, dec_seq, d)
    kv_shape_p = (1, batch, keep, N_KV_HEADS, HEAD_DIM)
    kv_shape_s = (1, dec_batch, dec_seq, N_KV_HEADS, HEAD_DIM)
    st_p = (1, batch, n_groups, S5_STATE)
    st_s = (1, dec_batch, n_groups, S5_STATE)
    return (y_prompt, y_sample,
            k_all[n_prompt - keep:n_prompt].reshape(kv_shape_p),
            v_all[n_prompt - keep:n_prompt].reshape(kv_shape_p),
            k_all[n_prompt:].reshape(kv_shape_s),
            v_all[n_prompt:].reshape(kv_shape_s),
            gvn.reshape(1, dec_batch, dec_seq, d_gm),
            sf_re[dec_batch].reshape(st_p), sf_im[dec_batch].reshape(st_p),
            sf_re[:dec_batch].reshape(st_s), sf_im[:dec_batch].reshape(st_s))
```

```python
import functools
import math

import jax
import jax.numpy as jnp
from jax import lax
from jax.experimental import pallas as pl
from jax.experimental.pallas import tpu as pltpu

F32 = jnp.float32
BF16 = jnp.bfloat16

CHUNK = 64
HEAD_DIM = 64
N_HEADS = 16
N_KV_HEADS = 2
Q_PER_KV = N_HEADS // N_KV_HEADS
WINDOW = 128
BAND = WINDOW + CHUNK
D_ATTN = N_HEADS * HEAD_DIM
D_KV = N_KV_HEADS * HEAD_DIM
GM_CHUNK = 128
N_GM_GROUPS = 16
S5_GROUP = 16
S5_STATE = 64
RMS_EPS = 1e-5
NEG_INF = -1e30

LANES = 128
SUBLANES = 8
VMEM_LIMIT_BYTES = 56 * 1024 * 1024

FFN_TM = 1024
FFN_TF = 256
FFN_DOWN_STEPS = 2

S5_L = SUBLANES
S5_LANE_GROUPS = LANES // S5_GROUP
S5_BLOCK_STATE = S5_LANE_GROUPS * S5_STATE
S5_WT_BLOCK = 256


def _gelu(x):
    return 0.5 * x * (1.0 + lax.erf(x * math.sqrt(0.5)))


def _rms_scale(x):
    return x * lax.rsqrt(jnp.mean(x * x, axis=-1, keepdims=True) + RMS_EPS)


def _const_spec(shape):
    zeros = (0,) * len(shape)
    return pl.BlockSpec(shape, lambda *_: zeros, pipeline_mode=pl.Buffered(1))


def _mix0_kernel(n_prompt_tiles, units,
                 xp_ref, xs_ref, ck_ref, cv_ref, g_ref, win_ref, bias_ref, sink_ref, gmn_ref,
                 wsp_ref, gmb_ref, wout_ref,
                 x1_ref, k_ref, v_ref, gvn_ref,
                 z_ref, q_ref, ocat_ref, kprev_ref, vprev_ref):
    i = pl.program_id(0)
    is_sample = i >= n_prompt_tiles
    tm = units * GM_CHUNK
    d_gm = gmn_ref.shape[-1]
    off_k = D_ATTN
    off_v = D_ATTN + D_KV
    off_gu = D_ATTN + 2 * D_KV
    off_gv = off_gu + d_gm
    n_q = Q_PER_KV * CHUNK

    @pl.when(i == 0)
    def _():
        kprev_ref[...] = jnp.zeros_like(kprev_ref)
        vprev_ref[...] = jnp.zeros_like(vprev_ref)

    x = jnp.where(is_sample, xs_ref[...], xp_ref[...])
    h = (_rms_scale(x) * g_ref[...]).astype(BF16)
    z_ref[...] = jnp.dot(h, win_ref[...], preferred_element_type=F32)

    q_ref[...] = (z_ref[:, 0:D_ATTN] * (HEAD_DIM ** -0.5)).astype(BF16)
    k = z_ref[:, off_k:off_k + D_KV]
    v = z_ref[:, off_v:off_v + D_KV]
    k_ref[...] = k
    v_ref[...] = v

    def lane_lo(rows):
        return lax.broadcasted_iota(jnp.int32, (rows, LANES), 1) < HEAD_DIM

    def replicate(a):
        r = pltpu.roll(a, HEAD_DIM, axis=1)
        lo = lane_lo(a.shape[0])
        return jnp.where(lo, a, r).astype(BF16), jnp.where(lo, r, a).astype(BF16)

    k_rep = replicate(k)
    v_rep = replicate(v)
    ck_rep = replicate(ck_ref[...])
    cv_rep = replicate(cv_ref[...])

    lo64 = lane_lo(CHUNK)
    band_pos = lax.broadcasted_iota(jnp.int32, (BAND, n_q), 0)
    chunks_per_tile = tm // CHUNK

    def scores(u, c2, kv):
        r0 = u * GM_CHUNK + c2 * CHUNK
        stream = r0 // CHUNK
        chunk_index = i * chunks_per_tile + stream
        valid_from = jnp.where(is_sample, 0, jnp.maximum(WINDOW - CHUNK * chunk_index, 0))
        valid = band_pos >= valid_from

        def band(cur, prev_ref, cached):
            if u == 0:
                prev_unit = prev_ref[kv]
            else:
                prev_unit = cur[(u - 1) * GM_CHUNK:u * GM_CHUNK]
            if c2 == 0:
                prompt_prev = prev_unit
            else:
                prompt_prev = jnp.concatenate(
                    [prev_unit[CHUNK:], cur[u * GM_CHUNK:u * GM_CHUNK + CHUNK]], axis=0)
            sample_prev = cached[stream * WINDOW:(stream + 1) * WINDOW]
            prev = jnp.where(is_sample, sample_prev, prompt_prev)
            return jnp.concatenate([prev, cur[r0:r0 + CHUNK]], axis=0)

        kb = band(k_rep[kv], kprev_ref, ck_rep[kv])
        vb = band(v_rep[kv], vprev_ref, cv_rep[kv])

        pieces = []
        for m in range(Q_PER_KV // 2):
            c0 = kv * Q_PER_KV * HEAD_DIM + m * LANES
            qp = q_ref[r0:r0 + CHUNK, c0:c0 + LANES]
            pieces.append(jnp.where(lo64, qp, jnp.zeros_like(qp)))
            pieces.append(jnp.where(lo64, jnp.zeros_like(qp), qp))
        qs = jnp.concatenate(pieces, axis=0)
        cols = slice(kv * n_q, (kv + 1) * n_q)
        st = lax.dot_general(kb, qs, (((1,), (1,)), ((), ())),
                             preferred_element_type=F32)
        return jnp.where(valid, st - bias_ref[:, cols], NEG_INF), vb

    def attend(u, c2, kv, st, vb):
        r0 = u * GM_CHUNK + c2 * CHUNK
        cols = slice(kv * n_q, (kv + 1) * n_q)
        sink = sink_ref[:, cols]
        mx = jnp.maximum(jnp.max(st, axis=0, keepdims=True), sink)
        p = jnp.exp(st - mx)
        denom = jnp.sum(p, axis=0, keepdims=True) + jnp.exp(sink - mx)
        pn = (p * (1.0 / denom)).astype(BF16)
        o = lax.dot_general(pn, vb, (((0,), (0,)), ((), ())),
                            preferred_element_type=F32)
        for m in range(Q_PER_KV // 2):
            o_pair = jnp.where(lo64, o[(2 * m) * CHUNK:(2 * m + 1) * CHUNK],
                               o[(2 * m + 1) * CHUNK:(2 * m + 2) * CHUNK])
            c0 = kv * Q_PER_KV * HEAD_DIM + m * LANES
            ocat_ref[r0:r0 + CHUNK, c0:c0 + LANES] = o_pair.astype(BF16)

    blocks = [(u, c2, kv) for u in range(units) for c2 in range(GM_CHUNK // CHUNK)
              for kv in range(N_KV_HEADS)]
    scored = [scores(*blk) for blk in blocks]
    for blk, (st, vb) in zip(blocks, scored):
        attend(*blk, st, vb)

    lo128 = lane_lo(GM_CHUNK)
    for u in range(units):
        rows = slice(u * GM_CHUNK, (u + 1) * GM_CHUNK)
        ua = _gelu(z_ref[rows, off_gu:off_gu + d_gm])
        gvn = _rms_scale(_gelu(z_ref[rows, off_gv:off_gv + d_gm])) * gmn_ref[...]
        gvn_ref[rows, :] = gvn
        gb = gvn.astype(BF16)
        for m in range(N_GM_GROUPS // 2):
            cols = slice(m * LANES, (m + 1) * LANES)
            rhs = gb[:, cols]
            rhs2 = jnp.concatenate([jnp.where(lo128, rhs, jnp.zeros_like(rhs)),
                                    jnp.where(lo128, jnp.zeros_like(rhs), rhs)], axis=0)
            sp = jnp.dot(wsp_ref[0, m], rhs2, preferred_element_type=F32) + gmb_ref[0, :, cols]
            ocat_ref[rows, D_ATTN + m * LANES:D_ATTN + (m + 1) * LANES] = (ua[:, cols] * sp).astype(BF16)

    x1_ref[...] = x + jnp.dot(ocat_ref[...], wout_ref[...], preferred_element_type=F32)

    for kv in range(N_KV_HEADS):
        kprev_ref[kv] = k_rep[kv][tm - GM_CHUNK:]
        vprev_ref[kv] = v_rep[kv][tm - GM_CHUNK:]


def _mix0(xp, xs, cache_k, cache_v, g, w_in, bias_tbl, sink_row, gm_norm, wsp, gmb, w_out, *, units):
    n_prompt_rows, d = xp.shape
    n_sample_rows = xs.shape[0]
    m_rows = n_prompt_rows + n_sample_rows
    tm = units * GM_CHUNK
    assert n_prompt_rows % tm == 0 and n_sample_rows % tm == 0
    n_tiles = m_rows // tm
    n_prompt_tiles = n_prompt_rows // tm
    d_in = w_in.shape[1]
    d_gm = gm_norm.shape[-1]
    cache_rows = (tm // CHUNK) * WINDOW

    def prompt_block(i):
        return jnp.minimum(i, n_prompt_tiles - 1)

    def sample_block(i):
        return jnp.maximum(i - n_prompt_tiles, 0)

    def kind(i):
        return jnp.where(i >= n_prompt_tiles, 1, 0)

    in_specs = [
        pl.BlockSpec((tm, d), lambda i: (prompt_block(i), 0)),
        pl.BlockSpec((tm, d), lambda i: (sample_block(i), 0)),
        pl.BlockSpec((cache_rows, D_KV), lambda i: (sample_block(i), 0)),
        pl.BlockSpec((cache_rows, D_KV), lambda i: (sample_block(i), 0)),
        _const_spec((1, d)),
        _const_spec((d, d_in)),
        _const_spec(bias_tbl.shape),
        _const_spec(sink_row.shape),
        _const_spec((1, d_gm)),
        pl.BlockSpec((1,) + wsp.shape[1:], lambda i: (kind(i), 0, 0, 0)),
        pl.BlockSpec((1, GM_CHUNK, d_gm), lambda i: (kind(i), 0, 0)),
        _const_spec(w_out.shape),
    ]
    out_specs = [
        pl.BlockSpec((tm, d), lambda i: (i, 0)),
        pl.BlockSpec((tm, D_KV), lambda i: (i, 0)),
        pl.BlockSpec((tm, D_KV), lambda i: (i, 0)),
        pl.BlockSpec((tm, d_gm), lambda i: (sample_block(i), 0)),
    ]
    out_shape = [
        jax.ShapeDtypeStruct((m_rows, d), F32),
        jax.ShapeDtypeStruct((m_rows, D_KV), F32),
        jax.ShapeDtypeStruct((m_rows, D_KV), F32),
        jax.ShapeDtypeStruct((n_sample_rows, d_gm), F32),
    ]
    scratch = [
        pltpu.VMEM((tm, d_in), F32),
        pltpu.VMEM((tm, D_ATTN), BF16),
        pltpu.VMEM((tm, D_ATTN + d_gm), BF16),
        pltpu.VMEM((N_KV_HEADS, GM_CHUNK, LANES), BF16),
        pltpu.VMEM((N_KV_HEADS, GM_CHUNK, LANES), BF16),
    ]
    return pl.pallas_call(
        functools.partial(_mix0_kernel, n_prompt_tiles, units),
        grid=(n_tiles,),
        in_specs=in_specs,
        out_specs=out_specs,
        out_shape=out_shape,
        scratch_shapes=scratch,
        compiler_params=pltpu.CompilerParams(
            dimension_semantics=("arbitrary",), vmem_limit_bytes=VMEM_LIMIT_BYTES),
        name="mix0",
    )(xp, xs, cache_k, cache_v, g, w_in, bias_tbl, sink_row, gm_norm, wsp, gmb, w_out)


def _ffn_kernel(final, x_ref, g_ref, wg_ref, wu_ref, wd_ref, gnext_ref, out_ref, *rest):
    h_ref, act_ref = rest[-2:]
    j = pl.program_id(1)
    tf = wg_ref.shape[2]

    @pl.when(j == 0)
    def _():
        x = x_ref[...]
        h_ref[...] = (_rms_scale(x) * g_ref[...]).astype(BF16)
        out_ref[...] = x

    h = h_ref[...]
    gate = jnp.dot(h, wg_ref[0].astype(BF16), preferred_element_type=F32)
    up = jnp.dot(h, wu_ref[0].astype(BF16), preferred_element_type=F32)
    act = (gate * jax.nn.sigmoid(gate) * up).astype(BF16)

    slot = j % FFN_DOWN_STEPS
    for n in range(FFN_DOWN_STEPS):
        @pl.when(slot == n)
        def _(n=n):
            act_ref[:, n * tf:(n + 1) * tf] = act

    @pl.when(slot == FFN_DOWN_STEPS - 1)
    def _():
        out_ref[...] += jnp.dot(act_ref[...], wd_ref[0].astype(BF16), preferred_element_type=F32)

    @pl.when(j == pl.num_programs(1) - 1)
    def _():
        out = out_ref[...]
        scale = lax.rsqrt(jnp.mean(out * out, axis=-1, keepdims=True) + RMS_EPS)
        if final:
            out_ref[...] = out * scale * gnext_ref[...]
        else:
            rest[0][...] = jnp.broadcast_to(scale, rest[0].shape)


def _ffn(x, g, wg, wu, wd, gnext, *, layer, tm, tf, final, row0=0, n_rows=None):
    d = x.shape[1]
    n_rows = x.shape[0] if n_rows is None else n_rows
    f = wg.shape[2]
    assert n_rows % tm == 0 and row0 % tm == 0 and f % (tf * FFN_DOWN_STEPS) == 0
    tile0 = row0 // tm
    out_specs = [pl.BlockSpec((tm, d), lambda i, j: (i, 0))]
    out_shape = [jax.ShapeDtypeStruct((n_rows, d), F32)]
    if not final:
        out_specs.append(pl.BlockSpec((tm, LANES), lambda i, j: (i, 0)))
        out_shape.append(jax.ShapeDtypeStruct((n_rows, LANES), F32))
    return pl.pallas_call(
        functools.partial(_ffn_kernel, final),
        grid=(n_rows // tm, f // tf),
        in_specs=[
            pl.BlockSpec((tm, d), lambda i, j: (i + tile0, 0)),
            pl.BlockSpec((1, d), lambda i, j: (0, 0)),
            pl.BlockSpec((1, d, tf), lambda i, j: (layer, 0, j)),
            pl.BlockSpec((1, d, tf), lambda i, j: (layer, 0, j)),
            pl.BlockSpec((1, tf * FFN_DOWN_STEPS, d), lambda i, j: (layer, j // FFN_DOWN_STEPS, 0)),
            pl.BlockSpec((1, d), lambda i, j: (0, 0)),
        ],
        out_specs=out_specs,
        out_shape=out_shape,
        scratch_shapes=[pltpu.VMEM((tm, d), BF16), pltpu.VMEM((tm, tf * FFN_DOWN_STEPS), BF16)],
        compiler_params=pltpu.CompilerParams(
            dimension_semantics=("arbitrary", "arbitrary"), vmem_limit_bytes=VMEM_LIMIT_BYTES),
        name="ffn",
    )(x, g, wg, wu, wd, gnext)


TAB_ROWS = 18


def _discretize(lr, li, log_dt):
    dt = jnp.exp(log_dt)
    mag = jnp.exp(lr * dt)
    ar = mag * jnp.cos(li * dt)
    ai = mag * jnp.sin(li * dt)
    den = lr * lr + li * li
    nr = ar - 1.0
    return ar, ai, (nr * lr + ai * li) / den, (ai * lr - nr * li) / den


def _cmul(pr, pi, qr, qi):
    return pr * qr - pi * qi, pr * qi + pi * qr


def _s5_prep_kernel(seg_rows, lam_ref, lamc_ref, b_ref, c_ref, dall_ref, wec_ref, wsc_ref, tab_ref,
                    bhi_ref, blo_ref):
    ns = S5_BLOCK_STATE

    ar, ai, fr, fi = _discretize(lamc_ref[0, 0], lamc_ref[0, 1], lamc_ref[0, 2])
    bbr, bbi = _cmul(fr, fi, b_ref[0, 0], b_ref[0, 1])
    c_re = c_ref[0, 0]
    c_im = c_ref[0, 1]

    def split(a):
        hi = a.astype(BF16)
        return hi, (a - hi.astype(F32)).astype(BF16)

    power = (jnp.ones_like(ar), jnp.zeros_like(ar))
    for l in range(S5_L):
        bkr, bki = _cmul(*power, bbr, bbi)
        power = _cmul(*power, ar, ai)
        rows = slice(l * LANES, (l + 1) * LANES)
        bk = jnp.concatenate([bkr, bki], axis=1)
        bhi_ref[rows, :], blo_ref[rows, :] = split(bk)
        wec_ref[0, (S5_L - 1 - l) * LANES:(S5_L - l) * LANES, :] = bk
        qr, qi = power
        wsc_ref[0, rows, :] = jnp.concatenate([c_re * qr - c_im * qi, -c_re * qi - c_im * qr], axis=1)

    c_hi, c_lo = split(jnp.concatenate([c_re, -c_im], axis=1))
    c_parts = jnp.concatenate([c_hi, c_lo], axis=0)
    dims = (((1,), (1,)), ((), ()))
    d_hi = lax.dot_general(bhi_ref[...], c_parts, dims, preferred_element_type=F32)
    d_lo = lax.dot_general(blo_ref[...], c_parts, dims, preferred_element_type=F32)
    d_all = d_hi[:, 0:LANES] + d_hi[:, LANES:2 * LANES] + d_lo[:, 0:LANES]
    shape = (S5_L * LANES, LANES)
    in_group = (lax.broadcasted_iota(jnp.int32, shape, 0) % LANES) // S5_GROUP
    out_group = lax.broadcasted_iota(jnp.int32, shape, 1) // S5_GROUP
    dall_ref[0] = jnp.where(in_group == out_group, d_all, 0.0).astype(BF16)

    ar, ai, _, _ = _discretize(lam_ref[0, 0], lam_ref[0, 1], lam_ref[0, 2])
    row = lax.broadcasted_iota(jnp.int32, (SUBLANES, ns), 0)

    def bcast(a):
        return jnp.broadcast_to(a, (SUBLANES, ns))

    def log_step_tables(t0, base):
        cur = base
        for n, shift in enumerate((1, 2, 4)):
            tab_ref[0, t0 + 2 * n] = jnp.where(row >= shift, bcast(cur[0]), 0.0)
            tab_ref[0, t0 + 2 * n + 1] = jnp.where(row >= shift, bcast(cur[1]), 0.0)
            cur = _cmul(*cur, *cur)
        return cur

    a1 = (ar, ai)
    for _ in range(S5_L.bit_length() - 1):
        a1 = _cmul(*a1, *a1)
    a8 = log_step_tables(0, a1)
    pr_tab = jnp.zeros((SUBLANES, ns), F32)
    pi_tab = jnp.zeros((SUBLANES, ns), F32)
    cur = a1
    for r in range(SUBLANES):
        pr_tab = jnp.where(row == r, bcast(cur[0]), pr_tab)
        pi_tab = jnp.where(row == r, bcast(cur[1]), pi_tab)
        cur = _cmul(*cur, *a1)
    tab_ref[0, 6] = pr_tab
    tab_ref[0, 7] = pi_tab
    tab_ref[0, 8] = bcast(a8[0])
    tab_ref[0, 9] = bcast(a8[1])
    tab_ref[0, 10] = bcast(a1[0])
    tab_ref[0, 11] = bcast(a1[1])
    seg = a1
    for _ in range(seg_rows.bit_length() - 1):
        seg = _cmul(*seg, *seg)
    log_step_tables(12, seg)


def _s5_prep(lam, lam_c, b_c, c_c, *, seg_rows):
    assert seg_rows & (seg_rows - 1) == 0 and S5_L & (S5_L - 1) == 0
    nb = lam.shape[0]
    ns = S5_BLOCK_STATE
    k = S5_L * LANES
    cspec = pl.BlockSpec((1, k, LANES), lambda j: (j, 0, 0))

    def whole(a):
        return pl.BlockSpec((1,) + a.shape[1:], lambda j: (j,) + (0,) * (a.ndim - 1))

    return pl.pallas_call(
        functools.partial(_s5_prep_kernel, seg_rows),
        grid=(nb,),
        in_specs=[whole(lam), whole(lam_c), whole(b_c), whole(c_c)],
        out_specs=[cspec, cspec, cspec,
                   pl.BlockSpec((1, TAB_ROWS, SUBLANES, ns), lambda j: (j, 0, 0, 0))],
        out_shape=[
            jax.ShapeDtypeStruct((nb, k, LANES), BF16),
            jax.ShapeDtypeStruct((nb, k, LANES), F32),
            jax.ShapeDtypeStruct((nb, k, LANES), F32),
            jax.ShapeDtypeStruct((nb, TAB_ROWS, SUBLANES, ns), F32),
        ],
        scratch_shapes=[pltpu.VMEM((k, LANES), BF16), pltpu.VMEM((k, LANES), BF16)],
        compiler_params=pltpu.CompilerParams(
            dimension_semantics=("arbitrary",), vmem_limit_bytes=VMEM_LIMIT_BYTES),
        name="s5_prep",
    )(lam, lam_c, b_c, c_c)


def _s5_core_kernel(n_prompt_rows, n_streams,
                    x_ref, rstd_ref, g_ref, dall_ref, wec_ref, wsc_ref, tab_ref, s0r_ref, s0i_ref,
                    d_ref, wglu_ref,
                    y_ref, sfr_ref, sfi_ref, wglu_bf16_ref,
                    u2f_ref, u2_ref, e_ref, wt_ref, we_ref, wst_ref):
    wglu_bf16_ref[...] = wglu_ref[...].astype(BF16)
    ns = S5_BLOCK_STATE
    n_rows = x_ref.shape[0] // S5_L
    n_sample_rows = n_rows - n_prompt_rows
    seg_rows = n_prompt_rows // SUBLANES
    n_prompt_frames = n_prompt_rows * S5_L

    shape = (S5_L * LANES, LANES)
    lane = lax.broadcasted_iota(jnp.int32, shape, 1)
    lane_lo = lane < S5_STATE
    row_group = (lax.broadcasted_iota(jnp.int32, shape, 0) % LANES) // S5_GROUP
    groups_per_vreg = LANES // S5_STATE

    def expand(compact_ref, full_ref):
        x = compact_ref[0]
        r = pltpu.roll(x, S5_STATE, axis=1)
        for part, rep in enumerate((jnp.where(lane_lo, x, r), jnp.where(lane_lo, r, x))):
            for v in range(S5_LANE_GROUPS // groups_per_vreg):
                own = row_group == groups_per_vreg * v + (lane // S5_STATE)
                col = part * S5_BLOCK_STATE + v * LANES
                full_ref[:, col:col + LANES] = jnp.where(own, rep, 0.0).astype(BF16)

    expand(wec_ref, we_ref)

    def frame_slices():
        for l in range(S5_L):
            for seg in range(SUBLANES):
                yield l, pl.ds(seg * seg_rows * S5_L + l, seg_rows, stride=S5_L), \
                    pl.ds(seg, seg_rows, stride=SUBLANES)
            yield l, pl.ds(n_prompt_frames + l, n_sample_rows, stride=S5_L), \
                pl.ds(n_prompt_rows, n_sample_rows)

    for l, frames, rows in frame_slices():
        u2f_ref[l, rows, :] = x_ref[frames, :] * rstd_ref[frames, :] * g_ref[...]
    for l in range(S5_L):
        u2_ref[:, l * LANES:(l + 1) * LANES] = u2f_ref[l].astype(BF16)
    e_ref[...] = jnp.dot(u2_ref[...], we_ref[...], preferred_element_type=F32)

    zero_blk = jnp.zeros((LANES, LANES), BF16)
    for l in range(S5_L):
        for l2 in range(S5_L):
            blk = dall_ref[0, (l2 - l) * LANES:(l2 - l + 1) * LANES, :] if l2 >= l else zero_blk
            wt_ref[l * LANES:(l + 1) * LANES, l2 * LANES:(l2 + 1) * LANES] = blk
    slabs_per_block = S5_WT_BLOCK // LANES
    n_col_blocks = S5_L * LANES // S5_WT_BLOCK
    for cb in range(n_col_blocks):
        k_end = (cb + 1) * S5_WT_BLOCK
        y2 = jnp.dot(u2_ref[:, 0:k_end], wt_ref[0:k_end, cb * S5_WT_BLOCK:k_end],
                     preferred_element_type=F32)
        for n in range(slabs_per_block):
            l = cb * slabs_per_block + n
            u2f_ref[l] = y2[:, n * LANES:(n + 1) * LANES] + d_ref[...] * u2f_ref[l]
    expand(wsc_ref, wst_ref)

    tabs = [tab_ref[0, t] for t in range(TAB_ROWS)]
    m_tabs, (pr, pi, a8r, a8i, ar, ai), b_tabs = tabs[0:6], tabs[6:12], tabs[12:18]
    row = lax.broadcasted_iota(jnp.int32, (SUBLANES, ns), 0)

    def log_step_scan(xr, xi, t):
        for n, shift in enumerate((1, 2, 4)):
            tr, ti = t[2 * n], t[2 * n + 1]
            sr = pltpu.roll(xr, shift, axis=0)
            si = pltpu.roll(xi, shift, axis=0)
            xr, xi = xr + tr * sr - ti * si, xi + tr * si + ti * sr
        return xr, xi

    def shift_down(xr, xi, fr, fi):
        first = row == 0
        return (jnp.where(first, fr, pltpu.roll(xr, 1, axis=0)),
                jnp.where(first, fi, pltpu.roll(xi, 1, axis=0)))

    def rows_of(k):
        return pl.ds(pl.multiple_of(k * SUBLANES, SUBLANES), SUBLANES)

    def local_step(k, carry):
        sr, si = carry
        er = e_ref[rows_of(k), 0:ns]
        ei = e_ref[rows_of(k), ns:2 * ns]
        e_ref[rows_of(k), 0:ns] = sr
        e_ref[rows_of(k), ns:2 * ns] = si
        return ar * sr - ai * si + er, ar * si + ai * sr + ei

    zero = jnp.zeros((SUBLANES, ns), F32)
    ends = lax.fori_loop(0, seg_rows, local_step, (zero, zero), unroll=True)
    ends = log_step_scan(*ends, b_tabs)
    sfr_ref[...] = jnp.zeros_like(sfr_ref)
    sfi_ref[...] = jnp.zeros_like(sfi_ref)
    sfr_ref[n_streams:n_streams + 1, :] = ends[0][SUBLANES - 1:SUBLANES]
    sfi_ref[n_streams:n_streams + 1, :] = ends[1][SUBLANES - 1:SUBLANES]

    def correct_step(k, carry):
        cr, ci = carry
        e_ref[rows_of(k), 0:ns] += cr
        e_ref[rows_of(k), ns:2 * ns] += ci
        return ar * cr - ai * ci, ar * ci + ai * cr

    lax.fori_loop(0, seg_rows, correct_step, shift_down(*ends, zero, zero), unroll=True)

    for b in range(n_streams):
        rows = pl.ds(n_prompt_rows + b * SUBLANES, SUBLANES)
        cr = jnp.broadcast_to(s0r_ref[b:b + 1, :], (SUBLANES, ns))
        ci = jnp.broadcast_to(s0i_ref[b:b + 1, :], (SUBLANES, ns))
        xr, xi = log_step_scan(e_ref[rows, 0:ns], e_ref[rows, ns:2 * ns], m_tabs)
        st_r = xr + pr * cr - pi * ci
        st_i = xi + pr * ci + pi * cr
        e_ref[rows, 0:ns], e_ref[rows, ns:2 * ns] = shift_down(st_r, st_i, cr, ci)
        sfr_ref[b:b + 1, :] = st_r[SUBLANES - 1:SUBLANES]
        sfi_ref[b:b + 1, :] = st_i[SUBLANES - 1:SUBLANES]

    sprev = e_ref[...].astype(BF16)
    for cb in range(n_col_blocks):
        cols = slice(cb * S5_WT_BLOCK, (cb + 1) * S5_WT_BLOCK)
        y2 = lax.dot_general(sprev, wst_ref[cols, :], (((1,), (1,)), ((), ())),
                             preferred_element_type=F32)
        for n in range(slabs_per_block):
            l = cb * slabs_per_block + n
            u2f_ref[l] += y2[:, n * LANES:(n + 1) * LANES]
    for l, frames, rows in frame_slices():
        y_ref[frames, :] = u2f_ref[l, rows, :]


def _s5_core(x, rstd, g, dall, wec, wsc, tab, s0_re, s0_im, d_skip, w_glu, *, n_prompt_rows):
    m_rows, d = x.shape
    nb = d // LANES
    glu_cols = w_glu.shape[1] // nb
    assert glu_cols % LANES == 0
    glu_blk = pl.BlockSpec((w_glu.shape[0], glu_cols), lambda j: (0, j))
    ns = S5_BLOCK_STATE
    k = S5_L * LANES
    n_rows = m_rows // S5_L
    n_streams = s0_re.shape[0]
    assert n_prompt_rows % (S5_L * SUBLANES) == 0
    assert (m_rows - n_prompt_rows) == n_streams * S5_L * SUBLANES
    sf_rows = n_streams + SUBLANES
    lane_blk = pl.BlockSpec((m_rows, LANES), lambda j: (0, j))
    cspec = pl.BlockSpec((1, k, LANES), lambda j: (j, 0, 0))
    st = pl.BlockSpec((n_streams, ns), lambda j: (0, j))
    sf = pl.BlockSpec((sf_rows, ns), lambda j: (0, j))
    return pl.pallas_call(
        functools.partial(_s5_core_kernel, n_prompt_rows // S5_L, n_streams),
        grid=(nb,),
        in_specs=[
            lane_blk,
            _const_spec((m_rows, LANES)),
            pl.BlockSpec((1, LANES), lambda j: (0, j)),
            cspec, cspec, cspec,
            pl.BlockSpec((1, TAB_ROWS, SUBLANES, ns), lambda j: (j, 0, 0, 0)),
            st, st,
            pl.BlockSpec((1, LANES), lambda j: (0, j)),
            glu_blk,
        ],
        out_specs=[lane_blk, sf, sf, glu_blk],
        out_shape=[
            jax.ShapeDtypeStruct((m_rows, d), F32),
            jax.ShapeDtypeStruct((sf_rows, nb * ns), F32),
            jax.ShapeDtypeStruct((sf_rows, nb * ns), F32),
            jax.ShapeDtypeStruct(w_glu.shape, BF16),
        ],
        scratch_shapes=[
            pltpu.VMEM((S5_L, n_rows, LANES), F32),
            pltpu.VMEM((n_rows, k), BF16),
            pltpu.VMEM((n_rows, 2 * ns), F32),
            pltpu.VMEM((k, k), BF16),
            pltpu.VMEM((k, 2 * ns), BF16),
            pltpu.VMEM((k, 2 * ns), BF16),
        ],
        compiler_params=pltpu.CompilerParams(
            dimension_semantics=("arbitrary",), vmem_limit_bytes=VMEM_LIMIT_BYTES),
        name="s5_core",
    )(x, rstd, g, dall, wec, wsc, tab, s0_re, s0_im, d_skip, w_glu)


def _glu_kernel(n_col_blocks, x_ref, y_ref, w_ref, out_ref):
    d = x_ref.shape[1]
    a = _gelu(y_ref[...]).astype(BF16)
    cb = d // n_col_blocks
    for n in range(n_col_blocks):
        cols = slice(n * cb, (n + 1) * cb)
        val = jnp.dot(a, w_ref[:, n * cb:(n + 1) * cb], preferred_element_type=F32)
        gate = jnp.dot(a, w_ref[:, d + n * cb:d + (n + 1) * cb], preferred_element_type=F32)
        out_ref[:, cols] = x_ref[:, cols] + val * jax.nn.sigmoid(gate)


def _glu(x, y, w, *, tm, n_col_blocks=4):
    m_rows, d = x.shape
    assert m_rows % tm == 0 and d % n_col_blocks == 0
    tile = pl.BlockSpec((tm, d), lambda i: (i, 0))
    return pl.pallas_call(
        functools.partial(_glu_kernel, n_col_blocks),
        grid=(m_rows // tm,),
        in_specs=[tile, tile, _const_spec(w.shape)],
        out_specs=tile,
        out_shape=jax.ShapeDtypeStruct((m_rows, d), F32),
        compiler_params=pltpu.CompilerParams(
            dimension_semantics=("arbitrary",), vmem_limit_bytes=VMEM_LIMIT_BYTES),
        name="glu",
    )(x, y, w)


def _attention_tables(sinks):
    slopes = jnp.exp2(-8.0 * jnp.arange(1, N_HEADS + 1, dtype=F32) / N_HEADS)
    frame = jnp.arange(CHUNK, dtype=F32)[None, :]
    band = jnp.arange(BAND, dtype=F32)[:, None]
    dist = jnp.abs(frame - (band - WINDOW))
    bias = (dist[:, None, :] * slopes[None, :, None]).reshape(BAND, N_HEADS * CHUNK)
    sink_row = jnp.repeat(sinks.astype(F32), CHUNK).reshape(1, N_HEADS * CHUNK)
    return bias, sink_row


def _gm_tables(gm_ws, gm_b, d_gm):
    blk = jnp.arange(GM_CHUNK) // CHUNK
    w_prompt = jnp.where((blk[:, None] >= blk[None, :])[None], gm_ws, 0.0)
    top = gm_ws[:, :CHUNK, :CHUNK]
    zeros = jnp.zeros_like(top)
    w_sample = jnp.concatenate(
        [jnp.concatenate([top, zeros], axis=2), jnp.concatenate([zeros, top], axis=2)], axis=1)
    wsp = jnp.stack([w_prompt, w_sample])
    wsp = wsp.reshape(2, N_GM_GROUPS // 2, 2, GM_CHUNK, GM_CHUNK).transpose(0, 1, 3, 2, 4)
    wsp = wsp.reshape(2, N_GM_GROUPS // 2, GM_CHUNK, 2 * GM_CHUNK).astype(BF16)
    b_prompt = gm_b.T
    b_sample = jnp.concatenate([gm_b[:, :CHUNK].T, gm_b[:, :CHUNK].T], axis=0)
    gmb = jnp.stack([b_prompt, b_sample]).astype(F32)
    gmb = jnp.repeat(gmb, d_gm // N_GM_GROUPS, axis=2)
    return wsp, gmb


def _s5_compact(lam_re, lam_im, log_dt, b_re, b_im, c_re, c_im):
    n_groups = lam_re.shape[0]
    nb = n_groups // S5_LANE_GROUPS
    per_state = jnp.stack([lam_re, lam_im, jnp.broadcast_to(log_dt[:, None], lam_re.shape)]).astype(F32)
    per_state = per_state.reshape(3, nb, S5_LANE_GROUPS, S5_STATE).transpose(1, 0, 2, 3)
    lam = per_state.reshape(nb, 3, 1, S5_BLOCK_STATE)
    lam_c = jnp.broadcast_to(per_state[:, :, :, None, :],
                             (nb, 3, S5_LANE_GROUPS, S5_GROUP, S5_STATE)).reshape(nb, 3, LANES, S5_STATE)
    b_c = jnp.stack([b_re, b_im]).astype(F32).transpose(1, 0, 3, 2)
    b_c = b_c.reshape(nb, S5_LANE_GROUPS, 2, S5_GROUP, S5_STATE).transpose(0, 2, 1, 3, 4)
    c_c = jnp.stack([c_re, c_im]).astype(F32).transpose(1, 0, 2, 3)
    c_c = c_c.reshape(nb, S5_LANE_GROUPS, 2, S5_GROUP, S5_STATE).transpose(0, 2, 1, 3, 4)
    return (lam, lam_c, b_c.reshape(nb, 2, LANES, S5_STATE), c_c.reshape(nb, 2, LANES, S5_STATE))


def kernel(x_prompt, x_sample, cache_swa_k, cache_swa_v, state_s5_re, state_s5_im, norm_mix, norm_ffn, norm_final, w_in0, attn_sinks, gm_norm, gm_ws, gm_b, w_out0, s5_lam_re, s5_lam_im, s5_log_dt, s5_b_re, s5_b_im, s5_c_re, s5_c_im, s5_d, s5_w_glu, ffn_w_gate, ffn_w_up, ffn_w_down):
    batch, seq, d = x_prompt.shape
    dec_batch, dec_seq, _ = x_sample.shape
    assert batch == 1 and dec_seq == CHUNK and norm_mix.shape[0] == 2
    n_prompt = batch * seq
    n_sample = dec_batch * dec_seq
    d_gm = gm_norm.shape[-1]

    bias_tbl, sink_row = _attention_tables(attn_sinks[0])
    wsp, gmb = _gm_tables(gm_ws[0], gm_b[0], d_gm)
    x1, k_all, v_all, gvn = _mix0(
        x_prompt.reshape(n_prompt, d), x_sample.reshape(n_sample, d),
        cache_swa_k[0].reshape(dec_batch * WINDOW, D_KV), cache_swa_v[0].reshape(dec_batch * WINDOW, D_KV),
        norm_mix[0].reshape(1, d), w_in0[0].astype(BF16), bias_tbl, sink_row,
        gm_norm[0].reshape(1, d_gm), wsp, gmb, w_out0[0].astype(BF16), units=2)

    x2, rstd2 = _ffn(x1, norm_ffn[0].reshape(1, d), ffn_w_gate, ffn_w_up, ffn_w_down,
                     norm_mix[1].reshape(1, d), layer=0, tm=FFN_TM, tf=FFN_TF, final=False)

    n_groups = s5_lam_re.shape[1]
    dall, wec, wsc, tab = _s5_prep(
        *_s5_compact(s5_lam_re[0], s5_lam_im[0], s5_log_dt[0], s5_b_re[0], s5_b_im[0], s5_c_re[0],
                     s5_c_im[0]),
        seg_rows=n_prompt // (S5_L * SUBLANES))
    ys5, sf_re, sf_im, w_glu_bf16 = _s5_core(
        x2, rstd2, norm_mix[1].reshape(1, d), dall, wec, wsc, tab,
        state_s5_re[0].reshape(dec_batch, n_groups * S5_STATE),
        state_s5_im[0].reshape(dec_batch, n_groups * S5_STATE), s5_d[0].reshape(1, d), s5_w_glu[0],
        n_prompt_rows=n_prompt)
    x3 = _glu(x2, ys5, w_glu_bf16, tm=256)

    def last_ffn(row0, n_rows):
        (y,) = _ffn(x3, norm_ffn[1].reshape(1, d), ffn_w_gate, ffn_w_up, ffn_w_down,
                    norm_final.reshape(1, d), layer=1, tm=FFN_TM, tf=FFN_TF, final=True,
                    row0=row0, n_rows=n_rows)
        return y

    y_prompt = last_ffn(0, n_prompt)
    y_sample = last_ffn(n_prompt, n_sample)

    keep = min(WINDOW, seq)
    y_prompt = y_prompt.reshape(batch, seq, d)
    y_sample = y_sample.reshape(dec_batch, dec_seq, d)
    kv_shape_p = (1, batch, keep, N_KV_HEADS, HEAD_DIM)
    kv_shape_s = (1, dec_batch, dec_seq, N_KV_HEADS, HEAD_DIM)
    st_p = (1, batch, n_groups, S5_STATE)
    st_s = (1, dec_batch, n_groups, S5_STATE)
    return (y_prompt, y_sample,
            k_all[n_prompt - keep:n_prompt].reshape(kv_shape_p),
            v_all[n_prompt - keep:n_prompt].reshape(kv_shape_p),
            k_all[n_prompt:].reshape(kv_shape_s),
            v_all[n_prompt:].reshape(kv_shape_s),
            gvn.reshape(1, dec_batch, dec_seq, d_gm),
            sf_re[dec_batch].reshape(st_p), sf_im[dec_batch].reshape(st_p),
            sf_re[:dec_batch].reshape(st_s), sf_im[:dec_batch].reshape(st_s))
```

```python
import functools
import math

import jax
import jax.numpy as jnp
from jax import lax
from jax.experimental import pallas as pl
from jax.experimental.pallas import tpu as pltpu

F32 = jnp.float32
BF16 = jnp.bfloat16

CHUNK = 64
HEAD_DIM = 64
N_HEADS = 16
N_KV_HEADS = 2
Q_PER_KV = N_HEADS // N_KV_HEADS
WINDOW = 128
BAND = WINDOW + CHUNK
D_ATTN = N_HEADS * HEAD_DIM
D_KV = N_KV_HEADS * HEAD_DIM
GM_CHUNK = 128
N_GM_GROUPS = 16
S5_GROUP = 16
S5_STATE = 64
RMS_EPS = 1e-5
NEG_INF = -1e30

LANES = 128
SUBLANES = 8
MXU_TILE = 256
VMEM_LIMIT_BYTES = 56 * 1024 * 1024

MIX_UNITS = 2
FFN_TM = 1024
FFN_TF = 256
GLU_TM = 512
GLU_COL_BLOCKS = 8

S5_L = SUBLANES
S5_LANE_GROUPS = LANES // S5_GROUP
S5_BLOCK_STATE = S5_LANE_GROUPS * S5_STATE
S5_WT_BLOCK = MXU_TILE


def _gelu(x):
    return 0.5 * x * (1.0 + lax.erf(x * math.sqrt(0.5)))


def _rms_scale(x):
    return x * lax.rsqrt(jnp.mean(x * x, axis=-1, keepdims=True) + RMS_EPS)


def _const_spec(shape):
    zeros = (0,) * len(shape)
    return pl.BlockSpec(shape, lambda *_: zeros, pipeline_mode=pl.Buffered(1))


def _mix0_kernel(n_prompt_tiles, units,
                 xp_ref, xs_ref, ck_ref, cv_ref, g_ref, win_ref, bias_ref, sink_ref, gmn_ref,
                 wsp_ref, gmb_ref, wout_ref,
                 x1_ref, k_ref, v_ref, gvn_ref,
                 z_ref, q_ref, ocat_ref, kprev_ref, vprev_ref):
    i = pl.program_id(0)
    is_sample = i >= n_prompt_tiles
    tm = units * GM_CHUNK
    d_gm = gmn_ref.shape[-1]
    off_k = D_ATTN
    off_v = D_ATTN + D_KV
    off_gu = D_ATTN + 2 * D_KV
    off_gv = off_gu + d_gm
    n_q = Q_PER_KV * CHUNK

    @pl.when(i == 0)
    def _():
        kprev_ref[...] = jnp.zeros_like(kprev_ref)
        vprev_ref[...] = jnp.zeros_like(vprev_ref)

    x = jnp.where(is_sample, xs_ref[...], xp_ref[...])
    h = (_rms_scale(x) * g_ref[...]).astype(BF16)
    z_ref[...] = jnp.dot(h, win_ref[...], preferred_element_type=F32)

    q_ref[...] = (z_ref[:, 0:D_ATTN] * (HEAD_DIM ** -0.5)).astype(BF16)
    k = z_ref[:, off_k:off_k + D_KV]
    v = z_ref[:, off_v:off_v + D_KV]
    k_ref[...] = k
    v_ref[...] = v

    def lane_lo(rows):
        return lax.broadcasted_iota(jnp.int32, (rows, LANES), 1) < HEAD_DIM

    def replicate(a):
        r = pltpu.roll(a, HEAD_DIM, axis=1)
        lo = lane_lo(a.shape[0])
        return jnp.where(lo, a, r).astype(BF16), jnp.where(lo, r, a).astype(BF16)

    k_rep = replicate(k)
    v_rep = replicate(v)
    ck_rep = replicate(ck_ref[...])
    cv_rep = replicate(cv_ref[...])

    lo64 = lane_lo(CHUNK)
    band_pos = lax.broadcasted_iota(jnp.int32, (BAND, n_q), 0)
    chunks_per_tile = tm // CHUNK

    def scores(u, c2, kv):
        r0 = u * GM_CHUNK + c2 * CHUNK
        stream = r0 // CHUNK
        chunk_index = i * chunks_per_tile + stream
        valid_from = jnp.where(is_sample, 0, jnp.maximum(WINDOW - CHUNK * chunk_index, 0))
        valid = band_pos >= valid_from

        def band(cur, prev_ref, cached):
            if u == 0:
                prev_unit = prev_ref[kv]
            else:
                prev_unit = cur[(u - 1) * GM_CHUNK:u * GM_CHUNK]
            if c2 == 0:
                prompt_prev = prev_unit
            else:
                prompt_prev = jnp.concatenate(
                    [prev_unit[CHUNK:], cur[u * GM_CHUNK:u * GM_CHUNK + CHUNK]], axis=0)
            sample_prev = cached[stream * WINDOW:(stream + 1) * WINDOW]
            prev = jnp.where(is_sample, sample_prev, prompt_prev)
            return jnp.concatenate([prev, cur[r0:r0 + CHUNK]], axis=0)

        kb = band(k_rep[kv], kprev_ref, ck_rep[kv])
        vb = band(v_rep[kv], vprev_ref, cv_rep[kv])

        pieces = []
        for m in range(Q_PER_KV // 2):
            c0 = kv * Q_PER_KV * HEAD_DIM + m * LANES
            qp = q_ref[r0:r0 + CHUNK, c0:c0 + LANES]
            pieces.append(jnp.where(lo64, qp, jnp.zeros_like(qp)))
            pieces.append(jnp.where(lo64, jnp.zeros_like(qp), qp))
        qs = jnp.concatenate(pieces, axis=0)
        cols = slice(kv * n_q, (kv + 1) * n_q)
        st = lax.dot_general(kb, qs, (((1,), (1,)), ((), ())),
                             preferred_element_type=F32)
        return jnp.where(valid, st - bias_ref[:, cols], NEG_INF), vb

    def attend(u, c2, kv, st, vb):
        r0 = u * GM_CHUNK + c2 * CHUNK
        cols = slice(kv * n_q, (kv + 1) * n_q)
        sink = sink_ref[:, cols]
        mx = jnp.maximum(jnp.max(st, axis=0, keepdims=True), sink)
        p = jnp.exp(st - mx)
        denom = jnp.sum(p, axis=0, keepdims=True) + jnp.exp(sink - mx)
        pn = (p * (1.0 / denom)).astype(BF16)
        o = lax.dot_general(pn, vb, (((0,), (0,)), ((), ())),
                            preferred_element_type=F32)
        for m in range(Q_PER_KV // 2):
            o_pair = jnp.where(lo64, o[(2 * m) * CHUNK:(2 * m + 1) * CHUNK],
                               o[(2 * m + 1) * CHUNK:(2 * m + 2) * CHUNK])
            c0 = kv * Q_PER_KV * HEAD_DIM + m * LANES
            ocat_ref[r0:r0 + CHUNK, c0:c0 + LANES] = o_pair.astype(BF16)

    blocks = [(u, c2, kv) for u in range(units) for c2 in range(GM_CHUNK // CHUNK)
              for kv in range(N_KV_HEADS)]
    scored = [scores(*blk) for blk in blocks]
    for blk, (st, vb) in zip(blocks, scored):
        attend(*blk, st, vb)

    lo128 = lane_lo(GM_CHUNK)
    for u in range(units):
        rows = slice(u * GM_CHUNK, (u + 1) * GM_CHUNK)
        ua = _gelu(z_ref[rows, off_gu:off_gu + d_gm])
        gvn = _rms_scale(_gelu(z_ref[rows, off_gv:off_gv + d_gm])) * gmn_ref[...]
        gvn_ref[rows, :] = gvn
        gb = gvn.astype(BF16)
        for m in range(N_GM_GROUPS // 2):
            cols = slice(m * LANES, (m + 1) * LANES)
            rhs = gb[:, cols]
            rhs2 = jnp.concatenate([jnp.where(lo128, rhs, jnp.zeros_like(rhs)),
                                    jnp.where(lo128, jnp.zeros_like(rhs), rhs)], axis=0)
            sp = jnp.dot(wsp_ref[0, m], rhs2, preferred_element_type=F32) + gmb_ref[0, :, cols]
            ocat_ref[rows, D_ATTN + m * LANES:D_ATTN + (m + 1) * LANES] = (ua[:, cols] * sp).astype(BF16)

    x1_ref[...] = x + jnp.dot(ocat_ref[...], wout_ref[...], preferred_element_type=F32)

    for kv in range(N_KV_HEADS):
        kprev_ref[kv] = k_rep[kv][tm - GM_CHUNK:]
        vprev_ref[kv] = v_rep[kv][tm - GM_CHUNK:]


def _mix0(xp, xs, cache_k, cache_v, g, w_in, bias_tbl, sink_row, gm_norm, wsp, gmb, w_out, *, units):
    n_prompt_rows, d = xp.shape
    n_sample_rows = xs.shape[0]
    m_rows = n_prompt_rows + n_sample_rows
    tm = units * GM_CHUNK
    assert n_prompt_rows % tm == 0 and n_sample_rows % tm == 0
    n_tiles = m_rows // tm
    n_prompt_tiles = n_prompt_rows // tm
    d_in = w_in.shape[1]
    d_gm = gm_norm.shape[-1]
    cache_rows = (tm // CHUNK) * WINDOW

    def prompt_block(i):
        return jnp.minimum(i, n_prompt_tiles - 1)

    def sample_block(i):
        return jnp.maximum(i - n_prompt_tiles, 0)

    def kind(i):
        return jnp.where(i >= n_prompt_tiles, 1, 0)

    in_specs = [
        pl.BlockSpec((tm, d), lambda i: (prompt_block(i), 0)),
        pl.BlockSpec((tm, d), lambda i: (sample_block(i), 0)),
        pl.BlockSpec((cache_rows, D_KV), lambda i: (sample_block(i), 0)),
        pl.BlockSpec((cache_rows, D_KV), lambda i: (sample_block(i), 0)),
        _const_spec((1, d)),
        _const_spec((d, d_in)),
        _const_spec(bias_tbl.shape),
        _const_spec(sink_row.shape),
        _const_spec((1, d_gm)),
        pl.BlockSpec((1,) + wsp.shape[1:], lambda i: (kind(i), 0, 0, 0)),
        pl.BlockSpec((1, GM_CHUNK, d_gm), lambda i: (kind(i), 0, 0)),
        _const_spec(w_out.shape),
    ]
    out_specs = [
        pl.BlockSpec((tm, d), lambda i: (i, 0)),
        pl.BlockSpec((tm, D_KV), lambda i: (i, 0)),
        pl.BlockSpec((tm, D_KV), lambda i: (i, 0)),
        pl.BlockSpec((tm, d_gm), lambda i: (sample_block(i), 0)),
    ]
    out_shape = [
        jax.ShapeDtypeStruct((m_rows, d), F32),
        jax.ShapeDtypeStruct((m_rows, D_KV), F32),
        jax.ShapeDtypeStruct((m_rows, D_KV), F32),
        jax.ShapeDtypeStruct((n_sample_rows, d_gm), F32),
    ]
    scratch = [
        pltpu.VMEM((tm, d_in), F32),
        pltpu.VMEM((tm, D_ATTN), BF16),
        pltpu.VMEM((tm, D_ATTN + d_gm), BF16),
        pltpu.VMEM((N_KV_HEADS, GM_CHUNK, LANES), BF16),
        pltpu.VMEM((N_KV_HEADS, GM_CHUNK, LANES), BF16),
    ]
    return pl.pallas_call(
        functools.partial(_mix0_kernel, n_prompt_tiles, units),
        grid=(n_tiles,),
        in_specs=in_specs,
        out_specs=out_specs,
        out_shape=out_shape,
        scratch_shapes=scratch,
        compiler_params=pltpu.CompilerParams(
            dimension_semantics=("arbitrary",), vmem_limit_bytes=VMEM_LIMIT_BYTES),
        name="mix0",
    )(xp, xs, cache_k, cache_v, g, w_in, bias_tbl, sink_row, gm_norm, wsp, gmb, w_out)


def _ffn_kernel(final, x_ref, g_ref, wg_ref, wu_ref, wd_ref, gnext_ref, out_ref, *rest):
    h_ref = rest[-1]
    j = pl.program_id(1)

    @pl.when(j == 0)
    def _():
        x = x_ref[...]
        h_ref[...] = (_rms_scale(x) * g_ref[...]).astype(BF16)
        out_ref[...] = x

    h = h_ref[...]
    gate = jnp.dot(h, wg_ref[0].astype(BF16), preferred_element_type=F32)
    up = jnp.dot(h, wu_ref[0].astype(BF16), preferred_element_type=F32)
    act = (gate * jax.nn.sigmoid(gate) * up).astype(BF16)
    out_ref[...] += jnp.dot(act, wd_ref[0].astype(BF16), preferred_element_type=F32)

    @pl.when(j == pl.num_programs(1) - 1)
    def _():
        out = out_ref[...]
        scale = lax.rsqrt(jnp.mean(out * out, axis=-1, keepdims=True) + RMS_EPS)
        if final:
            out_ref[...] = out * scale * gnext_ref[...]
        else:
            rest[0][...] = jnp.broadcast_to(scale, rest[0].shape)


def _ffn(x, g, wg, wu, wd, gnext, *, layer, tm, tf, final, row0=0, n_rows=None):
    d = x.shape[1]
    n_rows = x.shape[0] if n_rows is None else n_rows
    f = wg.shape[2]
    assert n_rows % tm == 0 and row0 % tm == 0 and f % tf == 0
    tile0 = row0 // tm
    out_specs = [pl.BlockSpec((tm, d), lambda i, j: (i, 0))]
    out_shape = [jax.ShapeDtypeStruct((n_rows, d), F32)]
    if not final:
        out_specs.append(pl.BlockSpec((tm, LANES), lambda i, j: (i, 0)))
        out_shape.append(jax.ShapeDtypeStruct((n_rows, LANES), F32))
    return pl.pallas_call(
        functools.partial(_ffn_kernel, final),
        grid=(n_rows // tm, f // tf),
        in_specs=[
            pl.BlockSpec((tm, d), lambda i, j: (i + tile0, 0)),
            pl.BlockSpec((1, d), lambda i, j: (0, 0)),
            pl.BlockSpec((1, d, tf), lambda i, j: (layer, 0, j)),
            pl.BlockSpec((1, d, tf), lambda i, j: (layer, 0, j)),
            pl.BlockSpec((1, tf, d), lambda i, j: (layer, j, 0)),
            pl.BlockSpec((1, d), lambda i, j: (0, 0)),
        ],
        out_specs=out_specs,
        out_shape=out_shape,
        scratch_shapes=[pltpu.VMEM((tm, d), BF16)],
        compiler_params=pltpu.CompilerParams(
            dimension_semantics=("arbitrary", "arbitrary"), vmem_limit_bytes=VMEM_LIMIT_BYTES),
        name="ffn",
    )(x, g, wg, wu, wd, gnext)


TAB_ROWS = 18


def _discretize(lr, li, log_dt):
    dt = jnp.exp(log_dt)
    mag = jnp.exp(lr * dt)
    ar = mag * jnp.cos(li * dt)
    ai = mag * jnp.sin(li * dt)
    den = lr * lr + li * li
    nr = ar - 1.0
    return ar, ai, (nr * lr + ai * li) / den, (ai * lr - nr * li) / den


def _cmul(pr, pi, qr, qi):
    return pr * qr - pi * qi, pr * qi + pi * qr


def _s5_prep_kernel(seg_rows, lam_ref, lamc_ref, b_ref, c_ref, dall_ref, wec_ref, wsc_ref, tab_ref,
                    bhi_ref, blo_ref):
    ns = S5_BLOCK_STATE

    ar, ai, fr, fi = _discretize(lamc_ref[0, 0], lamc_ref[0, 1], lamc_ref[0, 2])
    bbr, bbi = _cmul(fr, fi, b_ref[0, 0], b_ref[0, 1])
    c_re = c_ref[0, 0]
    c_im = c_ref[0, 1]

    def split(a):
        hi = a.astype(BF16)
        return hi, (a - hi.astype(F32)).astype(BF16)

    power = (jnp.ones_like(ar), jnp.zeros_like(ar))
    for l in range(S5_L):
        bkr, bki = _cmul(*power, bbr, bbi)
        power = _cmul(*power, ar, ai)
        rows = slice(l * LANES, (l + 1) * LANES)
        bk = jnp.concatenate([bkr, bki], axis=1)
        bhi_ref[rows, :], blo_ref[rows, :] = split(bk)
        wec_ref[0, (S5_L - 1 - l) * LANES:(S5_L - l) * LANES, :] = bk
        qr, qi = power
        wsc_ref[0, rows, :] = jnp.concatenate([c_re * qr - c_im * qi, -c_re * qi - c_im * qr], axis=1)

    c_hi, c_lo = split(jnp.concatenate([c_re, -c_im], axis=1))
    c_parts = jnp.concatenate([c_hi, c_lo], axis=0)
    dims = (((1,), (1,)), ((), ()))
    d_hi = lax.dot_general(bhi_ref[...], c_parts, dims, preferred_element_type=F32)
    d_lo = lax.dot_general(blo_ref[...], c_parts, dims, preferred_element_type=F32)
    d_all = d_hi[:, 0:LANES] + d_hi[:, LANES:2 * LANES] + d_lo[:, 0:LANES]
    shape = (S5_L * LANES, LANES)
    in_group = (lax.broadcasted_iota(jnp.int32, shape, 0) % LANES) // S5_GROUP
    out_group = lax.broadcasted_iota(jnp.int32, shape, 1) // S5_GROUP
    dall_ref[0] = jnp.where(in_group == out_group, d_all, 0.0).astype(BF16)

    ar, ai, _, _ = _discretize(lam_ref[0, 0], lam_ref[0, 1], lam_ref[0, 2])
    row = lax.broadcasted_iota(jnp.int32, (SUBLANES, ns), 0)

    def bcast(a):
        return jnp.broadcast_to(a, (SUBLANES, ns))

    def log_step_tables(t0, base):
        cur = base
        for n, shift in enumerate((1, 2, 4)):
            tab_ref[0, t0 + 2 * n] = jnp.where(row >= shift, bcast(cur[0]), 0.0)
            tab_ref[0, t0 + 2 * n + 1] = jnp.where(row >= shift, bcast(cur[1]), 0.0)
            cur = _cmul(*cur, *cur)
        return cur

    a1 = (ar, ai)
    for _ in range(S5_L.bit_length() - 1):
        a1 = _cmul(*a1, *a1)
    a8 = log_step_tables(0, a1)
    pr_tab = jnp.zeros((SUBLANES, ns), F32)
    pi_tab = jnp.zeros((SUBLANES, ns), F32)
    cur = a1
    for r in range(SUBLANES):
        pr_tab = jnp.where(row == r, bcast(cur[0]), pr_tab)
        pi_tab = jnp.where(row == r, bcast(cur[1]), pi_tab)
        cur = _cmul(*cur, *a1)
    tab_ref[0, 6] = pr_tab
    tab_ref[0, 7] = pi_tab
    tab_ref[0, 8] = bcast(a8[0])
    tab_ref[0, 9] = bcast(a8[1])
    tab_ref[0, 10] = bcast(a1[0])
    tab_ref[0, 11] = bcast(a1[1])
    seg = a1
    for _ in range(seg_rows.bit_length() - 1):
        seg = _cmul(*seg, *seg)
    log_step_tables(12, seg)


def _s5_prep(lam, lam_c, b_c, c_c, *, seg_rows):
    assert seg_rows & (seg_rows - 1) == 0 and S5_L & (S5_L - 1) == 0
    nb = lam.shape[0]
    ns = S5_BLOCK_STATE
    k = S5_L * LANES
    cspec = pl.BlockSpec((1, k, LANES), lambda j: (j, 0, 0))

    def whole(a):
        return pl.BlockSpec((1,) + a.shape[1:], lambda j: (j,) + (0,) * (a.ndim - 1))

    return pl.pallas_call(
        functools.partial(_s5_prep_kernel, seg_rows),
        grid=(nb,),
        in_specs=[whole(lam), whole(lam_c), whole(b_c), whole(c_c)],
        out_specs=[cspec, cspec, cspec,
                   pl.BlockSpec((1, TAB_ROWS, SUBLANES, ns), lambda j: (j, 0, 0, 0))],
        out_shape=[
            jax.ShapeDtypeStruct((nb, k, LANES), BF16),
            jax.ShapeDtypeStruct((nb, k, LANES), F32),
            jax.ShapeDtypeStruct((nb, k, LANES), F32),
            jax.ShapeDtypeStruct((nb, TAB_ROWS, SUBLANES, ns), F32),
        ],
        scratch_shapes=[pltpu.VMEM((k, LANES), BF16), pltpu.VMEM((k, LANES), BF16)],
        compiler_params=pltpu.CompilerParams(
            dimension_semantics=("arbitrary",), vmem_limit_bytes=VMEM_LIMIT_BYTES),
        name="s5_prep",
    )(lam, lam_c, b_c, c_c)


def _s5_core_kernel(n_prompt_rows, n_streams,
                    x_ref, rstd_ref, g_ref, dall_ref, wec_ref, wsc_ref, tab_ref, s0r_ref, s0i_ref,
                    d_ref, wglu_ref,
                    y_ref, sfr_ref, sfi_ref, wglu_bf16_ref,
                    u2f_ref, u2_ref, e_ref, wt_ref, we_ref, wst_ref):
    wglu_bf16_ref[...] = wglu_ref[...].astype(BF16)
    ns = S5_BLOCK_STATE
    n_rows = x_ref.shape[0] // S5_L
    n_sample_rows = n_rows - n_prompt_rows
    seg_rows = n_prompt_rows // SUBLANES
    n_prompt_frames = n_prompt_rows * S5_L

    shape = (S5_L * LANES, LANES)
    lane = lax.broadcasted_iota(jnp.int32, shape, 1)
    lane_lo = lane < S5_STATE
    row_group = (lax.broadcasted_iota(jnp.int32, shape, 0) % LANES) // S5_GROUP
    groups_per_vreg = LANES // S5_STATE

    def expand(compact_ref, full_ref):
        x = compact_ref[0]
        r = pltpu.roll(x, S5_STATE, axis=1)
        for part, rep in enumerate((jnp.where(lane_lo, x, r), jnp.where(lane_lo, r, x))):
            for v in range(S5_LANE_GROUPS // groups_per_vreg):
                own = row_group == groups_per_vreg * v + (lane // S5_STATE)
                col = part * S5_BLOCK_STATE + v * LANES
                full_ref[:, col:col + LANES] = jnp.where(own, rep, 0.0).astype(BF16)

    expand(wec_ref, we_ref)

    def frame_slices():
        for l in range(S5_L):
            for seg in range(SUBLANES):
                yield l, pl.ds(seg * seg_rows * S5_L + l, seg_rows, stride=S5_L), \
                    pl.ds(seg, seg_rows, stride=SUBLANES)
            yield l, pl.ds(n_prompt_frames + l, n_sample_rows, stride=S5_L), \
                pl.ds(n_prompt_rows, n_sample_rows)

    for l, frames, rows in frame_slices():
        u2f_ref[l, rows, :] = x_ref[frames, :] * rstd_ref[frames, :] * g_ref[...]
    for l in range(S5_L):
        u2_ref[:, l * LANES:(l + 1) * LANES] = u2f_ref[l].astype(BF16)
    e_ref[...] = jnp.dot(u2_ref[...], we_ref[...], preferred_element_type=F32)

    zero_blk = jnp.zeros((LANES, LANES), BF16)
    for l in range(S5_L):
        for l2 in range(S5_L):
            blk = dall_ref[0, (l2 - l) * LANES:(l2 - l + 1) * LANES, :] if l2 >= l else zero_blk
            wt_ref[l * LANES:(l + 1) * LANES, l2 * LANES:(l2 + 1) * LANES] = blk
    slabs_per_block = S5_WT_BLOCK // LANES
    n_col_blocks = S5_L * LANES // S5_WT_BLOCK
    for cb in range(n_col_blocks):
        k_end = (cb + 1) * S5_WT_BLOCK
        y2 = jnp.dot(u2_ref[:, 0:k_end], wt_ref[0:k_end, cb * S5_WT_BLOCK:k_end],
                     preferred_element_type=F32)
        for n in range(slabs_per_block):
            l = cb * slabs_per_block + n
            u2f_ref[l] = y2[:, n * LANES:(n + 1) * LANES] + d_ref[...] * u2f_ref[l]
    expand(wsc_ref, wst_ref)

    tabs = [tab_ref[0, t] for t in range(TAB_ROWS)]
    m_tabs, (pr, pi, a8r, a8i, ar, ai), b_tabs = tabs[0:6], tabs[6:12], tabs[12:18]
    row = lax.broadcasted_iota(jnp.int32, (SUBLANES, ns), 0)

    def log_step_scan(xr, xi, t):
        for n, shift in enumerate((1, 2, 4)):
            tr, ti = t[2 * n], t[2 * n + 1]
            sr = pltpu.roll(xr, shift, axis=0)
            si = pltpu.roll(xi, shift, axis=0)
            xr, xi = xr + tr * sr - ti * si, xi + tr * si + ti * sr
        return xr, xi

    def shift_down(xr, xi, fr, fi):
        first = row == 0
        return (jnp.where(first, fr, pltpu.roll(xr, 1, axis=0)),
                jnp.where(first, fi, pltpu.roll(xi, 1, axis=0)))

    def rows_of(k):
        return pl.ds(pl.multiple_of(k * SUBLANES, SUBLANES), SUBLANES)

    def local_step(k, carry):
        sr, si = carry
        er = e_ref[rows_of(k), 0:ns]
        ei = e_ref[rows_of(k), ns:2 * ns]
        e_ref[rows_of(k), 0:ns] = sr
        e_ref[rows_of(k), ns:2 * ns] = si
        return ar * sr - ai * si + er, ar * si + ai * sr + ei

    zero = jnp.zeros((SUBLANES, ns), F32)
    ends = lax.fori_loop(0, seg_rows, local_step, (zero, zero), unroll=True)
    ends = log_step_scan(*ends, b_tabs)
    sfr_ref[...] = jnp.zeros_like(sfr_ref)
    sfi_ref[...] = jnp.zeros_like(sfi_ref)
    sfr_ref[n_streams:n_streams + 1, :] = ends[0][SUBLANES - 1:SUBLANES]
    sfi_ref[n_streams:n_streams + 1, :] = ends[1][SUBLANES - 1:SUBLANES]

    def correct_step(k, carry):
        cr, ci = carry
        e_ref[rows_of(k), 0:ns] += cr
        e_ref[rows_of(k), ns:2 * ns] += ci
        return ar * cr - ai * ci, ar * ci + ai * cr

    lax.fori_loop(0, seg_rows, correct_step, shift_down(*ends, zero, zero), unroll=True)

    for b in range(n_streams):
        rows = pl.ds(n_prompt_rows + b * SUBLANES, SUBLANES)
        cr = jnp.broadcast_to(s0r_ref[b:b + 1, :], (SUBLANES, ns))
        ci = jnp.broadcast_to(s0i_ref[b:b + 1, :], (SUBLANES, ns))
        xr, xi = log_step_scan(e_ref[rows, 0:ns], e_ref[rows, ns:2 * ns], m_tabs)
        st_r = xr + pr * cr - pi * ci
        st_i = xi + pr * ci + pi * cr
        e_ref[rows, 0:ns], e_ref[rows, ns:2 * ns] = shift_down(st_r, st_i, cr, ci)
        sfr_ref[b:b + 1, :] = st_r[SUBLANES - 1:SUBLANES]
        sfi_ref[b:b + 1, :] = st_i[SUBLANES - 1:SUBLANES]

    sprev = e_ref[...].astype(BF16)
    for cb in range(n_col_blocks):
        cols = slice(cb * S5_WT_BLOCK, (cb + 1) * S5_WT_BLOCK)
        y2 = lax.dot_general(sprev, wst_ref[cols, :], (((1,), (1,)), ((), ())),
                             preferred_element_type=F32)
        for n in range(slabs_per_block):
            l = cb * slabs_per_block + n
            u2f_ref[l] += y2[:, n * LANES:(n + 1) * LANES]
    for l, frames, rows in frame_slices():
        y_ref[frames, :] = u2f_ref[l, rows, :]


def _s5_core(x, rstd, g, dall, wec, wsc, tab, s0_re, s0_im, d_skip, w_glu, *, n_prompt_rows):
    m_rows, d = x.shape
    nb = d // LANES
    glu_cols = w_glu.shape[1] // nb
    assert glu_cols % LANES == 0
    glu_blk = pl.BlockSpec((w_glu.shape[0], glu_cols), lambda j: (0, j))
    ns = S5_BLOCK_STATE
    k = S5_L * LANES
    n_rows = m_rows // S5_L
    n_streams = s0_re.shape[0]
    assert n_prompt_rows % (S5_L * SUBLANES) == 0
    assert (m_rows - n_prompt_rows) == n_streams * S5_L * SUBLANES
    sf_rows = n_streams + SUBLANES
    lane_blk = pl.BlockSpec((m_rows, LANES), lambda j: (0, j))
    cspec = pl.BlockSpec((1, k, LANES), lambda j: (j, 0, 0))
    st = pl.BlockSpec((n_streams, ns), lambda j: (0, j))
    sf = pl.BlockSpec((sf_rows, ns), lambda j: (0, j))
    return pl.pallas_call(
        functools.partial(_s5_core_kernel, n_prompt_rows // S5_L, n_streams),
        grid=(nb,),
        in_specs=[
            lane_blk,
            _const_spec((m_rows, LANES)),
            pl.BlockSpec((1, LANES), lambda j: (0, j)),
            cspec, cspec, cspec,
            pl.BlockSpec((1, TAB_ROWS, SUBLANES, ns), lambda j: (j, 0, 0, 0)),
            st, st,
            pl.BlockSpec((1, LANES), lambda j: (0, j)),
            glu_blk,
        ],
        out_specs=[lane_blk, sf, sf, glu_blk],
        out_shape=[
            jax.ShapeDtypeStruct((m_rows, d), F32),
            jax.ShapeDtypeStruct((sf_rows, nb * ns), F32),
            jax.ShapeDtypeStruct((sf_rows, nb * ns), F32),
            jax.ShapeDtypeStruct(w_glu.shape, BF16),
        ],
        scratch_shapes=[
            pltpu.VMEM((S5_L, n_rows, LANES), F32),
            pltpu.VMEM((n_rows, k), BF16),
            pltpu.VMEM((n_rows, 2 * ns), F32),
            pltpu.VMEM((k, k), BF16),
            pltpu.VMEM((k, 2 * ns), BF16),
            pltpu.VMEM((k, 2 * ns), BF16),
        ],
        compiler_params=pltpu.CompilerParams(
            dimension_semantics=("arbitrary",), vmem_limit_bytes=VMEM_LIMIT_BYTES),
        name="s5_core",
    )(x, rstd, g, dall, wec, wsc, tab, s0_re, s0_im, d_skip, w_glu)


def _glu_kernel(n_col_blocks, x_ref, y_ref, w_ref, out_ref):
    d = x_ref.shape[1]
    a = _gelu(y_ref[...]).astype(BF16)
    cb = d // n_col_blocks
    for n in range(n_col_blocks):
        cols = slice(n * cb, (n + 1) * cb)
        val = jnp.dot(a, w_ref[:, n * cb:(n + 1) * cb], preferred_element_type=F32)
        gate = jnp.dot(a, w_ref[:, d + n * cb:d + (n + 1) * cb], preferred_element_type=F32)
        out_ref[:, cols] = x_ref[:, cols] + val * jax.nn.sigmoid(gate)


def _glu(x, y, w, *, tm, n_col_blocks=GLU_COL_BLOCKS):
    m_rows, d = x.shape
    assert m_rows % tm == 0 and d % n_col_blocks == 0
    tile = pl.BlockSpec((tm, d), lambda i: (i, 0))
    return pl.pallas_call(
        functools.partial(_glu_kernel, n_col_blocks),
        grid=(m_rows // tm,),
        in_specs=[tile, tile, _const_spec(w.shape)],
        out_specs=tile,
        out_shape=jax.ShapeDtypeStruct((m_rows, d), F32),
        compiler_params=pltpu.CompilerParams(
            dimension_semantics=("arbitrary",), vmem_limit_bytes=VMEM_LIMIT_BYTES),
        name="glu",
    )(x, y, w)


def _attention_tables(sinks):
    slopes = jnp.exp2(-8.0 * jnp.arange(1, N_HEADS + 1, dtype=F32) / N_HEADS)
    frame = jnp.arange(CHUNK, dtype=F32)[None, :]
    band = jnp.arange(BAND, dtype=F32)[:, None]
    dist = jnp.abs(frame - (band - WINDOW))
    bias = (dist[:, None, :] * slopes[None, :, None]).reshape(BAND, N_HEADS * CHUNK)
    sink_row = jnp.repeat(sinks.astype(F32), CHUNK).reshape(1, N_HEADS * CHUNK)
    return bias, sink_row


def _gm_tables(gm_ws, gm_b, d_gm):
    blk = jnp.arange(GM_CHUNK) // CHUNK
    w_prompt = jnp.where((blk[:, None] >= blk[None, :])[None], gm_ws, 0.0)
    top = gm_ws[:, :CHUNK, :CHUNK]
    zeros = jnp.zeros_like(top)
    w_sample = jnp.concatenate(
        [jnp.concatenate([top, zeros], axis=2), jnp.concatenate([zeros, top], axis=2)], axis=1)
    wsp = jnp.stack([w_prompt, w_sample])
    wsp = wsp.reshape(2, N_GM_GROUPS // 2, 2, GM_CHUNK, GM_CHUNK).transpose(0, 1, 3, 2, 4)
    wsp = wsp.reshape(2, N_GM_GROUPS // 2, GM_CHUNK, 2 * GM_CHUNK).astype(BF16)
    b_prompt = gm_b.T
    b_sample = jnp.concatenate([gm_b[:, :CHUNK].T, gm_b[:, :CHUNK].T], axis=0)
    gmb = jnp.stack([b_prompt, b_sample]).astype(F32)
    gmb = jnp.repeat(gmb, d_gm // N_GM_GROUPS, axis=2)
    return wsp, gmb


def _s5_compact(lam_re, lam_im, log_dt, b_re, b_im, c_re, c_im):
    n_groups = lam_re.shape[0]
    nb = n_groups // S5_LANE_GROUPS
    per_state = jnp.stack([lam_re, lam_im, jnp.broadcast_to(log_dt[:, None], lam_re.shape)]).astype(F32)
    per_state = per_state.reshape(3, nb, S5_LANE_GROUPS, S5_STATE).transpose(1, 0, 2, 3)
    lam = per_state.reshape(nb, 3, 1, S5_BLOCK_STATE)
    lam_c = jnp.broadcast_to(per_state[:, :, :, None, :],
                             (nb, 3, S5_LANE_GROUPS, S5_GROUP, S5_STATE)).reshape(nb, 3, LANES, S5_STATE)
    b_c = jnp.stack([b_re, b_im]).astype(F32).transpose(1, 0, 3, 2)
    b_c = b_c.reshape(nb, S5_LANE_GROUPS, 2, S5_GROUP, S5_STATE).transpose(0, 2, 1, 3, 4)
    c_c = jnp.stack([c_re, c_im]).astype(F32).transpose(1, 0, 2, 3)
    c_c = c_c.reshape(nb, S5_LANE_GROUPS, 2, S5_GROUP, S5_STATE).transpose(0, 2, 1, 3, 4)
    return (lam, lam_c, b_c.reshape(nb, 2, LANES, S5_STATE), c_c.reshape(nb, 2, LANES, S5_STATE))


def kernel(x_prompt, x_sample, cache_swa_k, cache_swa_v, state_s5_re, state_s5_im, norm_mix, norm_ffn, norm_final, w_in0, attn_sinks, gm_norm, gm_ws, gm_b, w_out0, s5_lam_re, s5_lam_im, s5_log_dt, s5_b_re, s5_b_im, s5_c_re, s5_c_im, s5_d, s5_w_glu, ffn_w_gate, ffn_w_up, ffn_w_down):
    batch, seq, d = x_prompt.shape
    dec_batch, dec_seq, _ = x_sample.shape
    assert batch == 1 and dec_seq == CHUNK and norm_mix.shape[0] == 2
    n_prompt = batch * seq
    n_sample = dec_batch * dec_seq
    d_gm = gm_norm.shape[-1]

    bias_tbl, sink_row = _attention_tables(attn_sinks[0])
    wsp, gmb = _gm_tables(gm_ws[0], gm_b[0], d_gm)
    x1, k_all, v_all, gvn = _mix0(
        x_prompt.reshape(n_prompt, d), x_sample.reshape(n_sample, d),
        cache_swa_k[0].reshape(dec_batch * WINDOW, D_KV), cache_swa_v[0].reshape(dec_batch * WINDOW, D_KV),
        norm_mix[0].reshape(1, d), w_in0[0].astype(BF16), bias_tbl, sink_row,
        gm_norm[0].reshape(1, d_gm), wsp, gmb, w_out0[0].astype(BF16), units=MIX_UNITS)

    x2, rstd2 = _ffn(x1, norm_ffn[0].reshape(1, d), ffn_w_gate, ffn_w_up, ffn_w_down,
                     norm_mix[1].reshape(1, d), layer=0, tm=FFN_TM, tf=FFN_TF, final=False)

    n_groups = s5_lam_re.shape[1]
    dall, wec, wsc, tab = _s5_prep(
        *_s5_compact(s5_lam_re[0], s5_lam_im[0], s5_log_dt[0], s5_b_re[0], s5_b_im[0], s5_c_re[0],
                     s5_c_im[0]),
        seg_rows=n_prompt // (S5_L * SUBLANES))
    ys5, sf_re, sf_im, w_glu_bf16 = _s5_core(
        x2, rstd2, norm_mix[1].reshape(1, d), dall, wec, wsc, tab,
        state_s5_re[0].reshape(dec_batch, n_groups * S5_STATE),
        state_s5_im[0].reshape(dec_batch, n_groups * S5_STATE), s5_d[0].reshape(1, d), s5_w_glu[0],
        n_prompt_rows=n_prompt)
    x3 = _glu(x2, ys5, w_glu_bf16, tm=GLU_TM)

    def last_ffn(row0, n_rows):
        (y,) = _ffn(x3, norm_ffn[1].reshape(1, d), ffn_w_gate, ffn_w_up, ffn_w_down,
                    norm_final.reshape(1, d), layer=1, tm=FFN_TM, tf=FFN_TF, final=True,
                    row0=row0, n_rows=n_rows)
        return y

    y_prompt = last_ffn(0, n_prompt)
    y_sample = last_ffn(n_prompt, n_sample)

    keep = min(WINDOW, seq)
    y_prompt = y_prompt.reshape(batch, seq, d)
    y_sample = y_sample.reshape(dec_batch, dec_seq, d)
    kv_shape_p = (1, batch, keep, N_KV_HEADS, HEAD_DIM)
    kv_shape_s = (1, dec_batch, dec_seq, N_KV_HEADS, HEAD_DIM)
    st_p = (1, batch, n_groups, S5_STATE)
    st_s = (1, dec_batch, n_groups, S5_STATE)
    return (y_prompt, y_sample,
            k_all[n_prompt - keep:n_prompt].reshape(kv_shape_p),
            v_all[n_prompt - keep:n_prompt].reshape(kv_shape_p),
            k_all[n_prompt:].reshape(kv_shape_s),
            v_all[n_prompt:].reshape(kv_shape_s),
            gvn.reshape(1, dec_batch, dec_seq, d_gm),
            sf_re[dec_batch].reshape(st_p), sf_im[dec_batch].reshape(st_p),
            sf_re[:dec_batch].reshape(st_s), sf_im[:dec_batch].reshape(st_s))
```

```python
import functools
import math

import jax
import jax.numpy as jnp
from jax import lax
from jax.experimental import pallas as pl
from jax.experimental.pallas import tpu as pltpu

F32 = jnp.float32
BF16 = jnp.bfloat16

CHUNK = 64
HEAD_DIM = 64
N_HEADS = 16
N_KV_HEADS = 2
Q_PER_KV = N_HEADS // N_KV_HEADS
WINDOW = 128
BAND = WINDOW + CHUNK
D_ATTN = N_HEADS * HEAD_DIM
D_KV = N_KV_HEADS * HEAD_DIM
GM_CHUNK = 128
N_GM_GROUPS = 16
S5_GROUP = 16
S5_STATE = 64
RMS_EPS = 1e-5
NEG_INF = -1e30

LANES = 128
SUBLANES = 8
MXU_TILE = 256
VMEM_LIMIT_BYTES = 56 * 1024 * 1024

MIX_UNITS = 2
FFN_TM = 1024
FFN_TF = 256
GLU_TM = 512
GLU_COL_BLOCKS = 8

S5_L = SUBLANES
S5_LANE_GROUPS = LANES // S5_GROUP
S5_BLOCK_STATE = S5_LANE_GROUPS * S5_STATE
S5_WT_BLOCK = MXU_TILE


def _gelu(x):
    return 0.5 * x * (1.0 + lax.erf(x * math.sqrt(0.5)))


def _rms_scale(x):
    return x * lax.rsqrt(jnp.mean(x * x, axis=-1, keepdims=True) + RMS_EPS)


def _const_spec(shape):
    zeros = (0,) * len(shape)
    return pl.BlockSpec(shape, lambda *_: zeros, pipeline_mode=pl.Buffered(1))


def _mix0_kernel(n_prompt_tiles, units,
                 xp_ref, xs_ref, ck_ref, cv_ref, g_ref, win_ref, bias_ref, sink_ref, gmn_ref,
                 wsp_ref, gmb_ref, wout_ref,
                 x1_ref, k_ref, v_ref, gvn_ref,
                 z_ref, q_ref, ocat_ref, kprev_ref, vprev_ref):
    i = pl.program_id(0)
    is_sample = i >= n_prompt_tiles
    tm = units * GM_CHUNK
    d_gm = gmn_ref.shape[-1]
    off_k = D_ATTN
    off_v = D_ATTN + D_KV
    off_gu = D_ATTN + 2 * D_KV
    off_gv = off_gu + d_gm
    n_q = Q_PER_KV * CHUNK

    @pl.when(i == 0)
    def _():
        kprev_ref[...] = jnp.zeros_like(kprev_ref)
        vprev_ref[...] = jnp.zeros_like(vprev_ref)

    x = jnp.where(is_sample, xs_ref[...], xp_ref[...])
    h = (_rms_scale(x) * g_ref[...]).astype(BF16)
    z_ref[...] = jnp.dot(h, win_ref[...], preferred_element_type=F32)

    q_ref[...] = (z_ref[:, 0:D_ATTN] * (HEAD_DIM ** -0.5)).astype(BF16)
    k = z_ref[:, off_k:off_k + D_KV]
    v = z_ref[:, off_v:off_v + D_KV]
    k_ref[...] = k
    v_ref[...] = v

    def lane_lo(rows):
        return lax.broadcasted_iota(jnp.int32, (rows, LANES), 1) < HEAD_DIM

    def replicate(a):
        r = pltpu.roll(a, HEAD_DIM, axis=1)
        lo = lane_lo(a.shape[0])
        return jnp.where(lo, a, r).astype(BF16), jnp.where(lo, r, a).astype(BF16)

    k_rep = replicate(k)
    v_rep = replicate(v)
    ck_rep = replicate(ck_ref[...])
    cv_rep = replicate(cv_ref[...])

    lo64 = lane_lo(CHUNK)
    band_pos = lax.broadcasted_iota(jnp.int32, (BAND, n_q), 0)
    chunks_per_tile = tm // CHUNK

    def scores(u, c2, kv):
        r0 = u * GM_CHUNK + c2 * CHUNK
        stream = r0 // CHUNK
        chunk_index = i * chunks_per_tile + stream
        valid_from = jnp.where(is_sample, 0, jnp.maximum(WINDOW - CHUNK * chunk_index, 0))
        valid = band_pos >= valid_from

        def band(cur, prev_ref, cached):
            if u == 0:
                prev_unit = prev_ref[kv]
            else:
                prev_unit = cur[(u - 1) * GM_CHUNK:u * GM_CHUNK]
            if c2 == 0:
                prompt_prev = prev_unit
            else:
                prompt_prev = jnp.concatenate(
                    [prev_unit[CHUNK:], cur[u * GM_CHUNK:u * GM_CHUNK + CHUNK]], axis=0)
            sample_prev = cached[stream * WINDOW:(stream + 1) * WINDOW]
            prev = jnp.where(is_sample, sample_prev, prompt_prev)
            return jnp.concatenate([prev, cur[r0:r0 + CHUNK]], axis=0)

        kb = band(k_rep[kv], kprev_ref, ck_rep[kv])
        vb = band(v_rep[kv], vprev_ref, cv_rep[kv])

        pieces = []
        for m in range(Q_PER_KV // 2):
            c0 = kv * Q_PER_KV * HEAD_DIM + m * LANES
            qp = q_ref[r0:r0 + CHUNK, c0:c0 + LANES]
            pieces.append(jnp.where(lo64, qp, jnp.zeros_like(qp)))
            pieces.append(jnp.where(lo64, jnp.zeros_like(qp), qp))
        qs = jnp.concatenate(pieces, axis=0)
        cols = slice(kv * n_q, (kv + 1) * n_q)
        st = lax.dot_general(kb, qs, (((1,), (1,)), ((), ())),
                             preferred_element_type=F32)
        return jnp.where(valid, st - bias_ref[:, cols], NEG_INF), vb

    def attend(u, c2, kv, st, vb):
        r0 = u * GM_CHUNK + c2 * CHUNK
        cols = slice(kv * n_q, (kv + 1) * n_q)
        sink = sink_ref[:, cols]
        mx = jnp.maximum(jnp.max(st, axis=0, keepdims=True), sink)
        p = jnp.exp(st - mx)
        denom = jnp.sum(p, axis=0, keepdims=True) + jnp.exp(sink - mx)
        pn = (p * (1.0 / denom)).astype(BF16)
        o = lax.dot_general(pn, vb, (((0,), (0,)), ((), ())),
                            preferred_element_type=F32)
        for m in range(Q_PER_KV // 2):
            o_pair = jnp.where(lo64, o[(2 * m) * CHUNK:(2 * m + 1) * CHUNK],
                               o[(2 * m + 1) * CHUNK:(2 * m + 2) * CHUNK])
            c0 = kv * Q_PER_KV * HEAD_DIM + m * LANES
            ocat_ref[r0:r0 + CHUNK, c0:c0 + LANES] = o_pair.astype(BF16)

    blocks = [(u, c2, kv) for u in range(units) for c2 in range(GM_CHUNK // CHUNK)
              for kv in range(N_KV_HEADS)]
    scored = [scores(*blk) for blk in blocks]
    for blk, (st, vb) in zip(blocks, scored):
        attend(*blk, st, vb)

    lo128 = lane_lo(GM_CHUNK)
    for u in range(units):
        rows = slice(u * GM_CHUNK, (u + 1) * GM_CHUNK)
        ua = _gelu(z_ref[rows, off_gu:off_gu + d_gm])
        gvn = _rms_scale(_gelu(z_ref[rows, off_gv:off_gv + d_gm])) * gmn_ref[...]
        gvn_ref[rows, :] = gvn
        gb = gvn.astype(BF16)
        for m in range(N_GM_GROUPS // 2):
            cols = slice(m * LANES, (m + 1) * LANES)
            rhs = gb[:, cols]
            rhs2 = jnp.concatenate([jnp.where(lo128, rhs, jnp.zeros_like(rhs)),
                                    jnp.where(lo128, jnp.zeros_like(rhs), rhs)], axis=0)
            sp = jnp.dot(wsp_ref[0, m], rhs2, preferred_element_type=F32) + gmb_ref[0, :, cols]
            ocat_ref[rows, D_ATTN + m * LANES:D_ATTN + (m + 1) * LANES] = (ua[:, cols] * sp).astype(BF16)

    x1_ref[...] = x + jnp.dot(ocat_ref[...], wout_ref[...], preferred_element_type=F32)

    for kv in range(N_KV_HEADS):
        kprev_ref[kv] = k_rep[kv][tm - GM_CHUNK:]
        vprev_ref[kv] = v_rep[kv][tm - GM_CHUNK:]


def _mix0(xp, xs, cache_k, cache_v, g, w_in, bias_tbl, sink_row, gm_norm, wsp, gmb, w_out, *, units):
    n_prompt_rows, d = xp.shape
    n_sample_rows = xs.shape[0]
    m_rows = n_prompt_rows + n_sample_rows
    tm = units * GM_CHUNK
    assert n_prompt_rows % tm == 0 and n_sample_rows % tm == 0
    n_tiles = m_rows // tm
    n_prompt_tiles = n_prompt_rows // tm
    d_in = w_in.shape[1]
    d_gm = gm_norm.shape[-1]
    cache_rows = (tm // CHUNK) * WINDOW

    def prompt_block(i):
        return jnp.minimum(i, n_prompt_tiles - 1)

    def sample_block(i):
        return jnp.maximum(i - n_prompt_tiles, 0)

    def kind(i):
        return jnp.where(i >= n_prompt_tiles, 1, 0)

    in_specs = [
        pl.BlockSpec((tm, d), lambda i: (prompt_block(i), 0)),
        pl.BlockSpec((tm, d), lambda i: (sample_block(i), 0)),
        pl.BlockSpec((cache_rows, D_KV), lambda i: (sample_block(i), 0)),
        pl.BlockSpec((cache_rows, D_KV), lambda i: (sample_block(i), 0)),
        _const_spec((1, d)),
        _const_spec((d, d_in)),
        _const_spec(bias_tbl.shape),
        _const_spec(sink_row.shape),
        _const_spec((1, d_gm)),
        pl.BlockSpec((1,) + wsp.shape[1:], lambda i: (kind(i), 0, 0, 0)),
        pl.BlockSpec((1, GM_CHUNK, d_gm), lambda i: (kind(i), 0, 0)),
        _const_spec(w_out.shape),
    ]
    out_specs = [
        pl.BlockSpec((tm, d), lambda i: (i, 0)),
        pl.BlockSpec((tm, D_KV), lambda i: (i, 0)),
        pl.BlockSpec((tm, D_KV), lambda i: (i, 0)),
        pl.BlockSpec((tm, d_gm), lambda i: (sample_block(i), 0)),
    ]
    out_shape = [
        jax.ShapeDtypeStruct((m_rows, d), F32),
        jax.ShapeDtypeStruct((m_rows, D_KV), F32),
        jax.ShapeDtypeStruct((m_rows, D_KV), F32),
        jax.ShapeDtypeStruct((n_sample_rows, d_gm), F32),
    ]
    scratch = [
        pltpu.VMEM((tm, d_in), F32),
        pltpu.VMEM((tm, D_ATTN), BF16),
        pltpu.VMEM((tm, D_ATTN + d_gm), BF16),
        pltpu.VMEM((N_KV_HEADS, GM_CHUNK, LANES), BF16),
        pltpu.VMEM((N_KV_HEADS, GM_CHUNK, LANES), BF16),
    ]
    return pl.pallas_call(
        functools.partial(_mix0_kernel, n_prompt_tiles, units),
        grid=(n_tiles,),
        in_specs=in_specs,
        out_specs=out_specs,
        out_shape=out_shape,
        scratch_shapes=scratch,
        compiler_params=pltpu.CompilerParams(
            dimension_semantics=("arbitrary",), vmem_limit_bytes=VMEM_LIMIT_BYTES),
        name="mix0",
    )(xp, xs, cache_k, cache_v, g, w_in, bias_tbl, sink_row, gm_norm, wsp, gmb, w_out)


def _ffn_kernel(final, x_ref, g_ref, wg_ref, wu_ref, wd_ref, gnext_ref, out_ref, *rest):
    h_ref = rest[-1]
    j = pl.program_id(1)
    d = x_ref.shape[1]
    blocked = len(out_ref.shape) == 3

    def out_cols(c0, width):
        if not blocked:
            return [(out_ref, (slice(None), slice(c0, c0 + width)), slice(0, width))]
        return [(out_ref, (b, slice(None), slice(None)), slice((b - c0 // LANES) * LANES, (b - c0 // LANES + 1) * LANES))
                for b in range(c0 // LANES, (c0 + width) // LANES)]

    @pl.when(j == 0)
    def _():
        x = x_ref[...]
        h_ref[...] = (_rms_scale(x) * g_ref[...]).astype(BF16)
        for ref, idx, src in out_cols(0, d):
            ref[idx] = x[:, src]

    h = h_ref[...]
    gate = jnp.dot(h, wg_ref[0].astype(BF16), preferred_element_type=F32)
    up = jnp.dot(h, wu_ref[0].astype(BF16), preferred_element_type=F32)
    act = (gate * jax.nn.sigmoid(gate) * up).astype(BF16)
    wd = wd_ref[0].astype(BF16)
    for c0 in range(0, d, MXU_TILE):
        part = jnp.dot(act, wd[:, c0:c0 + MXU_TILE], preferred_element_type=F32)
        for ref, idx, src in out_cols(c0, MXU_TILE):
            ref[idx] += part[:, src]

    @pl.when(j == pl.num_programs(1) - 1)
    def _():
        if final:
            out = out_ref[...]
            scale = lax.rsqrt(jnp.mean(out * out, axis=-1, keepdims=True) + RMS_EPS)
            out_ref[...] = out * scale * gnext_ref[...]
        else:
            sq = None
            for ref, idx, _ in out_cols(0, d):
                v = ref[idx]
                sq = v * v if sq is None else sq + v * v
            scale = lax.rsqrt(jnp.sum(sq, axis=-1, keepdims=True) * (1.0 / d) + RMS_EPS)
            rest[0][...] = jnp.broadcast_to(scale, rest[0].shape)


def _ffn(x, g, wg, wu, wd, gnext, *, layer, tm, tf, final, row0=0, n_rows=None):
    d = x.shape[1]
    n_rows = x.shape[0] if n_rows is None else n_rows
    f = wg.shape[2]
    assert n_rows % tm == 0 and row0 % tm == 0 and f % tf == 0
    tile0 = row0 // tm
    if final:
        out_specs = [pl.BlockSpec((tm, d), lambda i, j: (i, 0))]
        out_shape = [jax.ShapeDtypeStruct((n_rows, d), F32)]
    else:
        out_specs = [pl.BlockSpec((d // LANES, tm, LANES), lambda i, j: (0, i, 0)),
                     pl.BlockSpec((tm, LANES), lambda i, j: (i, 0))]
        out_shape = [jax.ShapeDtypeStruct((d // LANES, n_rows, LANES), F32),
                     jax.ShapeDtypeStruct((n_rows, LANES), F32)]
    return pl.pallas_call(
        functools.partial(_ffn_kernel, final),
        grid=(n_rows // tm, f // tf),
        in_specs=[
            pl.BlockSpec((tm, d), lambda i, j: (i + tile0, 0)),
            pl.BlockSpec((1, d), lambda i, j: (0, 0)),
            pl.BlockSpec((1, d, tf), lambda i, j: (layer, 0, j)),
            pl.BlockSpec((1, d, tf), lambda i, j: (layer, 0, j)),
            pl.BlockSpec((1, tf, d), lambda i, j: (layer, j, 0)),
            pl.BlockSpec((1, d), lambda i, j: (0, 0)),
        ],
        out_specs=out_specs,
        out_shape=out_shape,
        scratch_shapes=[pltpu.VMEM((tm, d), BF16)],
        compiler_params=pltpu.CompilerParams(
            dimension_semantics=("arbitrary", "arbitrary"), vmem_limit_bytes=VMEM_LIMIT_BYTES),
        name="ffn",
    )(x, g, wg, wu, wd, gnext)


TAB_ROWS = 18


def _discretize(lr, li, log_dt):
    dt = jnp.exp(log_dt)
    mag = jnp.exp(lr * dt)
    ar = mag * jnp.cos(li * dt)
    ai = mag * jnp.sin(li * dt)
    den = lr * lr + li * li
    nr = ar - 1.0
    return ar, ai, (nr * lr + ai * li) / den, (ai * lr - nr * li) / den


def _cmul(pr, pi, qr, qi):
    return pr * qr - pi * qi, pr * qi + pi * qr


def _s5_prep_kernel(seg_rows, lam_ref, lamc_ref, b_ref, c_ref, dall_ref, wec_ref, wsc_ref, tab_ref,
                    bhi_ref, blo_ref):
    ns = S5_BLOCK_STATE

    ar, ai, fr, fi = _discretize(lamc_ref[0, 0], lamc_ref[0, 1], lamc_ref[0, 2])
    bbr, bbi = _cmul(fr, fi, b_ref[0, 0], b_ref[0, 1])
    c_re = c_ref[0, 0]
    c_im = c_ref[0, 1]

    def split(a):
        hi = a.astype(BF16)
        return hi, (a - hi.astype(F32)).astype(BF16)

    power = (jnp.ones_like(ar), jnp.zeros_like(ar))
    for l in range(S5_L):
        bkr, bki = _cmul(*power, bbr, bbi)
        power = _cmul(*power, ar, ai)
        rows = slice(l * LANES, (l + 1) * LANES)
        bk = jnp.concatenate([bkr, bki], axis=1)
        bhi_ref[rows, :], blo_ref[rows, :] = split(bk)
        wec_ref[0, (S5_L - 1 - l) * LANES:(S5_L - l) * LANES, :] = bk
        qr, qi = power
        wsc_ref[0, rows, :] = jnp.concatenate([c_re * qr - c_im * qi, -c_re * qi - c_im * qr], axis=1)

    c_hi, c_lo = split(jnp.concatenate([c_re, -c_im], axis=1))
    c_parts = jnp.concatenate([c_hi, c_lo], axis=0)
    dims = (((1,), (1,)), ((), ()))
    d_hi = lax.dot_general(bhi_ref[...], c_parts, dims, preferred_element_type=F32)
    d_lo = lax.dot_general(blo_ref[...], c_parts, dims, preferred_element_type=F32)
    d_all = d_hi[:, 0:LANES] + d_hi[:, LANES:2 * LANES] + d_lo[:, 0:LANES]
    shape = (S5_L * LANES, LANES)
    in_group = (lax.broadcasted_iota(jnp.int32, shape, 0) % LANES) // S5_GROUP
    out_group = lax.broadcasted_iota(jnp.int32, shape, 1) // S5_GROUP
    dall_ref[0] = jnp.where(in_group == out_group, d_all, 0.0).astype(BF16)

    ar, ai, _, _ = _discretize(lam_ref[0, 0], lam_ref[0, 1], lam_ref[0, 2])
    row = lax.broadcasted_iota(jnp.int32, (SUBLANES, ns), 0)

    def bcast(a):
        return jnp.broadcast_to(a, (SUBLANES, ns))

    def log_step_tables(t0, base):
        cur = base
        for n, shift in enumerate((1, 2, 4)):
            tab_ref[0, t0 + 2 * n] = jnp.where(row >= shift, bcast(cur[0]), 0.0)
            tab_ref[0, t0 + 2 * n + 1] = jnp.where(row >= shift, bcast(cur[1]), 0.0)
            cur = _cmul(*cur, *cur)
        return cur

    a1 = (ar, ai)
    for _ in range(S5_L.bit_length() - 1):
        a1 = _cmul(*a1, *a1)
    a8 = log_step_tables(0, a1)
    pr_tab = jnp.zeros((SUBLANES, ns), F32)
    pi_tab = jnp.zeros((SUBLANES, ns), F32)
    cur = a1
    for r in range(SUBLANES):
        pr_tab = jnp.where(row == r, bcast(cur[0]), pr_tab)
        pi_tab = jnp.where(row == r, bcast(cur[1]), pi_tab)
        cur = _cmul(*cur, *a1)
    tab_ref[0, 6] = pr_tab
    tab_ref[0, 7] = pi_tab
    tab_ref[0, 8] = bcast(a8[0])
    tab_ref[0, 9] = bcast(a8[1])
    tab_ref[0, 10] = bcast(a1[0])
    tab_ref[0, 11] = bcast(a1[1])
    seg = a1
    for _ in range(seg_rows.bit_length() - 1):
        seg = _cmul(*seg, *seg)
    log_step_tables(12, seg)


def _s5_prep(lam, lam_c, b_c, c_c, *, seg_rows):
    assert seg_rows & (seg_rows - 1) == 0 and S5_L & (S5_L - 1) == 0
    nb = lam.shape[0]
    ns = S5_BLOCK_STATE
    k = S5_L * LANES
    cspec = pl.BlockSpec((1, k, LANES), lambda j: (j, 0, 0))

    def whole(a):
        return pl.BlockSpec((1,) + a.shape[1:], lambda j: (j,) + (0,) * (a.ndim - 1))

    return pl.pallas_call(
        functools.partial(_s5_prep_kernel, seg_rows),
        grid=(nb,),
        in_specs=[whole(lam), whole(lam_c), whole(b_c), whole(c_c)],
        out_specs=[cspec, cspec, cspec,
                   pl.BlockSpec((1, TAB_ROWS, SUBLANES, ns), lambda j: (j, 0, 0, 0))],
        out_shape=[
            jax.ShapeDtypeStruct((nb, k, LANES), BF16),
            jax.ShapeDtypeStruct((nb, k, LANES), F32),
            jax.ShapeDtypeStruct((nb, k, LANES), F32),
            jax.ShapeDtypeStruct((nb, TAB_ROWS, SUBLANES, ns), F32),
        ],
        scratch_shapes=[pltpu.VMEM((k, LANES), BF16), pltpu.VMEM((k, LANES), BF16)],
        compiler_params=pltpu.CompilerParams(
            dimension_semantics=("arbitrary",), vmem_limit_bytes=VMEM_LIMIT_BYTES),
        name="s5_prep",
    )(lam, lam_c, b_c, c_c)


def _s5_core_kernel(n_prompt_rows, n_streams,
                    x_ref, rstd_ref, g_ref, dall_ref, wec_ref, wsc_ref, tab_ref, s0r_ref, s0i_ref,
                    d_ref, wglu_ref,
                    y_ref, sfr_ref, sfi_ref, wglu_bf16_ref,
                    u2f_ref, u2_ref, e_ref, wt_ref, we_ref, wst_ref):
    wglu_bf16_ref[...] = wglu_ref[...].astype(BF16)
    ns = S5_BLOCK_STATE
    n_rows = x_ref.shape[1] // S5_L
    n_sample_rows = n_rows - n_prompt_rows
    seg_rows = n_prompt_rows // SUBLANES
    n_prompt_frames = n_prompt_rows * S5_L

    shape = (S5_L * LANES, LANES)
    lane = lax.broadcasted_iota(jnp.int32, shape, 1)
    lane_lo = lane < S5_STATE
    row_group = (lax.broadcasted_iota(jnp.int32, shape, 0) % LANES) // S5_GROUP
    groups_per_vreg = LANES // S5_STATE

    def expand(compact_ref, full_ref):
        x = compact_ref[0]
        r = pltpu.roll(x, S5_STATE, axis=1)
        for part, rep in enumerate((jnp.where(lane_lo, x, r), jnp.where(lane_lo, r, x))):
            for v in range(S5_LANE_GROUPS // groups_per_vreg):
                own = row_group == groups_per_vreg * v + (lane // S5_STATE)
                col = part * S5_BLOCK_STATE + v * LANES
                full_ref[:, col:col + LANES] = jnp.where(own, rep, 0.0).astype(BF16)

    expand(wec_ref, we_ref)

    def frame_slices():
        for l in range(S5_L):
            for seg in range(SUBLANES):
                yield l, pl.ds(seg * seg_rows * S5_L + l, seg_rows, stride=S5_L), \
                    pl.ds(seg, seg_rows, stride=SUBLANES)
            yield l, pl.ds(n_prompt_frames + l, n_sample_rows, stride=S5_L), \
                pl.ds(n_prompt_rows, n_sample_rows)

    for l, frames, rows in frame_slices():
        u2f_ref[l, rows, :] = x_ref[0, frames, :] * rstd_ref[frames, :] * g_ref[...]
    for l in range(S5_L):
        u2_ref[:, l * LANES:(l + 1) * LANES] = u2f_ref[l].astype(BF16)
    e_ref[...] = jnp.dot(u2_ref[...], we_ref[...], preferred_element_type=F32)

    zero_blk = jnp.zeros((LANES, LANES), BF16)
    for l in range(S5_L):
        for l2 in range(S5_L):
            blk = dall_ref[0, (l2 - l) * LANES:(l2 - l + 1) * LANES, :] if l2 >= l else zero_blk
            wt_ref[l * LANES:(l + 1) * LANES, l2 * LANES:(l2 + 1) * LANES] = blk
    slabs_per_block = S5_WT_BLOCK // LANES
    n_col_blocks = S5_L * LANES // S5_WT_BLOCK
    for cb in range(n_col_blocks):
        k_end = (cb + 1) * S5_WT_BLOCK
        y2 = jnp.dot(u2_ref[:, 0:k_end], wt_ref[0:k_end, cb * S5_WT_BLOCK:k_end],
                     preferred_element_type=F32)
        for n in range(slabs_per_block):
            l = cb * slabs_per_block + n
            u2f_ref[l] = y2[:, n * LANES:(n + 1) * LANES] + d_ref[...] * u2f_ref[l]
    expand(wsc_ref, wst_ref)

    tabs = [tab_ref[0, t] for t in range(TAB_ROWS)]
    m_tabs, (pr, pi, a8r, a8i, ar, ai), b_tabs = tabs[0:6], tabs[6:12], tabs[12:18]
    row = lax.broadcasted_iota(jnp.int32, (SUBLANES, ns), 0)

    def log_step_scan(xr, xi, t):
        for n, shift in enumerate((1, 2, 4)):
            tr, ti = t[2 * n], t[2 * n + 1]
            sr = pltpu.roll(xr, shift, axis=0)
            si = pltpu.roll(xi, shift, axis=0)
            xr, xi = xr + tr * sr - ti * si, xi + tr * si + ti * sr
        return xr, xi

    def shift_down(xr, xi, fr, fi):
        first = row == 0
        return (jnp.where(first, fr, pltpu.roll(xr, 1, axis=0)),
                jnp.where(first, fi, pltpu.roll(xi, 1, axis=0)))

    def rows_of(k):
        return pl.ds(pl.multiple_of(k * SUBLANES, SUBLANES), SUBLANES)

    def local_step(k, carry):
        sr, si = carry
        er = e_ref[rows_of(k), 0:ns]
        ei = e_ref[rows_of(k), ns:2 * ns]
        e_ref[rows_of(k), 0:ns] = sr
        e_ref[rows_of(k), ns:2 * ns] = si
        return ar * sr - ai * si + er, ar * si + ai * sr + ei

    zero = jnp.zeros((SUBLANES, ns), F32)
    ends = lax.fori_loop(0, seg_rows, local_step, (zero, zero), unroll=True)
    ends = log_step_scan(*ends, b_tabs)
    sfr_ref[...] = jnp.zeros_like(sfr_ref)
    sfi_ref[...] = jnp.zeros_like(sfi_ref)
    sfr_ref[n_streams:n_streams + 1, :] = ends[0][SUBLANES - 1:SUBLANES]
    sfi_ref[n_streams:n_streams + 1, :] = ends[1][SUBLANES - 1:SUBLANES]

    def correct_step(k, carry):
        cr, ci = carry
        e_ref[rows_of(k), 0:ns] += cr
        e_ref[rows_of(k), ns:2 * ns] += ci
        return ar * cr - ai * ci, ar * ci + ai * cr

    lax.fori_loop(0, seg_rows, correct_step, shift_down(*ends, zero, zero), unroll=True)

    for b in range(n_streams):
        rows = pl.ds(n_prompt_rows + b * SUBLANES, SUBLANES)
        cr = jnp.broadcast_to(s0r_ref[b:b + 1, :], (SUBLANES, ns))
        ci = jnp.broadcast_to(s0i_ref[b:b + 1, :], (SUBLANES, ns))
        xr, xi = log_step_scan(e_ref[rows, 0:ns], e_ref[rows, ns:2 * ns], m_tabs)
        st_r = xr + pr * cr - pi * ci
        st_i = xi + pr * ci + pi * cr
        e_ref[rows, 0:ns], e_ref[rows, ns:2 * ns] = shift_down(st_r, st_i, cr, ci)
        sfr_ref[b:b + 1, :] = st_r[SUBLANES - 1:SUBLANES]
        sfi_ref[b:b + 1, :] = st_i[SUBLANES - 1:SUBLANES]

    sprev = e_ref[...].astype(BF16)
    for cb in range(n_col_blocks):
        cols = slice(cb * S5_WT_BLOCK, (cb + 1) * S5_WT_BLOCK)
        y2 = lax.dot_general(sprev, wst_ref[cols, :], (((1,), (1,)), ((), ())),
                             preferred_element_type=F32)
        for n in range(slabs_per_block):
            l = cb * slabs_per_block + n
            u2f_ref[l] += y2[:, n * LANES:(n + 1) * LANES]
    for l, frames, rows in frame_slices():
        y_ref[0, frames, :] = u2f_ref[l, rows, :]


def _s5_core(x, rstd, g, dall, wec, wsc, tab, s0_re, s0_im, d_skip, w_glu, *, n_prompt_rows):
    nb, m_rows, _ = x.shape
    d = nb * LANES
    glu_cols = w_glu.shape[1] // nb
    assert glu_cols % LANES == 0
    glu_blk = pl.BlockSpec((w_glu.shape[0], glu_cols), lambda j: (0, j))
    ns = S5_BLOCK_STATE
    k = S5_L * LANES
    n_rows = m_rows // S5_L
    n_streams = s0_re.shape[0]
    assert n_prompt_rows % (S5_L * SUBLANES) == 0
    assert (m_rows - n_prompt_rows) == n_streams * S5_L * SUBLANES
    sf_rows = n_streams + SUBLANES
    lane_blk = pl.BlockSpec((1, m_rows, LANES), lambda j: (j, 0, 0))
    cspec = pl.BlockSpec((1, k, LANES), lambda j: (j, 0, 0))
    st = pl.BlockSpec((n_streams, ns), lambda j: (0, j))
    sf = pl.BlockSpec((sf_rows, ns), lambda j: (0, j))
    return pl.pallas_call(
        functools.partial(_s5_core_kernel, n_prompt_rows // S5_L, n_streams),
        grid=(nb,),
        in_specs=[
            lane_blk,
            _const_spec((m_rows, LANES)),
            pl.BlockSpec((1, LANES), lambda j: (0, j)),
            cspec, cspec, cspec,
            pl.BlockSpec((1, TAB_ROWS, SUBLANES, ns), lambda j: (j, 0, 0, 0)),
            st, st,
            pl.BlockSpec((1, LANES), lambda j: (0, j)),
            glu_blk,
        ],
        out_specs=[lane_blk, sf, sf, glu_blk],
        out_shape=[
            jax.ShapeDtypeStruct((nb, m_rows, LANES), F32),
            jax.ShapeDtypeStruct((sf_rows, nb * ns), F32),
            jax.ShapeDtypeStruct((sf_rows, nb * ns), F32),
            jax.ShapeDtypeStruct(w_glu.shape, BF16),
        ],
        scratch_shapes=[
            pltpu.VMEM((S5_L, n_rows, LANES), F32),
            pltpu.VMEM((n_rows, k), BF16),
            pltpu.VMEM((n_rows, 2 * ns), F32),
            pltpu.VMEM((k, k), BF16),
            pltpu.VMEM((k, 2 * ns), BF16),
            pltpu.VMEM((k, 2 * ns), BF16),
        ],
        compiler_params=pltpu.CompilerParams(
            dimension_semantics=("arbitrary",), vmem_limit_bytes=VMEM_LIMIT_BYTES),
        name="s5_core",
    )(x, rstd, g, dall, wec, wsc, tab, s0_re, s0_im, d_skip, w_glu)


def _glu_kernel(n_col_blocks, x_ref, y_ref, w_ref, out_ref):
    nb = x_ref.shape[0]
    d = nb * LANES
    a = jnp.concatenate([_gelu(y_ref[b]).astype(BF16) for b in range(nb)], axis=1)
    cb = d // n_col_blocks
    for n in range(n_col_blocks):
        cols = slice(n * cb, (n + 1) * cb)
        val = jnp.dot(a, w_ref[:, n * cb:(n + 1) * cb], preferred_element_type=F32)
        gate = jnp.dot(a, w_ref[:, d + n * cb:d + (n + 1) * cb], preferred_element_type=F32)
        x = jnp.concatenate([x_ref[b] for b in range(n * cb // LANES, (n + 1) * cb // LANES)], axis=1)
        out_ref[:, cols] = x + val * jax.nn.sigmoid(gate)


def _glu(x, y, w, *, tm, n_col_blocks=GLU_COL_BLOCKS):
    nb, m_rows, _ = x.shape
    d = nb * LANES
    assert m_rows % tm == 0 and d % n_col_blocks == 0 and (d // n_col_blocks) % LANES == 0
    blocked = pl.BlockSpec((nb, tm, LANES), lambda i: (0, i, 0))
    return pl.pallas_call(
        functools.partial(_glu_kernel, n_col_blocks),
        grid=(m_rows // tm,),
        in_specs=[blocked, blocked, _const_spec(w.shape)],
        out_specs=pl.BlockSpec((tm, d), lambda i: (i, 0)),
        out_shape=jax.ShapeDtypeStruct((m_rows, d), F32),
        compiler_params=pltpu.CompilerParams(
            dimension_semantics=("arbitrary",), vmem_limit_bytes=VMEM_LIMIT_BYTES),
        name="glu",
    )(x, y, w)


def _attention_tables(sinks):
    slopes = jnp.exp2(-8.0 * jnp.arange(1, N_HEADS + 1, dtype=F32) / N_HEADS)
    frame = jnp.arange(CHUNK, dtype=F32)[None, :]
    band = jnp.arange(BAND, dtype=F32)[:, None]
    dist = jnp.abs(frame - (band - WINDOW))
    bias = (dist[:, None, :] * slopes[None, :, None]).reshape(BAND, N_HEADS * CHUNK)
    sink_row = jnp.repeat(sinks.astype(F32), CHUNK).reshape(1, N_HEADS * CHUNK)
    return bias, sink_row


def _gm_tables(gm_ws, gm_b, d_gm):
    blk = jnp.arange(GM_CHUNK) // CHUNK
    w_prompt = jnp.where((blk[:, None] >= blk[None, :])[None], gm_ws, 0.0)
    top = gm_ws[:, :CHUNK, :CHUNK]
    zeros = jnp.zeros_like(top)
    w_sample = jnp.concatenate(
        [jnp.concatenate([top, zeros], axis=2), jnp.concatenate([zeros, top], axis=2)], axis=1)
    wsp = jnp.stack([w_prompt, w_sample])
    wsp = wsp.reshape(2, N_GM_GROUPS // 2, 2, GM_CHUNK, GM_CHUNK).transpose(0, 1, 3, 2, 4)
    wsp = wsp.reshape(2, N_GM_GROUPS // 2, GM_CHUNK, 2 * GM_CHUNK).astype(BF16)
    b_prompt = gm_b.T
    b_sample = jnp.concatenate([gm_b[:, :CHUNK].T, gm_b[:, :CHUNK].T], axis=0)
    gmb = jnp.stack([b_prompt, b_sample]).astype(F32)
    gmb = jnp.repeat(gmb, d_gm // N_GM_GROUPS, axis=2)
    return wsp, gmb


def _s5_compact(lam_re, lam_im, log_dt, b_re, b_im, c_re, c_im):
    n_groups = lam_re.shape[0]
    nb = n_groups // S5_LANE_GROUPS
    per_state = jnp.stack([lam_re, lam_im, jnp.broadcast_to(log_dt[:, None], lam_re.shape)]).astype(F32)
    per_state = per_state.reshape(3, nb, S5_LANE_GROUPS, S5_STATE).transpose(1, 0, 2, 3)
    lam = per_state.reshape(nb, 3, 1, S5_BLOCK_STATE)
    lam_c = jnp.broadcast_to(per_state[:, :, :, None, :],
                             (nb, 3, S5_LANE_GROUPS, S5_GROUP, S5_STATE)).reshape(nb, 3, LANES, S5_STATE)
    b_c = jnp.stack([b_re, b_im]).astype(F32).transpose(1, 0, 3, 2)
    b_c = b_c.reshape(nb, S5_LANE_GROUPS, 2, S5_GROUP, S5_STATE).transpose(0, 2, 1, 3, 4)
    c_c = jnp.stack([c_re, c_im]).astype(F32).transpose(1, 0, 2, 3)
    c_c = c_c.reshape(nb, S5_LANE_GROUPS, 2, S5_GROUP, S5_STATE).transpose(0, 2, 1, 3, 4)
    return (lam, lam_c, b_c.reshape(nb, 2, LANES, S5_STATE), c_c.reshape(nb, 2, LANES, S5_STATE))


def kernel(x_prompt, x_sample, cache_swa_k, cache_swa_v, state_s5_re, state_s5_im, norm_mix, norm_ffn, norm_final, w_in0, attn_sinks, gm_norm, gm_ws, gm_b, w_out0, s5_lam_re, s5_lam_im, s5_log_dt, s5_b_re, s5_b_im, s5_c_re, s5_c_im, s5_d, s5_w_glu, ffn_w_gate, ffn_w_up, ffn_w_down):
    batch, seq, d = x_prompt.shape
    dec_batch, dec_seq, _ = x_sample.shape
    assert batch == 1 and dec_seq == CHUNK and norm_mix.shape[0] == 2
    n_prompt = batch * seq
    n_sample = dec_batch * dec_seq
    d_gm = gm_norm.shape[-1]

    bias_tbl, sink_row = _attention_tables(attn_sinks[0])
    wsp, gmb = _gm_tables(gm_ws[0], gm_b[0], d_gm)
    x1, k_all, v_all, gvn = _mix0(
        x_prompt.reshape(n_prompt, d), x_sample.reshape(n_sample, d),
        cache_swa_k[0].reshape(dec_batch * WINDOW, D_KV), cache_swa_v[0].reshape(dec_batch * WINDOW, D_KV),
        norm_mix[0].reshape(1, d), w_in0[0].astype(BF16), bias_tbl, sink_row,
        gm_norm[0].reshape(1, d_gm), wsp, gmb, w_out0[0].astype(BF16), units=MIX_UNITS)

    x2, rstd2 = _ffn(x1, norm_ffn[0].reshape(1, d), ffn_w_gate, ffn_w_up, ffn_w_down,
                     norm_mix[1].reshape(1, d), layer=0, tm=FFN_TM, tf=FFN_TF, final=False)

    n_groups = s5_lam_re.shape[1]
    dall, wec, wsc, tab = _s5_prep(
        *_s5_compact(s5_lam_re[0], s5_lam_im[0], s5_log_dt[0], s5_b_re[0], s5_b_im[0], s5_c_re[0],
                     s5_c_im[0]),
        seg_rows=n_prompt // (S5_L * SUBLANES))
    ys5, sf_re, sf_im, w_glu_bf16 = _s5_core(
        x2, rstd2, norm_mix[1].reshape(1, d), dall, wec, wsc, tab,
        state_s5_re[0].reshape(dec_batch, n_groups * S5_STATE),
        state_s5_im[0].reshape(dec_batch, n_groups * S5_STATE), s5_d[0].reshape(1, d), s5_w_glu[0],
        n_prompt_rows=n_prompt)
    x3 = _glu(x2, ys5, w_glu_bf16, tm=GLU_TM)

    def last_ffn(row0, n_rows):
        (y,) = _ffn(x3, norm_ffn[1].reshape(1, d), ffn_w_gate, ffn_w_up, ffn_w_down,
                    norm_final.reshape(1, d), layer=1, tm=FFN_TM, tf=FFN_TF, final=True,
                    row0=row0, n_rows=n_rows)
        return y

    y_prompt = last_ffn(0, n_prompt)
    y_sample = last_ffn(n_prompt, n_sample)

    keep = min(WINDOW, seq)
    y_prompt = y_prompt.reshape(batch, seq, d)
    y_sample = y_sample.reshape(dec_batch, dec_seq, d)
    kv_shape_p = (1, batch, keep, N_KV_HEADS, HEAD_DIM)
    kv_shape_s = (1, dec_batch, dec_seq, N_KV_HEADS, HEAD_DIM)
    st_p = (1, batch, n_groups, S5_STATE)
    st_s = (1, dec_batch, n_groups, S5_STATE)
    return (y_prompt, y_sample,
            k_all[n_prompt - keep:n_prompt].reshape(kv_shape_p),
            v_all[n_prompt - keep:n_prompt].reshape(kv_shape_p),
            k_all[n_prompt:].reshape(kv_shape_s),
            v_all[n_prompt:].reshape(kv_shape_s),
            gvn.reshape(1, dec_batch, dec_seq, d_gm),
            sf_re[dec_batch].reshape(st_p), sf_im[dec_batch].reshape(st_p),
            sf_re[:dec_batch].reshape(st_s), sf_im[:dec_batch].reshape(st_s))
```

```python
import functools
import math

import jax
import jax.numpy as jnp
from jax import lax
from jax.experimental import pallas as pl
from jax.experimental.pallas import tpu as pltpu

F32 = jnp.float32
BF16 = jnp.bfloat16

CHUNK = 64
HEAD_DIM = 64
N_HEADS = 16
N_KV_HEADS = 2
Q_PER_KV = N_HEADS // N_KV_HEADS
WINDOW = 128
BAND = WINDOW + CHUNK
D_ATTN = N_HEADS * HEAD_DIM
D_KV = N_KV_HEADS * HEAD_DIM
GM_CHUNK = 128
N_GM_GROUPS = 16
S5_GROUP = 16
S5_STATE = 64
RMS_EPS = 1e-5
NEG_INF = -1e30

LANES = 128
SUBLANES = 8
MXU_TILE = 256
VMEM_LIMIT_BYTES = 56 * 1024 * 1024

MIX_UNITS = 2
FFN_TM = 1024
FFN_TF = 256
GLU_TM = 512
GLU_COL_BLOCKS = 8

S5_L = SUBLANES
S5_LANE_GROUPS = LANES // S5_GROUP
S5_BLOCK_STATE = S5_LANE_GROUPS * S5_STATE
S5_WT_BLOCK = MXU_TILE


def _gelu(x):
    return 0.5 * x * (1.0 + lax.erf(x * math.sqrt(0.5)))


def _rms_scale(x):
    return x * lax.rsqrt(jnp.mean(x * x, axis=-1, keepdims=True) + RMS_EPS)


def _const_spec(shape):
    zeros = (0,) * len(shape)
    return pl.BlockSpec(shape, lambda *_: zeros, pipeline_mode=pl.Buffered(1))


def _mix0_kernel(n_prompt_tiles, units,
                 xp_ref, xs_ref, ck_ref, cv_ref, g_ref, win_ref, bias_ref, sink_ref, gmn_ref,
                 wsp_ref, gmb_ref, wout_ref,
                 x1_ref, k_ref, v_ref, gvn_ref,
                 z_ref, q_ref, ocat_ref, kprev_ref, vprev_ref):
    i = pl.program_id(0)
    is_sample = i >= n_prompt_tiles
    tm = units * GM_CHUNK
    d_gm = gmn_ref.shape[-1]
    off_k = D_ATTN
    off_v = D_ATTN + D_KV
    off_gu = D_ATTN + 2 * D_KV
    off_gv = off_gu + d_gm
    n_q = Q_PER_KV * CHUNK

    @pl.when(i == 0)
    def _():
        kprev_ref[...] = jnp.zeros_like(kprev_ref)
        vprev_ref[...] = jnp.zeros_like(vprev_ref)

    x = jnp.where(is_sample, xs_ref[...], xp_ref[...])
    h = (_rms_scale(x) * g_ref[...]).astype(BF16)
    z_ref[...] = jnp.dot(h, win_ref[...], preferred_element_type=F32)

    q_ref[...] = (z_ref[:, 0:D_ATTN] * (HEAD_DIM ** -0.5)).astype(BF16)
    k = z_ref[:, off_k:off_k + D_KV]
    v = z_ref[:, off_v:off_v + D_KV]
    k_ref[...] = k
    v_ref[...] = v

    def lane_lo(rows):
        return lax.broadcasted_iota(jnp.int32, (rows, LANES), 1) < HEAD_DIM

    def replicate(a):
        r = pltpu.roll(a, HEAD_DIM, axis=1)
        lo = lane_lo(a.shape[0])
        return jnp.where(lo, a, r).astype(BF16), jnp.where(lo, r, a).astype(BF16)

    k_rep = replicate(k)
    v_rep = replicate(v)
    ck_rep = replicate(ck_ref[...])
    cv_rep = replicate(cv_ref[...])

    lo64 = lane_lo(CHUNK)
    band_pos = lax.broadcasted_iota(jnp.int32, (BAND, n_q), 0)
    chunks_per_tile = tm // CHUNK

    def scores(u, c2, kv):
        r0 = u * GM_CHUNK + c2 * CHUNK
        stream = r0 // CHUNK
        chunk_index = i * chunks_per_tile + stream
        valid_from = jnp.where(is_sample, 0, jnp.maximum(WINDOW - CHUNK * chunk_index, 0))
        valid = band_pos >= valid_from

        def band(cur, prev_ref, cached):
            if u == 0:
                prev_unit = prev_ref[kv]
            else:
                prev_unit = cur[(u - 1) * GM_CHUNK:u * GM_CHUNK]
            if c2 == 0:
                prompt_prev = prev_unit
            else:
                prompt_prev = jnp.concatenate(
                    [prev_unit[CHUNK:], cur[u * GM_CHUNK:u * GM_CHUNK + CHUNK]], axis=0)
            sample_prev = cached[stream * WINDOW:(stream + 1) * WINDOW]
            prev = jnp.where(is_sample, sample_prev, prompt_prev)
            return jnp.concatenate([prev, cur[r0:r0 + CHUNK]], axis=0)

        kb = band(k_rep[kv], kprev_ref, ck_rep[kv])
        vb = band(v_rep[kv], vprev_ref, cv_rep[kv])

        pieces = []
        for m in range(Q_PER_KV // 2):
            c0 = kv * Q_PER_KV * HEAD_DIM + m * LANES
            qp = q_ref[r0:r0 + CHUNK, c0:c0 + LANES]
            pieces.append(jnp.where(lo64, qp, jnp.zeros_like(qp)))
            pieces.append(jnp.where(lo64, jnp.zeros_like(qp), qp))
        qs = jnp.concatenate(pieces, axis=0)
        cols = slice(kv * n_q, (kv + 1) * n_q)
        st = lax.dot_general(kb, qs, (((1,), (1,)), ((), ())),
                             preferred_element_type=F32)
        return jnp.where(valid, st - bias_ref[:, cols], NEG_INF), vb

    def attend(u, c2, kv, st, vb):
        r0 = u * GM_CHUNK + c2 * CHUNK
        cols = slice(kv * n_q, (kv + 1) * n_q)
        sink = sink_ref[:, cols]
        mx = jnp.maximum(jnp.max(st, axis=0, keepdims=True), sink)
        p = jnp.exp(st - mx)
        denom = jnp.sum(p, axis=0, keepdims=True) + jnp.exp(sink - mx)
        pn = (p * (1.0 / denom)).astype(BF16)
        o = lax.dot_general(pn, vb, (((0,), (0,)), ((), ())),
                            preferred_element_type=F32)
        for m in range(Q_PER_KV // 2):
            o_pair = jnp.where(lo64, o[(2 * m) * CHUNK:(2 * m + 1) * CHUNK],
                               o[(2 * m + 1) * CHUNK:(2 * m + 2) * CHUNK])
            c0 = kv * Q_PER_KV * HEAD_DIM + m * LANES
            ocat_ref[r0:r0 + CHUNK, c0:c0 + LANES] = o_pair.astype(BF16)

    blocks = [(u, c2, kv) for u in range(units) for c2 in range(GM_CHUNK // CHUNK)
              for kv in range(N_KV_HEADS)]
    scored = [scores(*blk) for blk in blocks]
    for blk, (st, vb) in zip(blocks, scored):
        attend(*blk, st, vb)

    lo128 = lane_lo(GM_CHUNK)
    for u in range(units):
        rows = slice(u * GM_CHUNK, (u + 1) * GM_CHUNK)
        ua = _gelu(z_ref[rows, off_gu:off_gu + d_gm])
        gvn = _rms_scale(_gelu(z_ref[rows, off_gv:off_gv + d_gm])) * gmn_ref[...]
        gvn_ref[rows, :] = gvn
        gb = gvn.astype(BF16)
        for m in range(N_GM_GROUPS // 2):
            cols = slice(m * LANES, (m + 1) * LANES)
            rhs = gb[:, cols]
            rhs2 = jnp.concatenate([jnp.where(lo128, rhs, jnp.zeros_like(rhs)),
                                    jnp.where(lo128, jnp.zeros_like(rhs), rhs)], axis=0)
            sp = jnp.dot(wsp_ref[0, m], rhs2, preferred_element_type=F32) + gmb_ref[0, :, cols]
            ocat_ref[rows, D_ATTN + m * LANES:D_ATTN + (m + 1) * LANES] = (ua[:, cols] * sp).astype(BF16)

    x1_ref[...] = x + jnp.dot(ocat_ref[...], wout_ref[...], preferred_element_type=F32)

    for kv in range(N_KV_HEADS):
        kprev_ref[kv] = k_rep[kv][tm - GM_CHUNK:]
        vprev_ref[kv] = v_rep[kv][tm - GM_CHUNK:]


def _mix0(xp, xs, cache_k, cache_v, g, w_in, bias_tbl, sink_row, gm_norm, wsp, gmb, w_out, *, units):
    n_prompt_rows, d = xp.shape
    n_sample_rows = xs.shape[0]
    m_rows = n_prompt_rows + n_sample_rows
    tm = units * GM_CHUNK
    assert n_prompt_rows % tm == 0 and n_sample_rows % tm == 0
    n_tiles = m_rows // tm
    n_prompt_tiles = n_prompt_rows // tm
    d_in = w_in.shape[1]
    d_gm = gm_norm.shape[-1]
    cache_rows = (tm // CHUNK) * WINDOW

    def prompt_block(i):
        return jnp.minimum(i, n_prompt_tiles - 1)

    def sample_block(i):
        return jnp.maximum(i - n_prompt_tiles, 0)

    def kind(i):
        return jnp.where(i >= n_prompt_tiles, 1, 0)

    in_specs = [
        pl.BlockSpec((tm, d), lambda i: (prompt_block(i), 0)),
        pl.BlockSpec((tm, d), lambda i: (sample_block(i), 0)),
        pl.BlockSpec((cache_rows, D_KV), lambda i: (sample_block(i), 0)),
        pl.BlockSpec((cache_rows, D_KV), lambda i: (sample_block(i), 0)),
        _const_spec((1, d)),
        _const_spec((d, d_in)),
        _const_spec(bias_tbl.shape),
        _const_spec(sink_row.shape),
        _const_spec((1, d_gm)),
        pl.BlockSpec((1,) + wsp.shape[1:], lambda i: (kind(i), 0, 0, 0)),
        pl.BlockSpec((1, GM_CHUNK, d_gm), lambda i: (kind(i), 0, 0)),
        _const_spec(w_out.shape),
    ]
    out_specs = [
        pl.BlockSpec((tm, d), lambda i: (i, 0)),
        pl.BlockSpec((tm, D_KV), lambda i: (i, 0)),
        pl.BlockSpec((tm, D_KV), lambda i: (i, 0)),
        pl.BlockSpec((tm, d_gm), lambda i: (sample_block(i), 0)),
    ]
    out_shape = [
        jax.ShapeDtypeStruct((m_rows, d), F32),
        jax.ShapeDtypeStruct((m_rows, D_KV), F32),
        jax.ShapeDtypeStruct((m_rows, D_KV), F32),
        jax.ShapeDtypeStruct((n_sample_rows, d_gm), F32),
    ]
    scratch = [
        pltpu.VMEM((tm, d_in), F32),
        pltpu.VMEM((tm, D_ATTN), BF16),
        pltpu.VMEM((tm, D_ATTN + d_gm), BF16),
        pltpu.VMEM((N_KV_HEADS, GM_CHUNK, LANES), BF16),
        pltpu.VMEM((N_KV_HEADS, GM_CHUNK, LANES), BF16),
    ]
    return pl.pallas_call(
        functools.partial(_mix0_kernel, n_prompt_tiles, units),
        grid=(n_tiles,),
        in_specs=in_specs,
        out_specs=out_specs,
        out_shape=out_shape,
        scratch_shapes=scratch,
        compiler_params=pltpu.CompilerParams(
            dimension_semantics=("arbitrary",), vmem_limit_bytes=VMEM_LIMIT_BYTES),
        name="mix0",
    )(xp, xs, cache_k, cache_v, g, w_in, bias_tbl, sink_row, gm_norm, wsp, gmb, w_out)


def _ffn_kernel(final, x_ref, g_ref, wg_ref, wu_ref, wd_ref, gnext_ref, out_ref, *rest):
    h_ref = rest[-1]
    j = pl.program_id(1)

    @pl.when(j == 0)
    def _():
        x = x_ref[...]
        h_ref[...] = (_rms_scale(x) * g_ref[...]).astype(BF16)
        out_ref[...] = x

    h = h_ref[...]
    gate = jnp.dot(h, wg_ref[0].astype(BF16), preferred_element_type=F32)
    up = jnp.dot(h, wu_ref[0].astype(BF16), preferred_element_type=F32)
    act = (gate * jax.nn.sigmoid(gate) * up).astype(BF16)
    out_ref[...] += jnp.dot(act, wd_ref[0].astype(BF16), preferred_element_type=F32)

    @pl.when(j == pl.num_programs(1) - 1)
    def _():
        out = out_ref[...]
        scale = lax.rsqrt(jnp.mean(out * out, axis=-1, keepdims=True) + RMS_EPS)
        if final:
            out_ref[...] = out * scale * gnext_ref[...]
        else:
            rest[0][...] = jnp.broadcast_to(scale, rest[0].shape)


def _ffn(x, g, wg, wu, wd, gnext, *, layer, tm, tf, final, row0=0, n_rows=None):
    d = x.shape[1]
    n_rows = x.shape[0] if n_rows is None else n_rows
    f = wg.shape[2]
    assert n_rows % tm == 0 and row0 % tm == 0 and f % tf == 0
    tile0 = row0 // tm
    out_specs = [pl.BlockSpec((tm, d), lambda i, j: (i, 0))]
    out_shape = [jax.ShapeDtypeStruct((n_rows, d), F32)]
    if not final:
        out_specs.append(pl.BlockSpec((tm, LANES), lambda i, j: (i, 0)))
        out_shape.append(jax.ShapeDtypeStruct((n_rows, LANES), F32))
    return pl.pallas_call(
        functools.partial(_ffn_kernel, final),
        grid=(n_rows // tm, f // tf),
        in_specs=[
            pl.BlockSpec((tm, d), lambda i, j: (i + tile0, 0)),
            pl.BlockSpec((1, d), lambda i, j: (0, 0)),
            pl.BlockSpec((1, d, tf), lambda i, j: (layer, 0, j)),
            pl.BlockSpec((1, d, tf), lambda i, j: (layer, 0, j)),
            pl.BlockSpec((1, tf, d), lambda i, j: (layer, j, 0)),
            pl.BlockSpec((1, d), lambda i, j: (0, 0)),
        ],
        out_specs=out_specs,
        out_shape=out_shape,
        scratch_shapes=[pltpu.VMEM((tm, d), BF16)],
        compiler_params=pltpu.CompilerParams(
            dimension_semantics=("arbitrary", "arbitrary"), vmem_limit_bytes=VMEM_LIMIT_BYTES),
        name="ffn",
    )(x, g, wg, wu, wd, gnext)


TAB_ROWS = 18


def _discretize(lr, li, log_dt):
    dt = jnp.exp(log_dt)
    mag = jnp.exp(lr * dt)
    ar = mag * jnp.cos(li * dt)
    ai = mag * jnp.sin(li * dt)
    den = lr * lr + li * li
    nr = ar - 1.0
    return ar, ai, (nr * lr + ai * li) / den, (ai * lr - nr * li) / den


def _cmul(pr, pi, qr, qi):
    return pr * qr - pi * qi, pr * qi + pi * qr


def _s5_prep_kernel(seg_rows, lam_ref, lamc_ref, b_ref, c_ref, dall_ref, wec_ref, wsc_ref, tab_ref,
                    bhi_ref, blo_ref):
    ns = S5_BLOCK_STATE

    ar, ai, fr, fi = _discretize(lamc_ref[0, 0], lamc_ref[0, 1], lamc_ref[0, 2])
    bbr, bbi = _cmul(fr, fi, b_ref[0, 0], b_ref[0, 1])
    c_re = c_ref[0, 0]
    c_im = c_ref[0, 1]

    def split(a):
        hi = a.astype(BF16)
        return hi, (a - hi.astype(F32)).astype(BF16)

    power = (jnp.ones_like(ar), jnp.zeros_like(ar))
    for l in range(S5_L):
        bkr, bki = _cmul(*power, bbr, bbi)
        power = _cmul(*power, ar, ai)
        rows = slice(l * LANES, (l + 1) * LANES)
        bk = jnp.concatenate([bkr, bki], axis=1)
        bhi_ref[rows, :], blo_ref[rows, :] = split(bk)
        wec_ref[0, (S5_L - 1 - l) * LANES:(S5_L - l) * LANES, :] = bk
        qr, qi = power
        wsc_ref[0, rows, :] = jnp.concatenate([c_re * qr - c_im * qi, -c_re * qi - c_im * qr], axis=1)

    c_hi, c_lo = split(jnp.concatenate([c_re, -c_im], axis=1))
    c_parts = jnp.concatenate([c_hi, c_lo], axis=0)
    dims = (((1,), (1,)), ((), ()))
    d_hi = lax.dot_general(bhi_ref[...], c_parts, dims, preferred_element_type=F32)
    d_lo = lax.dot_general(blo_ref[...], c_parts, dims, preferred_element_type=F32)
    d_all = d_hi[:, 0:LANES] + d_hi[:, LANES:2 * LANES] + d_lo[:, 0:LANES]
    shape = (S5_L * LANES, LANES)
    in_group = (lax.broadcasted_iota(jnp.int32, shape, 0) % LANES) // S5_GROUP
    out_group = lax.broadcasted_iota(jnp.int32, shape, 1) // S5_GROUP
    dall_ref[0] = jnp.where(in_group == out_group, d_all, 0.0).astype(BF16)

    ar, ai, _, _ = _discretize(lam_ref[0, 0], lam_ref[0, 1], lam_ref[0, 2])
    row = lax.broadcasted_iota(jnp.int32, (SUBLANES, ns), 0)

    def bcast(a):
        return jnp.broadcast_to(a, (SUBLANES, ns))

    def log_step_tables(t0, base):
        cur = base
        for n, shift in enumerate((1, 2, 4)):
            tab_ref[0, t0 + 2 * n] = jnp.where(row >= shift, bcast(cur[0]), 0.0)
            tab_ref[0, t0 + 2 * n + 1] = jnp.where(row >= shift, bcast(cur[1]), 0.0)
            cur = _cmul(*cur, *cur)
        return cur

    a1 = (ar, ai)
    for _ in range(S5_L.bit_length() - 1):
        a1 = _cmul(*a1, *a1)
    a8 = log_step_tables(0, a1)
    pr_tab = jnp.zeros((SUBLANES, ns), F32)
    pi_tab = jnp.zeros((SUBLANES, ns), F32)
    cur = a1
    for r in range(SUBLANES):
        pr_tab = jnp.where(row == r, bcast(cur[0]), pr_tab)
        pi_tab = jnp.where(row == r, bcast(cur[1]), pi_tab)
        cur = _cmul(*cur, *a1)
    tab_ref[0, 6] = pr_tab
    tab_ref[0, 7] = pi_tab
    tab_ref[0, 8] = bcast(a8[0])
    tab_ref[0, 9] = bcast(a8[1])
    tab_ref[0, 10] = bcast(a1[0])
    tab_ref[0, 11] = bcast(a1[1])
    seg = a1
    for _ in range(seg_rows.bit_length() - 1):
        seg = _cmul(*seg, *seg)
    log_step_tables(12, seg)


def _s5_prep(lam, lam_c, b_c, c_c, *, seg_rows):
    assert seg_rows & (seg_rows - 1) == 0 and S5_L & (S5_L - 1) == 0
    nb = lam.shape[0]
    ns = S5_BLOCK_STATE
    k = S5_L * LANES
    cspec = pl.BlockSpec((1, k, LANES), lambda j: (j, 0, 0))

    def whole(a):
        return pl.BlockSpec((1,) + a.shape[1:], lambda j: (j,) + (0,) * (a.ndim - 1))

    return pl.pallas_call(
        functools.partial(_s5_prep_kernel, seg_rows),
        grid=(nb,),
        in_specs=[whole(lam), whole(lam_c), whole(b_c), whole(c_c)],
        out_specs=[cspec, cspec, cspec,
                   pl.BlockSpec((1, TAB_ROWS, SUBLANES, ns), lambda j: (j, 0, 0, 0))],
        out_shape=[
            jax.ShapeDtypeStruct((nb, k, LANES), BF16),
            jax.ShapeDtypeStruct((nb, k, LANES), F32),
            jax.ShapeDtypeStruct((nb, k, LANES), F32),
            jax.ShapeDtypeStruct((nb, TAB_ROWS, SUBLANES, ns), F32),
        ],
        scratch_shapes=[pltpu.VMEM((k, LANES), BF16), pltpu.VMEM((k, LANES), BF16)],
        compiler_params=pltpu.CompilerParams(
            dimension_semantics=("arbitrary",), vmem_limit_bytes=VMEM_LIMIT_BYTES),
        name="s5_prep",
    )(lam, lam_c, b_c, c_c)


def _s5_core_kernel(n_prompt_rows, n_streams,
                    x_ref, rstd_ref, g_ref, dall_ref, wec_ref, wsc_ref, tab_ref, s0r_ref, s0i_ref,
                    d_ref, wglu_ref,
                    y_ref, sfr_ref, sfi_ref, wglu_bf16_ref,
                    u2f_ref, u2_ref, e_ref, wt_ref, we_ref, wst_ref):
    wglu_bf16_ref[...] = wglu_ref[...].astype(BF16)
    ns = S5_BLOCK_STATE
    n_rows = x_ref.shape[0] // S5_L
    n_sample_rows = n_rows - n_prompt_rows
    seg_rows = n_prompt_rows // SUBLANES
    n_prompt_frames = n_prompt_rows * S5_L

    shape = (S5_L * LANES, LANES)
    lane = lax.broadcasted_iota(jnp.int32, shape, 1)
    lane_lo = lane < S5_STATE
    row_group = (lax.broadcasted_iota(jnp.int32, shape, 0) % LANES) // S5_GROUP
    groups_per_vreg = LANES // S5_STATE

    def expand(compact_ref, full_ref):
        x = compact_ref[0]
        r = pltpu.roll(x, S5_STATE, axis=1)
        for part, rep in enumerate((jnp.where(lane_lo, x, r), jnp.where(lane_lo, r, x))):
            for v in range(S5_LANE_GROUPS // groups_per_vreg):
                own = row_group == groups_per_vreg * v + (lane // S5_STATE)
                col = part * S5_BLOCK_STATE + v * LANES
                full_ref[:, col:col + LANES] = jnp.where(own, rep, 0.0).astype(BF16)

    expand(wec_ref, we_ref)

    def frame_slices():
        for l in range(S5_L):
            for seg in range(SUBLANES):
                yield l, pl.ds(seg * seg_rows * S5_L + l, seg_rows, stride=S5_L), \
                    pl.ds(seg, seg_rows, stride=SUBLANES), pl.ds(seg * seg_rows, seg_rows)
            yield l, pl.ds(n_prompt_frames + l, n_sample_rows, stride=S5_L), \
                pl.ds(n_prompt_rows, n_sample_rows), pl.ds(n_prompt_rows, n_sample_rows)

    for l, frames, rows, time_rows in frame_slices():
        u2f_ref[l, rows, :] = x_ref[frames, :] * rstd_ref[l, time_rows, :] * g_ref[...]
    for l in range(S5_L):
        u2_ref[:, l * LANES:(l + 1) * LANES] = u2f_ref[l].astype(BF16)
    e_ref[...] = jnp.dot(u2_ref[...], we_ref[...], preferred_element_type=F32)

    zero_blk = jnp.zeros((LANES, LANES), BF16)
    for l in range(S5_L):
        for l2 in range(S5_L):
            blk = dall_ref[0, (l2 - l) * LANES:(l2 - l + 1) * LANES, :] if l2 >= l else zero_blk
            wt_ref[l * LANES:(l + 1) * LANES, l2 * LANES:(l2 + 1) * LANES] = blk
    slabs_per_block = S5_WT_BLOCK // LANES
    n_col_blocks = S5_L * LANES // S5_WT_BLOCK
    for cb in range(n_col_blocks):
        k_end = (cb + 1) * S5_WT_BLOCK
        y2 = jnp.dot(u2_ref[:, 0:k_end], wt_ref[0:k_end, cb * S5_WT_BLOCK:k_end],
                     preferred_element_type=F32)
        for n in range(slabs_per_block):
            l = cb * slabs_per_block + n
            u2f_ref[l] = y2[:, n * LANES:(n + 1) * LANES] + d_ref[...] * u2f_ref[l]
    expand(wsc_ref, wst_ref)

    tabs = [tab_ref[0, t] for t in range(TAB_ROWS)]
    m_tabs, (pr, pi, a8r, a8i, ar, ai), b_tabs = tabs[0:6], tabs[6:12], tabs[12:18]
    row = lax.broadcasted_iota(jnp.int32, (SUBLANES, ns), 0)

    def log_step_scan(xr, xi, t):
        for n, shift in enumerate((1, 2, 4)):
            tr, ti = t[2 * n], t[2 * n + 1]
            sr = pltpu.roll(xr, shift, axis=0)
            si = pltpu.roll(xi, shift, axis=0)
            xr, xi = xr + tr * sr - ti * si, xi + tr * si + ti * sr
        return xr, xi

    def shift_down(xr, xi, fr, fi):
        first = row == 0
        return (jnp.where(first, fr, pltpu.roll(xr, 1, axis=0)),
                jnp.where(first, fi, pltpu.roll(xi, 1, axis=0)))

    def rows_of(k):
        return pl.ds(pl.multiple_of(k * SUBLANES, SUBLANES), SUBLANES)

    def local_step(k, carry):
        sr, si = carry
        er = e_ref[rows_of(k), 0:ns]
        ei = e_ref[rows_of(k), ns:2 * ns]
        e_ref[rows_of(k), 0:ns] = sr
        e_ref[rows_of(k), ns:2 * ns] = si
        return ar * sr - ai * si + er, ar * si + ai * sr + ei

    zero = jnp.zeros((SUBLANES, ns), F32)
    ends = lax.fori_loop(0, seg_rows, local_step, (zero, zero), unroll=True)
    ends = log_step_scan(*ends, b_tabs)
    sfr_ref[...] = jnp.zeros_like(sfr_ref)
    sfi_ref[...] = jnp.zeros_like(sfi_ref)
    sfr_ref[n_streams:n_streams + 1, :] = ends[0][SUBLANES - 1:SUBLANES]
    sfi_ref[n_streams:n_streams + 1, :] = ends[1][SUBLANES - 1:SUBLANES]

    def correct_step(k, carry):
        cr, ci = carry
        e_ref[rows_of(k), 0:ns] += cr
        e_ref[rows_of(k), ns:2 * ns] += ci
        return ar * cr - ai * ci, ar * ci + ai * cr

    lax.fori_loop(0, seg_rows, correct_step, shift_down(*ends, zero, zero), unroll=True)

    for b in range(n_streams):
        rows = pl.ds(n_prompt_rows + b * SUBLANES, SUBLANES)
        cr = jnp.broadcast_to(s0r_ref[b:b + 1, :], (SUBLANES, ns))
        ci = jnp.broadcast_to(s0i_ref[b:b + 1, :], (SUBLANES, ns))
        xr, xi = log_step_scan(e_ref[rows, 0:ns], e_ref[rows, ns:2 * ns], m_tabs)
        st_r = xr + pr * cr - pi * ci
        st_i = xi + pr * ci + pi * cr
        e_ref[rows, 0:ns], e_ref[rows, ns:2 * ns] = shift_down(st_r, st_i, cr, ci)
        sfr_ref[b:b + 1, :] = st_r[SUBLANES - 1:SUBLANES]
        sfi_ref[b:b + 1, :] = st_i[SUBLANES - 1:SUBLANES]

    sprev = e_ref[...].astype(BF16)
    for cb in range(n_col_blocks):
        cols = slice(cb * S5_WT_BLOCK, (cb + 1) * S5_WT_BLOCK)
        y2 = lax.dot_general(sprev, wst_ref[cols, :], (((1,), (1,)), ((), ())),
                             preferred_element_type=F32)
        for n in range(slabs_per_block):
            l = cb * slabs_per_block + n
            u2f_ref[l] += y2[:, n * LANES:(n + 1) * LANES]
    for l, frames, rows, _ in frame_slices():
        y_ref[frames, :] = u2f_ref[l, rows, :]


def _s5_core(x, rstd, g, dall, wec, wsc, tab, s0_re, s0_im, d_skip, w_glu, *, n_prompt_rows):
    m_rows, d = x.shape
    nb = d // LANES
    glu_cols = w_glu.shape[1] // nb
    assert glu_cols % LANES == 0
    glu_blk = pl.BlockSpec((w_glu.shape[0], glu_cols), lambda j: (0, j))
    ns = S5_BLOCK_STATE
    k = S5_L * LANES
    n_rows = m_rows // S5_L
    n_streams = s0_re.shape[0]
    assert n_prompt_rows % (S5_L * SUBLANES) == 0
    assert (m_rows - n_prompt_rows) == n_streams * S5_L * SUBLANES
    sf_rows = n_streams + SUBLANES
    lane_blk = pl.BlockSpec((m_rows, LANES), lambda j: (0, j))
    cspec = pl.BlockSpec((1, k, LANES), lambda j: (j, 0, 0))
    st = pl.BlockSpec((n_streams, ns), lambda j: (0, j))
    sf = pl.BlockSpec((sf_rows, ns), lambda j: (0, j))
    return pl.pallas_call(
        functools.partial(_s5_core_kernel, n_prompt_rows // S5_L, n_streams),
        grid=(nb,),
        in_specs=[
            lane_blk,
            _const_spec((S5_L, n_rows, LANES)),
            pl.BlockSpec((1, LANES), lambda j: (0, j)),
            cspec, cspec, cspec,
            pl.BlockSpec((1, TAB_ROWS, SUBLANES, ns), lambda j: (j, 0, 0, 0)),
            st, st,
            pl.BlockSpec((1, LANES), lambda j: (0, j)),
            glu_blk,
        ],
        out_specs=[lane_blk, sf, sf, glu_blk],
        out_shape=[
            jax.ShapeDtypeStruct((m_rows, d), F32),
            jax.ShapeDtypeStruct((sf_rows, nb * ns), F32),
            jax.ShapeDtypeStruct((sf_rows, nb * ns), F32),
            jax.ShapeDtypeStruct(w_glu.shape, BF16),
        ],
        scratch_shapes=[
            pltpu.VMEM((S5_L, n_rows, LANES), F32),
            pltpu.VMEM((n_rows, k), BF16),
            pltpu.VMEM((n_rows, 2 * ns), F32),
            pltpu.VMEM((k, k), BF16),
            pltpu.VMEM((k, 2 * ns), BF16),
            pltpu.VMEM((k, 2 * ns), BF16),
        ],
        compiler_params=pltpu.CompilerParams(
            dimension_semantics=("arbitrary",), vmem_limit_bytes=VMEM_LIMIT_BYTES),
        name="s5_core",
    )(x, rstd, g, dall, wec, wsc, tab, s0_re, s0_im, d_skip, w_glu)


def _glu_kernel(n_col_blocks, x_ref, y_ref, w_ref, out_ref):
    d = x_ref.shape[1]
    a = _gelu(y_ref[...]).astype(BF16)
    cb = d // n_col_blocks
    for n in range(n_col_blocks):
        cols = slice(n * cb, (n + 1) * cb)
        val = jnp.dot(a, w_ref[:, n * cb:(n + 1) * cb], preferred_element_type=F32)
        gate = jnp.dot(a, w_ref[:, d + n * cb:d + (n + 1) * cb], preferred_element_type=F32)
        out_ref[:, cols] = x_ref[:, cols] + val * jax.nn.sigmoid(gate)


def _glu(x, y, w, *, tm, n_col_blocks=GLU_COL_BLOCKS):
    m_rows, d = x.shape
    assert m_rows % tm == 0 and d % n_col_blocks == 0
    tile = pl.BlockSpec((tm, d), lambda i: (i, 0))
    return pl.pallas_call(
        functools.partial(_glu_kernel, n_col_blocks),
        grid=(m_rows // tm,),
        in_specs=[tile, tile, _const_spec(w.shape)],
        out_specs=tile,
        out_shape=jax.ShapeDtypeStruct((m_rows, d), F32),
        compiler_params=pltpu.CompilerParams(
            dimension_semantics=("arbitrary",), vmem_limit_bytes=VMEM_LIMIT_BYTES),
        name="glu",
    )(x, y, w)


def _attention_tables(sinks):
    slopes = jnp.exp2(-8.0 * jnp.arange(1, N_HEADS + 1, dtype=F32) / N_HEADS)
    frame = jnp.arange(CHUNK, dtype=F32)[None, :]
    band = jnp.arange(BAND, dtype=F32)[:, None]
    dist = jnp.abs(frame - (band - WINDOW))
    bias = (dist[:, None, :] * slopes[None, :, None]).reshape(BAND, N_HEADS * CHUNK)
    sink_row = jnp.repeat(sinks.astype(F32), CHUNK).reshape(1, N_HEADS * CHUNK)
    return bias, sink_row


def _gm_tables(gm_ws, gm_b, d_gm):
    blk = jnp.arange(GM_CHUNK) // CHUNK
    w_prompt = jnp.where((blk[:, None] >= blk[None, :])[None], gm_ws, 0.0)
    top = gm_ws[:, :CHUNK, :CHUNK]
    zeros = jnp.zeros_like(top)
    w_sample = jnp.concatenate(
        [jnp.concatenate([top, zeros], axis=2), jnp.concatenate([zeros, top], axis=2)], axis=1)
    wsp = jnp.stack([w_prompt, w_sample])
    wsp = wsp.reshape(2, N_GM_GROUPS // 2, 2, GM_CHUNK, GM_CHUNK).transpose(0, 1, 3, 2, 4)
    wsp = wsp.reshape(2, N_GM_GROUPS // 2, GM_CHUNK, 2 * GM_CHUNK).astype(BF16)
    b_prompt = gm_b.T
    b_sample = jnp.concatenate([gm_b[:, :CHUNK].T, gm_b[:, :CHUNK].T], axis=0)
    gmb = jnp.stack([b_prompt, b_sample]).astype(F32)
    gmb = jnp.repeat(gmb, d_gm // N_GM_GROUPS, axis=2)
    return wsp, gmb


def _s5_compact(lam_re, lam_im, log_dt, b_re, b_im, c_re, c_im):
    n_groups = lam_re.shape[0]
    nb = n_groups // S5_LANE_GROUPS
    per_state = jnp.stack([lam_re, lam_im, jnp.broadcast_to(log_dt[:, None], lam_re.shape)]).astype(F32)
    per_state = per_state.reshape(3, nb, S5_LANE_GROUPS, S5_STATE).transpose(1, 0, 2, 3)
    lam = per_state.reshape(nb, 3, 1, S5_BLOCK_STATE)
    lam_c = jnp.broadcast_to(per_state[:, :, :, None, :],
                             (nb, 3, S5_LANE_GROUPS, S5_GROUP, S5_STATE)).reshape(nb, 3, LANES, S5_STATE)
    b_c = jnp.stack([b_re, b_im]).astype(F32).transpose(1, 0, 3, 2)
    b_c = b_c.reshape(nb, S5_LANE_GROUPS, 2, S5_GROUP, S5_STATE).transpose(0, 2, 1, 3, 4)
    c_c = jnp.stack([c_re, c_im]).astype(F32).transpose(1, 0, 2, 3)
    c_c = c_c.reshape(nb, S5_LANE_GROUPS, 2, S5_GROUP, S5_STATE).transpose(0, 2, 1, 3, 4)
    return (lam, lam_c, b_c.reshape(nb, 2, LANES, S5_STATE), c_c.reshape(nb, 2, LANES, S5_STATE))


def kernel(x_prompt, x_sample, cache_swa_k, cache_swa_v, state_s5_re, state_s5_im, norm_mix, norm_ffn, norm_final, w_in0, attn_sinks, gm_norm, gm_ws, gm_b, w_out0, s5_lam_re, s5_lam_im, s5_log_dt, s5_b_re, s5_b_im, s5_c_re, s5_c_im, s5_d, s5_w_glu, ffn_w_gate, ffn_w_up, ffn_w_down):
    batch, seq, d = x_prompt.shape
    dec_batch, dec_seq, _ = x_sample.shape
    assert batch == 1 and dec_seq == CHUNK and norm_mix.shape[0] == 2
    n_prompt = batch * seq
    n_sample = dec_batch * dec_seq
    d_gm = gm_norm.shape[-1]

    bias_tbl, sink_row = _attention_tables(attn_sinks[0])
    wsp, gmb = _gm_tables(gm_ws[0], gm_b[0], d_gm)
    x1, k_all, v_all, gvn = _mix0(
        x_prompt.reshape(n_prompt, d), x_sample.reshape(n_sample, d),
        cache_swa_k[0].reshape(dec_batch * WINDOW, D_KV), cache_swa_v[0].reshape(dec_batch * WINDOW, D_KV),
        norm_mix[0].reshape(1, d), w_in0[0].astype(BF16), bias_tbl, sink_row,
        gm_norm[0].reshape(1, d_gm), wsp, gmb, w_out0[0].astype(BF16), units=MIX_UNITS)

    x2, rstd2 = _ffn(x1, norm_ffn[0].reshape(1, d), ffn_w_gate, ffn_w_up, ffn_w_down,
                     norm_mix[1].reshape(1, d), layer=0, tm=FFN_TM, tf=FFN_TF, final=False)

    n_groups = s5_lam_re.shape[1]
    dall, wec, wsc, tab = _s5_prep(
        *_s5_compact(s5_lam_re[0], s5_lam_im[0], s5_log_dt[0], s5_b_re[0], s5_b_im[0], s5_c_re[0],
                     s5_c_im[0]),
        seg_rows=n_prompt // (S5_L * SUBLANES))
    ys5, sf_re, sf_im, w_glu_bf16 = _s5_core(
        x2, rstd2.reshape(-1, S5_L, LANES).transpose(1, 0, 2), norm_mix[1].reshape(1, d), dall, wec, wsc, tab,
        state_s5_re[0].reshape(dec_batch, n_groups * S5_STATE),
        state_s5_im[0].reshape(dec_batch, n_groups * S5_STATE), s5_d[0].reshape(1, d), s5_w_glu[0],
        n_prompt_rows=n_prompt)
    x3 = _glu(x2, ys5, w_glu_bf16, tm=GLU_TM)

    def last_ffn(row0, n_rows):
        (y,) = _ffn(x3, norm_ffn[1].reshape(1, d), ffn_w_gate, ffn_w_up, ffn_w_down,
                    norm_final.reshape(1, d), layer=1, tm=FFN_TM, tf=FFN_TF, final=True,
                    row0=row0, n_rows=n_rows)
        return y

    y_prompt = last_ffn(0, n_prompt)
    y_sample = last_ffn(n_prompt, n_sample)

    keep = min(WINDOW, seq)
    y_prompt = y_prompt.reshape(batch, seq, d)
    y_sample = y_sample.reshape(dec_batch, dec_seq, d)
    kv_shape_p = (1, batch, keep, N_KV_HEADS, HEAD_DIM)
    kv_shape_s = (1, dec_batch, dec_seq, N_KV_HEADS, HEAD_DIM)
    st_p = (1, batch, n_groups, S5_STATE)
    st_s = (1, dec_batch, n_groups, S5_STATE)
    return (y_prompt, y_sample,
            k_all[n_prompt - keep:n_prompt].reshape(kv_shape_p),
            v_all[n_prompt - keep:n_prompt].reshape(kv_shape_p),
            k_all[n_prompt:].reshape(kv_shape_s),
            v_all[n_prompt:].reshape(kv_shape_s),
            gvn.reshape(1, dec_batch, dec_seq, d_gm),
            sf_re[dec_batch].reshape(st_p), sf_im[dec_batch].reshape(st_p),
            sf_re[:dec_batch].reshape(st_s), sf_im[:dec_batch].reshape(st_s))
```

```python
import functools
import math

import jax
import jax.numpy as jnp
from jax import lax
from jax.experimental import pallas as pl
from jax.experimental.pallas import tpu as pltpu

F32 = jnp.float32
BF16 = jnp.bfloat16

CHUNK = 64
HEAD_DIM = 64
N_HEADS = 16
N_KV_HEADS = 2
Q_PER_KV = N_HEADS // N_KV_HEADS
WINDOW = 128
BAND = WINDOW + CHUNK
D_ATTN = N_HEADS * HEAD_DIM
D_KV = N_KV_HEADS * HEAD_DIM
GM_CHUNK = 128
N_GM_GROUPS = 16
S5_GROUP = 16
S5_STATE = 64
RMS_EPS = 1e-5
NEG_INF = -1e30

LANES = 128
SUBLANES = 8
MXU_TILE = 256
VMEM_LIMIT_BYTES = 56 * 1024 * 1024

MIX_UNITS = 2
FFN_TM = 1024
FFN_TF = 256
GLU_TM = 512
GLU_COL_BLOCKS = 8

S5_L = SUBLANES
S5_LANE_GROUPS = LANES // S5_GROUP
S5_BLOCK_STATE = S5_LANE_GROUPS * S5_STATE
S5_WT_BLOCK = MXU_TILE


def _gelu(x):
    return 0.5 * x * (1.0 + lax.erf(x * math.sqrt(0.5)))


def _rms_scale(x):
    return x * lax.rsqrt(jnp.mean(x * x, axis=-1, keepdims=True) + RMS_EPS)


def _const_spec(shape):
    zeros = (0,) * len(shape)
    return pl.BlockSpec(shape, lambda *_: zeros, pipeline_mode=pl.Buffered(1))


def _mix0_kernel(n_prompt_tiles, units,
                 xp_ref, xs_ref, ck_ref, cv_ref, g_ref, win_ref, bias_ref, sink_ref, gmn_ref,
                 wsp_ref, gmb_ref, wout_ref,
                 x1_ref, k_ref, v_ref, gvn_ref,
                 z_ref, q_ref, ocat_ref, kprev_ref, vprev_ref):
    i = pl.program_id(0)
    is_sample = i >= n_prompt_tiles
    tm = units * GM_CHUNK
    d_gm = gmn_ref.shape[-1]
    off_k = D_ATTN
    off_v = D_ATTN + D_KV
    off_gu = D_ATTN + 2 * D_KV
    off_gv = off_gu + d_gm
    n_q = Q_PER_KV * CHUNK

    @pl.when(i == 0)
    def _():
        kprev_ref[...] = jnp.zeros_like(kprev_ref)
        vprev_ref[...] = jnp.zeros_like(vprev_ref)

    x = jnp.where(is_sample, xs_ref[...], xp_ref[...])
    h = (_rms_scale(x) * g_ref[...]).astype(BF16)
    z_ref[...] = jnp.dot(h, win_ref[...], preferred_element_type=F32)

    q_ref[...] = (z_ref[:, 0:D_ATTN] * (HEAD_DIM ** -0.5)).astype(BF16)
    k = z_ref[:, off_k:off_k + D_KV]
    v = z_ref[:, off_v:off_v + D_KV]
    k_ref[...] = k
    v_ref[...] = v

    def lane_lo(rows):
        return lax.broadcasted_iota(jnp.int32, (rows, LANES), 1) < HEAD_DIM

    def replicate(a):
        r = pltpu.roll(a, HEAD_DIM, axis=1)
        lo = lane_lo(a.shape[0])
        return jnp.where(lo, a, r).astype(BF16), jnp.where(lo, r, a).astype(BF16)

    k_rep = replicate(k)
    v_rep = replicate(v)
    ck_rep = replicate(ck_ref[...])
    cv_rep = replicate(cv_ref[...])

    lo64 = lane_lo(CHUNK)
    band_pos = lax.broadcasted_iota(jnp.int32, (BAND, n_q), 0)
    chunks_per_tile = tm // CHUNK

    def scores(u, c2, kv):
        r0 = u * GM_CHUNK + c2 * CHUNK
        stream = r0 // CHUNK
        chunk_index = i * chunks_per_tile + stream
        valid_from = jnp.where(is_sample, 0, jnp.maximum(WINDOW - CHUNK * chunk_index, 0))
        valid = band_pos >= valid_from

        def band(cur, prev_ref, cached):
            if u == 0:
                prev_unit = prev_ref[kv]
            else:
                prev_unit = cur[(u - 1) * GM_CHUNK:u * GM_CHUNK]
            if c2 == 0:
                prompt_prev = prev_unit
            else:
                prompt_prev = jnp.concatenate(
                    [prev_unit[CHUNK:], cur[u * GM_CHUNK:u * GM_CHUNK + CHUNK]], axis=0)
            sample_prev = cached[stream * WINDOW:(stream + 1) * WINDOW]
            prev = jnp.where(is_sample, sample_prev, prompt_prev)
            return jnp.concatenate([prev, cur[r0:r0 + CHUNK]], axis=0)

        kb = band(k_rep[kv], kprev_ref, ck_rep[kv])
        vb = band(v_rep[kv], vprev_ref, cv_rep[kv])

        pieces = []
        for m in range(Q_PER_KV // 2):
            c0 = kv * Q_PER_KV * HEAD_DIM + m * LANES
            qp = q_ref[r0:r0 + CHUNK, c0:c0 + LANES]
            pieces.append(jnp.where(lo64, qp, jnp.zeros_like(qp)))
            pieces.append(jnp.where(lo64, jnp.zeros_like(qp), qp))
        qs = jnp.concatenate(pieces, axis=0)
        cols = slice(kv * n_q, (kv + 1) * n_q)
        st = lax.dot_general(kb, qs, (((1,), (1,)), ((), ())),
                             preferred_element_type=F32)
        return jnp.where(valid, st - bias_ref[:, cols], NEG_INF), vb

    def attend(u, c2, kv, st, vb):
        r0 = u * GM_CHUNK + c2 * CHUNK
        cols = slice(kv * n_q, (kv + 1) * n_q)
        sink = sink_ref[:, cols]
        mx = jnp.maximum(jnp.max(st, axis=0, keepdims=True), sink)
        p = jnp.exp(st - mx)
        denom = jnp.sum(p, axis=0, keepdims=True) + jnp.exp(sink - mx)
        pn = (p * (1.0 / denom)).astype(BF16)
        o = lax.dot_general(pn, vb, (((0,), (0,)), ((), ())),
                            preferred_element_type=F32)
        for m in range(Q_PER_KV // 2):
            o_pair = jnp.where(lo64, o[(2 * m) * CHUNK:(2 * m + 1) * CHUNK],
                               o[(2 * m + 1) * CHUNK:(2 * m + 2) * CHUNK])
            c0 = kv * Q_PER_KV * HEAD_DIM + m * LANES
            ocat_ref[r0:r0 + CHUNK, c0:c0 + LANES] = o_pair.astype(BF16)

    blocks = [(u, c2, kv) for u in range(units) for c2 in range(GM_CHUNK // CHUNK)
              for kv in range(N_KV_HEADS)]
    scored = [scores(*blk) for blk in blocks]
    for blk, (st, vb) in zip(blocks, scored):
        attend(*blk, st, vb)

    lo128 = lane_lo(GM_CHUNK)
    for u in range(units):
        rows = slice(u * GM_CHUNK, (u + 1) * GM_CHUNK)
        ua = _gelu(z_ref[rows, off_gu:off_gu + d_gm])
        gvn = _rms_scale(_gelu(z_ref[rows, off_gv:off_gv + d_gm])) * gmn_ref[...]
        gvn_ref[rows, :] = gvn
        gb = gvn.astype(BF16)
        for m in range(N_GM_GROUPS // 2):
            cols = slice(m * LANES, (m + 1) * LANES)
            rhs = gb[:, cols]
            rhs2 = jnp.concatenate([jnp.where(lo128, rhs, jnp.zeros_like(rhs)),
                                    jnp.where(lo128, jnp.zeros_like(rhs), rhs)], axis=0)
            sp = jnp.dot(wsp_ref[0, m], rhs2, preferred_element_type=F32) + gmb_ref[0, :, cols]
            ocat_ref[rows, D_ATTN + m * LANES:D_ATTN + (m + 1) * LANES] = (ua[:, cols] * sp).astype(BF16)

    x1_ref[...] = x + jnp.dot(ocat_ref[...], wout_ref[...], preferred_element_type=F32)

    for kv in range(N_KV_HEADS):
        kprev_ref[kv] = k_rep[kv][tm - GM_CHUNK:]
        vprev_ref[kv] = v_rep[kv][tm - GM_CHUNK:]


def _mix0(xp, xs, cache_k, cache_v, g, w_in, bias_tbl, sink_row, gm_norm, wsp, gmb, w_out, *, units):
    n_prompt_rows, d = xp.shape
    n_sample_rows = xs.shape[0]
    m_rows = n_prompt_rows + n_sample_rows
    tm = units * GM_CHUNK
    assert n_prompt_rows % tm == 0 and n_sample_rows % tm == 0
    n_tiles = m_rows // tm
    n_prompt_tiles = n_prompt_rows // tm
    d_in = w_in.shape[1]
    d_gm = gm_norm.shape[-1]
    cache_rows = (tm // CHUNK) * WINDOW

    def prompt_block(i):
        return jnp.minimum(i, n_prompt_tiles - 1)

    def sample_block(i):
        return jnp.maximum(i - n_prompt_tiles, 0)

    def kind(i):
        return jnp.where(i >= n_prompt_tiles, 1, 0)

    in_specs = [
        pl.BlockSpec((tm, d), lambda i: (prompt_block(i), 0)),
        pl.BlockSpec((tm, d), lambda i: (sample_block(i), 0)),
        pl.BlockSpec((cache_rows, D_KV), lambda i: (sample_block(i), 0)),
        pl.BlockSpec((cache_rows, D_KV), lambda i: (sample_block(i), 0)),
        _const_spec((1, d)),
        _const_spec((d, d_in)),
        _const_spec(bias_tbl.shape),
        _const_spec(sink_row.shape),
        _const_spec((1, d_gm)),
        pl.BlockSpec((1,) + wsp.shape[1:], lambda i: (kind(i), 0, 0, 0)),
        pl.BlockSpec((1, GM_CHUNK, d_gm), lambda i: (kind(i), 0, 0)),
        _const_spec(w_out.shape),
    ]
    out_specs = [
        pl.BlockSpec((tm, d), lambda i: (i, 0)),
        pl.BlockSpec((tm, D_KV), lambda i: (i, 0)),
        pl.BlockSpec((tm, D_KV), lambda i: (i, 0)),
        pl.BlockSpec((tm, d_gm), lambda i: (sample_block(i), 0)),
    ]
    out_shape = [
        jax.ShapeDtypeStruct((m_rows, d), F32),
        jax.ShapeDtypeStruct((m_rows, D_KV), F32),
        jax.ShapeDtypeStruct((m_rows, D_KV), F32),
        jax.ShapeDtypeStruct((n_sample_rows, d_gm), F32),
    ]
    scratch = [
        pltpu.VMEM((tm, d_in), F32),
        pltpu.VMEM((tm, D_ATTN), BF16),
        pltpu.VMEM((tm, D_ATTN + d_gm), BF16),
        pltpu.VMEM((N_KV_HEADS, GM_CHUNK, LANES), BF16),
        pltpu.VMEM((N_KV_HEADS, GM_CHUNK, LANES), BF16),
    ]
    return pl.pallas_call(
        functools.partial(_mix0_kernel, n_prompt_tiles, units),
        grid=(n_tiles,),
        in_specs=in_specs,
        out_specs=out_specs,
        out_shape=out_shape,
        scratch_shapes=scratch,
        compiler_params=pltpu.CompilerParams(
            dimension_semantics=("arbitrary",), vmem_limit_bytes=VMEM_LIMIT_BYTES),
        name="mix0",
    )(xp, xs, cache_k, cache_v, g, w_in, bias_tbl, sink_row, gm_norm, wsp, gmb, w_out)


def _ffn_kernel(final, x_ref, g_ref, wg_ref, wu_ref, wd_ref, gnext_ref, out_ref, *rest):
    h_ref = rest[-1]
    j = pl.program_id(1)

    @pl.when(j == 0)
    def _():
        x = x_ref[...]
        h_ref[...] = (_rms_scale(x) * g_ref[...]).astype(BF16)
        out_ref[...] = x

    h = h_ref[...]
    gate = jnp.dot(h, wg_ref[0].astype(BF16), preferred_element_type=F32)
    up = jnp.dot(h, wu_ref[0].astype(BF16), preferred_element_type=F32)
    act = (gate * jax.nn.sigmoid(gate) * up).astype(BF16)
    out_ref[...] += jnp.dot(act, wd_ref[0].astype(BF16), preferred_element_type=F32)

    @pl.when(j == pl.num_programs(1) - 1)
    def _():
        out = out_ref[...]
        scale = lax.rsqrt(jnp.mean(out * out, axis=-1, keepdims=True) + RMS_EPS)
        if final:
            out_ref[...] = out * scale * gnext_ref[...]
        else:
            rest[0][...] = jnp.broadcast_to(scale, rest[0].shape)


def _ffn(x, g, wg, wu, wd, gnext, *, layer, tm, tf, final, row0=0, n_rows=None):
    d = x.shape[1]
    n_rows = x.shape[0] if n_rows is None else n_rows
    f = wg.shape[2]
    assert n_rows % tm == 0 and row0 % tm == 0 and f % tf == 0
    tile0 = row0 // tm
    out_specs = [pl.BlockSpec((tm, d), lambda i, j: (i, 0))]
    out_shape = [jax.ShapeDtypeStruct((n_rows, d), F32)]
    if not final:
        out_specs.append(pl.BlockSpec((tm, LANES), lambda i, j: (i, 0)))
        out_shape.append(jax.ShapeDtypeStruct((n_rows, LANES), F32))
    return pl.pallas_call(
        functools.partial(_ffn_kernel, final),
        grid=(n_rows // tm, f // tf),
        in_specs=[
            pl.BlockSpec((tm, d), lambda i, j: (i + tile0, 0)),
            pl.BlockSpec((1, d), lambda i, j: (0, 0)),
            pl.BlockSpec((1, d, tf), lambda i, j: (layer, 0, j)),
            pl.BlockSpec((1, d, tf), lambda i, j: (layer, 0, j)),
            pl.BlockSpec((1, tf, d), lambda i, j: (layer, j, 0)),
            pl.BlockSpec((1, d), lambda i, j: (0, 0)),
        ],
        out_specs=out_specs,
        out_shape=out_shape,
        scratch_shapes=[pltpu.VMEM((tm, d), BF16)],
        compiler_params=pltpu.CompilerParams(
            dimension_semantics=("arbitrary", "arbitrary"), vmem_limit_bytes=VMEM_LIMIT_BYTES),
        name="ffn",
    )(x, g, wg, wu, wd, gnext)


TAB_ROWS = 18


def _discretize(lr, li, log_dt):
    dt = jnp.exp(log_dt)
    mag = jnp.exp(lr * dt)
    ar = mag * jnp.cos(li * dt)
    ai = mag * jnp.sin(li * dt)
    den = lr * lr + li * li
    nr = ar - 1.0
    return ar, ai, (nr * lr + ai * li) / den, (ai * lr - nr * li) / den


def _cmul(pr, pi, qr, qi):
    return pr * qr - pi * qi, pr * qi + pi * qr


def _s5_prep_kernel(seg_rows, lam_ref, lamc_ref, b_ref, c_ref, dall_ref, wec_ref, wsc_ref, tab_ref,
                    bhi_ref, blo_ref):
    ns = S5_BLOCK_STATE

    ar, ai, fr, fi = _discretize(lamc_ref[0, 0], lamc_ref[0, 1], lamc_ref[0, 2])
    bbr, bbi = _cmul(fr, fi, b_ref[0, 0], b_ref[0, 1])
    c_re = c_ref[0, 0]
    c_im = c_ref[0, 1]

    def split(a):
        hi = a.astype(BF16)
        return hi, (a - hi.astype(F32)).astype(BF16)

    power = (jnp.ones_like(ar), jnp.zeros_like(ar))
    for l in range(S5_L):
        bkr, bki = _cmul(*power, bbr, bbi)
        power = _cmul(*power, ar, ai)
        rows = slice(l * LANES, (l + 1) * LANES)
        bk = jnp.concatenate([bkr, bki], axis=1)
        bhi_ref[rows, :], blo_ref[rows, :] = split(bk)
        wec_ref[0, (S5_L - 1 - l) * LANES:(S5_L - l) * LANES, :] = bk
        qr, qi = power
        wsc_ref[0, rows, :] = jnp.concatenate([c_re * qr - c_im * qi, -c_re * qi - c_im * qr], axis=1)

    c_hi, c_lo = split(jnp.concatenate([c_re, -c_im], axis=1))
    c_parts = jnp.concatenate([c_hi, c_lo], axis=0)
    dims = (((1,), (1,)), ((), ()))
    d_hi = lax.dot_general(bhi_ref[...], c_parts, dims, preferred_element_type=F32)
    d_lo = lax.dot_general(blo_ref[...], c_parts, dims, preferred_element_type=F32)
    d_all = d_hi[:, 0:LANES] + d_hi[:, LANES:2 * LANES] + d_lo[:, 0:LANES]
    shape = (S5_L * LANES, LANES)
    in_group = (lax.broadcasted_iota(jnp.int32, shape, 0) % LANES) // S5_GROUP
    out_group = lax.broadcasted_iota(jnp.int32, shape, 1) // S5_GROUP
    dall_ref[0] = jnp.where(in_group == out_group, d_all, 0.0).astype(BF16)

    ar, ai, _, _ = _discretize(lam_ref[0, 0], lam_ref[0, 1], lam_ref[0, 2])
    row = lax.broadcasted_iota(jnp.int32, (SUBLANES, ns), 0)

    def bcast(a):
        return jnp.broadcast_to(a, (SUBLANES, ns))

    def log_step_tables(t0, base):
        cur = base
        for n, shift in enumerate((1, 2, 4)):
            tab_ref[0, t0 + 2 * n] = jnp.where(row >= shift, bcast(cur[0]), 0.0)
            tab_ref[0, t0 + 2 * n + 1] = jnp.where(row >= shift, bcast(cur[1]), 0.0)
            cur = _cmul(*cur, *cur)
        return cur

    a1 = (ar, ai)
    for _ in range(S5_L.bit_length() - 1):
        a1 = _cmul(*a1, *a1)
    a8 = log_step_tables(0, a1)
    pr_tab = jnp.zeros((SUBLANES, ns), F32)
    pi_tab = jnp.zeros((SUBLANES, ns), F32)
    cur = a1
    for r in range(SUBLANES):
        pr_tab = jnp.where(row == r, bcast(cur[0]), pr_tab)
        pi_tab = jnp.where(row == r, bcast(cur[1]), pi_tab)
        cur = _cmul(*cur, *a1)
    tab_ref[0, 6] = pr_tab
    tab_ref[0, 7] = pi_tab
    tab_ref[0, 8] = bcast(a8[0])
    tab_ref[0, 9] = bcast(a8[1])
    tab_ref[0, 10] = bcast(a1[0])
    tab_ref[0, 11] = bcast(a1[1])
    seg = a1
    for _ in range(seg_rows.bit_length() - 1):
        seg = _cmul(*seg, *seg)
    log_step_tables(12, seg)


def _s5_prep(lam, lam_c, b_c, c_c, *, seg_rows):
    assert seg_rows & (seg_rows - 1) == 0 and S5_L & (S5_L - 1) == 0
    nb = lam.shape[0]
    ns = S5_BLOCK_STATE
    k = S5_L * LANES
    cspec = pl.BlockSpec((1, k, LANES), lambda j: (j, 0, 0))

    def whole(a):
        return pl.BlockSpec((1,) + a.shape[1:], lambda j: (j,) + (0,) * (a.ndim - 1))

    return pl.pallas_call(
        functools.partial(_s5_prep_kernel, seg_rows),
        grid=(nb,),
        in_specs=[whole(lam), whole(lam_c), whole(b_c), whole(c_c)],
        out_specs=[cspec, cspec, cspec,
                   pl.BlockSpec((1, TAB_ROWS, SUBLANES, ns), lambda j: (j, 0, 0, 0))],
        out_shape=[
            jax.ShapeDtypeStruct((nb, k, LANES), BF16),
            jax.ShapeDtypeStruct((nb, k, LANES), F32),
            jax.ShapeDtypeStruct((nb, k, LANES), F32),
            jax.ShapeDtypeStruct((nb, TAB_ROWS, SUBLANES, ns), F32),
        ],
        scratch_shapes=[pltpu.VMEM((k, LANES), BF16), pltpu.VMEM((k, LANES), BF16)],
        compiler_params=pltpu.CompilerParams(
            dimension_semantics=("arbitrary",), vmem_limit_bytes=VMEM_LIMIT_BYTES),
        name="s5_prep",
    )(lam, lam_c, b_c, c_c)


def _s5_core_kernel(n_prompt_rows, n_streams,
                    x_hbm, rstd_ref, g_ref, dall_ref, wec_ref, wsc_ref, tab_ref, s0r_ref, s0i_ref,
                    d_ref, wglu_ref,
                    y_hbm, sfr_ref, sfi_ref, wglu_bf16_ref,
                    xp_ref, xs_ref, yp_ref, ys_ref, in_sem, out_sem,
                    u2f_ref, u2_ref, e_ref, wt_ref, we_ref, wst_ref):
    wglu_bf16_ref[...] = wglu_ref[...].astype(BF16)
    ns = S5_BLOCK_STATE
    j = pl.program_id(0)
    nb = pl.num_programs(0)
    slot = j % 2
    n_rows = x_hbm.shape[0]
    n_sample_rows = n_rows - n_prompt_rows
    seg_rows = n_prompt_rows // SUBLANES

    def copies(block, slot, load):
        lanes = pl.ds(pl.multiple_of(block * LANES, LANES), LANES)
        out = []
        for l in range(S5_L):
            pairs = [(pl.ds(seg * seg_rows, seg_rows), (xp_ref if load else yp_ref).at[slot, l, :, seg, :])
                     for seg in range(SUBLANES)]
            pairs.append((pl.ds(n_prompt_rows, n_sample_rows), (xs_ref if load else ys_ref).at[slot, l]))
            for rows, buf in pairs:
                if load:
                    out.append(pltpu.make_async_copy(x_hbm.at[rows, l, lanes], buf, in_sem.at[slot]))
                else:
                    out.append(pltpu.make_async_copy(buf, y_hbm.at[rows, l, lanes], out_sem.at[slot]))
        return out

    @pl.when(j == 0)
    def _():
        for c in copies(0, 0, True):
            c.start()

    @pl.when(j + 1 < nb)
    def _():
        for c in copies(j + 1, 1 - slot, True):
            c.start()

    for c in copies(j, slot, True):
        c.wait()


    shape = (S5_L * LANES, LANES)
    lane = lax.broadcasted_iota(jnp.int32, shape, 1)
    lane_lo = lane < S5_STATE
    row_group = (lax.broadcasted_iota(jnp.int32, shape, 0) % LANES) // S5_GROUP
    groups_per_vreg = LANES // S5_STATE

    def expand(compact_ref, full_ref):
        x = compact_ref[0]
        r = pltpu.roll(x, S5_STATE, axis=1)
        for part, rep in enumerate((jnp.where(lane_lo, x, r), jnp.where(lane_lo, r, x))):
            for v in range(S5_LANE_GROUPS // groups_per_vreg):
                own = row_group == groups_per_vreg * v + (lane // S5_STATE)
                col = part * S5_BLOCK_STATE + v * LANES
                full_ref[:, col:col + LANES] = jnp.where(own, rep, 0.0).astype(BF16)

    expand(wec_ref, we_ref)

    for l in range(S5_L):
        xl = xp_ref[slot, l].reshape(n_prompt_rows, LANES)
        u2f_ref[l, 0:n_prompt_rows, :] = xl * rstd_ref[l, 0:n_prompt_rows, :] * g_ref[...]
        u2f_ref[l, n_prompt_rows:, :] = xs_ref[slot, l] * rstd_ref[l, n_prompt_rows:, :] * g_ref[...]
    for l in range(S5_L):
        u2_ref[:, l * LANES:(l + 1) * LANES] = u2f_ref[l].astype(BF16)
    e_ref[...] = jnp.dot(u2_ref[...], we_ref[...], preferred_element_type=F32)

    zero_blk = jnp.zeros((LANES, LANES), BF16)
    for l in range(S5_L):
        for l2 in range(S5_L):
            blk = dall_ref[0, (l2 - l) * LANES:(l2 - l + 1) * LANES, :] if l2 >= l else zero_blk
            wt_ref[l * LANES:(l + 1) * LANES, l2 * LANES:(l2 + 1) * LANES] = blk
    slabs_per_block = S5_WT_BLOCK // LANES
    n_col_blocks = S5_L * LANES // S5_WT_BLOCK
    for cb in range(n_col_blocks):
        k_end = (cb + 1) * S5_WT_BLOCK
        y2 = jnp.dot(u2_ref[:, 0:k_end], wt_ref[0:k_end, cb * S5_WT_BLOCK:k_end],
                     preferred_element_type=F32)
        for n in range(slabs_per_block):
            l = cb * slabs_per_block + n
            u2f_ref[l] = y2[:, n * LANES:(n + 1) * LANES] + d_ref[...] * u2f_ref[l]
    expand(wsc_ref, wst_ref)

    tabs = [tab_ref[0, t] for t in range(TAB_ROWS)]
    m_tabs, (pr, pi, a8r, a8i, ar, ai), b_tabs = tabs[0:6], tabs[6:12], tabs[12:18]
    row = lax.broadcasted_iota(jnp.int32, (SUBLANES, ns), 0)

    def log_step_scan(xr, xi, t):
        for n, shift in enumerate((1, 2, 4)):
            tr, ti = t[2 * n], t[2 * n + 1]
            sr = pltpu.roll(xr, shift, axis=0)
            si = pltpu.roll(xi, shift, axis=0)
            xr, xi = xr + tr * sr - ti * si, xi + tr * si + ti * sr
        return xr, xi

    def shift_down(xr, xi, fr, fi):
        first = row == 0
        return (jnp.where(first, fr, pltpu.roll(xr, 1, axis=0)),
                jnp.where(first, fi, pltpu.roll(xi, 1, axis=0)))

    def rows_of(k):
        return pl.ds(pl.multiple_of(k * SUBLANES, SUBLANES), SUBLANES)

    def local_step(k, carry):
        sr, si = carry
        er = e_ref[rows_of(k), 0:ns]
        ei = e_ref[rows_of(k), ns:2 * ns]
        e_ref[rows_of(k), 0:ns] = sr
        e_ref[rows_of(k), ns:2 * ns] = si
        return ar * sr - ai * si + er, ar * si + ai * sr + ei

    zero = jnp.zeros((SUBLANES, ns), F32)
    ends = lax.fori_loop(0, seg_rows, local_step, (zero, zero), unroll=True)
    ends = log_step_scan(*ends, b_tabs)
    sfr_ref[...] = jnp.zeros_like(sfr_ref)
    sfi_ref[...] = jnp.zeros_like(sfi_ref)
    sfr_ref[n_streams:n_streams + 1, :] = ends[0][SUBLANES - 1:SUBLANES]
    sfi_ref[n_streams:n_streams + 1, :] = ends[1][SUBLANES - 1:SUBLANES]

    def correct_step(k, carry):
        cr, ci = carry
        e_ref[rows_of(k), 0:ns] += cr
        e_ref[rows_of(k), ns:2 * ns] += ci
        return ar * cr - ai * ci, ar * ci + ai * cr

    lax.fori_loop(0, seg_rows, correct_step, shift_down(*ends, zero, zero), unroll=True)

    for b in range(n_streams):
        rows = pl.ds(n_prompt_rows + b * SUBLANES, SUBLANES)
        cr = jnp.broadcast_to(s0r_ref[b:b + 1, :], (SUBLANES, ns))
        ci = jnp.broadcast_to(s0i_ref[b:b + 1, :], (SUBLANES, ns))
        xr, xi = log_step_scan(e_ref[rows, 0:ns], e_ref[rows, ns:2 * ns], m_tabs)
        st_r = xr + pr * cr - pi * ci
        st_i = xi + pr * ci + pi * cr
        e_ref[rows, 0:ns], e_ref[rows, ns:2 * ns] = shift_down(st_r, st_i, cr, ci)
        sfr_ref[b:b + 1, :] = st_r[SUBLANES - 1:SUBLANES]
        sfi_ref[b:b + 1, :] = st_i[SUBLANES - 1:SUBLANES]

    @pl.when(j >= 2)
    def _():
        for c in copies(j - 2, slot, False):
            c.wait()

    sprev = e_ref[...].astype(BF16)
    for cb in range(n_col_blocks):
        cols = slice(cb * S5_WT_BLOCK, (cb + 1) * S5_WT_BLOCK)
        y2 = lax.dot_general(sprev, wst_ref[cols, :], (((1,), (1,)), ((), ())),
                             preferred_element_type=F32)
        for n in range(slabs_per_block):
            l = cb * slabs_per_block + n
            yl = u2f_ref[l] + y2[:, n * LANES:(n + 1) * LANES]
            yp_ref[slot, l] = yl[0:n_prompt_rows].reshape(seg_rows, SUBLANES, LANES)
            ys_ref[slot, l] = yl[n_prompt_rows:]
    for c in copies(j, slot, False):
        c.start()

    @pl.when(j == nb - 1)
    def _():
        for c in copies(j - 1, 1 - slot, False) + copies(j, slot, False):
            c.wait()


def _s5_core(x, rstd, g, dall, wec, wsc, tab, s0_re, s0_im, d_skip, w_glu, *, n_prompt_rows):
    m_rows, d = x.shape
    nb = d // LANES
    assert nb >= 2
    glu_cols = w_glu.shape[1] // nb
    assert glu_cols % LANES == 0
    glu_blk = pl.BlockSpec((w_glu.shape[0], glu_cols), lambda j: (0, j))
    ns = S5_BLOCK_STATE
    k = S5_L * LANES
    n_rows = m_rows // S5_L
    n_streams = s0_re.shape[0]
    assert n_prompt_rows % (S5_L * SUBLANES) == 0
    assert (m_rows - n_prompt_rows) == n_streams * S5_L * SUBLANES
    n_prompt_mrows = n_prompt_rows // S5_L
    n_sample_mrows = n_rows - n_prompt_mrows
    seg_rows = n_prompt_mrows // SUBLANES
    sf_rows = n_streams + SUBLANES
    hbm = pl.BlockSpec(memory_space=pl.ANY)
    cspec = pl.BlockSpec((1, k, LANES), lambda j: (j, 0, 0))
    st = pl.BlockSpec((n_streams, ns), lambda j: (0, j))
    sf = pl.BlockSpec((sf_rows, ns), lambda j: (0, j))
    slab_p = pltpu.VMEM((2, S5_L, seg_rows, SUBLANES, LANES), F32)
    slab_s = pltpu.VMEM((2, S5_L, n_sample_mrows, LANES), F32)
    y, sf_re, sf_im, w_glu_bf16 = pl.pallas_call(
        functools.partial(_s5_core_kernel, n_prompt_mrows, n_streams),
        grid=(nb,),
        in_specs=[
            hbm,
            _const_spec((S5_L, n_rows, LANES)),
            pl.BlockSpec((1, LANES), lambda j: (0, j)),
            cspec, cspec, cspec,
            pl.BlockSpec((1, TAB_ROWS, SUBLANES, ns), lambda j: (j, 0, 0, 0)),
            st, st,
            pl.BlockSpec((1, LANES), lambda j: (0, j)),
            glu_blk,
        ],
        out_specs=[hbm, sf, sf, glu_blk],
        out_shape=[
            jax.ShapeDtypeStruct((n_rows, S5_L, d), F32),
            jax.ShapeDtypeStruct((sf_rows, nb * ns), F32),
            jax.ShapeDtypeStruct((sf_rows, nb * ns), F32),
            jax.ShapeDtypeStruct(w_glu.shape, BF16),
        ],
        scratch_shapes=[
            slab_p, slab_s, slab_p, slab_s,
            pltpu.SemaphoreType.DMA((2,)), pltpu.SemaphoreType.DMA((2,)),
            pltpu.VMEM((S5_L, n_rows, LANES), F32),
            pltpu.VMEM((n_rows, k), BF16),
            pltpu.VMEM((n_rows, 2 * ns), F32),
            pltpu.VMEM((k, k), BF16),
            pltpu.VMEM((k, 2 * ns), BF16),
            pltpu.VMEM((k, 2 * ns), BF16),
        ],
        compiler_params=pltpu.CompilerParams(
            dimension_semantics=("arbitrary",), vmem_limit_bytes=VMEM_LIMIT_BYTES),
        name="s5_core",
    )(x.reshape(n_rows, S5_L, d), rstd, g, dall, wec, wsc, tab, s0_re, s0_im, d_skip, w_glu)
    return y.reshape(m_rows, d), sf_re, sf_im, w_glu_bf16


def _slab_row_order(a, n_prompt_rows):
    c = a.shape[1]
    seg_rows = n_prompt_rows // (S5_L * SUBLANES)
    p = a[:n_prompt_rows].reshape(SUBLANES, seg_rows, S5_L, c).transpose(2, 1, 0, 3)
    s_ = a[n_prompt_rows:].reshape(-1, S5_L, c).transpose(1, 0, 2)
    return jnp.concatenate([p.reshape(S5_L, seg_rows * SUBLANES, c), s_], axis=1)


def _glu_kernel(n_col_blocks, x_ref, y_ref, w_ref, out_ref):
    d = x_ref.shape[1]
    a = _gelu(y_ref[...]).astype(BF16)
    cb = d // n_col_blocks
    for n in range(n_col_blocks):
        cols = slice(n * cb, (n + 1) * cb)
        val = jnp.dot(a, w_ref[:, n * cb:(n + 1) * cb], preferred_element_type=F32)
        gate = jnp.dot(a, w_ref[:, d + n * cb:d + (n + 1) * cb], preferred_element_type=F32)
        out_ref[:, cols] = x_ref[:, cols] + val * jax.nn.sigmoid(gate)


def _glu(x, y, w, *, tm, n_col_blocks=GLU_COL_BLOCKS):
    m_rows, d = x.shape
    assert m_rows % tm == 0 and d % n_col_blocks == 0
    tile = pl.BlockSpec((tm, d), lambda i: (i, 0))
    return pl.pallas_call(
        functools.partial(_glu_kernel, n_col_blocks),
        grid=(m_rows // tm,),
        in_specs=[tile, tile, _const_spec(w.shape)],
        out_specs=tile,
        out_shape=jax.ShapeDtypeStruct((m_rows, d), F32),
        compiler_params=pltpu.CompilerParams(
            dimension_semantics=("arbitrary",), vmem_limit_bytes=VMEM_LIMIT_BYTES),
        name="glu",
    )(x, y, w)


def _attention_tables(sinks):
    slopes = jnp.exp2(-8.0 * jnp.arange(1, N_HEADS + 1, dtype=F32) / N_HEADS)
    frame = jnp.arange(CHUNK, dtype=F32)[None, :]
    band = jnp.arange(BAND, dtype=F32)[:, None]
    dist = jnp.abs(frame - (band - WINDOW))
    bias = (dist[:, None, :] * slopes[None, :, None]).reshape(BAND, N_HEADS * CHUNK)
    sink_row = jnp.repeat(sinks.astype(F32), CHUNK).reshape(1, N_HEADS * CHUNK)
    return bias, sink_row


def _gm_tables(gm_ws, gm_b, d_gm):
    blk = jnp.arange(GM_CHUNK) // CHUNK
    w_prompt = jnp.where((blk[:, None] >= blk[None, :])[None], gm_ws, 0.0)
    top = gm_ws[:, :CHUNK, :CHUNK]
    zeros = jnp.zeros_like(top)
    w_sample = jnp.concatenate(
        [jnp.concatenate([top, zeros], axis=2), jnp.concatenate([zeros, top], axis=2)], axis=1)
    wsp = jnp.stack([w_prompt, w_sample])
    wsp = wsp.reshape(2, N_GM_GROUPS // 2, 2, GM_CHUNK, GM_CHUNK).transpose(0, 1, 3, 2, 4)
    wsp = wsp.reshape(2, N_GM_GROUPS // 2, GM_CHUNK, 2 * GM_CHUNK).astype(BF16)
    b_prompt = gm_b.T
    b_sample = jnp.concatenate([gm_b[:, :CHUNK].T, gm_b[:, :CHUNK].T], axis=0)
    gmb = jnp.stack([b_prompt, b_sample]).astype(F32)
    gmb = jnp.repeat(gmb, d_gm // N_GM_GROUPS, axis=2)
    return wsp, gmb


def _s5_compact(lam_re, lam_im, log_dt, b_re, b_im, c_re, c_im):
    n_groups = lam_re.shape[0]
    nb = n_groups // S5_LANE_GROUPS
    per_state = jnp.stack([lam_re, lam_im, jnp.broadcast_to(log_dt[:, None], lam_re.shape)]).astype(F32)
    per_state = per_state.reshape(3, nb, S5_LANE_GROUPS, S5_STATE).transpose(1, 0, 2, 3)
    lam = per_state.reshape(nb, 3, 1, S5_BLOCK_STATE)
    lam_c = jnp.broadcast_to(per_state[:, :, :, None, :],
                             (nb, 3, S5_LANE_GROUPS, S5_GROUP, S5_STATE)).reshape(nb, 3, LANES, S5_STATE)
    b_c = jnp.stack([b_re, b_im]).astype(F32).transpose(1, 0, 3, 2)
    b_c = b_c.reshape(nb, S5_LANE_GROUPS, 2, S5_GROUP, S5_STATE).transpose(0, 2, 1, 3, 4)
    c_c = jnp.stack([c_re, c_im]).astype(F32).transpose(1, 0, 2, 3)
    c_c = c_c.reshape(nb, S5_LANE_GROUPS, 2, S5_GROUP, S5_STATE).transpose(0, 2, 1, 3, 4)
    return (lam, lam_c, b_c.reshape(nb, 2, LANES, S5_STATE), c_c.reshape(nb, 2, LANES, S5_STATE))


def kernel(x_prompt, x_sample, cache_swa_k, cache_swa_v, state_s5_re, state_s5_im, norm_mix, norm_ffn, norm_final, w_in0, attn_sinks, gm_norm, gm_ws, gm_b, w_out0, s5_lam_re, s5_lam_im, s5_log_dt, s5_b_re, s5_b_im, s5_c_re, s5_c_im, s5_d, s5_w_glu, ffn_w_gate, ffn_w_up, ffn_w_down):
    batch, seq, d = x_prompt.shape
    dec_batch, dec_seq, _ = x_sample.shape
    assert batch == 1 and dec_seq == CHUNK and norm_mix.shape[0] == 2
    n_prompt = batch * seq
    n_sample = dec_batch * dec_seq
    d_gm = gm_norm.shape[-1]

    bias_tbl, sink_row = _attention_tables(attn_sinks[0])
    wsp, gmb = _gm_tables(gm_ws[0], gm_b[0], d_gm)
    x1, k_all, v_all, gvn = _mix0(
        x_prompt.reshape(n_prompt, d), x_sample.reshape(n_sample, d),
        cache_swa_k[0].reshape(dec_batch * WINDOW, D_KV), cache_swa_v[0].reshape(dec_batch * WINDOW, D_KV),
        norm_mix[0].reshape(1, d), w_in0[0].astype(BF16), bias_tbl, sink_row,
        gm_norm[0].reshape(1, d_gm), wsp, gmb, w_out0[0].astype(BF16), units=MIX_UNITS)

    x2, rstd2 = _ffn(x1, norm_ffn[0].reshape(1, d), ffn_w_gate, ffn_w_up, ffn_w_down,
                     norm_mix[1].reshape(1, d), layer=0, tm=FFN_TM, tf=FFN_TF, final=False)

    n_groups = s5_lam_re.shape[1]
    dall, wec, wsc, tab = _s5_prep(
        *_s5_compact(s5_lam_re[0], s5_lam_im[0], s5_log_dt[0], s5_b_re[0], s5_b_im[0], s5_c_re[0],
                     s5_c_im[0]),
        seg_rows=n_prompt // (S5_L * SUBLANES))
    ys5, sf_re, sf_im, w_glu_bf16 = _s5_core(
        x2, _slab_row_order(rstd2, n_prompt), norm_mix[1].reshape(1, d), dall, wec, wsc, tab,
        state_s5_re[0].reshape(dec_batch, n_groups * S5_STATE),
        state_s5_im[0].reshape(dec_batch, n_groups * S5_STATE), s5_d[0].reshape(1, d), s5_w_glu[0],
        n_prompt_rows=n_prompt)
    x3 = _glu(x2, ys5, w_glu_bf16, tm=GLU_TM)

    def last_ffn(row0, n_rows):
        (y,) = _ffn(x3, norm_ffn[1].reshape(1, d), ffn_w_gate, ffn_w_up, ffn_w_down,
                    norm_final.reshape(1, d), layer=1, tm=FFN_TM, tf=FFN_TF, final=True,
                    row0=row0, n_rows=n_rows)
        return y

    y_prompt = last_ffn(0, n_prompt)
    y_sample = last_ffn(n_prompt, n_sample)

    keep = min(WINDOW, seq)
    y_prompt = y_prompt.reshape(batch, seq, d)
    y_sample = y_sample.reshape(dec_batch, dec_seq, d)
    kv_shape_p = (1, batch, keep, N_KV_HEADS, HEAD_DIM)
    kv_shape_s = (1, dec_batch, dec_seq, N_KV_HEADS, HEAD_DIM)
    st_p = (1, batch, n_groups, S5_STATE)
    st_s = (1, dec_batch, n_groups, S5_STATE)
    return (y_prompt, y_sample,
            k_all[n_prompt - keep:n_prompt].reshape(kv_shape_p),
            v_all[n_prompt - keep:n_prompt].reshape(kv_shape_p),
            k_all[n_prompt:].reshape(kv_shape_s),
            v_all[n_prompt:].reshape(kv_shape_s),
            gvn.reshape(1, dec_batch, dec_seq, d_gm),
            sf_re[dec_batch].reshape(st_p), sf_im[dec_batch].reshape(st_p),
            sf_re[:dec_batch].reshape(st_s), sf_im[:dec_batch].reshape(st_s))
```

```python
import functools
import math

import jax
import jax.numpy as jnp
from jax import lax
from jax.experimental import pallas as pl
from jax.experimental.pallas import tpu as pltpu

F32 = jnp.float32
BF16 = jnp.bfloat16

CHUNK = 64
HEAD_DIM = 64
N_HEADS = 16
N_KV_HEADS = 2
Q_PER_KV = N_HEADS // N_KV_HEADS
WINDOW = 128
BAND = WINDOW + CHUNK
D_ATTN = N_HEADS * HEAD_DIM
D_KV = N_KV_HEADS * HEAD_DIM
GM_CHUNK = 128
N_GM_GROUPS = 16
S5_GROUP = 16
S5_STATE = 64
RMS_EPS = 1e-5
NEG_INF = -1e30

LANES = 128
SUBLANES = 8
MXU_TILE = 256
VMEM_LIMIT_BYTES = 56 * 1024 * 1024

MIX_UNITS = 2
FFN_TM = 1024
FFN_TF = 256
FFN_BLOCKS_PER_ITER = 2
GLU_TM = 512
GLU_COL_BLOCKS = 8

S5_L = SUBLANES
S5_LANE_GROUPS = LANES // S5_GROUP
S5_BLOCK_STATE = S5_LANE_GROUPS * S5_STATE
S5_WT_BLOCK = MXU_TILE


def _gelu(x):
    return 0.5 * x * (1.0 + lax.erf(x * math.sqrt(0.5)))


def _rms_scale(x):
    return x * lax.rsqrt(jnp.mean(x * x, axis=-1, keepdims=True) + RMS_EPS)


def _const_spec(shape):
    zeros = (0,) * len(shape)
    return pl.BlockSpec(shape, lambda *_: zeros, pipeline_mode=pl.Buffered(1))


def _mix0_kernel(n_prompt_tiles, units,
                 xp_ref, xs_ref, ck_ref, cv_ref, g_ref, win_ref, bias_ref, sink_ref, gmn_ref,
                 wsp_ref, gmb_ref, wout_ref,
                 x1_ref, k_ref, v_ref, gvn_ref,
                 z_ref, q_ref, ocat_ref, kprev_ref, vprev_ref):
    i = pl.program_id(0)
    is_sample = i >= n_prompt_tiles
    tm = units * GM_CHUNK
    d_gm = gmn_ref.shape[-1]
    off_k = D_ATTN
    off_v = D_ATTN + D_KV
    off_gu = D_ATTN + 2 * D_KV
    off_gv = off_gu + d_gm
    n_q = Q_PER_KV * CHUNK

    @pl.when(i == 0)
    def _():
        kprev_ref[...] = jnp.zeros_like(kprev_ref)
        vprev_ref[...] = jnp.zeros_like(vprev_ref)

    x = jnp.where(is_sample, xs_ref[...], xp_ref[...])
    h = (_rms_scale(x) * g_ref[...]).astype(BF16)
    z_ref[...] = jnp.dot(h, win_ref[...], preferred_element_type=F32)

    q_ref[...] = (z_ref[:, 0:D_ATTN] * (HEAD_DIM ** -0.5)).astype(BF16)
    k = z_ref[:, off_k:off_k + D_KV]
    v = z_ref[:, off_v:off_v + D_KV]
    k_ref[...] = k
    v_ref[...] = v

    def lane_lo(rows):
        return lax.broadcasted_iota(jnp.int32, (rows, LANES), 1) < HEAD_DIM

    def replicate(a):
        r = pltpu.roll(a, HEAD_DIM, axis=1)
        lo = lane_lo(a.shape[0])
        return jnp.where(lo, a, r).astype(BF16), jnp.where(lo, r, a).astype(BF16)

    k_rep = replicate(k)
    v_rep = replicate(v)
    ck_rep = replicate(ck_ref[...])
    cv_rep = replicate(cv_ref[...])

    lo64 = lane_lo(CHUNK)
    band_pos = lax.broadcasted_iota(jnp.int32, (BAND, n_q), 0)
    chunks_per_tile = tm // CHUNK

    def scores(u, c2, kv):
        r0 = u * GM_CHUNK + c2 * CHUNK
        stream = r0 // CHUNK
        chunk_index = i * chunks_per_tile + stream
        valid_from = jnp.where(is_sample, 0, jnp.maximum(WINDOW - CHUNK * chunk_index, 0))
        valid = band_pos >= valid_from

        def band(cur, prev_ref, cached):
            if u == 0:
                prev_unit = prev_ref[kv]
            else:
                prev_unit = cur[(u - 1) * GM_CHUNK:u * GM_CHUNK]
            if c2 == 0:
                prompt_prev = prev_unit
            else:
                prompt_prev = jnp.concatenate(
                    [prev_unit[CHUNK:], cur[u * GM_CHUNK:u * GM_CHUNK + CHUNK]], axis=0)
            sample_prev = cached[stream * WINDOW:(stream + 1) * WINDOW]
            prev = jnp.where(is_sample, sample_prev, prompt_prev)
            return jnp.concatenate([prev, cur[r0:r0 + CHUNK]], axis=0)

        kb = band(k_rep[kv], kprev_ref, ck_rep[kv])
        vb = band(v_rep[kv], vprev_ref, cv_rep[kv])

        pieces = []
        for m in range(Q_PER_KV // 2):
            c0 = kv * Q_PER_KV * HEAD_DIM + m * LANES
            qp = q_ref[r0:r0 + CHUNK, c0:c0 + LANES]
            pieces.append(jnp.where(lo64, qp, jnp.zeros_like(qp)))
            pieces.append(jnp.where(lo64, jnp.zeros_like(qp), qp))
        qs = jnp.concatenate(pieces, axis=0)
        cols = slice(kv * n_q, (kv + 1) * n_q)
        st = lax.dot_general(kb, qs, (((1,), (1,)), ((), ())),
                             preferred_element_type=F32)
        return jnp.where(valid, st - bias_ref[:, cols], NEG_INF), vb

    def attend(u, c2, kv, st, vb):
        r0 = u * GM_CHUNK + c2 * CHUNK
        cols = slice(kv * n_q, (kv + 1) * n_q)
        sink = sink_ref[:, cols]
        mx = jnp.maximum(jnp.max(st, axis=0, keepdims=True), sink)
        p = jnp.exp(st - mx)
        denom = jnp.sum(p, axis=0, keepdims=True) + jnp.exp(sink - mx)
        pn = (p * (1.0 / denom)).astype(BF16)
        o = lax.dot_general(pn, vb, (((0,), (0,)), ((), ())),
                            preferred_element_type=F32)
        for m in range(Q_PER_KV // 2):
            o_pair = jnp.where(lo64, o[(2 * m) * CHUNK:(2 * m + 1) * CHUNK],
                               o[(2 * m + 1) * CHUNK:(2 * m + 2) * CHUNK])
            c0 = kv * Q_PER_KV * HEAD_DIM + m * LANES
            ocat_ref[r0:r0 + CHUNK, c0:c0 + LANES] = o_pair.astype(BF16)

    blocks = [(u, c2, kv) for u in range(units) for c2 in range(GM_CHUNK // CHUNK)
              for kv in range(N_KV_HEADS)]
    scored = [scores(*blk) for blk in blocks]
    for blk, (st, vb) in zip(blocks, scored):
        attend(*blk, st, vb)

    lo128 = lane_lo(GM_CHUNK)
    for u in range(units):
        rows = slice(u * GM_CHUNK, (u + 1) * GM_CHUNK)
        ua = _gelu(z_ref[rows, off_gu:off_gu + d_gm])
        gvn = _rms_scale(_gelu(z_ref[rows, off_gv:off_gv + d_gm])) * gmn_ref[...]
        gvn_ref[rows, :] = gvn
        gb = gvn.astype(BF16)
        for m in range(N_GM_GROUPS // 2):
            cols = slice(m * LANES, (m + 1) * LANES)
            rhs = gb[:, cols]
            rhs2 = jnp.concatenate([jnp.where(lo128, rhs, jnp.zeros_like(rhs)),
                                    jnp.where(lo128, jnp.zeros_like(rhs), rhs)], axis=0)
            sp = jnp.dot(wsp_ref[0, m], rhs2, preferred_element_type=F32) + gmb_ref[0, :, cols]
            ocat_ref[rows, D_ATTN + m * LANES:D_ATTN + (m + 1) * LANES] = (ua[:, cols] * sp).astype(BF16)

    x1_ref[...] = x + jnp.dot(ocat_ref[...], wout_ref[...], preferred_element_type=F32)

    for kv in range(N_KV_HEADS):
        kprev_ref[kv] = k_rep[kv][tm - GM_CHUNK:]
        vprev_ref[kv] = v_rep[kv][tm - GM_CHUNK:]


def _mix0(xp, xs, cache_k, cache_v, g, w_in, bias_tbl, sink_row, gm_norm, wsp, gmb, w_out, *, units):
    n_prompt_rows, d = xp.shape
    n_sample_rows = xs.shape[0]
    m_rows = n_prompt_rows + n_sample_rows
    tm = units * GM_CHUNK
    assert n_prompt_rows % tm == 0 and n_sample_rows % tm == 0
    n_tiles = m_rows // tm
    n_prompt_tiles = n_prompt_rows // tm
    d_in = w_in.shape[1]
    d_gm = gm_norm.shape[-1]
    cache_rows = (tm // CHUNK) * WINDOW

    def prompt_block(i):
        return jnp.minimum(i, n_prompt_tiles - 1)

    def sample_block(i):
        return jnp.maximum(i - n_prompt_tiles, 0)

    def kind(i):
        return jnp.where(i >= n_prompt_tiles, 1, 0)

    in_specs = [
        pl.BlockSpec((tm, d), lambda i: (prompt_block(i), 0)),
        pl.BlockSpec((tm, d), lambda i: (sample_block(i), 0)),
        pl.BlockSpec((cache_rows, D_KV), lambda i: (sample_block(i), 0)),
        pl.BlockSpec((cache_rows, D_KV), lambda i: (sample_block(i), 0)),
        _const_spec((1, d)),
        _const_spec((d, d_in)),
        _const_spec(bias_tbl.shape),
        _const_spec(sink_row.shape),
        _const_spec((1, d_gm)),
        pl.BlockSpec((1,) + wsp.shape[1:], lambda i: (kind(i), 0, 0, 0)),
        pl.BlockSpec((1, GM_CHUNK, d_gm), lambda i: (kind(i), 0, 0)),
        _const_spec(w_out.shape),
    ]
    out_specs = [
        pl.BlockSpec((tm, d), lambda i: (i, 0)),
        pl.BlockSpec((tm, D_KV), lambda i: (i, 0)),
        pl.BlockSpec((tm, D_KV), lambda i: (i, 0)),
        pl.BlockSpec((tm, d_gm), lambda i: (sample_block(i), 0)),
    ]
    out_shape = [
        jax.ShapeDtypeStruct((m_rows, d), F32),
        jax.ShapeDtypeStruct((m_rows, D_KV), F32),
        jax.ShapeDtypeStruct((m_rows, D_KV), F32),
        jax.ShapeDtypeStruct((n_sample_rows, d_gm), F32),
    ]
    scratch = [
        pltpu.VMEM((tm, d_in), F32),
        pltpu.VMEM((tm, D_ATTN), BF16),
        pltpu.VMEM((tm, D_ATTN + d_gm), BF16),
        pltpu.VMEM((N_KV_HEADS, GM_CHUNK, LANES), BF16),
        pltpu.VMEM((N_KV_HEADS, GM_CHUNK, LANES), BF16),
    ]
    return pl.pallas_call(
        functools.partial(_mix0_kernel, n_prompt_tiles, units),
        grid=(n_tiles,),
        in_specs=in_specs,
        out_specs=out_specs,
        out_shape=out_shape,
        scratch_shapes=scratch,
        compiler_params=pltpu.CompilerParams(
            dimension_semantics=("arbitrary",), vmem_limit_bytes=VMEM_LIMIT_BYTES),
        name="mix0",
    )(xp, xs, cache_k, cache_v, g, w_in, bias_tbl, sink_row, gm_norm, wsp, gmb, w_out)


def _ffn_kernel(final, layer, tf, x_ref, g_ref, wg_hbm, wu_hbm, wd_hbm, gnext_ref, out_ref, *rest):
    h_ref, wg_buf, wu_buf, wd_buf, sem = rest[-5:]
    i = pl.program_id(0)
    n_tiles = pl.num_programs(0)
    n_blocks = wg_hbm.shape[2] // tf

    def copies(block, slot):
        cols = pl.ds(pl.multiple_of(block * tf, tf), tf)
        return (pltpu.make_async_copy(wg_hbm.at[layer, :, cols], wg_buf.at[slot], sem.at[slot, 0]),
                pltpu.make_async_copy(wu_hbm.at[layer, :, cols], wu_buf.at[slot], sem.at[slot, 1]),
                pltpu.make_async_copy(wd_hbm.at[layer, cols, :], wd_buf.at[slot], sem.at[slot, 2]))

    @pl.when(i == 0)
    def _():
        for c in copies(0, 0):
            c.start()

    x = x_ref[...]
    h_ref[...] = (_rms_scale(x) * g_ref[...]).astype(BF16)
    out_ref[...] = x

    def iteration(it, carry):
        for n in range(FFN_BLOCKS_PER_ITER):
            block = it * FFN_BLOCKS_PER_ITER + n
            slot = n % 2
            nxt = block + 1

            @pl.when(nxt < n_blocks)
            def _():
                for c in copies(nxt, 1 - slot):
                    c.start()

            @pl.when(jnp.logical_and(nxt == n_blocks, i + 1 < n_tiles))
            def _():
                for c in copies(0, 1 - slot):
                    c.start()

            for c in copies(block, slot):
                c.wait()
            h = h_ref[...]
            gate = jnp.dot(h, wg_buf[slot].astype(BF16), preferred_element_type=F32)
            up = jnp.dot(h, wu_buf[slot].astype(BF16), preferred_element_type=F32)
            act = (gate * jax.nn.sigmoid(gate) * up).astype(BF16)
            out_ref[...] += jnp.dot(act, wd_buf[slot].astype(BF16), preferred_element_type=F32)
        return carry

    lax.fori_loop(0, n_blocks // FFN_BLOCKS_PER_ITER, iteration, 0)

    out = out_ref[...]
    scale = lax.rsqrt(jnp.mean(out * out, axis=-1, keepdims=True) + RMS_EPS)
    if final:
        out_ref[...] = out * scale * gnext_ref[...]
    else:
        rest[0][...] = jnp.broadcast_to(scale, rest[0].shape)


def _ffn(x, g, wg, wu, wd, gnext, *, layer, tm, tf, final, row0=0, n_rows=None):
    d = x.shape[1]
    n_rows = x.shape[0] if n_rows is None else n_rows
    f = wg.shape[2]
    n_blocks = f // tf
    assert n_rows % tm == 0 and row0 % tm == 0 and f % tf == 0
    assert n_blocks % FFN_BLOCKS_PER_ITER == 0 and FFN_BLOCKS_PER_ITER % 2 == 0
    tile0 = row0 // tm
    hbm = pl.BlockSpec(memory_space=pl.ANY)
    out_specs = [pl.BlockSpec((tm, d), lambda i: (i, 0))]
    out_shape = [jax.ShapeDtypeStruct((n_rows, d), F32)]
    if not final:
        out_specs.append(pl.BlockSpec((tm, LANES), lambda i: (i, 0)))
        out_shape.append(jax.ShapeDtypeStruct((n_rows, LANES), F32))
    return pl.pallas_call(
        functools.partial(_ffn_kernel, final, layer, tf),
        grid=(n_rows // tm,),
        in_specs=[
            pl.BlockSpec((tm, d), lambda i: (i + tile0, 0)),
            pl.BlockSpec((1, d), lambda i: (0, 0)),
            hbm, hbm, hbm,
            pl.BlockSpec((1, d), lambda i: (0, 0)),
        ],
        out_specs=out_specs,
        out_shape=out_shape,
        scratch_shapes=[
            pltpu.VMEM((tm, d), BF16),
            pltpu.VMEM((2, d, tf), F32),
            pltpu.VMEM((2, d, tf), F32),
            pltpu.VMEM((2, tf, d), F32),
            pltpu.SemaphoreType.DMA((2, 3)),
        ],
        compiler_params=pltpu.CompilerParams(
            dimension_semantics=("arbitrary",), vmem_limit_bytes=VMEM_LIMIT_BYTES),
        name="ffn",
    )(x, g, wg, wu, wd, gnext)


TAB_ROWS = 18


def _discretize(lr, li, log_dt):
    dt = jnp.exp(log_dt)
    mag = jnp.exp(lr * dt)
    ar = mag * jnp.cos(li * dt)
    ai = mag * jnp.sin(li * dt)
    den = lr * lr + li * li
    nr = ar - 1.0
    return ar, ai, (nr * lr + ai * li) / den, (ai * lr - nr * li) / den


def _cmul(pr, pi, qr, qi):
    return pr * qr - pi * qi, pr * qi + pi * qr


def _s5_prep_kernel(seg_rows, lam_ref, lamc_ref, b_ref, c_ref, dall_ref, wec_ref, wsc_ref, tab_ref,
                    bhi_ref, blo_ref):
    ns = S5_BLOCK_STATE

    ar, ai, fr, fi = _discretize(lamc_ref[0, 0], lamc_ref[0, 1], lamc_ref[0, 2])
    bbr, bbi = _cmul(fr, fi, b_ref[0, 0], b_ref[0, 1])
    c_re = c_ref[0, 0]
    c_im = c_ref[0, 1]

    def split(a):
        hi = a.astype(BF16)
        return hi, (a - hi.astype(F32)).astype(BF16)

    power = (jnp.ones_like(ar), jnp.zeros_like(ar))
    for l in range(S5_L):
        bkr, bki = _cmul(*power, bbr, bbi)
        power = _cmul(*power, ar, ai)
        rows = slice(l * LANES, (l + 1) * LANES)
        bk = jnp.concatenate([bkr, bki], axis=1)
        bhi_ref[rows, :], blo_ref[rows, :] = split(bk)
        wec_ref[0, (S5_L - 1 - l) * LANES:(S5_L - l) * LANES, :] = bk
        qr, qi = power
        wsc_ref[0, rows, :] = jnp.concatenate([c_re * qr - c_im * qi, -c_re * qi - c_im * qr], axis=1)

    c_hi, c_lo = split(jnp.concatenate([c_re, -c_im], axis=1))
    c_parts = jnp.concatenate([c_hi, c_lo], axis=0)
    dims = (((1,), (1,)), ((), ()))
    d_hi = lax.dot_general(bhi_ref[...], c_parts, dims, preferred_element_type=F32)
    d_lo = lax.dot_general(blo_ref[...], c_parts, dims, preferred_element_type=F32)
    d_all = d_hi[:, 0:LANES] + d_hi[:, LANES:2 * LANES] + d_lo[:, 0:LANES]
    shape = (S5_L * LANES, LANES)
    in_group = (lax.broadcasted_iota(jnp.int32, shape, 0) % LANES) // S5_GROUP
    out_group = lax.broadcasted_iota(jnp.int32, shape, 1) // S5_GROUP
    dall_ref[0] = jnp.where(in_group == out_group, d_all, 0.0).astype(BF16)

    ar, ai, _, _ = _discretize(lam_ref[0, 0], lam_ref[0, 1], lam_ref[0, 2])
    row = lax.broadcasted_iota(jnp.int32, (SUBLANES, ns), 0)

    def bcast(a):
        return jnp.broadcast_to(a, (SUBLANES, ns))

    def log_step_tables(t0, base):
        cur = base
        for n, shift in enumerate((1, 2, 4)):
            tab_ref[0, t0 + 2 * n] = jnp.where(row >= shift, bcast(cur[0]), 0.0)
            tab_ref[0, t0 + 2 * n + 1] = jnp.where(row >= shift, bcast(cur[1]), 0.0)
            cur = _cmul(*cur, *cur)
        return cur

    a1 = (ar, ai)
    for _ in range(S5_L.bit_length() - 1):
        a1 = _cmul(*a1, *a1)
    a8 = log_step_tables(0, a1)
    pr_tab = jnp.zeros((SUBLANES, ns), F32)
    pi_tab = jnp.zeros((SUBLANES, ns), F32)
    cur = a1
    for r in range(SUBLANES):
        pr_tab = jnp.where(row == r, bcast(cur[0]), pr_tab)
        pi_tab = jnp.where(row == r, bcast(cur[1]), pi_tab)
        cur = _cmul(*cur, *a1)
    tab_ref[0, 6] = pr_tab
    tab_ref[0, 7] = pi_tab
    tab_ref[0, 8] = bcast(a8[0])
    tab_ref[0, 9] = bcast(a8[1])
    tab_ref[0, 10] = bcast(a1[0])
    tab_ref[0, 11] = bcast(a1[1])
    seg = a1
    for _ in range(seg_rows.bit_length() - 1):
        seg = _cmul(*seg, *seg)
    log_step_tables(12, seg)


def _s5_prep(lam, lam_c, b_c, c_c, *, seg_rows):
    assert seg_rows & (seg_rows - 1) == 0 and S5_L & (S5_L - 1) == 0
    nb = lam.shape[0]
    ns = S5_BLOCK_STATE
    k = S5_L * LANES
    cspec = pl.BlockSpec((1, k, LANES), lambda j: (j, 0, 0))

    def whole(a):
        return pl.BlockSpec((1,) + a.shape[1:], lambda j: (j,) + (0,) * (a.ndim - 1))

    return pl.pallas_call(
        functools.partial(_s5_prep_kernel, seg_rows),
        grid=(nb,),
        in_specs=[whole(lam), whole(lam_c), whole(b_c), whole(c_c)],
        out_specs=[cspec, cspec, cspec,
                   pl.BlockSpec((1, TAB_ROWS, SUBLANES, ns), lambda j: (j, 0, 0, 0))],
        out_shape=[
            jax.ShapeDtypeStruct((nb, k, LANES), BF16),
            jax.ShapeDtypeStruct((nb, k, LANES), F32),
            jax.ShapeDtypeStruct((nb, k, LANES), F32),
            jax.ShapeDtypeStruct((nb, TAB_ROWS, SUBLANES, ns), F32),
        ],
        scratch_shapes=[pltpu.VMEM((k, LANES), BF16), pltpu.VMEM((k, LANES), BF16)],
        compiler_params=pltpu.CompilerParams(
            dimension_semantics=("arbitrary",), vmem_limit_bytes=VMEM_LIMIT_BYTES),
        name="s5_prep",
    )(lam, lam_c, b_c, c_c)


def _s5_core_kernel(n_prompt_rows, n_streams,
                    x_hbm, rstd_ref, g_ref, dall_ref, wec_ref, wsc_ref, tab_ref, s0r_ref, s0i_ref,
                    d_ref, wglu_ref,
                    y_hbm, sfr_ref, sfi_ref, wglu_bf16_ref,
                    xp_ref, xs_ref, yp_ref, ys_ref, in_sem, out_sem,
                    u2f_ref, u2_ref, e_ref, wt_ref, we_ref, wst_ref):
    wglu_bf16_ref[...] = wglu_ref[...].astype(BF16)
    ns = S5_BLOCK_STATE
    j = pl.program_id(0)
    nb = pl.num_programs(0)
    slot = j % 2
    n_rows = x_hbm.shape[0]
    n_sample_rows = n_rows - n_prompt_rows
    seg_rows = n_prompt_rows // SUBLANES

    def copies(block, slot, load):
        lanes = pl.ds(pl.multiple_of(block * LANES, LANES), LANES)
        out = []
        for l in range(S5_L):
            pairs = [(pl.ds(seg * seg_rows, seg_rows), (xp_ref if load else yp_ref).at[slot, l, :, seg, :])
                     for seg in range(SUBLANES)]
            pairs.append((pl.ds(n_prompt_rows, n_sample_rows), (xs_ref if load else ys_ref).at[slot, l]))
            for rows, buf in pairs:
                if load:
                    out.append(pltpu.make_async_copy(x_hbm.at[rows, l, lanes], buf, in_sem.at[slot]))
                else:
                    out.append(pltpu.make_async_copy(buf, y_hbm.at[rows, l, lanes], out_sem.at[slot]))
        return out

    @pl.when(j == 0)
    def _():
        for c in copies(0, 0, True):
            c.start()

    @pl.when(j + 1 < nb)
    def _():
        for c in copies(j + 1, 1 - slot, True):
            c.start()

    for c in copies(j, slot, True):
        c.wait()


    shape = (S5_L * LANES, LANES)
    lane = lax.broadcasted_iota(jnp.int32, shape, 1)
    lane_lo = lane < S5_STATE
    row_group = (lax.broadcasted_iota(jnp.int32, shape, 0) % LANES) // S5_GROUP
    groups_per_vreg = LANES // S5_STATE

    def expand(compact_ref, full_ref):
        x = compact_ref[0]
        r = pltpu.roll(x, S5_STATE, axis=1)
        for part, rep in enumerate((jnp.where(lane_lo, x, r), jnp.where(lane_lo, r, x))):
            for v in range(S5_LANE_GROUPS // groups_per_vreg):
                own = row_group == groups_per_vreg * v + (lane // S5_STATE)
                col = part * S5_BLOCK_STATE + v * LANES
                full_ref[:, col:col + LANES] = jnp.where(own, rep, 0.0).astype(BF16)

    expand(wec_ref, we_ref)

    for l in range(S5_L):
        xl = xp_ref[slot, l].reshape(n_prompt_rows, LANES)
        u2f_ref[l, 0:n_prompt_rows, :] = xl * rstd_ref[l, 0:n_prompt_rows, :] * g_ref[...]
        u2f_ref[l, n_prompt_rows:, :] = xs_ref[slot, l] * rstd_ref[l, n_prompt_rows:, :] * g_ref[...]
    for l in range(S5_L):
        u2_ref[:, l * LANES:(l + 1) * LANES] = u2f_ref[l].astype(BF16)
    e_ref[...] = jnp.dot(u2_ref[...], we_ref[...], preferred_element_type=F32)

    zero_blk = jnp.zeros((LANES, LANES), BF16)
    for l in range(S5_L):
        for l2 in range(S5_L):
            blk = dall_ref[0, (l2 - l) * LANES:(l2 - l + 1) * LANES, :] if l2 >= l else zero_blk
            wt_ref[l * LANES:(l + 1) * LANES, l2 * LANES:(l2 + 1) * LANES] = blk
    slabs_per_block = S5_WT_BLOCK // LANES
    n_col_blocks = S5_L * LANES // S5_WT_BLOCK
    for cb in range(n_col_blocks):
        k_end = (cb + 1) * S5_WT_BLOCK
        y2 = jnp.dot(u2_ref[:, 0:k_end], wt_ref[0:k_end, cb * S5_WT_BLOCK:k_end],
                     preferred_element_type=F32)
        for n in range(slabs_per_block):
            l = cb * slabs_per_block + n
            u2f_ref[l] = y2[:, n * LANES:(n + 1) * LANES] + d_ref[...] * u2f_ref[l]
    expand(wsc_ref, wst_ref)

    tabs = [tab_ref[0, t] for t in range(TAB_ROWS)]
    m_tabs, (pr, pi, a8r, a8i, ar, ai), b_tabs = tabs[0:6], tabs[6:12], tabs[12:18]
    row = lax.broadcasted_iota(jnp.int32, (SUBLANES, ns), 0)

    def log_step_scan(xr, xi, t):
        for n, shift in enumerate((1, 2, 4)):
            tr, ti = t[2 * n], t[2 * n + 1]
            sr = pltpu.roll(xr, shift, axis=0)
            si = pltpu.roll(xi, shift, axis=0)
            xr, xi = xr + tr * sr - ti * si, xi + tr * si + ti * sr
        return xr, xi

    def shift_down(xr, xi, fr, fi):
        first = row == 0
        return (jnp.where(first, fr, pltpu.roll(xr, 1, axis=0)),
                jnp.where(first, fi, pltpu.roll(xi, 1, axis=0)))

    def rows_of(k):
        return pl.ds(pl.multiple_of(k * SUBLANES, SUBLANES), SUBLANES)

    def local_step(k, carry):
        sr, si = carry
        er = e_ref[rows_of(k), 0:ns]
        ei = e_ref[rows_of(k), ns:2 * ns]
        e_ref[rows_of(k), 0:ns] = sr
        e_ref[rows_of(k), ns:2 * ns] = si
        return ar * sr - ai * si + er, ar * si + ai * sr + ei

    zero = jnp.zeros((SUBLANES, ns), F32)
    ends = lax.fori_loop(0, seg_rows, local_step, (zero, zero), unroll=True)
    ends = log_step_scan(*ends, b_tabs)
    sfr_ref[...] = jnp.zeros_like(sfr_ref)
    sfi_ref[...] = jnp.zeros_like(sfi_ref)
    sfr_ref[n_streams:n_streams + 1, :] = ends[0][SUBLANES - 1:SUBLANES]
    sfi_ref[n_streams:n_streams + 1, :] = ends[1][SUBLANES - 1:SUBLANES]

    def correct_step(k, carry):
        cr, ci = carry
        e_ref[rows_of(k), 0:ns] += cr
        e_ref[rows_of(k), ns:2 * ns] += ci
        return ar * cr - ai * ci, ar * ci + ai * cr

    lax.fori_loop(0, seg_rows, correct_step, shift_down(*ends, zero, zero), unroll=True)

    for b in range(n_streams):
        rows = pl.ds(n_prompt_rows + b * SUBLANES, SUBLANES)
        cr = jnp.broadcast_to(s0r_ref[b:b + 1, :], (SUBLANES, ns))
        ci = jnp.broadcast_to(s0i_ref[b:b + 1, :], (SUBLANES, ns))
        xr, xi = log_step_scan(e_ref[rows, 0:ns], e_ref[rows, ns:2 * ns], m_tabs)
        st_r = xr + pr * cr - pi * ci
        st_i = xi + pr * ci + pi * cr
        e_ref[rows, 0:ns], e_ref[rows, ns:2 * ns] = shift_down(st_r, st_i, cr, ci)
        sfr_ref[b:b + 1, :] = st_r[SUBLANES - 1:SUBLANES]
        sfi_ref[b:b + 1, :] = st_i[SUBLANES - 1:SUBLANES]

    @pl.when(j >= 2)
    def _():
        for c in copies(j - 2, slot, False):
            c.wait()

    sprev = e_ref[...].astype(BF16)
    for cb in range(n_col_blocks):
        cols = slice(cb * S5_WT_BLOCK, (cb + 1) * S5_WT_BLOCK)
        y2 = lax.dot_general(sprev, wst_ref[cols, :], (((1,), (1,)), ((), ())),
                             preferred_element_type=F32)
        for n in range(slabs_per_block):
            l = cb * slabs_per_block + n
            yl = u2f_ref[l] + y2[:, n * LANES:(n + 1) * LANES]
            yp_ref[slot, l] = yl[0:n_prompt_rows].reshape(seg_rows, SUBLANES, LANES)
            ys_ref[slot, l] = yl[n_prompt_rows:]
    for c in copies(j, slot, False):
        c.start()

    @pl.when(j == nb - 1)
    def _():
        for c in copies(j - 1, 1 - slot, False) + copies(j, slot, False):
            c.wait()


def _s5_core(x, rstd, g, dall, wec, wsc, tab, s0_re, s0_im, d_skip, w_glu, *, n_prompt_rows):
    m_rows, d = x.shape
    nb = d // LANES
    assert nb >= 2
    glu_cols = w_glu.shape[1] // nb
    assert glu_cols % LANES == 0
    glu_blk = pl.BlockSpec((w_glu.shape[0], glu_cols), lambda j: (0, j))
    ns = S5_BLOCK_STATE
    k = S5_L * LANES
    n_rows = m_rows // S5_L
    n_streams = s0_re.shape[0]
    assert n_prompt_rows % (S5_L * SUBLANES) == 0
    assert (m_rows - n_prompt_rows) == n_streams * S5_L * SUBLANES
    n_prompt_mrows = n_prompt_rows // S5_L
    n_sample_mrows = n_rows - n_prompt_mrows
    seg_rows = n_prompt_mrows // SUBLANES
    sf_rows = n_streams + SUBLANES
    hbm = pl.BlockSpec(memory_space=pl.ANY)
    cspec = pl.BlockSpec((1, k, LANES), lambda j: (j, 0, 0))
    st = pl.BlockSpec((n_streams, ns), lambda j: (0, j))
    sf = pl.BlockSpec((sf_rows, ns), lambda j: (0, j))
    slab_p = pltpu.VMEM((2, S5_L, seg_rows, SUBLANES, LANES), F32)
    slab_s = pltpu.VMEM((2, S5_L, n_sample_mrows, LANES), F32)
    y, sf_re, sf_im, w_glu_bf16 = pl.pallas_call(
        functools.partial(_s5_core_kernel, n_prompt_mrows, n_streams),
        grid=(nb,),
        in_specs=[
            hbm,
            _const_spec((S5_L, n_rows, LANES)),
            pl.BlockSpec((1, LANES), lambda j: (0, j)),
            cspec, cspec, cspec,
            pl.BlockSpec((1, TAB_ROWS, SUBLANES, ns), lambda j: (j, 0, 0, 0)),
            st, st,
            pl.BlockSpec((1, LANES), lambda j: (0, j)),
            glu_blk,
        ],
        out_specs=[hbm, sf, sf, glu_blk],
        out_shape=[
            jax.ShapeDtypeStruct((n_rows, S5_L, d), F32),
            jax.ShapeDtypeStruct((sf_rows, nb * ns), F32),
            jax.ShapeDtypeStruct((sf_rows, nb * ns), F32),
            jax.ShapeDtypeStruct(w_glu.shape, BF16),
        ],
        scratch_shapes=[
            slab_p, slab_s, slab_p, slab_s,
            pltpu.SemaphoreType.DMA((2,)), pltpu.SemaphoreType.DMA((2,)),
            pltpu.VMEM((S5_L, n_rows, LANES), F32),
            pltpu.VMEM((n_rows, k), BF16),
            pltpu.VMEM((n_rows, 2 * ns), F32),
            pltpu.VMEM((k, k), BF16),
            pltpu.VMEM((k, 2 * ns), BF16),
            pltpu.VMEM((k, 2 * ns), BF16),
        ],
        compiler_params=pltpu.CompilerParams(
            dimension_semantics=("arbitrary",), vmem_limit_bytes=VMEM_LIMIT_BYTES),
        name="s5_core",
    )(x.reshape(n_rows, S5_L, d), rstd, g, dall, wec, wsc, tab, s0_re, s0_im, d_skip, w_glu)
    return y.reshape(m_rows, d), sf_re, sf_im, w_glu_bf16


def _slab_row_order(a, n_prompt_rows):
    c = a.shape[1]
    seg_rows = n_prompt_rows // (S5_L * SUBLANES)
    p = a[:n_prompt_rows].reshape(SUBLANES, seg_rows, S5_L, c).transpose(2, 1, 0, 3)
    s_ = a[n_prompt_rows:].reshape(-1, S5_L, c).transpose(1, 0, 2)
    return jnp.concatenate([p.reshape(S5_L, seg_rows * SUBLANES, c), s_], axis=1)


def _glu_kernel(n_col_blocks, x_ref, y_ref, w_ref, out_ref):
    d = x_ref.shape[1]
    a = _gelu(y_ref[...]).astype(BF16)
    cb = d // n_col_blocks
    for n in range(n_col_blocks):
        cols = slice(n * cb, (n + 1) * cb)
        val = jnp.dot(a, w_ref[:, n * cb:(n + 1) * cb], preferred_element_type=F32)
        gate = jnp.dot(a, w_ref[:, d + n * cb:d + (n + 1) * cb], preferred_element_type=F32)
        out_ref[:, cols] = x_ref[:, cols] + val * jax.nn.sigmoid(gate)


def _glu(x, y, w, *, tm, n_col_blocks=GLU_COL_BLOCKS):
    m_rows, d = x.shape
    assert m_rows % tm == 0 and d % n_col_blocks == 0
    tile = pl.BlockSpec((tm, d), lambda i: (i, 0))
    return pl.pallas_call(
        functools.partial(_glu_kernel, n_col_blocks),
        grid=(m_rows // tm,),
        in_specs=[tile, tile, _const_spec(w.shape)],
        out_specs=tile,
        out_shape=jax.ShapeDtypeStruct((m_rows, d), F32),
        compiler_params=pltpu.CompilerParams(
            dimension_semantics=("arbitrary",), vmem_limit_bytes=VMEM_LIMIT_BYTES),
        name="glu",
    )(x, y, w)


def _attention_tables(sinks):
    slopes = jnp.exp2(-8.0 * jnp.arange(1, N_HEADS + 1, dtype=F32) / N_HEADS)
    frame = jnp.arange(CHUNK, dtype=F32)[None, :]
    band = jnp.arange(BAND, dtype=F32)[:, None]
    dist = jnp.abs(frame - (band - WINDOW))
    bias = (dist[:, None, :] * slopes[None, :, None]).reshape(BAND, N_HEADS * CHUNK)
    sink_row = jnp.repeat(sinks.astype(F32), CHUNK).reshape(1, N_HEADS * CHUNK)
    return bias, sink_row


def _gm_tables(gm_ws, gm_b, d_gm):
    blk = jnp.arange(GM_CHUNK) // CHUNK
    w_prompt = jnp.where((blk[:, None] >= blk[None, :])[None], gm_ws, 0.0)
    top = gm_ws[:, :CHUNK, :CHUNK]
    zeros = jnp.zeros_like(top)
    w_sample = jnp.concatenate(
        [jnp.concatenate([top, zeros], axis=2), jnp.concatenate([zeros, top], axis=2)], axis=1)
    wsp = jnp.stack([w_prompt, w_sample])
    wsp = wsp.reshape(2, N_GM_GROUPS // 2, 2, GM_CHUNK, GM_CHUNK).transpose(0, 1, 3, 2, 4)
    wsp = wsp.reshape(2, N_GM_GROUPS // 2, GM_CHUNK, 2 * GM_CHUNK).astype(BF16)
    b_prompt = gm_b.T
    b_sample = jnp.concatenate([gm_b[:, :CHUNK].T, gm_b[:, :CHUNK].T], axis=0)
    gmb = jnp.stack([b_prompt, b_sample]).astype(F32)
    gmb = jnp.repeat(gmb, d_gm // N_GM_GROUPS, axis=2)
    return wsp, gmb


def _s5_compact(lam_re, lam_im, log_dt, b_re, b_im, c_re, c_im):
    n_groups = lam_re.shape[0]
    nb = n_groups // S5_LANE_GROUPS
    per_state = jnp.stack([lam_re, lam_im, jnp.broadcast_to(log_dt[:, None], lam_re.shape)]).astype(F32)
    per_state = per_state.reshape(3, nb, S5_LANE_GROUPS, S5_STATE).transpose(1, 0, 2, 3)
    lam = per_state.reshape(nb, 3, 1, S5_BLOCK_STATE)
    lam_c = jnp.broadcast_to(per_state[:, :, :, None, :],
                             (nb, 3, S5_LANE_GROUPS, S5_GROUP, S5_STATE)).reshape(nb, 3, LANES, S5_STATE)
    b_c = jnp.stack([b_re, b_im]).astype(F32).transpose(1, 0, 3, 2)
    b_c = b_c.reshape(nb, S5_LANE_GROUPS, 2, S5_GROUP, S5_STATE).transpose(0, 2, 1, 3, 4)
    c_c = jnp.stack([c_re, c_im]).astype(F32).transpose(1, 0, 2, 3)
    c_c = c_c.reshape(nb, S5_LANE_GROUPS, 2, S5_GROUP, S5_STATE).transpose(0, 2, 1, 3, 4)
    return (lam, lam_c, b_c.reshape(nb, 2, LANES, S5_STATE), c_c.reshape(nb, 2, LANES, S5_STATE))


def kernel(x_prompt, x_sample, cache_swa_k, cache_swa_v, state_s5_re, state_s5_im, norm_mix, norm_ffn, norm_final, w_in0, attn_sinks, gm_norm, gm_ws, gm_b, w_out0, s5_lam_re, s5_lam_im, s5_log_dt, s5_b_re, s5_b_im, s5_c_re, s5_c_im, s5_d, s5_w_glu, ffn_w_gate, ffn_w_up, ffn_w_down):
    batch, seq, d = x_prompt.shape
    dec_batch, dec_seq, _ = x_sample.shape
    assert batch == 1 and dec_seq == CHUNK and norm_mix.shape[0] == 2
    n_prompt = batch * seq
    n_sample = dec_batch * dec_seq
    d_gm = gm_norm.shape[-1]

    bias_tbl, sink_row = _attention_tables(attn_sinks[0])
    wsp, gmb = _gm_tables(gm_ws[0], gm_b[0], d_gm)
    x1, k_all, v_all, gvn = _mix0(
        x_prompt.reshape(n_prompt, d), x_sample.reshape(n_sample, d),
        cache_swa_k[0].reshape(dec_batch * WINDOW, D_KV), cache_swa_v[0].reshape(dec_batch * WINDOW, D_KV),
        norm_mix[0].reshape(1, d), w_in0[0].astype(BF16), bias_tbl, sink_row,
        gm_norm[0].reshape(1, d_gm), wsp, gmb, w_out0[0].astype(BF16), units=MIX_UNITS)

    x2, rstd2 = _ffn(x1, norm_ffn[0].reshape(1, d), ffn_w_gate, ffn_w_up, ffn_w_down,
                     norm_mix[1].reshape(1, d), layer=0, tm=FFN_TM, tf=FFN_TF, final=False)

    n_groups = s5_lam_re.shape[1]
    dall, wec, wsc, tab = _s5_prep(
        *_s5_compact(s5_lam_re[0], s5_lam_im[0], s5_log_dt[0], s5_b_re[0], s5_b_im[0], s5_c_re[0],
                     s5_c_im[0]),
        seg_rows=n_prompt // (S5_L * SUBLANES))
    ys5, sf_re, sf_im, w_glu_bf16 = _s5_core(
        x2, _slab_row_order(rstd2, n_prompt), norm_mix[1].reshape(1, d), dall, wec, wsc, tab,
        state_s5_re[0].reshape(dec_batch, n_groups * S5_STATE),
        state_s5_im[0].reshape(dec_batch, n_groups * S5_STATE), s5_d[0].reshape(1, d), s5_w_glu[0],
        n_prompt_rows=n_prompt)
    x3 = _glu(x2, ys5, w_glu_bf16, tm=GLU_TM)

    def last_ffn(row0, n_rows):
        (y,) = _ffn(x3, norm_ffn[1].reshape(1, d), ffn_w_gate, ffn_w_up, ffn_w_down,
                    norm_final.reshape(1, d), layer=1, tm=FFN_TM, tf=FFN_TF, final=True,
                    row0=row0, n_rows=n_rows)
        return y

    y_prompt = last_ffn(0, n_prompt)
    y_sample = last_ffn(n_prompt, n_sample)

    keep = min(WINDOW, seq)
    y_prompt = y_prompt.reshape(batch, seq, d)
    y_sample = y_sample.reshape(dec_batch, dec_seq, d)
    kv_shape_p = (1, batch, keep, N_KV_HEADS, HEAD_DIM)
    kv_shape_s = (1, dec_batch, dec_seq, N_KV_HEADS, HEAD_DIM)
    st_p = (1, batch, n_groups, S5_STATE)
    st_s = (1, dec_batch, n_groups, S5_STATE)
    return (y_prompt, y_sample,
            k_all[n_prompt - keep:n_prompt].reshape(kv_shape_p),
            v_all[n_prompt - keep:n_prompt].reshape(kv_shape_p),
            k_all[n_prompt:].reshape(kv_shape_s),
            v_all[n_prompt:].reshape(kv_shape_s),
            gvn.reshape(1, dec_batch, dec_seq, d_gm),
            sf_re[dec_batch].reshape(st_p), sf_im[dec_batch].reshape(st_p),
            sf_re[:dec_batch].reshape(st_s), sf_im[:dec_batch].reshape(st_s))
```

```python
import functools
import math

import jax
import jax.numpy as jnp
from jax import lax
from jax.experimental import pallas as pl
from jax.experimental.pallas import tpu as pltpu

F32 = jnp.float32
BF16 = jnp.bfloat16

CHUNK = 64
HEAD_DIM = 64
N_HEADS = 16
N_KV_HEADS = 2
Q_PER_KV = N_HEADS // N_KV_HEADS
WINDOW = 128
BAND = WINDOW + CHUNK
D_ATTN = N_HEADS * HEAD_DIM
D_KV = N_KV_HEADS * HEAD_DIM
GM_CHUNK = 128
N_GM_GROUPS = 16
S5_GROUP = 16
S5_STATE = 64
RMS_EPS = 1e-5
NEG_INF = -1e30

LANES = 128
SUBLANES = 8
MXU_TILE = 256
VMEM_LIMIT_BYTES = 56 * 1024 * 1024

MIX_UNITS = 2
FFN_TM = 1024
FFN_TF = 256
FFN_BLOCKS_PER_ITER = 2
GLU_TM = 512
GLU_COL_BLOCKS = 8

S5_L = SUBLANES
S5_LANE_GROUPS = LANES // S5_GROUP
S5_BLOCK_STATE = S5_LANE_GROUPS * S5_STATE
S5_WT_BLOCK = MXU_TILE


def _gelu(x):
    return 0.5 * x * (1.0 + lax.erf(x * math.sqrt(0.5)))


def _rms_scale(x):
    return x * lax.rsqrt(jnp.mean(x * x, axis=-1, keepdims=True) + RMS_EPS)


def _const_spec(shape):
    zeros = (0,) * len(shape)
    return pl.BlockSpec(shape, lambda *_: zeros, pipeline_mode=pl.Buffered(1))


def _mix0_kernel(n_prompt_tiles, units,
                 xp_ref, xs_ref, ck_ref, cv_ref, g_ref, win_ref, bias_ref, sink_ref, gmn_ref,
                 wsp_ref, gmb_ref, wout_ref,
                 x1_ref, k_ref, v_ref, gvn_ref,
                 z_ref, q_ref, ocat_ref, kprev_ref, vprev_ref):
    i = pl.program_id(0)
    is_sample = i >= n_prompt_tiles
    tm = units * GM_CHUNK
    d_gm = gmn_ref.shape[-1]
    off_k = D_ATTN
    off_v = D_ATTN + D_KV
    off_gu = D_ATTN + 2 * D_KV
    off_gv = off_gu + d_gm
    n_q = Q_PER_KV * CHUNK

    @pl.when(i == 0)
    def _():
        kprev_ref[...] = jnp.zeros_like(kprev_ref)
        vprev_ref[...] = jnp.zeros_like(vprev_ref)

    x = jnp.where(is_sample, xs_ref[...], xp_ref[...])
    h = (_rms_scale(x) * g_ref[...]).astype(BF16)
    z_ref[...] = jnp.dot(h, win_ref[...], preferred_element_type=F32)

    q_ref[...] = (z_ref[:, 0:D_ATTN] * (HEAD_DIM ** -0.5)).astype(BF16)
    k = z_ref[:, off_k:off_k + D_KV]
    v = z_ref[:, off_v:off_v + D_KV]
    k_ref[...] = k
    v_ref[...] = v

    def lane_lo(rows):
        return lax.broadcasted_iota(jnp.int32, (rows, LANES), 1) < HEAD_DIM

    def replicate(a):
        r = pltpu.roll(a, HEAD_DIM, axis=1)
        lo = lane_lo(a.shape[0])
        return jnp.where(lo, a, r).astype(BF16), jnp.where(lo, r, a).astype(BF16)

    k_rep = replicate(k)
    v_rep = replicate(v)
    ck_rep = replicate(ck_ref[...])
    cv_rep = replicate(cv_ref[...])

    lo64 = lane_lo(CHUNK)
    band_pos = lax.broadcasted_iota(jnp.int32, (BAND, n_q), 0)
    chunks_per_tile = tm // CHUNK

    def scores(u, c2, kv):
        r0 = u * GM_CHUNK + c2 * CHUNK
        stream = r0 // CHUNK
        chunk_index = i * chunks_per_tile + stream
        valid_from = jnp.where(is_sample, 0, jnp.maximum(WINDOW - CHUNK * chunk_index, 0))
        valid = band_pos >= valid_from

        def band(cur, prev_ref, cached):
            if u == 0:
                prev_unit = prev_ref[kv]
            else:
                prev_unit = cur[(u - 1) * GM_CHUNK:u * GM_CHUNK]
            if c2 == 0:
                prompt_prev = prev_unit
            else:
                prompt_prev = jnp.concatenate(
                    [prev_unit[CHUNK:], cur[u * GM_CHUNK:u * GM_CHUNK + CHUNK]], axis=0)
            sample_prev = cached[stream * WINDOW:(stream + 1) * WINDOW]
            prev = jnp.where(is_sample, sample_prev, prompt_prev)
            return jnp.concatenate([prev, cur[r0:r0 + CHUNK]], axis=0)

        kb = band(k_rep[kv], kprev_ref, ck_rep[kv])
        vb = band(v_rep[kv], vprev_ref, cv_rep[kv])

        pieces = []
        for m in range(Q_PER_KV // 2):
            c0 = kv * Q_PER_KV * HEAD_DIM + m * LANES
            qp = q_ref[r0:r0 + CHUNK, c0:c0 + LANES]
            pieces.append(jnp.where(lo64, qp, jnp.zeros_like(qp)))
            pieces.append(jnp.where(lo64, jnp.zeros_like(qp), qp))
        qs = jnp.concatenate(pieces, axis=0)
        cols = slice(kv * n_q, (kv + 1) * n_q)
        st = lax.dot_general(kb, qs, (((1,), (1,)), ((), ())),
                             preferred_element_type=F32)
        return jnp.where(valid, st - bias_ref[:, cols], NEG_INF), vb

    def attend(u, c2, kv, st, vb):
        r0 = u * GM_CHUNK + c2 * CHUNK
        cols = slice(kv * n_q, (kv + 1) * n_q)
        sink = sink_ref[:, cols]
        mx = jnp.maximum(jnp.max(st, axis=0, keepdims=True), sink)
        p = jnp.exp(st - mx)
        denom = jnp.sum(p, axis=0, keepdims=True) + jnp.exp(sink - mx)
        pn = (p * (1.0 / denom)).astype(BF16)
        o = lax.dot_general(pn, vb, (((0,), (0,)), ((), ())),
                            preferred_element_type=F32)
        for m in range(Q_PER_KV // 2):
            o_pair = jnp.where(lo64, o[(2 * m) * CHUNK:(2 * m + 1) * CHUNK],
                               o[(2 * m + 1) * CHUNK:(2 * m + 2) * CHUNK])
            c0 = kv * Q_PER_KV * HEAD_DIM + m * LANES
            ocat_ref[r0:r0 + CHUNK, c0:c0 + LANES] = o_pair.astype(BF16)

    blocks = [(u, c2, kv) for u in range(units) for c2 in range(GM_CHUNK // CHUNK)
              for kv in range(N_KV_HEADS)]
    scored = [scores(*blk) for blk in blocks]
    for blk, (st, vb) in zip(blocks, scored):
        attend(*blk, st, vb)

    lo128 = lane_lo(GM_CHUNK)
    for u in range(units):
        rows = slice(u * GM_CHUNK, (u + 1) * GM_CHUNK)
        ua = _gelu(z_ref[rows, off_gu:off_gu + d_gm])
        gvn = _rms_scale(_gelu(z_ref[rows, off_gv:off_gv + d_gm])) * gmn_ref[...]
        gvn_ref[rows, :] = gvn
        gb = gvn.astype(BF16)
        for m in range(N_GM_GROUPS // 2):
            cols = slice(m * LANES, (m + 1) * LANES)
            rhs = gb[:, cols]
            rhs2 = jnp.concatenate([jnp.where(lo128, rhs, jnp.zeros_like(rhs)),
                                    jnp.where(lo128, jnp.zeros_like(rhs), rhs)], axis=0)
            sp = jnp.dot(wsp_ref[0, m], rhs2, preferred_element_type=F32) + gmb_ref[0, :, cols]
            ocat_ref[rows, D_ATTN + m * LANES:D_ATTN + (m + 1) * LANES] = (ua[:, cols] * sp).astype(BF16)

    x1_ref[...] = x + jnp.dot(ocat_ref[...], wout_ref[...], preferred_element_type=F32)

    for kv in range(N_KV_HEADS):
        kprev_ref[kv] = k_rep[kv][tm - GM_CHUNK:]
        vprev_ref[kv] = v_rep[kv][tm - GM_CHUNK:]


def _mix0(xp, xs, cache_k, cache_v, g, w_in, bias_tbl, sink_row, gm_norm, wsp, gmb, w_out, *, units):
    n_prompt_rows, d = xp.shape
    n_sample_rows = xs.shape[0]
    m_rows = n_prompt_rows + n_sample_rows
    tm = units * GM_CHUNK
    assert n_prompt_rows % tm == 0 and n_sample_rows % tm == 0
    n_tiles = m_rows // tm
    n_prompt_tiles = n_prompt_rows // tm
    d_in = w_in.shape[1]
    d_gm = gm_norm.shape[-1]
    cache_rows = (tm // CHUNK) * WINDOW

    def prompt_block(i):
        return jnp.minimum(i, n_prompt_tiles - 1)

    def sample_block(i):
        return jnp.maximum(i - n_prompt_tiles, 0)

    def kind(i):
        return jnp.where(i >= n_prompt_tiles, 1, 0)

    in_specs = [
        pl.BlockSpec((tm, d), lambda i: (prompt_block(i), 0)),
        pl.BlockSpec((tm, d), lambda i: (sample_block(i), 0)),
        pl.BlockSpec((cache_rows, D_KV), lambda i: (sample_block(i), 0)),
        pl.BlockSpec((cache_rows, D_KV), lambda i: (sample_block(i), 0)),
        _const_spec((1, d)),
        _const_spec((d, d_in)),
        _const_spec(bias_tbl.shape),
        _const_spec(sink_row.shape),
        _const_spec((1, d_gm)),
        pl.BlockSpec((1,) + wsp.shape[1:], lambda i: (kind(i), 0, 0, 0)),
        pl.BlockSpec((1, GM_CHUNK, d_gm), lambda i: (kind(i), 0, 0)),
        _const_spec(w_out.shape),
    ]
    out_specs = [
        pl.BlockSpec((tm, d), lambda i: (i, 0)),
        pl.BlockSpec((tm, D_KV), lambda i: (i, 0)),
        pl.BlockSpec((tm, D_KV), lambda i: (i, 0)),
        pl.BlockSpec((tm, d_gm), lambda i: (sample_block(i), 0)),
    ]
    out_shape = [
        jax.ShapeDtypeStruct((m_rows, d), F32),
        jax.ShapeDtypeStruct((m_rows, D_KV), F32),
        jax.ShapeDtypeStruct((m_rows, D_KV), F32),
        jax.ShapeDtypeStruct((n_sample_rows, d_gm), F32),
    ]
    scratch = [
        pltpu.VMEM((tm, d_in), F32),
        pltpu.VMEM((tm, D_ATTN), BF16),
        pltpu.VMEM((tm, D_ATTN + d_gm), BF16),
        pltpu.VMEM((N_KV_HEADS, GM_CHUNK, LANES), BF16),
        pltpu.VMEM((N_KV_HEADS, GM_CHUNK, LANES), BF16),
    ]
    return pl.pallas_call(
        functools.partial(_mix0_kernel, n_prompt_tiles, units),
        grid=(n_tiles,),
        in_specs=in_specs,
        out_specs=out_specs,
        out_shape=out_shape,
        scratch_shapes=scratch,
        compiler_params=pltpu.CompilerParams(
            dimension_semantics=("arbitrary",), vmem_limit_bytes=VMEM_LIMIT_BYTES),
        name="mix0",
    )(xp, xs, cache_k, cache_v, g, w_in, bias_tbl, sink_row, gm_norm, wsp, gmb, w_out)


def _ffn_kernel(final, layer, tf, x_ref, g_ref, wg_hbm, wu_hbm, wd_hbm, gnext_ref, out_ref, *rest):
    h_ref, wg_buf, wu_buf, wd_buf, sem = rest[-5:]
    i = pl.program_id(0)
    n_tiles = pl.num_programs(0)
    n_blocks = wg_hbm.shape[2] // tf

    def copies(block, slot):
        cols = pl.ds(pl.multiple_of(block * tf, tf), tf)
        return (pltpu.make_async_copy(wg_hbm.at[layer, :, cols], wg_buf.at[slot], sem.at[slot, 0]),
                pltpu.make_async_copy(wu_hbm.at[layer, :, cols], wu_buf.at[slot], sem.at[slot, 1]),
                pltpu.make_async_copy(wd_hbm.at[layer, cols, :], wd_buf.at[slot], sem.at[slot, 2]))

    @pl.when(i == 0)
    def _():
        for c in copies(0, 0):
            c.start()

    x = x_ref[...]
    h_ref[...] = (_rms_scale(x) * g_ref[...]).astype(BF16)
    out_ref[...] = x

    def iteration(it, carry):
        for n in range(FFN_BLOCKS_PER_ITER):
            block = it * FFN_BLOCKS_PER_ITER + n
            slot = n % 2
            nxt = block + 1

            @pl.when(nxt < n_blocks)
            def _():
                for c in copies(nxt, 1 - slot):
                    c.start()

            @pl.when(jnp.logical_and(nxt == n_blocks, i + 1 < n_tiles))
            def _():
                for c in copies(0, 1 - slot):
                    c.start()

            for c in copies(block, slot):
                c.wait()
            h = h_ref[...]
            gate = jnp.dot(h, wg_buf[slot].astype(BF16), preferred_element_type=F32)
            up = jnp.dot(h, wu_buf[slot].astype(BF16), preferred_element_type=F32)
            act = (gate * jax.nn.sigmoid(gate) * up).astype(BF16)
            out_ref[...] += jnp.dot(act, wd_buf[slot].astype(BF16), preferred_element_type=F32)
        return carry

    lax.fori_loop(0, n_blocks // FFN_BLOCKS_PER_ITER, iteration, 0)

    out = out_ref[...]
    scale = lax.rsqrt(jnp.mean(out * out, axis=-1, keepdims=True) + RMS_EPS)
    if final:
        out_ref[...] = out * scale * gnext_ref[...]
    else:
        rest[0][...] = jnp.broadcast_to(scale, rest[0].shape)


def _ffn(x, g, wg, wu, wd, gnext, *, layer, tm, tf, final, row0=0, n_rows=None):
    d = x.shape[1]
    n_rows = x.shape[0] if n_rows is None else n_rows
    f = wg.shape[2]
    n_blocks = f // tf
    assert n_rows % tm == 0 and row0 % tm == 0 and f % tf == 0
    assert n_blocks % FFN_BLOCKS_PER_ITER == 0 and FFN_BLOCKS_PER_ITER % 2 == 0
    tile0 = row0 // tm
    hbm = pl.BlockSpec(memory_space=pl.ANY)
    out_specs = [pl.BlockSpec((tm, d), lambda i: (i, 0))]
    out_shape = [jax.ShapeDtypeStruct((n_rows, d), F32)]
    if not final:
        out_specs.append(pl.BlockSpec((tm, LANES), lambda i: (i, 0)))
        out_shape.append(jax.ShapeDtypeStruct((n_rows, LANES), F32))
    return pl.pallas_call(
        functools.partial(_ffn_kernel, final, layer, tf),
        grid=(n_rows // tm,),
        in_specs=[
            pl.BlockSpec((tm, d), lambda i: (i + tile0, 0)),
            pl.BlockSpec((1, d), lambda i: (0, 0)),
            hbm, hbm, hbm,
            pl.BlockSpec((1, d), lambda i: (0, 0)),
        ],
        out_specs=out_specs,
        out_shape=out_shape,
        scratch_shapes=[
            pltpu.VMEM((tm, d), BF16),
            pltpu.VMEM((2, d, tf), F32),
            pltpu.VMEM((2, d, tf), F32),
            pltpu.VMEM((2, tf, d), F32),
            pltpu.SemaphoreType.DMA((2, 3)),
        ],
        compiler_params=pltpu.CompilerParams(
            dimension_semantics=("arbitrary",), vmem_limit_bytes=VMEM_LIMIT_BYTES),
        name="ffn",
    )(x, g, wg, wu, wd, gnext)


TAB_ROWS = 18


def _discretize(lr, li, log_dt):
    dt = jnp.exp(log_dt)
    mag = jnp.exp(lr * dt)
    ar = mag * jnp.cos(li * dt)
    ai = mag * jnp.sin(li * dt)
    den = lr * lr + li * li
    nr = ar - 1.0
    return ar, ai, (nr * lr + ai * li) / den, (ai * lr - nr * li) / den


def _cmul(pr, pi, qr, qi):
    return pr * qr - pi * qi, pr * qi + pi * qr


def _s5_prep_kernel(seg_rows, lam_ref, lamc_ref, b_ref, c_ref, dall_ref, wec_ref, wsc_ref, tab_ref,
                    bhi_ref, blo_ref):
    ns = S5_BLOCK_STATE

    ar, ai, fr, fi = _discretize(lamc_ref[0, 0], lamc_ref[0, 1], lamc_ref[0, 2])
    bbr, bbi = _cmul(fr, fi, b_ref[0, 0], b_ref[0, 1])
    c_re = c_ref[0, 0]
    c_im = c_ref[0, 1]

    def split(a):
        hi = a.astype(BF16)
        return hi, (a - hi.astype(F32)).astype(BF16)

    power = (jnp.ones_like(ar), jnp.zeros_like(ar))
    for l in range(S5_L):
        bkr, bki = _cmul(*power, bbr, bbi)
        power = _cmul(*power, ar, ai)
        rows = slice(l * LANES, (l + 1) * LANES)
        bk = jnp.concatenate([bkr, bki], axis=1)
        bhi_ref[rows, :], blo_ref[rows, :] = split(bk)
        wec_ref[0, (S5_L - 1 - l) * LANES:(S5_L - l) * LANES, :] = bk
        qr, qi = power
        wsc_ref[0, rows, :] = jnp.concatenate([c_re * qr - c_im * qi, -c_re * qi - c_im * qr], axis=1)

    c_hi, c_lo = split(jnp.concatenate([c_re, -c_im], axis=1))
    c_parts = jnp.concatenate([c_hi, c_lo], axis=0)
    dims = (((1,), (1,)), ((), ()))
    d_hi = lax.dot_general(bhi_ref[...], c_parts, dims, preferred_element_type=F32)
    d_lo = lax.dot_general(blo_ref[...], c_parts, dims, preferred_element_type=F32)
    d_all = d_hi[:, 0:LANES] + d_hi[:, LANES:2 * LANES] + d_lo[:, 0:LANES]
    shape = (S5_L * LANES, LANES)
    in_group = (lax.broadcasted_iota(jnp.int32, shape, 0) % LANES) // S5_GROUP
    out_group = lax.broadcasted_iota(jnp.int32, shape, 1) // S5_GROUP
    dall_ref[0] = jnp.where(in_group == out_group, d_all, 0.0).astype(BF16)

    ar, ai, _, _ = _discretize(lam_ref[0, 0], lam_ref[0, 1], lam_ref[0, 2])
    row = lax.broadcasted_iota(jnp.int32, (SUBLANES, ns), 0)

    def bcast(a):
        return jnp.broadcast_to(a, (SUBLANES, ns))

    def log_step_tables(t0, base):
        cur = base
        for n, shift in enumerate((1, 2, 4)):
            tab_ref[0, t0 + 2 * n] = jnp.where(row >= shift, bcast(cur[0]), 0.0)
            tab_ref[0, t0 + 2 * n + 1] = jnp.where(row >= shift, bcast(cur[1]), 0.0)
            cur = _cmul(*cur, *cur)
        return cur

    a1 = (ar, ai)
    for _ in range(S5_L.bit_length() - 1):
        a1 = _cmul(*a1, *a1)
    a8 = log_step_tables(0, a1)
    pr_tab = jnp.zeros((SUBLANES, ns), F32)
    pi_tab = jnp.zeros((SUBLANES, ns), F32)
    cur = a1
    for r in range(SUBLANES):
        pr_tab = jnp.where(row == r, bcast(cur[0]), pr_tab)
        pi_tab = jnp.where(row == r, bcast(cur[1]), pi_tab)
        cur = _cmul(*cur, *a1)
    tab_ref[0, 6] = pr_tab
    tab_ref[0, 7] = pi_tab
    tab_ref[0, 8] = bcast(a8[0])
    tab_ref[0, 9] = bcast(a8[1])
    tab_ref[0, 10] = bcast(a1[0])
    tab_ref[0, 11] = bcast(a1[1])
    seg = a1
    for _ in range(seg_rows.bit_length() - 1):
        seg = _cmul(*seg, *seg)
    log_step_tables(12, seg)


def _s5_prep(lam, lam_c, b_c, c_c, *, seg_rows):
    assert seg_rows & (seg_rows - 1) == 0 and S5_L & (S5_L - 1) == 0
    nb = lam.shape[0]
    ns = S5_BLOCK_STATE
    k = S5_L * LANES
    cspec = pl.BlockSpec((1, k, LANES), lambda j: (j, 0, 0))

    def whole(a):
        return pl.BlockSpec((1,) + a.shape[1:], lambda j: (j,) + (0,) * (a.ndim - 1))

    return pl.pallas_call(
        functools.partial(_s5_prep_kernel, seg_rows),
        grid=(nb,),
        in_specs=[whole(lam), whole(lam_c), whole(b_c), whole(c_c)],
        out_specs=[cspec, cspec, cspec,
                   pl.BlockSpec((1, TAB_ROWS, SUBLANES, ns), lambda j: (j, 0, 0, 0))],
        out_shape=[
            jax.ShapeDtypeStruct((nb, k, LANES), BF16),
            jax.ShapeDtypeStruct((nb, k, LANES), F32),
            jax.ShapeDtypeStruct((nb, k, LANES), F32),
            jax.ShapeDtypeStruct((nb, TAB_ROWS, SUBLANES, ns), F32),
        ],
        scratch_shapes=[pltpu.VMEM((k, LANES), BF16), pltpu.VMEM((k, LANES), BF16)],
        compiler_params=pltpu.CompilerParams(
            dimension_semantics=("arbitrary",), vmem_limit_bytes=VMEM_LIMIT_BYTES),
        name="s5_prep",
    )(lam, lam_c, b_c, c_c)


def _s5_core_kernel(n_prompt_rows, n_streams,
                    x_hbm, rstd_ref, g_ref, dall_ref, wec_ref, wsc_ref, tab_ref, s0r_ref, s0i_ref,
                    d_ref, wglu_ref,
                    y_hbm, sfr_ref, sfi_ref, wglu_bf16_ref,
                    xp_ref, xs_ref, yp_ref, ys_ref, in_sem, out_sem,
                    u2f_ref, u2_ref, e_ref, wt_ref, we_ref, wst_ref):
    wglu_bf16_ref[...] = wglu_ref[...].astype(BF16)
    ns = S5_BLOCK_STATE
    j = pl.program_id(0)
    nb = pl.num_programs(0)
    slot = j % 2
    n_rows = x_hbm.shape[0]
    n_sample_rows = n_rows - n_prompt_rows
    seg_rows = n_prompt_rows // SUBLANES

    def copies(block, slot, load):
        lanes = pl.ds(pl.multiple_of(block * LANES, LANES), LANES)
        out = []
        for l in range(S5_L):
            pairs = [(pl.ds(seg * seg_rows, seg_rows), (xp_ref if load else yp_ref).at[slot, l, :, seg, :])
                     for seg in range(SUBLANES)]
            pairs.append((pl.ds(n_prompt_rows, n_sample_rows), (xs_ref if load else ys_ref).at[slot, l]))
            for rows, buf in pairs:
                if load:
                    out.append(pltpu.make_async_copy(x_hbm.at[rows, l, lanes], buf, in_sem.at[slot]))
                else:
                    out.append(pltpu.make_async_copy(buf, y_hbm.at[rows, l, lanes], out_sem.at[slot]))
        return out

    def start_all(cs):
        for n, c in enumerate(cs):
            c.start(priority=n % 2)

    @pl.when(j == 0)
    def _():
        start_all(copies(0, 0, True))

    @pl.when(j + 1 < nb)
    def _():
        start_all(copies(j + 1, 1 - slot, True))

    for c in copies(j, slot, True):
        c.wait()


    shape = (S5_L * LANES, LANES)
    lane = lax.broadcasted_iota(jnp.int32, shape, 1)
    lane_lo = lane < S5_STATE
    row_group = (lax.broadcasted_iota(jnp.int32, shape, 0) % LANES) // S5_GROUP
    groups_per_vreg = LANES // S5_STATE

    def expand(compact_ref, full_ref):
        x = compact_ref[0]
        r = pltpu.roll(x, S5_STATE, axis=1)
        for part, rep in enumerate((jnp.where(lane_lo, x, r), jnp.where(lane_lo, r, x))):
            for v in range(S5_LANE_GROUPS // groups_per_vreg):
                own = row_group == groups_per_vreg * v + (lane // S5_STATE)
                col = part * S5_BLOCK_STATE + v * LANES
                full_ref[:, col:col + LANES] = jnp.where(own, rep, 0.0).astype(BF16)

    expand(wec_ref, we_ref)

    for l in range(S5_L):
        xl = xp_ref[slot, l].reshape(n_prompt_rows, LANES)
        u2f_ref[l, 0:n_prompt_rows, :] = xl * rstd_ref[l, 0:n_prompt_rows, :] * g_ref[...]
        u2f_ref[l, n_prompt_rows:, :] = xs_ref[slot, l] * rstd_ref[l, n_prompt_rows:, :] * g_ref[...]
    for l in range(S5_L):
        u2_ref[:, l * LANES:(l + 1) * LANES] = u2f_ref[l].astype(BF16)
    e_ref[...] = jnp.dot(u2_ref[...], we_ref[...], preferred_element_type=F32)

    zero_blk = jnp.zeros((LANES, LANES), BF16)
    for l in range(S5_L):
        for l2 in range(S5_L):
            blk = dall_ref[0, (l2 - l) * LANES:(l2 - l + 1) * LANES, :] if l2 >= l else zero_blk
            wt_ref[l * LANES:(l + 1) * LANES, l2 * LANES:(l2 + 1) * LANES] = blk
    slabs_per_block = S5_WT_BLOCK // LANES
    n_col_blocks = S5_L * LANES // S5_WT_BLOCK
    for cb in range(n_col_blocks):
        k_end = (cb + 1) * S5_WT_BLOCK
        y2 = jnp.dot(u2_ref[:, 0:k_end], wt_ref[0:k_end, cb * S5_WT_BLOCK:k_end],
                     preferred_element_type=F32)
        for n in range(slabs_per_block):
            l = cb * slabs_per_block + n
            u2f_ref[l] = y2[:, n * LANES:(n + 1) * LANES] + d_ref[...] * u2f_ref[l]
    expand(wsc_ref, wst_ref)

    tabs = [tab_ref[0, t] for t in range(TAB_ROWS)]
    m_tabs, (pr, pi, a8r, a8i, ar, ai), b_tabs = tabs[0:6], tabs[6:12], tabs[12:18]
    row = lax.broadcasted_iota(jnp.int32, (SUBLANES, ns), 0)

    def log_step_scan(xr, xi, t):
        for n, shift in enumerate((1, 2, 4)):
            tr, ti = t[2 * n], t[2 * n + 1]
            sr = pltpu.roll(xr, shift, axis=0)
            si = pltpu.roll(xi, shift, axis=0)
            xr, xi = xr + tr * sr - ti * si, xi + tr * si + ti * sr
        return xr, xi

    def shift_down(xr, xi, fr, fi):
        first = row == 0
        return (jnp.where(first, fr, pltpu.roll(xr, 1, axis=0)),
                jnp.where(first, fi, pltpu.roll(xi, 1, axis=0)))

    def rows_of(k):
        return pl.ds(pl.multiple_of(k * SUBLANES, SUBLANES), SUBLANES)

    def local_step(k, carry):
        sr, si = carry
        er = e_ref[rows_of(k), 0:ns]
        ei = e_ref[rows_of(k), ns:2 * ns]
        e_ref[rows_of(k), 0:ns] = sr
        e_ref[rows_of(k), ns:2 * ns] = si
        return ar * sr - ai * si + er, ar * si + ai * sr + ei

    zero = jnp.zeros((SUBLANES, ns), F32)
    ends = lax.fori_loop(0, seg_rows, local_step, (zero, zero), unroll=True)
    ends = log_step_scan(*ends, b_tabs)
    sfr_ref[...] = jnp.zeros_like(sfr_ref)
    sfi_ref[...] = jnp.zeros_like(sfi_ref)
    sfr_ref[n_streams:n_streams + 1, :] = ends[0][SUBLANES - 1:SUBLANES]
    sfi_ref[n_streams:n_streams + 1, :] = ends[1][SUBLANES - 1:SUBLANES]

    def correct_step(k, carry):
        cr, ci = carry
        e_ref[rows_of(k), 0:ns] += cr
        e_ref[rows_of(k), ns:2 * ns] += ci
        return ar * cr - ai * ci, ar * ci + ai * cr

    lax.fori_loop(0, seg_rows, correct_step, shift_down(*ends, zero, zero), unroll=True)

    for b in range(n_streams):
        rows = pl.ds(n_prompt_rows + b * SUBLANES, SUBLANES)
        cr = jnp.broadcast_to(s0r_ref[b:b + 1, :], (SUBLANES, ns))
        ci = jnp.broadcast_to(s0i_ref[b:b + 1, :], (SUBLANES, ns))
        xr, xi = log_step_scan(e_ref[rows, 0:ns], e_ref[rows, ns:2 * ns], m_tabs)
        st_r = xr + pr * cr - pi * ci
        st_i = xi + pr * ci + pi * cr
        e_ref[rows, 0:ns], e_ref[rows, ns:2 * ns] = shift_down(st_r, st_i, cr, ci)
        sfr_ref[b:b + 1, :] = st_r[SUBLANES - 1:SUBLANES]
        sfi_ref[b:b + 1, :] = st_i[SUBLANES - 1:SUBLANES]

    @pl.when(j >= 2)
    def _():
        for c in copies(j - 2, slot, False):
            c.wait()

    sprev = e_ref[...].astype(BF16)
    for cb in range(n_col_blocks):
        cols = slice(cb * S5_WT_BLOCK, (cb + 1) * S5_WT_BLOCK)
        y2 = lax.dot_general(sprev, wst_ref[cols, :], (((1,), (1,)), ((), ())),
                             preferred_element_type=F32)
        for n in range(slabs_per_block):
            l = cb * slabs_per_block + n
            yl = u2f_ref[l] + y2[:, n * LANES:(n + 1) * LANES]
            yp_ref[slot, l] = yl[0:n_prompt_rows].reshape(seg_rows, SUBLANES, LANES)
            ys_ref[slot, l] = yl[n_prompt_rows:]
    start_all(copies(j, slot, False))

    @pl.when(j == nb - 1)
    def _():
        for c in copies(j - 1, 1 - slot, False) + copies(j, slot, False):
            c.wait()


def _s5_core(x, rstd, g, dall, wec, wsc, tab, s0_re, s0_im, d_skip, w_glu, *, n_prompt_rows):
    m_rows, d = x.shape
    nb = d // LANES
    assert nb >= 2
    glu_cols = w_glu.shape[1] // nb
    assert glu_cols % LANES == 0
    glu_blk = pl.BlockSpec((w_glu.shape[0], glu_cols), lambda j: (0, j))
    ns = S5_BLOCK_STATE
    k = S5_L * LANES
    n_rows = m_rows // S5_L
    n_streams = s0_re.shape[0]
    assert n_prompt_rows % (S5_L * SUBLANES) == 0
    assert (m_rows - n_prompt_rows) == n_streams * S5_L * SUBLANES
    n_prompt_mrows = n_prompt_rows // S5_L
    n_sample_mrows = n_rows - n_prompt_mrows
    seg_rows = n_prompt_mrows // SUBLANES
    sf_rows = n_streams + SUBLANES
    hbm = pl.BlockSpec(memory_space=pl.ANY)
    cspec = pl.BlockSpec((1, k, LANES), lambda j: (j, 0, 0))
    st = pl.BlockSpec((n_streams, ns), lambda j: (0, j))
    sf = pl.BlockSpec((sf_rows, ns), lambda j: (0, j))
    slab_p = pltpu.VMEM((2, S5_L, seg_rows, SUBLANES, LANES), F32)
    slab_s = pltpu.VMEM((2, S5_L, n_sample_mrows, LANES), F32)
    y, sf_re, sf_im, w_glu_bf16 = pl.pallas_call(
        functools.partial(_s5_core_kernel, n_prompt_mrows, n_streams),
        grid=(nb,),
        in_specs=[
            hbm,
            _const_spec((S5_L, n_rows, LANES)),
            pl.BlockSpec((1, LANES), lambda j: (0, j)),
            cspec, cspec, cspec,
            pl.BlockSpec((1, TAB_ROWS, SUBLANES, ns), lambda j: (j, 0, 0, 0)),
            st, st,
            pl.BlockSpec((1, LANES), lambda j: (0, j)),
            glu_blk,
        ],
        out_specs=[hbm, sf, sf, glu_blk],
        out_shape=[
            jax.ShapeDtypeStruct((n_rows, S5_L, d), F32),
            jax.ShapeDtypeStruct((sf_rows, nb * ns), F32),
            jax.ShapeDtypeStruct((sf_rows, nb * ns), F32),
            jax.ShapeDtypeStruct(w_glu.shape, BF16),
        ],
        scratch_shapes=[
            slab_p, slab_s, slab_p, slab_s,
            pltpu.SemaphoreType.DMA((2,)), pltpu.SemaphoreType.DMA((2,)),
            pltpu.VMEM((S5_L, n_rows, LANES), F32),
            pltpu.VMEM((n_rows, k), BF16),
            pltpu.VMEM((n_rows, 2 * ns), F32),
            pltpu.VMEM((k, k), BF16),
            pltpu.VMEM((k, 2 * ns), BF16),
            pltpu.VMEM((k, 2 * ns), BF16),
        ],
        compiler_params=pltpu.CompilerParams(
            dimension_semantics=("arbitrary",), vmem_limit_bytes=VMEM_LIMIT_BYTES),
        name="s5_core",
    )(x.reshape(n_rows, S5_L, d), rstd, g, dall, wec, wsc, tab, s0_re, s0_im, d_skip, w_glu)
    return y.reshape(m_rows, d), sf_re, sf_im, w_glu_bf16


def _slab_row_order(a, n_prompt_rows):
    c = a.shape[1]
    seg_rows = n_prompt_rows // (S5_L * SUBLANES)
    p = a[:n_prompt_rows].reshape(SUBLANES, seg_rows, S5_L, c).transpose(2, 1, 0, 3)
    s_ = a[n_prompt_rows:].reshape(-1, S5_L, c).transpose(1, 0, 2)
    return jnp.concatenate([p.reshape(S5_L, seg_rows * SUBLANES, c), s_], axis=1)


def _glu_kernel(n_col_blocks, x_ref, y_ref, w_ref, out_ref):
    d = x_ref.shape[1]
    a = _gelu(y_ref[...]).astype(BF16)
    cb = d // n_col_blocks
    for n in range(n_col_blocks):
        cols = slice(n * cb, (n + 1) * cb)
        val = jnp.dot(a, w_ref[:, n * cb:(n + 1) * cb], preferred_element_type=F32)
        gate = jnp.dot(a, w_ref[:, d + n * cb:d + (n + 1) * cb], preferred_element_type=F32)
        out_ref[:, cols] = x_ref[:, cols] + val * jax.nn.sigmoid(gate)


def _glu(x, y, w, *, tm, n_col_blocks=GLU_COL_BLOCKS):
    m_rows, d = x.shape
    assert m_rows % tm == 0 and d % n_col_blocks == 0
    tile = pl.BlockSpec((tm, d), lambda i: (i, 0))
    return pl.pallas_call(
        functools.partial(_glu_kernel, n_col_blocks),
        grid=(m_rows // tm,),
        in_specs=[tile, tile, _const_spec(w.shape)],
        out_specs=tile,
        out_shape=jax.ShapeDtypeStruct((m_rows, d), F32),
        compiler_params=pltpu.CompilerParams(
            dimension_semantics=("arbitrary",), vmem_limit_bytes=VMEM_LIMIT_BYTES),
        name="glu",
    )(x, y, w)


def _attention_tables(sinks):
    slopes = jnp.exp2(-8.0 * jnp.arange(1, N_HEADS + 1, dtype=F32) / N_HEADS)
    frame = jnp.arange(CHUNK, dtype=F32)[None, :]
    band = jnp.arange(BAND, dtype=F32)[:, None]
    dist = jnp.abs(frame - (band - WINDOW))
    bias = (dist[:, None, :] * slopes[None, :, None]).reshape(BAND, N_HEADS * CHUNK)
    sink_row = jnp.repeat(sinks.astype(F32), CHUNK).reshape(1, N_HEADS * CHUNK)
    return bias, sink_row


def _gm_tables(gm_ws, gm_b, d_gm):
    blk = jnp.arange(GM_CHUNK) // CHUNK
    w_prompt = jnp.where((blk[:, None] >= blk[None, :])[None], gm_ws, 0.0)
    top = gm_ws[:, :CHUNK, :CHUNK]
    zeros = jnp.zeros_like(top)
    w_sample = jnp.concatenate(
        [jnp.concatenate([top, zeros], axis=2), jnp.concatenate([zeros, top], axis=2)], axis=1)
    wsp = jnp.stack([w_prompt, w_sample])
    wsp = wsp.reshape(2, N_GM_GROUPS // 2, 2, GM_CHUNK, GM_CHUNK).transpose(0, 1, 3, 2, 4)
    wsp = wsp.reshape(2, N_GM_GROUPS // 2, GM_CHUNK, 2 * GM_CHUNK).astype(BF16)
    b_prompt = gm_b.T
    b_sample = jnp.concatenate([gm_b[:, :CHUNK].T, gm_b[:, :CHUNK].T], axis=0)
    gmb = jnp.stack([b_prompt, b_sample]).astype(F32)
    gmb = jnp.repeat(gmb, d_gm // N_GM_GROUPS, axis=2)
    return wsp, gmb


def _s5_compact(lam_re, lam_im, log_dt, b_re, b_im, c_re, c_im):
    n_groups = lam_re.shape[0]
    nb = n_groups // S5_LANE_GROUPS
    per_state = jnp.stack([lam_re, lam_im, jnp.broadcast_to(log_dt[:, None], lam_re.shape)]).astype(F32)
    per_state = per_state.reshape(3, nb, S5_LANE_GROUPS, S5_STATE).transpose(1, 0, 2, 3)
    lam = per_state.reshape(nb, 3, 1, S5_BLOCK_STATE)
    lam_c = jnp.broadcast_to(per_state[:, :, :, None, :],
                             (nb, 3, S5_LANE_GROUPS, S5_GROUP, S5_STATE)).reshape(nb, 3, LANES, S5_STATE)
    b_c = jnp.stack([b_re, b_im]).astype(F32).transpose(1, 0, 3, 2)
    b_c = b_c.reshape(nb, S5_LANE_GROUPS, 2, S5_GROUP, S5_STATE).transpose(0, 2, 1, 3, 4)
    c_c = jnp.stack([c_re, c_im]).astype(F32).transpose(1, 0, 2, 3)
    c_c = c_c.reshape(nb, S5_LANE_GROUPS, 2, S5_GROUP, S5_STATE).transpose(0, 2, 1, 3, 4)
    return (lam, lam_c, b_c.reshape(nb, 2, LANES, S5_STATE), c_c.reshape(nb, 2, LANES, S5_STATE))


def kernel(x_prompt, x_sample, cache_swa_k, cache_swa_v, state_s5_re, state_s5_im, norm_mix, norm_ffn, norm_final, w_in0, attn_sinks, gm_norm, gm_ws, gm_b, w_out0, s5_lam_re, s5_lam_im, s5_log_dt, s5_b_re, s5_b_im, s5_c_re, s5_c_im, s5_d, s5_w_glu, ffn_w_gate, ffn_w_up, ffn_w_down):
    batch, seq, d = x_prompt.shape
    dec_batch, dec_seq, _ = x_sample.shape
    assert batch == 1 and dec_seq == CHUNK and norm_mix.shape[0] == 2
    n_prompt = batch * seq
    n_sample = dec_batch * dec_seq
    d_gm = gm_norm.shape[-1]

    bias_tbl, sink_row = _attention_tables(attn_sinks[0])
    wsp, gmb = _gm_tables(gm_ws[0], gm_b[0], d_gm)
    x1, k_all, v_all, gvn = _mix0(
        x_prompt.reshape(n_prompt, d), x_sample.reshape(n_sample, d),
        cache_swa_k[0].reshape(dec_batch * WINDOW, D_KV), cache_swa_v[0].reshape(dec_batch * WINDOW, D_KV),
        norm_mix[0].reshape(1, d), w_in0[0].astype(BF16), bias_tbl, sink_row,
        gm_norm[0].reshape(1, d_gm), wsp, gmb, w_out0[0].astype(BF16), units=MIX_UNITS)

    x2, rstd2 = _ffn(x1, norm_ffn[0].reshape(1, d), ffn_w_gate, ffn_w_up, ffn_w_down,
                     norm_mix[1].reshape(1, d), layer=0, tm=FFN_TM, tf=FFN_TF, final=False)

    n_groups = s5_lam_re.shape[1]
    dall, wec, wsc, tab = _s5_prep(
        *_s5_compact(s5_lam_re[0], s5_lam_im[0], s5_log_dt[0], s5_b_re[0], s5_b_im[0], s5_c_re[0],
                     s5_c_im[0]),
        seg_rows=n_prompt // (S5_L * SUBLANES))
    ys5, sf_re, sf_im, w_glu_bf16 = _s5_core(
        x2, _slab_row_order(rstd2, n_prompt), norm_mix[1].reshape(1, d), dall, wec, wsc, tab,
        state_s5_re[0].reshape(dec_batch, n_groups * S5_STATE),
        state_s5_im[0].reshape(dec_batch, n_groups * S5_STATE), s5_d[0].reshape(1, d), s5_w_glu[0],
        n_prompt_rows=n_prompt)
    x3 = _glu(x2, ys5, w_glu_bf16, tm=GLU_TM)

    def last_ffn(row0, n_rows):
        (y,) = _ffn(x3, norm_ffn[1].reshape(1, d), ffn_w_gate, ffn_w_up, ffn_w_down,
                    norm_final.reshape(1, d), layer=1, tm=FFN_TM, tf=FFN_TF, final=True,
                    row0=row0, n_rows=n_rows)
        return y

    y_prompt = last_ffn(0, n_prompt)
    y_sample = last_ffn(n_prompt, n_sample)

    keep = min(WINDOW, seq)
    y_prompt = y_prompt.reshape(batch, seq, d)
    y_sample = y_sample.reshape(dec_batch, dec_seq, d)
    kv_shape_p = (1, batch, keep, N_KV_HEADS, HEAD_DIM)
    kv_shape_s = (1, dec_batch, dec_seq, N_KV_HEADS, HEAD_DIM)
    st_p = (1, batch, n_groups, S5_STATE)
    st_s = (1, dec_batch, n_groups, S5_STATE)
    return (y_prompt, y_sample,
            k_all[n_prompt - keep:n_prompt].reshape(kv_shape_p),
            v_all[n_prompt - keep:n_prompt].reshape(kv_shape_p),
            k_all[n_prompt:].reshape(kv_shape_s),
            v_all[n_prompt:].reshape(kv_shape_s),
            gvn.reshape(1, dec_batch, dec_seq, d_gm),
            sf_re[dec_batch].reshape(st_p), sf_im[dec_batch].reshape(st_p),
            sf_re[:dec_batch].reshape(st_s), sf_im[:dec_batch].reshape(st_s))
```

```python
import functools
import math

import jax
import jax.numpy as jnp
from jax import lax
from jax.experimental import pallas as pl
from jax.experimental.pallas import tpu as pltpu

F32 = jnp.float32
BF16 = jnp.bfloat16

CHUNK = 64
HEAD_DIM = 64
N_HEADS = 16
N_KV_HEADS = 2
Q_PER_KV = N_HEADS // N_KV_HEADS
WINDOW = 128
BAND = WINDOW + CHUNK
D_ATTN = N_HEADS * HEAD_DIM
D_KV = N_KV_HEADS * HEAD_DIM
GM_CHUNK = 128
N_GM_GROUPS = 16
S5_GROUP = 16
S5_STATE = 64
RMS_EPS = 1e-5
NEG_INF = -1e30

LANES = 128
SUBLANES = 8
MXU_TILE = 256
VMEM_LIMIT_BYTES = 56 * 1024 * 1024

MIX_UNITS = 2
FFN_TM = 1024
FFN_TF = 256
FFN_BLOCKS_PER_ITER = 2
GLU_TM = 512
GLU_COL_BLOCKS = 8

S5_L = SUBLANES
S5_LANE_GROUPS = LANES // S5_GROUP
S5_BLOCK_STATE = S5_LANE_GROUPS * S5_STATE
S5_WT_BLOCK = MXU_TILE


def _gelu(x):
    return 0.5 * x * (1.0 + lax.erf(x * math.sqrt(0.5)))


def _rms_scale(x):
    return x * lax.rsqrt(jnp.mean(x * x, axis=-1, keepdims=True) + RMS_EPS)


def _const_spec(shape):
    zeros = (0,) * len(shape)
    return pl.BlockSpec(shape, lambda *_: zeros, pipeline_mode=pl.Buffered(1))


def _mix0_kernel(n_prompt_tiles, units,
                 xp_ref, xs_ref, ck_ref, cv_ref, g_ref, win_ref, bias_ref, sink_ref, gmn_ref,
                 wsp_ref, gmb_ref, wout_ref,
                 x1_ref, k_ref, v_ref, gvn_ref,
                 z_ref, q_ref, ocat_ref, kprev_ref, vprev_ref):
    i = pl.program_id(0)
    is_sample = i >= n_prompt_tiles
    tm = units * GM_CHUNK
    d_gm = gmn_ref.shape[-1]
    off_k = D_ATTN
    off_v = D_ATTN + D_KV
    off_gu = D_ATTN + 2 * D_KV
    off_gv = off_gu + d_gm
    n_q = Q_PER_KV * CHUNK

    @pl.when(i == 0)
    def _():
        kprev_ref[...] = jnp.zeros_like(kprev_ref)
        vprev_ref[...] = jnp.zeros_like(vprev_ref)

    x = jnp.where(is_sample, xs_ref[...], xp_ref[...])
    h = (_rms_scale(x) * g_ref[...]).astype(BF16)
    z_ref[...] = jnp.dot(h, win_ref[...], preferred_element_type=F32)

    q_ref[...] = (z_ref[:, 0:D_ATTN] * (HEAD_DIM ** -0.5)).astype(BF16)
    k = z_ref[:, off_k:off_k + D_KV]
    v = z_ref[:, off_v:off_v + D_KV]
    k_ref[...] = k
    v_ref[...] = v

    def lane_lo(rows):
        return lax.broadcasted_iota(jnp.int32, (rows, LANES), 1) < HEAD_DIM

    def replicate(a):
        r = pltpu.roll(a, HEAD_DIM, axis=1)
        lo = lane_lo(a.shape[0])
        return jnp.where(lo, a, r).astype(BF16), jnp.where(lo, r, a).astype(BF16)

    k_rep = replicate(k)
    v_rep = replicate(v)
    ck_rep = replicate(ck_ref[...])
    cv_rep = replicate(cv_ref[...])

    lo64 = lane_lo(CHUNK)
    band_pos = lax.broadcasted_iota(jnp.int32, (BAND, n_q), 0)
    chunks_per_tile = tm // CHUNK

    def scores(u, c2, kv):
        r0 = u * GM_CHUNK + c2 * CHUNK
        stream = r0 // CHUNK
        chunk_index = i * chunks_per_tile + stream
        valid_from = jnp.where(is_sample, 0, jnp.maximum(WINDOW - CHUNK * chunk_index, 0))
        valid = band_pos >= valid_from

        def band(cur, prev_ref, cached):
            if u == 0:
                prev_unit = prev_ref[kv]
            else:
                prev_unit = cur[(u - 1) * GM_CHUNK:u * GM_CHUNK]
            if c2 == 0:
                prompt_prev = prev_unit
            else:
                prompt_prev = jnp.concatenate(
                    [prev_unit[CHUNK:], cur[u * GM_CHUNK:u * GM_CHUNK + CHUNK]], axis=0)
            sample_prev = cached[stream * WINDOW:(stream + 1) * WINDOW]
            prev = jnp.where(is_sample, sample_prev, prompt_prev)
            return jnp.concatenate([prev, cur[r0:r0 + CHUNK]], axis=0)

        kb = band(k_rep[kv], kprev_ref, ck_rep[kv])
        vb = band(v_rep[kv], vprev_ref, cv_rep[kv])

        pieces = []
        for m in range(Q_PER_KV // 2):
            c0 = kv * Q_PER_KV * HEAD_DIM + m * LANES
            qp = q_ref[r0:r0 + CHUNK, c0:c0 + LANES]
            pieces.append(jnp.where(lo64, qp, jnp.zeros_like(qp)))
            pieces.append(jnp.where(lo64, jnp.zeros_like(qp), qp))
        qs = jnp.concatenate(pieces, axis=0)
        cols = slice(kv * n_q, (kv + 1) * n_q)
        st = lax.dot_general(kb, qs, (((1,), (1,)), ((), ())),
                             preferred_element_type=F32)
        return jnp.where(valid, st - bias_ref[:, cols], NEG_INF), vb

    def attend(u, c2, kv, st, vb):
        r0 = u * GM_CHUNK + c2 * CHUNK
        cols = slice(kv * n_q, (kv + 1) * n_q)
        sink = sink_ref[:, cols]
        mx = jnp.maximum(jnp.max(st, axis=0, keepdims=True), sink)
        p = jnp.exp(st - mx)
        denom = jnp.sum(p, axis=0, keepdims=True) + jnp.exp(sink - mx)
        pn = (p * (1.0 / denom)).astype(BF16)
        o = lax.dot_general(pn, vb, (((0,), (0,)), ((), ())),
                            preferred_element_type=F32)
        for m in range(Q_PER_KV // 2):
            o_pair = jnp.where(lo64, o[(2 * m) * CHUNK:(2 * m + 1) * CHUNK],
                               o[(2 * m + 1) * CHUNK:(2 * m + 2) * CHUNK])
            c0 = kv * Q_PER_KV * HEAD_DIM + m * LANES
            ocat_ref[r0:r0 + CHUNK, c0:c0 + LANES] = o_pair.astype(BF16)

    blocks = [(u, c2, kv) for u in range(units) for c2 in range(GM_CHUNK // CHUNK)
              for kv in range(N_KV_HEADS)]
    scored = [scores(*blk) for blk in blocks]
    for blk, (st, vb) in zip(blocks, scored):
        attend(*blk, st, vb)

    lo128 = lane_lo(GM_CHUNK)
    for u in range(units):
        rows = slice(u * GM_CHUNK, (u + 1) * GM_CHUNK)
        ua = _gelu(z_ref[rows, off_gu:off_gu + d_gm])
        gvn = _rms_scale(_gelu(z_ref[rows, off_gv:off_gv + d_gm])) * gmn_ref[...]
        gvn_ref[rows, :] = gvn
        gb = gvn.astype(BF16)
        for m in range(N_GM_GROUPS // 2):
            cols = slice(m * LANES, (m + 1) * LANES)
            rhs = gb[:, cols]
            rhs2 = jnp.concatenate([jnp.where(lo128, rhs, jnp.zeros_like(rhs)),
                                    jnp.where(lo128, jnp.zeros_like(rhs), rhs)], axis=0)
            sp = jnp.dot(wsp_ref[0, m], rhs2, preferred_element_type=F32) + gmb_ref[0, :, cols]
            ocat_ref[rows, D_ATTN + m * LANES:D_ATTN + (m + 1) * LANES] = (ua[:, cols] * sp).astype(BF16)

    x1_ref[...] = x + jnp.dot(ocat_ref[...], wout_ref[...], preferred_element_type=F32)

    for kv in range(N_KV_HEADS):
        kprev_ref[kv] = k_rep[kv][tm - GM_CHUNK:]
        vprev_ref[kv] = v_rep[kv][tm - GM_CHUNK:]


def _mix0(xp, xs, cache_k, cache_v, g, w_in, bias_tbl, sink_row, gm_norm, wsp, gmb, w_out, *, units):
    n_prompt_rows, d = xp.shape
    n_sample_rows = xs.shape[0]
    m_rows = n_prompt_rows + n_sample_rows
    tm = units * GM_CHUNK
    assert n_prompt_rows % tm == 0 and n_sample_rows % tm == 0
    n_tiles = m_rows // tm
    n_prompt_tiles = n_prompt_rows // tm
    d_in = w_in.shape[1]
    d_gm = gm_norm.shape[-1]
    cache_rows = (tm // CHUNK) * WINDOW

    def prompt_block(i):
        return jnp.minimum(i, n_prompt_tiles - 1)

    def sample_block(i):
        return jnp.maximum(i - n_prompt_tiles, 0)

    def kind(i):
        return jnp.where(i >= n_prompt_tiles, 1, 0)

    in_specs = [
        pl.BlockSpec((tm, d), lambda i: (prompt_block(i), 0)),
        pl.BlockSpec((tm, d), lambda i: (sample_block(i), 0)),
        pl.BlockSpec((cache_rows, D_KV), lambda i: (sample_block(i), 0)),
        pl.BlockSpec((cache_rows, D_KV), lambda i: (sample_block(i), 0)),
        _const_spec((1, d)),
        _const_spec((d, d_in)),
        _const_spec(bias_tbl.shape),
        _const_spec(sink_row.shape),
        _const_spec((1, d_gm)),
        pl.BlockSpec((1,) + wsp.shape[1:], lambda i: (kind(i), 0, 0, 0)),
        pl.BlockSpec((1, GM_CHUNK, d_gm), lambda i: (kind(i), 0, 0)),
        _const_spec(w_out.shape),
    ]
    out_specs = [
        pl.BlockSpec((tm, d), lambda i: (i, 0)),
        pl.BlockSpec((tm, D_KV), lambda i: (i, 0)),
        pl.BlockSpec((tm, D_KV), lambda i: (i, 0)),
        pl.BlockSpec((tm, d_gm), lambda i: (sample_block(i), 0)),
    ]
    out_shape = [
        jax.ShapeDtypeStruct((m_rows, d), F32),
        jax.ShapeDtypeStruct((m_rows, D_KV), F32),
        jax.ShapeDtypeStruct((m_rows, D_KV), F32),
        jax.ShapeDtypeStruct((n_sample_rows, d_gm), F32),
    ]
    scratch = [
        pltpu.VMEM((tm, d_in), F32),
        pltpu.VMEM((tm, D_ATTN), BF16),
        pltpu.VMEM((tm, D_ATTN + d_gm), BF16),
        pltpu.VMEM((N_KV_HEADS, GM_CHUNK, LANES), BF16),
        pltpu.VMEM((N_KV_HEADS, GM_CHUNK, LANES), BF16),
    ]
    return pl.pallas_call(
        functools.partial(_mix0_kernel, n_prompt_tiles, units),
        grid=(n_tiles,),
        in_specs=in_specs,
        out_specs=out_specs,
        out_shape=out_shape,
        scratch_shapes=scratch,
        compiler_params=pltpu.CompilerParams(
            dimension_semantics=("arbitrary",), vmem_limit_bytes=VMEM_LIMIT_BYTES),
        name="mix0",
    )(xp, xs, cache_k, cache_v, g, w_in, bias_tbl, sink_row, gm_norm, wsp, gmb, w_out)


def _ffn_kernel(final, layer, tf, x_ref, g_ref, wg_hbm, wu_hbm, wd_hbm, gnext_ref, out_ref, *rest):
    h_ref, wg_buf, wu_buf, wd_buf, sem = rest[-5:]
    i = pl.program_id(0)
    n_tiles = pl.num_programs(0)
    n_blocks = wg_hbm.shape[2] // tf

    def copies(block, slot):
        cols = pl.ds(pl.multiple_of(block * tf, tf), tf)
        return (pltpu.make_async_copy(wg_hbm.at[layer, :, cols], wg_buf.at[slot], sem.at[slot, 0]),
                pltpu.make_async_copy(wu_hbm.at[layer, :, cols], wu_buf.at[slot], sem.at[slot, 1]),
                pltpu.make_async_copy(wd_hbm.at[layer, cols, :], wd_buf.at[slot], sem.at[slot, 2]))

    @pl.when(i == 0)
    def _():
        for c in copies(0, 0) + copies(1, 1):
            c.start()

    x = x_ref[...]
    h_ref[...] = (_rms_scale(x) * g_ref[...]).astype(BF16)
    out_ref[...] = x

    def iteration(it, carry):
        for n in range(FFN_BLOCKS_PER_ITER):
            block = it * FFN_BLOCKS_PER_ITER + n
            slot = n % 2
            for c in copies(block, slot):
                c.wait()
            h = h_ref[...]
            wg = wg_buf[slot].astype(BF16)
            wu = wu_buf[slot].astype(BF16)
            wd = wd_buf[slot].astype(BF16)
            for c in copies((block + 2) % n_blocks, slot):
                c.start()
            gate = jnp.dot(h, wg, preferred_element_type=F32)
            up = jnp.dot(h, wu, preferred_element_type=F32)
            act = (gate * jax.nn.sigmoid(gate) * up).astype(BF16)
            out_ref[...] += jnp.dot(act, wd, preferred_element_type=F32)
        return carry

    lax.fori_loop(0, n_blocks // FFN_BLOCKS_PER_ITER, iteration, 0)

    @pl.when(i == n_tiles - 1)
    def _():
        for c in copies(0, 0) + copies(1, 1):
            c.wait()

    out = out_ref[...]
    scale = lax.rsqrt(jnp.mean(out * out, axis=-1, keepdims=True) + RMS_EPS)
    if final:
        out_ref[...] = out * scale * gnext_ref[...]
    else:
        rest[0][...] = jnp.broadcast_to(scale, rest[0].shape)


def _ffn(x, g, wg, wu, wd, gnext, *, layer, tm, tf, final, row0=0, n_rows=None):
    d = x.shape[1]
    n_rows = x.shape[0] if n_rows is None else n_rows
    f = wg.shape[2]
    n_blocks = f // tf
    assert n_rows % tm == 0 and row0 % tm == 0 and f % tf == 0
    assert n_blocks % FFN_BLOCKS_PER_ITER == 0 and FFN_BLOCKS_PER_ITER % 2 == 0
    tile0 = row0 // tm
    hbm = pl.BlockSpec(memory_space=pl.ANY)
    out_specs = [pl.BlockSpec((tm, d), lambda i: (i, 0))]
    out_shape = [jax.ShapeDtypeStruct((n_rows, d), F32)]
    if not final:
        out_specs.append(pl.BlockSpec((tm, LANES), lambda i: (i, 0)))
        out_shape.append(jax.ShapeDtypeStruct((n_rows, LANES), F32))
    return pl.pallas_call(
        functools.partial(_ffn_kernel, final, layer, tf),
        grid=(n_rows // tm,),
        in_specs=[
            pl.BlockSpec((tm, d), lambda i: (i + tile0, 0)),
            pl.BlockSpec((1, d), lambda i: (0, 0)),
            hbm, hbm, hbm,
            pl.BlockSpec((1, d), lambda i: (0, 0)),
        ],
        out_specs=out_specs,
        out_shape=out_shape,
        scratch_shapes=[
            pltpu.VMEM((tm, d), BF16),
            pltpu.VMEM((2, d, tf), F32),
            pltpu.VMEM((2, d, tf), F32),
            pltpu.VMEM((2, tf, d), F32),
            pltpu.SemaphoreType.DMA((2, 3)),
        ],
        compiler_params=pltpu.CompilerParams(
            dimension_semantics=("arbitrary",), vmem_limit_bytes=VMEM_LIMIT_BYTES),
        name="ffn",
    )(x, g, wg, wu, wd, gnext)


TAB_ROWS = 18


def _discretize(lr, li, log_dt):
    dt = jnp.exp(log_dt)
    mag = jnp.exp(lr * dt)
    ar = mag * jnp.cos(li * dt)
    ai = mag * jnp.sin(li * dt)
    den = lr * lr + li * li
    nr = ar - 1.0
    return ar, ai, (nr * lr + ai * li) / den, (ai * lr - nr * li) / den


def _cmul(pr, pi, qr, qi):
    return pr * qr - pi * qi, pr * qi + pi * qr


def _s5_prep_kernel(seg_rows, lam_ref, lamc_ref, b_ref, c_ref, dall_ref, wec_ref, wsc_ref, tab_ref,
                    bhi_ref, blo_ref):
    ns = S5_BLOCK_STATE

    ar, ai, fr, fi = _discretize(lamc_ref[0, 0], lamc_ref[0, 1], lamc_ref[0, 2])
    bbr, bbi = _cmul(fr, fi, b_ref[0, 0], b_ref[0, 1])
    c_re = c_ref[0, 0]
    c_im = c_ref[0, 1]

    def split(a):
        hi = a.astype(BF16)
        return hi, (a - hi.astype(F32)).astype(BF16)

    power = (jnp.ones_like(ar), jnp.zeros_like(ar))
    for l in range(S5_L):
        bkr, bki = _cmul(*power, bbr, bbi)
        power = _cmul(*power, ar, ai)
        rows = slice(l * LANES, (l + 1) * LANES)
        bk = jnp.concatenate([bkr, bki], axis=1)
        bhi_ref[rows, :], blo_ref[rows, :] = split(bk)
        wec_ref[0, (S5_L - 1 - l) * LANES:(S5_L - l) * LANES, :] = bk
        qr, qi = power
        wsc_ref[0, rows, :] = jnp.concatenate([c_re * qr - c_im * qi, -c_re * qi - c_im * qr], axis=1)

    c_hi, c_lo = split(jnp.concatenate([c_re, -c_im], axis=1))
    c_parts = jnp.concatenate([c_hi, c_lo], axis=0)
    dims = (((1,), (1,)), ((), ()))
    d_hi = lax.dot_general(bhi_ref[...], c_parts, dims, preferred_element_type=F32)
    d_lo = lax.dot_general(blo_ref[...], c_parts, dims, preferred_element_type=F32)
    d_all = d_hi[:, 0:LANES] + d_hi[:, LANES:2 * LANES] + d_lo[:, 0:LANES]
    shape = (S5_L * LANES, LANES)
    in_group = (lax.broadcasted_iota(jnp.int32, shape, 0) % LANES) // S5_GROUP
    out_group = lax.broadcasted_iota(jnp.int32, shape, 1) // S5_GROUP
    dall_ref[0] = jnp.where(in_group == out_group, d_all, 0.0).astype(BF16)

    ar, ai, _, _ = _discretize(lam_ref[0, 0], lam_ref[0, 1], lam_ref[0, 2])
    row = lax.broadcasted_iota(jnp.int32, (SUBLANES, ns), 0)

    def bcast(a):
        return jnp.broadcast_to(a, (SUBLANES, ns))

    def log_step_tables(t0, base):
        cur = base
        for n, shift in enumerate((1, 2, 4)):
            tab_ref[0, t0 + 2 * n] = jnp.where(row >= shift, bcast(cur[0]), 0.0)
            tab_ref[0, t0 + 2 * n + 1] = jnp.where(row >= shift, bcast(cur[1]), 0.0)
            cur = _cmul(*cur, *cur)
        return cur

    a1 = (ar, ai)
    for _ in range(S5_L.bit_length() - 1):
        a1 = _cmul(*a1, *a1)
    a8 = log_step_tables(0, a1)
    pr_tab = jnp.zeros((SUBLANES, ns), F32)
    pi_tab = jnp.zeros((SUBLANES, ns), F32)
    cur = a1
    for r in range(SUBLANES):
        pr_tab = jnp.where(row == r, bcast(cur[0]), pr_tab)
        pi_tab = jnp.where(row == r, bcast(cur[1]), pi_tab)
        cur = _cmul(*cur, *a1)
    tab_ref[0, 6] = pr_tab
    tab_ref[0, 7] = pi_tab
    tab_ref[0, 8] = bcast(a8[0])
    tab_ref[0, 9] = bcast(a8[1])
    tab_ref[0, 10] = bcast(a1[0])
    tab_ref[0, 11] = bcast(a1[1])
    seg = a1
    for _ in range(seg_rows.bit_length() - 1):
        seg = _cmul(*seg, *seg)
    log_step_tables(12, seg)


def _s5_prep(lam, lam_c, b_c, c_c, *, seg_rows):
    assert seg_rows & (seg_rows - 1) == 0 and S5_L & (S5_L - 1) == 0
    nb = lam.shape[0]
    ns = S5_BLOCK_STATE
    k = S5_L * LANES
    cspec = pl.BlockSpec((1, k, LANES), lambda j: (j, 0, 0))

    def whole(a):
        return pl.BlockSpec((1,) + a.shape[1:], lambda j: (j,) + (0,) * (a.ndim - 1))

    return pl.pallas_call(
        functools.partial(_s5_prep_kernel, seg_rows),
        grid=(nb,),
        in_specs=[whole(lam), whole(lam_c), whole(b_c), whole(c_c)],
        out_specs=[cspec, cspec, cspec,
                   pl.BlockSpec((1, TAB_ROWS, SUBLANES, ns), lambda j: (j, 0, 0, 0))],
        out_shape=[
            jax.ShapeDtypeStruct((nb, k, LANES), BF16),
            jax.ShapeDtypeStruct((nb, k, LANES), F32),
            jax.ShapeDtypeStruct((nb, k, LANES), F32),
            jax.ShapeDtypeStruct((nb, TAB_ROWS, SUBLANES, ns), F32),
        ],
        scratch_shapes=[pltpu.VMEM((k, LANES), BF16), pltpu.VMEM((k, LANES), BF16)],
        compiler_params=pltpu.CompilerParams(
            dimension_semantics=("arbitrary",), vmem_limit_bytes=VMEM_LIMIT_BYTES),
        name="s5_prep",
    )(lam, lam_c, b_c, c_c)


def _s5_core_kernel(n_prompt_rows, n_streams,
                    x_hbm, rstd_ref, g_ref, dall_ref, wec_ref, wsc_ref, tab_ref, s0r_ref, s0i_ref,
                    d_ref, wglu_ref,
                    y_hbm, sfr_ref, sfi_ref, wglu_bf16_ref,
                    xp_ref, xs_ref, yp_ref, ys_ref, in_sem, out_sem,
                    u2f_ref, u2_ref, e_ref, wt_ref, we_ref, wst_ref):
    wglu_bf16_ref[...] = wglu_ref[...].astype(BF16)
    ns = S5_BLOCK_STATE
    j = pl.program_id(0)
    nb = pl.num_programs(0)
    slot = j % 2
    n_rows = x_hbm.shape[0]
    n_sample_rows = n_rows - n_prompt_rows
    seg_rows = n_prompt_rows // SUBLANES

    def copies(block, slot, load):
        lanes = pl.ds(pl.multiple_of(block * LANES, LANES), LANES)
        out = []
        for l in range(S5_L):
            pairs = [(pl.ds(seg * seg_rows, seg_rows), (xp_ref if load else yp_ref).at[slot, l, :, seg, :])
                     for seg in range(SUBLANES)]
            pairs.append((pl.ds(n_prompt_rows, n_sample_rows), (xs_ref if load else ys_ref).at[slot, l]))
            for rows, buf in pairs:
                if load:
                    out.append(pltpu.make_async_copy(x_hbm.at[rows, l, lanes], buf, in_sem.at[slot]))
                else:
                    out.append(pltpu.make_async_copy(buf, y_hbm.at[rows, l, lanes], out_sem.at[slot]))
        return out

    @pl.when(j == 0)
    def _():
        for c in copies(0, 0, True):
            c.start()

    @pl.when(j + 1 < nb)
    def _():
        for c in copies(j + 1, 1 - slot, True):
            c.start()

    for c in copies(j, slot, True):
        c.wait()


    shape = (S5_L * LANES, LANES)
    lane = lax.broadcasted_iota(jnp.int32, shape, 1)
    lane_lo = lane < S5_STATE
    row_group = (lax.broadcasted_iota(jnp.int32, shape, 0) % LANES) // S5_GROUP
    groups_per_vreg = LANES // S5_STATE

    def expand(compact_ref, full_ref):
        x = compact_ref[0]
        r = pltpu.roll(x, S5_STATE, axis=1)
        for part, rep in enumerate((jnp.where(lane_lo, x, r), jnp.where(lane_lo, r, x))):
            for v in range(S5_LANE_GROUPS // groups_per_vreg):
                own = row_group == groups_per_vreg * v + (lane // S5_STATE)
                col = part * S5_BLOCK_STATE + v * LANES
                full_ref[:, col:col + LANES] = jnp.where(own, rep, 0.0).astype(BF16)

    expand(wec_ref, we_ref)

    for l in range(S5_L):
        xl = xp_ref[slot, l].reshape(n_prompt_rows, LANES)
        u2f_ref[l, 0:n_prompt_rows, :] = xl * rstd_ref[l, 0:n_prompt_rows, :] * g_ref[...]
        u2f_ref[l, n_prompt_rows:, :] = xs_ref[slot, l] * rstd_ref[l, n_prompt_rows:, :] * g_ref[...]
    for l in range(S5_L):
        u2_ref[:, l * LANES:(l + 1) * LANES] = u2f_ref[l].astype(BF16)
    e_ref[...] = jnp.dot(u2_ref[...], we_ref[...], preferred_element_type=F32)

    zero_blk = jnp.zeros((LANES, LANES), BF16)
    for l in range(S5_L):
        for l2 in range(S5_L):
            blk = dall_ref[0, (l2 - l) * LANES:(l2 - l + 1) * LANES, :] if l2 >= l else zero_blk
            wt_ref[l * LANES:(l + 1) * LANES, l2 * LANES:(l2 + 1) * LANES] = blk
    slabs_per_block = S5_WT_BLOCK // LANES
    n_col_blocks = S5_L * LANES // S5_WT_BLOCK
    for cb in range(n_col_blocks):
        k_end = (cb + 1) * S5_WT_BLOCK
        y2 = jnp.dot(u2_ref[:, 0:k_end], wt_ref[0:k_end, cb * S5_WT_BLOCK:k_end],
                     preferred_element_type=F32)
        for n in range(slabs_per_block):
            l = cb * slabs_per_block + n
            u2f_ref[l] = y2[:, n * LANES:(n + 1) * LANES] + d_ref[...] * u2f_ref[l]
    expand(wsc_ref, wst_ref)

    tabs = [tab_ref[0, t] for t in range(TAB_ROWS)]
    m_tabs, (pr, pi, a8r, a8i, ar, ai), b_tabs = tabs[0:6], tabs[6:12], tabs[12:18]
    row = lax.broadcasted_iota(jnp.int32, (SUBLANES, ns), 0)

    def log_step_scan(xr, xi, t):
        for n, shift in enumerate((1, 2, 4)):
            tr, ti = t[2 * n], t[2 * n + 1]
            sr = pltpu.roll(xr, shift, axis=0)
            si = pltpu.roll(xi, shift, axis=0)
            xr, xi = xr + tr * sr - ti * si, xi + tr * si + ti * sr
        return xr, xi

    def shift_down(xr, xi, fr, fi):
        first = row == 0
        return (jnp.where(first, fr, pltpu.roll(xr, 1, axis=0)),
                jnp.where(first, fi, pltpu.roll(xi, 1, axis=0)))

    def rows_of(k):
        return pl.ds(pl.multiple_of(k * SUBLANES, SUBLANES), SUBLANES)

    def local_step(k, carry):
        sr, si = carry
        er = e_ref[rows_of(k), 0:ns]
        ei = e_ref[rows_of(k), ns:2 * ns]
        e_ref[rows_of(k), 0:ns] = sr
        e_ref[rows_of(k), ns:2 * ns] = si
        return ar * sr - ai * si + er, ar * si + ai * sr + ei

    zero = jnp.zeros((SUBLANES, ns), F32)
    ends = lax.fori_loop(0, seg_rows, local_step, (zero, zero), unroll=True)
    ends = log_step_scan(*ends, b_tabs)
    sfr_ref[...] = jnp.zeros_like(sfr_ref)
    sfi_ref[...] = jnp.zeros_like(sfi_ref)
    sfr_ref[n_streams:n_streams + 1, :] = ends[0][SUBLANES - 1:SUBLANES]
    sfi_ref[n_streams:n_streams + 1, :] = ends[1][SUBLANES - 1:SUBLANES]

    def correct_step(k, carry):
        cr, ci = carry
        e_ref[rows_of(k), 0:ns] += cr
        e_ref[rows_of(k), ns:2 * ns] += ci
        return ar * cr - ai * ci, ar * ci + ai * cr

    lax.fori_loop(0, seg_rows, correct_step, shift_down(*ends, zero, zero), unroll=True)

    for b in range(n_streams):
        rows = pl.ds(n_prompt_rows + b * SUBLANES, SUBLANES)
        cr = jnp.broadcast_to(s0r_ref[b:b + 1, :], (SUBLANES, ns))
        ci = jnp.broadcast_to(s0i_ref[b:b + 1, :], (SUBLANES, ns))
        xr, xi = log_step_scan(e_ref[rows, 0:ns], e_ref[rows, ns:2 * ns], m_tabs)
        st_r = xr + pr * cr - pi * ci
        st_i = xi + pr * ci + pi * cr
        e_ref[rows, 0:ns], e_ref[rows, ns:2 * ns] = shift_down(st_r, st_i, cr, ci)
        sfr_ref[b:b + 1, :] = st_r[SUBLANES - 1:SUBLANES]
        sfi_ref[b:b + 1, :] = st_i[SUBLANES - 1:SUBLANES]

    @pl.when(j >= 2)
    def _():
        for c in copies(j - 2, slot, False):
            c.wait()

    sprev = e_ref[...].astype(BF16)
    for cb in range(n_col_blocks):
        cols = slice(cb * S5_WT_BLOCK, (cb + 1) * S5_WT_BLOCK)
        y2 = lax.dot_general(sprev, wst_ref[cols, :], (((1,), (1,)), ((), ())),
                             preferred_element_type=F32)
        for n in range(slabs_per_block):
            l = cb * slabs_per_block + n
            yl = u2f_ref[l] + y2[:, n * LANES:(n + 1) * LANES]
            yp_ref[slot, l] = yl[0:n_prompt_rows].reshape(seg_rows, SUBLANES, LANES)
            ys_ref[slot, l] = yl[n_prompt_rows:]
    for c in copies(j, slot, False):
        c.start()

    @pl.when(j == nb - 1)
    def _():
        for c in copies(j - 1, 1 - slot, False) + copies(j, slot, False):
            c.wait()


def _s5_core(x, rstd, g, dall, wec, wsc, tab, s0_re, s0_im, d_skip, w_glu, *, n_prompt_rows):
    m_rows, d = x.shape
    nb = d // LANES
    assert nb >= 2
    glu_cols = w_glu.shape[1] // nb
    assert glu_cols % LANES == 0
    glu_blk = pl.BlockSpec((w_glu.shape[0], glu_cols), lambda j: (0, j))
    ns = S5_BLOCK_STATE
    k = S5_L * LANES
    n_rows = m_rows // S5_L
    n_streams = s0_re.shape[0]
    assert n_prompt_rows % (S5_L * SUBLANES) == 0
    assert (m_rows - n_prompt_rows) == n_streams * S5_L * SUBLANES
    n_prompt_mrows = n_prompt_rows // S5_L
    n_sample_mrows = n_rows - n_prompt_mrows
    seg_rows = n_prompt_mrows // SUBLANES
    sf_rows = n_streams + SUBLANES
    hbm = pl.BlockSpec(memory_space=pl.ANY)
    cspec = pl.BlockSpec((1, k, LANES), lambda j: (j, 0, 0))
    st = pl.BlockSpec((n_streams, ns), lambda j: (0, j))
    sf = pl.BlockSpec((sf_rows, ns), lambda j: (0, j))
    slab_p = pltpu.VMEM((2, S5_L, seg_rows, SUBLANES, LANES), F32)
    slab_s = pltpu.VMEM((2, S5_L, n_sample_mrows, LANES), F32)
    y, sf_re, sf_im, w_glu_bf16 = pl.pallas_call(
        functools.partial(_s5_core_kernel, n_prompt_mrows, n_streams),
        grid=(nb,),
        in_specs=[
            hbm,
            _const_spec((S5_L, n_rows, LANES)),
            pl.BlockSpec((1, LANES), lambda j: (0, j)),
            cspec, cspec, cspec,
            pl.BlockSpec((1, TAB_ROWS, SUBLANES, ns), lambda j: (j, 0, 0, 0)),
            st, st,
            pl.BlockSpec((1, LANES), lambda j: (0, j)),
            glu_blk,
        ],
        out_specs=[hbm, sf, sf, glu_blk],
        out_shape=[
            jax.ShapeDtypeStruct((n_rows, S5_L, d), F32),
            jax.ShapeDtypeStruct((sf_rows, nb * ns), F32),
            jax.ShapeDtypeStruct((sf_rows, nb * ns), F32),
            jax.ShapeDtypeStruct(w_glu.shape, BF16),
        ],
        scratch_shapes=[
            slab_p, slab_s, slab_p, slab_s,
            pltpu.SemaphoreType.DMA((2,)), pltpu.SemaphoreType.DMA((2,)),
            pltpu.VMEM((S5_L, n_rows, LANES), F32),
            pltpu.VMEM((n_rows, k), BF16),
            pltpu.VMEM((n_rows, 2 * ns), F32),
            pltpu.VMEM((k, k), BF16),
            pltpu.VMEM((k, 2 * ns), BF16),
            pltpu.VMEM((k, 2 * ns), BF16),
        ],
        compiler_params=pltpu.CompilerParams(
            dimension_semantics=("arbitrary",), vmem_limit_bytes=VMEM_LIMIT_BYTES),
        name="s5_core",
    )(x.reshape(n_rows, S5_L, d), rstd, g, dall, wec, wsc, tab, s0_re, s0_im, d_skip, w_glu)
    return y.reshape(m_rows, d), sf_re, sf_im, w_glu_bf16


def _slab_row_order(a, n_prompt_rows):
    c = a.shape[1]
    seg_rows = n_prompt_rows // (S5_L * SUBLANES)
    p = a[:n_prompt_rows].reshape(SUBLANES, seg_rows, S5_L, c).transpose(2, 1, 0, 3)
    s_ = a[n_prompt_rows:].reshape(-1, S5_L, c).transpose(1, 0, 2)
    return jnp.concatenate([p.reshape(S5_L, seg_rows * SUBLANES, c), s_], axis=1)


def _glu_kernel(n_col_blocks, x_ref, y_ref, w_ref, out_ref):
    d = x_ref.shape[1]
    a = _gelu(y_ref[...]).astype(BF16)
    cb = d // n_col_blocks
    for n in range(n_col_blocks):
        cols = slice(n * cb, (n + 1) * cb)
        val = jnp.dot(a, w_ref[:, n * cb:(n + 1) * cb], preferred_element_type=F32)
        gate = jnp.dot(a, w_ref[:, d + n * cb:d + (n + 1) * cb], preferred_element_type=F32)
        out_ref[:, cols] = x_ref[:, cols] + val * jax.nn.sigmoid(gate)


def _glu(x, y, w, *, tm, n_col_blocks=GLU_COL_BLOCKS):
    m_rows, d = x.shape
    assert m_rows % tm == 0 and d % n_col_blocks == 0
    tile = pl.BlockSpec((tm, d), lambda i: (i, 0))
    return pl.pallas_call(
        functools.partial(_glu_kernel, n_col_blocks),
        grid=(m_rows // tm,),
        in_specs=[tile, tile, _const_spec(w.shape)],
        out_specs=tile,
        out_shape=jax.ShapeDtypeStruct((m_rows, d), F32),
        compiler_params=pltpu.CompilerParams(
            dimension_semantics=("arbitrary",), vmem_limit_bytes=VMEM_LIMIT_BYTES),
        name="glu",
    )(x, y, w)


def _attention_tables(sinks):
    slopes = jnp.exp2(-8.0 * jnp.arange(1, N_HEADS + 1, dtype=F32) / N_HEADS)
    frame = jnp.arange(CHUNK, dtype=F32)[None, :]
    band = jnp.arange(BAND, dtype=F32)[:, None]
    dist = jnp.abs(frame - (band - WINDOW))
    bias = (dist[:, None, :] * slopes[None, :, None]).reshape(BAND, N_HEADS * CHUNK)
    sink_row = jnp.repeat(sinks.astype(F32), CHUNK).reshape(1, N_HEADS * CHUNK)
    return bias, sink_row


def _gm_tables(gm_ws, gm_b, d_gm):
    blk = jnp.arange(GM_CHUNK) // CHUNK
    w_prompt = jnp.where((blk[:, None] >= blk[None, :])[None], gm_ws, 0.0)
    top = gm_ws[:, :CHUNK, :CHUNK]
    zeros = jnp.zeros_like(top)
    w_sample = jnp.concatenate(
        [jnp.concatenate([top, zeros], axis=2), jnp.concatenate([zeros, top], axis=2)], axis=1)
    wsp = jnp.stack([w_prompt, w_sample])
    wsp = wsp.reshape(2, N_GM_GROUPS // 2, 2, GM_CHUNK, GM_CHUNK).transpose(0, 1, 3, 2, 4)
    wsp = wsp.reshape(2, N_GM_GROUPS // 2, GM_CHUNK, 2 * GM_CHUNK).astype(BF16)
    b_prompt = gm_b.T
    b_sample = jnp.concatenate([gm_b[:, :CHUNK].T, gm_b[:, :CHUNK].T], axis=0)
    gmb = jnp.stack([b_prompt, b_sample]).astype(F32)
    gmb = jnp.repeat(gmb, d_gm // N_GM_GROUPS, axis=2)
    return wsp, gmb


def _s5_compact(lam_re, lam_im, log_dt, b_re, b_im, c_re, c_im):
    n_groups = lam_re.shape[0]
    nb = n_groups // S5_LANE_GROUPS
    per_state = jnp.stack([lam_re, lam_im, jnp.broadcast_to(log_dt[:, None], lam_re.shape)]).astype(F32)
    per_state = per_state.reshape(3, nb, S5_LANE_GROUPS, S5_STATE).transpose(1, 0, 2, 3)
    lam = per_state.reshape(nb, 3, 1, S5_BLOCK_STATE)
    lam_c = jnp.broadcast_to(per_state[:, :, :, None, :],
                             (nb, 3, S5_LANE_GROUPS, S5_GROUP, S5_STATE)).reshape(nb, 3, LANES, S5_STATE)
    b_c = jnp.stack([b_re, b_im]).astype(F32).transpose(1, 0, 3, 2)
    b_c = b_c.reshape(nb, S5_LANE_GROUPS, 2, S5_GROUP, S5_STATE).transpose(0, 2, 1, 3, 4)
    c_c = jnp.stack([c_re, c_im]).astype(F32).transpose(1, 0, 2, 3)
    c_c = c_c.reshape(nb, S5_LANE_GROUPS, 2, S5_GROUP, S5_STATE).transpose(0, 2, 1, 3, 4)
    return (lam, lam_c, b_c.reshape(nb, 2, LANES, S5_STATE), c_c.reshape(nb, 2, LANES, S5_STATE))


def kernel(x_prompt, x_sample, cache_swa_k, cache_swa_v, state_s5_re, state_s5_im, norm_mix, norm_ffn, norm_final, w_in0, attn_sinks, gm_norm, gm_ws, gm_b, w_out0, s5_lam_re, s5_lam_im, s5_log_dt, s5_b_re, s5_b_im, s5_c_re, s5_c_im, s5_d, s5_w_glu, ffn_w_gate, ffn_w_up, ffn_w_down):
    batch, seq, d = x_prompt.shape
    dec_batch, dec_seq, _ = x_sample.shape
    assert batch == 1 and dec_seq == CHUNK and norm_mix.shape[0] == 2
    n_prompt = batch * seq
    n_sample = dec_batch * dec_seq
    d_gm = gm_norm.shape[-1]

    bias_tbl, sink_row = _attention_tables(attn_sinks[0])
    wsp, gmb = _gm_tables(gm_ws[0], gm_b[0], d_gm)
    x1, k_all, v_all, gvn = _mix0(
        x_prompt.reshape(n_prompt, d), x_sample.reshape(n_sample, d),
        cache_swa_k[0].reshape(dec_batch * WINDOW, D_KV), cache_swa_v[0].reshape(dec_batch * WINDOW, D_KV),
        norm_mix[0].reshape(1, d), w_in0[0].astype(BF16), bias_tbl, sink_row,
        gm_norm[0].reshape(1, d_gm), wsp, gmb, w_out0[0].astype(BF16), units=MIX_UNITS)

    x2, rstd2 = _ffn(x1, norm_ffn[0].reshape(1, d), ffn_w_gate, ffn_w_up, ffn_w_down,
                     norm_mix[1].reshape(1, d), layer=0, tm=FFN_TM, tf=FFN_TF, final=False)

    n_groups = s5_lam_re.shape[1]
    dall, wec, wsc, tab = _s5_prep(
        *_s5_compact(s5_lam_re[0], s5_lam_im[0], s5_log_dt[0], s5_b_re[0], s5_b_im[0], s5_c_re[0],
                     s5_c_im[0]),
        seg_rows=n_prompt // (S5_L * SUBLANES))
    ys5, sf_re, sf_im, w_glu_bf16 = _s5_core(
        x2, _slab_row_order(rstd2, n_prompt), norm_mix[1].reshape(1, d), dall, wec, wsc, tab,
        state_s5_re[0].reshape(dec_batch, n_groups * S5_STATE),
        state_s5_im[0].reshape(dec_batch, n_groups * S5_STATE), s5_d[0].reshape(1, d), s5_w_glu[0],
        n_prompt_rows=n_prompt)
    x3 = _glu(x2, ys5, w_glu_bf16, tm=GLU_TM)

    def last_ffn(row0, n_rows):
        (y,) = _ffn(x3, norm_ffn[1].reshape(1, d), ffn_w_gate, ffn_w_up, ffn_w_down,
                    norm_final.reshape(1, d), layer=1, tm=FFN_TM, tf=FFN_TF, final=True,
                    row0=row0, n_rows=n_rows)
        return y

    y_prompt = last_ffn(0, n_prompt)
    y_sample = last_ffn(n_prompt, n_sample)

    keep = min(WINDOW, seq)
    y_prompt = y_prompt.reshape(batch, seq, d)
    y_sample = y_sample.reshape(dec_batch, dec_seq, d)
    kv_shape_p = (1, batch, keep, N_KV_HEADS, HEAD_DIM)
    kv_shape_s = (1, dec_batch, dec_seq, N_KV_HEADS, HEAD_DIM)
    st_p = (1, batch, n_groups, S5_STATE)
    st_s = (1, dec_batch, n_groups, S5_STATE)
    return (y_prompt, y_sample,
            k_all[n_prompt - keep:n_prompt].reshape(kv_shape_p),
            v_all[n_prompt - keep:n_prompt].reshape(kv_shape_p),
            k_all[n_prompt:].reshape(kv_shape_s),
            v_all[n_prompt:].reshape(kv_shape_s),
            gvn.reshape(1, dec_batch, dec_seq, d_gm),
            sf_re[dec_batch].reshape(st_p), sf_im[dec_batch].reshape(st_p),
            sf_re[:dec_batch].reshape(st_s), sf_im[:dec_batch].reshape(st_s))
```
